```python
import math
import jax, jax.numpy as jnp
from jax import lax
import numpy as np

D_MODEL = 1024
BATCH = 8
SEQ = 4096
DEPTH = 1

ATTN_HEADS = 8
ATTN_KV_HEADS = 2
ATTN_HEAD_DIM = 64
ATTN_GROUP = ATTN_HEADS // ATTN_KV_HEADS
WINDOW = 128
ATTN_BLOCK = WINDOW
ATTN_Q_WIDTH = ATTN_HEADS * ATTN_HEAD_DIM
ATTN_KV_WIDTH = ATTN_KV_HEADS * ATTN_HEAD_DIM
MLSTM_HEADS = 4
MLSTM_QK_DIM = 64
MLSTM_V_DIM = 128
MLSTM_QK_WIDTH = MLSTM_HEADS * MLSTM_QK_DIM
MLSTM_V_WIDTH = MLSTM_HEADS * MLSTM_V_DIM
MLSTM_CHUNK = 128
CONV_WIDTH = 4
D_FF = -(-(8 * D_MODEL) // (3 * 256)) * 256
NORM_EPS = 1e-6

SPLIT_SIZES = (ATTN_Q_WIDTH, ATTN_KV_WIDTH, ATTN_KV_WIDTH,
               2 * MLSTM_QK_WIDTH, MLSTM_V_WIDTH,
               MLSTM_V_WIDTH, MLSTM_HEADS, MLSTM_HEADS,
               D_MODEL, D_MODEL)
IN_WIDTH = sum(SPLIT_SIZES)

kernel_name = "hybrid_swa_sink_alibi_mlstm_gated_swiglu"


def _split_points():
    pts, acc = [], 0
    for s in SPLIT_SIZES[:-1]:
        acc += s
        pts.append(acc)
    return pts


def rmsnorm(x, g):
    x32 = x.astype(jnp.float32)
    y = x32 * lax.rsqrt(jnp.mean(x32 * x32, axis=-1, keepdims=True) + NORM_EPS)
    return (y * g.astype(jnp.float32)).astype(x.dtype)


def alibi_slopes(n_heads):
    return jnp.exp2(-8.0 * jnp.arange(1, n_heads + 1, dtype=jnp.float32) / n_heads)


def sliding_window_attention(q, k, v, sinks):
    B, S, _ = q.shape
    W = ATTN_BLOCK
    nb = S // W
    qb = q.reshape(B, nb, W, ATTN_KV_HEADS, ATTN_GROUP, ATTN_HEAD_DIM)
    kb = k.reshape(B, nb, W, ATTN_KV_HEADS, ATTN_HEAD_DIM)
    vb = v.reshape(B, nb, W, ATTN_KV_HEADS, ATTN_HEAD_DIM)

    def with_prev(t):
        prev = jnp.pad(t, ((0, 0), (1, 0), (0, 0), (0, 0), (0, 0)))[:, :-1]
        return jnp.concatenate([prev, t], axis=2)

    kk, vv = with_prev(kb), with_prev(vb)
    s = jnp.einsum('bnqhgd,bnkhd->bhgnqk', qb, kk, preferred_element_type=jnp.float32)
    s = s * (ATTN_HEAD_DIM ** -0.5)
    qi = jnp.arange(W)[:, None]
    kj = jnp.arange(2 * W)[None, :]
    dist = qi - kj + W
    blk = jnp.arange(nb)[:, None, None]
    valid = (dist >= 0) & (dist < WINDOW) & ((blk > 0) | (kj >= W))
    slopes = alibi_slopes(ATTN_HEADS).reshape(ATTN_KV_HEADS, ATTN_GROUP)
    s = s - slopes[:, :, None, None, None] * dist.astype(jnp.float32)
    s = jnp.where(valid, s, -jnp.inf)
    sink = sinks.astype(jnp.float32).reshape(ATTN_KV_HEADS, ATTN_GROUP)[:, :, None, None, None]
    mx = jnp.maximum(jnp.max(s, axis=-1, keepdims=True), sink)
    p = jnp.exp(s - mx)
    p = p / (jnp.sum(p, axis=-1, keepdims=True) + jnp.exp(sink - mx))
    o = jnp.einsum('bhgnqk,bnkhd->bnqhgd', p.astype(v.dtype), vv)
    return o.reshape(B, S, ATTN_Q_WIDTH)


def causal_conv(x, w, b):
    S = x.shape[1]
    xp = jnp.pad(x, ((0, 0), (CONV_WIDTH - 1, 0), (0, 0)))
    out = xp[:, 0:S] * w[0]
    for j in range(1, CONV_WIDTH):
        out = out + xp[:, j:j + S] * w[j]
    return out + b


def mlstm_chunkwise(q, k, v, i_pre, f_pre):
    B, S, H, dk = q.shape
    dv = v.shape[-1]
    L = MLSTM_CHUNK
    nc = S // L

    def chunks(t):
        return t.astype(jnp.float32).reshape(B, nc, L, H, -1).transpose(1, 0, 3, 2, 4)

    def gchunks(t):
        return t.astype(jnp.float32).reshape(B, nc, L, H).transpose(1, 0, 3, 2)

    qs = chunks(q)
    ks = chunks(k) * (dk ** -0.5)
    vs = chunks(v)
    igs = gchunks(i_pre)
    lfs = gchunks(jax.nn.log_sigmoid(f_pre.astype(jnp.float32)))
    causal = jnp.tril(jnp.ones((L, L), dtype=bool))

    def step(carry, xs):
        C, n, m = carry
        qc, kc, vc, ic, lfc = xs
        b = jnp.cumsum(lfc, axis=-1)
        dlog = b[..., :, None] - b[..., None, :] + ic[..., None, :]
        dlog = jnp.where(causal, dlog, -jnp.inf)
        a = b + m[..., None]
        m_t = jnp.maximum(a, jnp.max(dlog, axis=-1))
        w_intra = jnp.exp(dlog - m_t[..., None])
        w_inter = jnp.exp(a - m_t)
        sc = jnp.einsum('bhtd,bhsd->bhts', qc, kc) * w_intra
        num = jnp.einsum('bhts,bhsv->bhtv', sc, vc) + w_inter[..., None] * jnp.einsum('bhtk,bhkv->bhtv', qc, C)
        nq = jnp.sum(sc, axis=-1) + w_inter * jnp.einsum('bhtk,bhk->bht', qc, n)
        h = num / jnp.maximum(jnp.abs(nq), jnp.exp(-m_t))[..., None]
        m_new = m_t[..., -1]
        w_state = jnp.exp(b[..., -1:] - b + ic - m_new[..., None])
        decay = jnp.exp(b[..., -1] + m - m_new)
        C_new = decay[..., None, None] * C + jnp.einsum('bhs,bhsk,bhsv->bhkv', w_state, kc, vc)
        n_new = decay[..., None] * n + jnp.einsum('bhs,bhsk->bhk', w_state, kc)
        return (C_new, n_new, m_new), h

    init = (jnp.zeros((B, H, dk, dv), jnp.float32),
            jnp.zeros((B, H, dk), jnp.float32),
            jnp.zeros((B, H), jnp.float32))
    _, hs = lax.scan(step, init, (qs, ks, vs, igs, lfs))
    return hs.transpose(1, 0, 3, 2, 4).reshape(B, S, H, dv)


def hybrid_layer(x, norm1_g, w_in, conv_w, conv_b, i_bias, f_bias, mlstm_norm_g, attn_sinks,
                 w_attn_branch, w_mlstm_branch, w_out, norm2_g, w_ffn_gate, w_ffn_up, w_ffn_down):
    B, S, _ = x.shape
    u = rmsnorm(x, norm1_g)
    proj = u @ w_in
    aq, ak, av, mqk, mv, mo, mi, mf, ga, gm = jnp.split(proj, _split_points(), axis=-1)

    ya = sliding_window_attention(aq, ak, av, attn_sinks) @ w_attn_branch

    qk = jax.nn.silu(causal_conv(mqk, conv_w, conv_b))
    mq, mk = jnp.split(qk, 2, axis=-1)
    h = mlstm_chunkwise(mq.reshape(B, S, MLSTM_HEADS, MLSTM_QK_DIM),
                        mk.reshape(B, S, MLSTM_HEADS, MLSTM_QK_DIM),
                        mv.reshape(B, S, MLSTM_HEADS, MLSTM_V_DIM),
                        mi + i_bias, mf + f_bias)
    h = rmsnorm(h, mlstm_norm_g.reshape(MLSTM_HEADS, MLSTM_V_DIM)).astype(x.dtype)
    h = h.reshape(B, S, MLSTM_V_WIDTH) * jax.nn.sigmoid(mo)
    ym = h @ w_mlstm_branch

    x = x + (jax.nn.sigmoid(ga) * ya + jax.nn.sigmoid(gm) * ym) @ w_out

    f = rmsnorm(x, norm2_g)
    x = x + (jax.nn.silu(f @ w_ffn_gate) * (f @ w_ffn_up)) @ w_ffn_down
    return x


def setup_inputs(seed: int = 0) -> dict:
    key = jax.random.key(seed)
    ks = jax.random.split(key, 20)
    f32 = jnp.float32

    def nrm(k, shape, scale):
        return jax.random.normal(k, shape, f32) * scale

    return {
        "x": nrm(ks[0], (BATCH, SEQ, D_MODEL), 1.0),
        "norm1_g": 1.0 + nrm(ks[1], (DEPTH, D_MODEL), 0.02),
        "w_in": nrm(ks[2], (DEPTH, D_MODEL, IN_WIDTH), D_MODEL ** -0.5),
        "conv_w": nrm(ks[3], (DEPTH, CONV_WIDTH, 2 * MLSTM_QK_WIDTH), CONV_WIDTH ** -0.5),
        "conv_b": nrm(ks[4], (DEPTH, 2 * MLSTM_QK_WIDTH), 0.01),
        "i_bias": nrm(ks[5], (DEPTH, MLSTM_HEADS), 0.1),
        "f_bias": jnp.linspace(3.0, 6.0, MLSTM_HEADS, dtype=f32)[None, :] + nrm(ks[6], (DEPTH, MLSTM_HEADS), 0.1),
        "mlstm_norm_g": 1.0 + nrm(ks[7], (DEPTH, MLSTM_V_WIDTH), 0.02),
        "attn_sinks": nrm(ks[8], (DEPTH, ATTN_HEADS), 0.5),
        "w_attn_branch": nrm(ks[9], (DEPTH, ATTN_Q_WIDTH, D_MODEL), ATTN_Q_WIDTH ** -0.5),
        "w_mlstm_branch": nrm(ks[10], (DEPTH, MLSTM_V_WIDTH, D_MODEL), MLSTM_V_WIDTH ** -0.5),
        "w_out": nrm(ks[11], (DEPTH, D_MODEL, D_MODEL), D_MODEL ** -0.5),
        "norm2_g": 1.0 + nrm(ks[12], (DEPTH, D_MODEL), 0.02),
        "w_ffn_gate": nrm(ks[13], (DEPTH, D_MODEL, D_FF), D_MODEL ** -0.5),
        "w_ffn_up": nrm(ks[14], (DEPTH, D_MODEL, D_FF), D_MODEL ** -0.5),
        "w_ffn_down": nrm(ks[15], (DEPTH, D_FF, D_MODEL), D_FF ** -0.5),
        "final_norm_g": 1.0 + nrm(ks[16], (D_MODEL,), 0.02),
    }


def reference(x, norm1_g, w_in, conv_w, conv_b, i_bias, f_bias, mlstm_norm_g, attn_sinks,
              w_attn_branch, w_mlstm_branch, w_out, norm2_g, w_ffn_gate, w_ffn_up, w_ffn_down,
              final_norm_g):
    for l in range(DEPTH):
        x = hybrid_layer(x, norm1_g[l], w_in[l], conv_w[l], conv_b[l], i_bias[l], f_bias[l],
                         mlstm_norm_g[l], attn_sinks[l], w_attn_branch[l], w_mlstm_branch[l],
                         w_out[l], norm2_g[l], w_ffn_gate[l], w_ffn_up[l], w_ffn_down[l])
    return rmsnorm(x, final_norm_g)
```

```python
import functools

import jax
import jax.numpy as jnp
from jax import lax
from jax.experimental import pallas as pl
from jax.experimental.pallas import tpu as pltpu

D_MODEL = 1024
ATTN_HEADS = 8
ATTN_KV_HEADS = 2
ATTN_HEAD_DIM = 64
ATTN_GROUP = ATTN_HEADS // ATTN_KV_HEADS
WINDOW = 128
ATTN_Q_WIDTH = ATTN_HEADS * ATTN_HEAD_DIM
ATTN_KV_WIDTH = ATTN_KV_HEADS * ATTN_HEAD_DIM
MLSTM_HEADS = 4
MLSTM_QK_DIM = 64
MLSTM_V_DIM = 128
MLSTM_QK_WIDTH = MLSTM_HEADS * MLSTM_QK_DIM
MLSTM_V_WIDTH = MLSTM_HEADS * MLSTM_V_DIM
CHUNK = 128
CONV_WIDTH = 4
NORM_EPS = 1e-6

LANES = 128
CONV_PAD = 8
VMEM_LIMIT_BYTES = 56 * 1024 * 1024

QKV_WIDTH = ATTN_Q_WIDTH + 2 * ATTN_KV_WIDTH
MM_WIDTH = 2 * MLSTM_QK_WIDTH + 2 * MLSTM_V_WIDTH
GATE_WIDTH = LANES
MERGE_WIDTH = 2 * D_MODEL

BF16 = jnp.bfloat16
F32 = jnp.float32
NEG_INF = float("-inf")


def _dot(a, b):
    return jnp.dot(a, b, preferred_element_type=F32)


def _dot_nt(a, b):
    return lax.dot_general(a, b, (((1,), (1,)), ((), ())), preferred_element_type=F32)


def _rms_scale(x):
    return lax.rsqrt(jnp.mean(x * x, axis=-1, keepdims=True) + NORM_EPS)


def _const_spec(shape):
    nd = len(shape)
    return pl.BlockSpec(shape, lambda *_: (0,) * nd, pipeline_mode=pl.Buffered(1))


def _inproj_kernel(x_ref, g_ref, wa_ref, wm_ref, wgt_ref, wgg_ref, qkv_ref, mm_ref, gt_ref, gg_ref):
    x = x_ref[...]
    u = (x * _rms_scale(x) * g_ref[...]).astype(BF16)
    qkv_ref[...] = _dot(u, wa_ref[...]).astype(BF16)
    mm_ref[...] = _dot(u, wm_ref[...]).astype(BF16)
    gt_ref[...] = _dot(u, wgt_ref[...])
    gg_ref[...] = _dot(u, wgg_ref[...]).astype(BF16)


def _inproj(x2d, g1, wa, wm, wgt, wgg, tm):
    t = x2d.shape[0]
    row = lambda w: pl.BlockSpec((tm, w), lambda i: (i, 0))
    return pl.pallas_call(
        _inproj_kernel,
        grid=(t // tm,),
        in_specs=[row(D_MODEL), _const_spec(g1.shape), _const_spec(wa.shape), _const_spec(wm.shape),
                  _const_spec(wgt.shape), _const_spec(wgg.shape)],
        out_specs=[row(QKV_WIDTH), row(MM_WIDTH), row(GATE_WIDTH), row(MERGE_WIDTH)],
        out_shape=[jax.ShapeDtypeStruct((t, QKV_WIDTH), BF16), jax.ShapeDtypeStruct((t, MM_WIDTH), BF16),
                   jax.ShapeDtypeStruct((t, GATE_WIDTH), F32), jax.ShapeDtypeStruct((t, MERGE_WIDTH), BF16)],
        compiler_params=pltpu.CompilerParams(dimension_semantics=("arbitrary",),
                                             vmem_limit_bytes=VMEM_LIMIT_BYTES),
        name="inproj",
    )(x2d, g1, wa, wm, wgt, wgg)


def _log_sigmoid(x):
    return jnp.minimum(x, 0.0) - jnp.log1p(jnp.exp(-jnp.abs(x)))


def _swap_halves(x_bf16):
    return pltpu.roll(x_bf16.astype(F32), LANES // 2, axis=1).astype(BF16)


def _mixer_kernel(sinks_ref, qkv_ref, mm_ref, gt_ref, gbias_ref, convw_ref, convb_ref, normg_ref,
                  ao_ref, hm_ref,
                  bias_ref, kvp_ref, conv_ref, qk_ref, cst_ref, mst_ref, *, nch):
    b_idx = pl.program_id(0)
    j_idx = pl.program_id(1)
    tm = nch * CHUNK

    @pl.when((b_idx == 0) & (j_idx == 0))
    def _():
        qi = lax.broadcasted_iota(jnp.int32, (CHUNK, 2 * CHUNK), 0)
        kj = lax.broadcasted_iota(jnp.int32, (CHUNK, 2 * CHUNK), 1)
        dist = qi - kj + WINDOW
        valid = (dist >= 0) & (dist < WINDOW)
        valid_first = valid & (kj >= CHUNK)
        distf = dist.astype(F32)
        for j in range(ATTN_HEADS):
            slope = 2.0 ** (-8.0 * (j + 1) / ATTN_HEADS)
            bias_ref[0, j] = jnp.where(valid, -slope * distf, NEG_INF)
            bias_ref[1, j] = jnp.where(valid_first, -slope * distf, NEG_INF)

    @pl.when(j_idx == 0)
    def _():
        kvp_ref[...] = jnp.zeros_like(kvp_ref)
        conv_ref[0:CONV_PAD, :] = jnp.zeros((CONV_PAD, 2 * MLSTM_QK_WIDTH), F32)
        cst_ref[...] = jnp.zeros_like(cst_ref)
        mst_ref[...] = jnp.zeros_like(mst_ref)

    conv_ref[CONV_PAD:CONV_PAD + tm, :] = mm_ref[:, 0:2 * MLSTM_QK_WIDTH].astype(F32)
    acc = conv_ref[pl.ds(CONV_PAD - (CONV_WIDTH - 1), tm), :] * convw_ref[0:1, :]
    for t in range(1, CONV_WIDTH):
        acc = acc + conv_ref[pl.ds(CONV_PAD - (CONV_WIDTH - 1) + t, tm), :] * convw_ref[t:t + 1, :]
    acc = acc + convb_ref[...]
    qk_ref[...] = acc * jax.nn.sigmoid(acc)
    conv_ref[0:CONV_PAD, :] = conv_ref[tm:tm + CONV_PAD, :]

    lane_kv = lax.broadcasted_iota(jnp.int32, (2 * CHUNK, LANES), 1)
    left_kv = lane_kv < (LANES // 2)
    lane_o = lax.broadcasted_iota(jnp.int32, (CHUNK, LANES), 1)
    left_o = lane_o < (LANES // 2)
    row_c = lax.broadcasted_iota(jnp.int32, (CHUNK, CHUNK), 0)
    col_c = lax.broadcasted_iota(jnp.int32, (CHUNK, CHUNK), 1)
    causal = col_c <= row_c
    tril = causal.astype(F32)
    top_half = row_c < MLSTM_QK_DIM
    top_half_w = lax.broadcasted_iota(jnp.int32, (CHUNK, 2 * LANES), 0) < MLSTM_QK_DIM
    top_col = lax.broadcasted_iota(jnp.int32, (CHUNK, 1), 0) < MLSTM_QK_DIM
    lane_row = lax.broadcasted_iota(jnp.int32, (1, LANES), 1)
    ones_v = jnp.ones((CHUNK, LANES), BF16)

    def chunk_body(c, carry):
        r0 = pl.multiple_of(c * CHUNK, CHUNK)
        rows = pl.ds(r0, CHUNK)
        first = ((j_idx == 0) & (c == 0)).astype(jnp.int32)

        q = qkv_ref[rows, 0:ATTN_Q_WIDTH] * jnp.asarray(ATTN_HEAD_DIM ** -0.5, BF16)
        k_cur = qkv_ref[rows, ATTN_Q_WIDTH:ATTN_Q_WIDTH + ATTN_KV_WIDTH]
        v_cur = qkv_ref[rows, ATTN_Q_WIDTH + ATTN_KV_WIDTH:QKV_WIDTH]
        k_cur_sw = _swap_halves(k_cur)
        v_cur_sw = _swap_halves(v_cur)
        kk = jnp.concatenate([kvp_ref[0], k_cur], axis=0)
        kk_sw = jnp.concatenate([kvp_ref[1], k_cur_sw], axis=0)
        vv = jnp.concatenate([kvp_ref[2], v_cur], axis=0)
        vv_sw = jnp.concatenate([kvp_ref[3], v_cur_sw], axis=0)
        kvp_ref[0] = k_cur
        kvp_ref[1] = k_cur_sw
        kvp_ref[2] = v_cur
        kvp_ref[3] = v_cur_sw
        zero_b = jnp.zeros((), BF16)
        one_b = jnp.ones((), BF16)
        for h in range(ATTN_KV_HEADS):
            k_side, v_side = [], []
            for par in range(2):
                mask = left_kv if par == 0 else jnp.logical_not(left_kv)
                plain = (h == par)
                k_side.append(jnp.where(mask, kk if plain else kk_sw, zero_b))
                v_side.append(jnp.where(mask, vv if plain else vv_sw, one_b))
            for gp in range(ATTN_GROUP // 2):
                pair = h * (ATTN_GROUP // 2) + gp
                qpair = q[:, pair * LANES:(pair + 1) * LANES]
                outs = []
                for par in range(2):
                    j = 2 * pair + par
                    sink = sinks_ref[j]
                    s = _dot_nt(qpair, k_side[par]) + bias_ref[first, j]
                    mx = jnp.maximum(jnp.max(s, axis=-1, keepdims=True), sink)
                    p = jnp.exp(s - mx).astype(BF16)
                    oa = _dot(p, v_side[par])
                    rowsum = oa[:, LANES // 2:LANES // 2 + 1] if par == 0 else oa[:, 0:1]
                    outs.append(oa / (rowsum + jnp.exp(sink - mx)))
                ao_ref[rows, pair * LANES:(pair + 1) * LANES] = jnp.where(left_o, outs[0], outs[1]).astype(BF16)

        gb = gt_ref[rows, :] + gbias_ref[...]
        f_lanes = (lane_o >= MLSTM_HEADS) & (lane_o < 2 * MLSTM_HEADS)
        lf = jnp.where(f_lanes, _log_sigmoid(gb), 0.0)
        bcum = jnp.dot(tril, lf, precision=lax.Precision.HIGHEST, preferred_element_type=F32)
        gcat = jnp.where(lane_o < MLSTM_HEADS, gb, bcum)
        gcat_t = gcat.T
        kt = (qk_ref[rows, MLSTM_QK_WIDTH:2 * MLSTM_QK_WIDTH] * (MLSTM_QK_DIM ** -0.5)).T
        m_row = mst_ref[...]
        m_row_new = m_row
        for pr in range(MLSTM_HEADS // 2):
            qpair = qk_ref[rows, pr * LANES:(pr + 1) * LANES].astype(BF16)
            kt_pair = kt[pr * LANES:(pr + 1) * LANES, :]
            c_pair = cst_ref[pr]
            c_new = None
            decs = []
            for half in range(2):
                h = 2 * pr + half
                sel = top_half if half == 0 else jnp.logical_not(top_half)
                sel_w = top_half_w if half == 0 else jnp.logical_not(top_half_w)
                i_row = gcat_t[h:h + 1, :]
                b_row = gcat_t[MLSTM_HEADS + h:MLSTM_HEADS + h + 1, :]
                b_col = bcum[:, MLSTM_HEADS + h:MLSTM_HEADS + h + 1]
                m_prev = m_row[:, h:h + 1]
                b_last = b_col[CHUNK - 1:CHUNK, :]
                dlog = jnp.where(causal, b_col + (i_row - b_row), NEG_INF)
                a_col = b_col + m_prev
                m_t = jnp.maximum(a_col, jnp.max(dlog, axis=-1, keepdims=True))
                w_intra = jnp.exp(dlog - m_t)
                w_inter = jnp.exp(a_col - m_t)
                m_new = m_t[CHUNK - 1:CHUNK, :]
                w_state = jnp.exp(b_last - b_row + i_row - m_new)
                decs.append(jnp.exp(b_last + m_prev - m_new))
                m_row_new = jnp.where(lane_row == h, m_new, m_row_new)

                v_h = mm_ref[rows, 2 * MLSTM_QK_WIDTH + h * MLSTM_V_DIM:2 * MLSTM_QK_WIDTH + (h + 1) * MLSTM_V_DIM]
                v_aug = jnp.concatenate([v_h, ones_v], axis=1)
                kt_h = jnp.where(sel, kt_pair, 0.0)
                sc = _dot(qpair, kt_h.astype(BF16))
                scw = (sc * w_intra).astype(BF16)
                c_h = jnp.where(sel_w, c_pair, 0.0).astype(BF16)
                num = _dot(scw, v_aug) + w_inter * _dot(qpair, c_h)
                nq = num[:, MLSTM_V_DIM:MLSTM_V_DIM + 1]
                hv = num[:, 0:MLSTM_V_DIM] / jnp.maximum(jnp.abs(nq), jnp.exp(-m_t))
                hn = hv * _rms_scale(hv) * normg_ref[:, h * MLSTM_V_DIM:(h + 1) * MLSTM_V_DIM]
                og = mm_ref[rows, 2 * MLSTM_QK_WIDTH + MLSTM_V_WIDTH + h * MLSTM_V_DIM:
                            2 * MLSTM_QK_WIDTH + MLSTM_V_WIDTH + (h + 1) * MLSTM_V_DIM].astype(F32)
                hm_ref[rows, h * MLSTM_V_DIM:(h + 1) * MLSTM_V_DIM] = (hn * jax.nn.sigmoid(og)).astype(BF16)

                upd = _dot((kt_h * w_state).astype(BF16), v_aug)
                c_new = upd if c_new is None else c_new + upd
            dec_col = jnp.where(top_col, decs[0], decs[1])
            cst_ref[pr] = dec_col * c_pair + c_new
        mst_ref[...] = m_row_new
        return carry

    lax.fori_loop(0, nch, chunk_body, 0)


def _mixer(sinks, qkv, mm, gt, gbias, convw, convb, normg, batch, seq, nch):
    tm = nch * CHUNK
    nj = seq // tm
    t = batch * seq
    row = lambda w: pl.BlockSpec((tm, w), lambda b, j: (b * nj + j, 0))
    return pl.pallas_call(
        functools.partial(_mixer_kernel, nch=nch),
        grid=(batch, nj),
        in_specs=[pl.BlockSpec(memory_space=pltpu.SMEM),
                  row(QKV_WIDTH), row(MM_WIDTH), row(GATE_WIDTH),
                  _const_spec(gbias.shape), _const_spec(convw.shape), _const_spec(convb.shape),
                  _const_spec(normg.shape)],
        out_specs=[row(ATTN_Q_WIDTH), row(MLSTM_V_WIDTH)],
        out_shape=[jax.ShapeDtypeStruct((t, ATTN_Q_WIDTH), BF16), jax.ShapeDtypeStruct((t, MLSTM_V_WIDTH), BF16)],
        scratch_shapes=[
            pltpu.VMEM((2, ATTN_HEADS, CHUNK, 2 * CHUNK), F32),
            pltpu.VMEM((4, CHUNK, ATTN_KV_WIDTH), BF16),
            pltpu.VMEM((CONV_PAD + tm, 2 * MLSTM_QK_WIDTH), F32),
            pltpu.VMEM((tm, 2 * MLSTM_QK_WIDTH), F32),
            pltpu.VMEM((MLSTM_HEADS // 2, 2 * MLSTM_QK_DIM, 2 * MLSTM_V_DIM), F32),
            pltpu.VMEM((1, LANES), F32),
        ],
        compiler_params=pltpu.CompilerParams(dimension_semantics=("arbitrary", "arbitrary"),
                                             vmem_limit_bytes=VMEM_LIMIT_BYTES),
        name="mixer",
    )(sinks, qkv, mm, gt, gbias, convw, convb, normg)


def _tail_kernel(x_ref, ao_ref, hm_ref, gg_ref, wab_ref, wmb_ref, wo_ref, g2_ref, wg_ref, wu_ref, wd_ref,
                 gf_ref, out_ref, *, ff_split):
    ya = _dot(ao_ref[...], wab_ref[...])
    ym = _dot(hm_ref[...], wmb_ref[...])
    ga = jax.nn.sigmoid(gg_ref[:, 0:D_MODEL].astype(F32))
    gm = jax.nn.sigmoid(gg_ref[:, D_MODEL:2 * D_MODEL].astype(F32))
    z = (ga * ya + gm * ym).astype(BF16)
    x1 = x_ref[...] + _dot(z, wo_ref[...])
    f = (x1 * _rms_scale(x1) * g2_ref[...]).astype(BF16)
    d_ff = wg_ref.shape[1]
    step = d_ff // ff_split
    x2 = x1
    for s in range(ff_split):
        cols = slice(s * step, (s + 1) * step)
        gte = _dot(f, wg_ref[:, cols])
        up = _dot(f, wu_ref[:, cols])
        hh = (gte * jax.nn.sigmoid(gte) * up).astype(BF16)
        x2 = x2 + _dot(hh, wd_ref[cols, :])
    out_ref[...] = x2 * _rms_scale(x2) * gf_ref[...]


def _tail(x2d, ao, hm, gg, wab, wmb, wo, g2, wg, wu, wd, gf, tm, ff_split):
    t = x2d.shape[0]
    row = lambda w: pl.BlockSpec((tm, w), lambda i: (i, 0))
    consts = (wab, wmb, wo, g2, wg, wu, wd, gf)
    return pl.pallas_call(
        functools.partial(_tail_kernel, ff_split=ff_split),
        grid=(t // tm,),
        in_specs=[row(D_MODEL), row(ATTN_Q_WIDTH), row(MLSTM_V_WIDTH), row(MERGE_WIDTH)]
                 + [_const_spec(c.shape) for c in consts],
        out_specs=row(D_MODEL),
        out_shape=jax.ShapeDtypeStruct((t, D_MODEL), F32),
        compiler_params=pltpu.CompilerParams(dimension_semantics=("arbitrary",),
                                             vmem_limit_bytes=VMEM_LIMIT_BYTES),
        name="tail",
    )(x2d, ao, hm, gg, *consts)


def _layer(x2d, batch, seq, norm1_g, w_in, conv_w, conv_b, i_bias, f_bias, mlstm_norm_g, attn_sinks,
           w_attn_branch, w_mlstm_branch, w_out, norm2_g, w_ffn_gate, w_ffn_up, w_ffn_down, out_g):
    o = 0
    wa = w_in[:, o:o + QKV_WIDTH]; o += QKV_WIDTH
    wm_qk = w_in[:, o:o + 2 * MLSTM_QK_WIDTH]; o += 2 * MLSTM_QK_WIDTH
    wm_v = w_in[:, o:o + MLSTM_V_WIDTH]; o += MLSTM_V_WIDTH
    wm_o = w_in[:, o:o + MLSTM_V_WIDTH]; o += MLSTM_V_WIDTH
    w_if = w_in[:, o:o + 2 * MLSTM_HEADS]; o += 2 * MLSTM_HEADS
    wgg = w_in[:, o:o + MERGE_WIDTH]
    wm = jnp.concatenate([wm_qk, wm_v, wm_o], axis=1)
    wgt = jnp.pad(w_if, ((0, 0), (0, GATE_WIDTH - 2 * MLSTM_HEADS)))
    gbias = jnp.pad(jnp.concatenate([i_bias, f_bias]), (0, GATE_WIDTH - 2 * MLSTM_HEADS)).reshape(1, GATE_WIDTH)

    qkv, mm, gt, gg = _inproj(x2d, norm1_g.reshape(1, D_MODEL), wa.astype(BF16), wm.astype(BF16),
                              wgt.astype(BF16), wgg.astype(BF16), tm=512)
    ao, hm = _mixer(attn_sinks, qkv, mm, gt, gbias, conv_w, conv_b.reshape(1, -1),
                    mlstm_norm_g.reshape(1, MLSTM_V_WIDTH), batch, seq, nch=4)
    return _tail(x2d, ao, hm, gg, w_attn_branch.astype(BF16), w_mlstm_branch.astype(BF16), w_out.astype(BF16),
                 norm2_g.reshape(1, D_MODEL), w_ffn_gate.astype(BF16), w_ffn_up.astype(BF16),
                 w_ffn_down.astype(BF16), out_g.reshape(1, D_MODEL), tm=512, ff_split=2)


def kernel(x, norm1_g, w_in, conv_w, conv_b, i_bias, f_bias, mlstm_norm_g, attn_sinks, w_attn_branch,
           w_mlstm_branch, w_out, norm2_g, w_ffn_gate, w_ffn_up, w_ffn_down, final_norm_g):
    batch, seq, d = x.shape
    depth = norm1_g.shape[0]
    assert depth == 1 and d == D_MODEL and seq % (4 * CHUNK) == 0
    out = _layer(x.reshape(batch * seq, d), batch, seq, norm1_g[0], w_in[0], conv_w[0], conv_b[0], i_bias[0],
                 f_bias[0], mlstm_norm_g[0], attn_sinks[0], w_attn_branch[0], w_mlstm_branch[0], w_out[0],
                 norm2_g[0], w_ffn_gate[0], w_ffn_up[0], w_ffn_down[0], final_norm_g)
    return out.reshape(batch, seq, d)
```

```python
import functools

import jax
import jax.numpy as jnp
from jax import lax
from jax.experimental import pallas as pl
from jax.experimental.pallas import tpu as pltpu

D_MODEL = 1024
ATTN_HEADS = 8
ATTN_KV_HEADS = 2
ATTN_HEAD_DIM = 64
ATTN_GROUP = ATTN_HEADS // ATTN_KV_HEADS
WINDOW = 128
ATTN_Q_WIDTH = ATTN_HEADS * ATTN_HEAD_DIM
ATTN_KV_WIDTH = ATTN_KV_HEADS * ATTN_HEAD_DIM
MLSTM_HEADS = 4
MLSTM_QK_DIM = 64
MLSTM_V_DIM = 128
MLSTM_QK_WIDTH = MLSTM_HEADS * MLSTM_QK_DIM
MLSTM_V_WIDTH = MLSTM_HEADS * MLSTM_V_DIM
CHUNK = 128
CONV_WIDTH = 4
NORM_EPS = 1e-6

LANES = 128
HALF = LANES // 2
CONV_PAD = 8
VMEM_LIMIT_BYTES = 56 * 1024 * 1024

QKV_WIDTH = ATTN_Q_WIDTH + 2 * ATTN_KV_WIDTH
MM_WIDTH = 2 * MLSTM_QK_WIDTH + 2 * MLSTM_V_WIDTH
GATE_WIDTH = LANES
MERGE_WIDTH = 2 * D_MODEL

Q_HEAD_ORDER = tuple(h * ATTN_GROUP + g for g in range(ATTN_GROUP) for h in range(ATTN_KV_HEADS))

BF16 = jnp.bfloat16
F32 = jnp.float32
NEG_INF = float("-inf")


def _dot(a, b):
    return jnp.dot(a, b, preferred_element_type=F32)


def _dot_nt(a, b):
    return lax.dot_general(a, b, (((1,), (1,)), ((), ())), preferred_element_type=F32)


def _rms_scale(x):
    return lax.rsqrt(jnp.mean(x * x, axis=-1, keepdims=True) + NORM_EPS)


def _const_spec(shape):
    nd = len(shape)
    return pl.BlockSpec(shape, lambda *_: (0,) * nd, pipeline_mode=pl.Buffered(1))


def _inproj_kernel(x_ref, g_ref, wa_ref, wqk_ref, wvo_ref, wgt_ref, wgg_ref, convw_ref, convb_ref,
                   qkv_ref, mm_ref, gt_ref, gg_ref, conv_ref, *, tiles_per_seq):
    tm = x_ref.shape[0]

    @pl.when(pl.program_id(0) % tiles_per_seq == 0)
    def _():
        conv_ref[0:CONV_PAD, :] = jnp.zeros((CONV_PAD, 2 * MLSTM_QK_WIDTH), F32)

    x = x_ref[...]
    u = (x * _rms_scale(x) * g_ref[...]).astype(BF16)
    q_scale = ATTN_HEAD_DIM ** -0.5
    qkv_ref[:, 0:ATTN_Q_WIDTH] = (_dot(u, wa_ref[:, 0:ATTN_Q_WIDTH]) * q_scale).astype(BF16)
    qkv_ref[:, ATTN_Q_WIDTH:QKV_WIDTH] = _dot(u, wa_ref[:, ATTN_Q_WIDTH:QKV_WIDTH]).astype(BF16)
    mm_ref[:, 2 * MLSTM_QK_WIDTH:MM_WIDTH] = _dot(u, wvo_ref[...]).astype(BF16)
    gt_ref[...] = _dot(u, wgt_ref[...])
    gg_ref[...] = _dot(u, wgg_ref[...]).astype(BF16)

    conv_ref[CONV_PAD:CONV_PAD + tm, :] = _dot(u, wqk_ref[...])
    acc = conv_ref[pl.ds(CONV_PAD - (CONV_WIDTH - 1), tm), :] * convw_ref[0:1, :]
    for t in range(1, CONV_WIDTH):
        acc = acc + conv_ref[pl.ds(CONV_PAD - (CONV_WIDTH - 1) + t, tm), :] * convw_ref[t:t + 1, :]
    acc = acc + convb_ref[...]
    lane = lax.broadcasted_iota(jnp.int32, (1, 2 * MLSTM_QK_WIDTH), 1)
    k_scale = jnp.where(lane < MLSTM_QK_WIDTH, 1.0, MLSTM_QK_DIM ** -0.5)
    mm_ref[:, 0:2 * MLSTM_QK_WIDTH] = (acc * jax.nn.sigmoid(acc) * k_scale).astype(BF16)
    conv_ref[0:CONV_PAD, :] = conv_ref[tm:tm + CONV_PAD, :]


def _inproj(x2d, g1, wa, wqk, wvo, wgt, wgg, convw, convb, tm, seq):
    t = x2d.shape[0]
    row = lambda w: pl.BlockSpec((tm, w), lambda i: (i, 0))
    consts = (g1, wa, wqk, wvo, wgt, wgg, convw, convb)
    return pl.pallas_call(
        functools.partial(_inproj_kernel, tiles_per_seq=seq // tm),
        grid=(t // tm,),
        in_specs=[row(D_MODEL)] + [_const_spec(c.shape) for c in consts],
        out_specs=[row(QKV_WIDTH), row(MM_WIDTH), row(GATE_WIDTH), row(MERGE_WIDTH)],
        out_shape=[jax.ShapeDtypeStruct((t, QKV_WIDTH), BF16), jax.ShapeDtypeStruct((t, MM_WIDTH), BF16),
                   jax.ShapeDtypeStruct((t, GATE_WIDTH), F32), jax.ShapeDtypeStruct((t, MERGE_WIDTH), BF16)],
        scratch_shapes=[pltpu.VMEM((CONV_PAD + tm, 2 * MLSTM_QK_WIDTH), F32)],
        compiler_params=pltpu.CompilerParams(dimension_semantics=("arbitrary",),
                                             vmem_limit_bytes=VMEM_LIMIT_BYTES),
        name="inproj",
    )(x2d, *consts)


def _log_sigmoid(x):
    return jnp.minimum(x, 0.0) - jnp.log1p(jnp.exp(-jnp.abs(x)))


def _split3(x):
    hi = x.astype(BF16)
    r1 = x - hi.astype(F32)
    mid = r1.astype(BF16)
    lo = (r1 - mid.astype(F32)).astype(BF16)
    return hi, mid, lo


def _mixer_kernel(sinks_ref, qkv_ref, mm_ref, gt_ref, gbias_ref, normg_ref,
                  ao_ref, hm_ref,
                  btab_ref, kvp_ref, cst_ref, mst_ref, *, nch):
    b_idx = pl.program_id(0)
    j_idx = pl.program_id(1)

    row_c = lax.broadcasted_iota(jnp.int32, (CHUNK, CHUNK), 0)
    col_c = lax.broadcasted_iota(jnp.int32, (CHUNK, CHUNK), 1)

    @pl.when((b_idx == 0) & (j_idx == 0))
    def _():
        cur = row_c <= col_c
        dist = jnp.where(cur, col_c - row_c, col_c - row_c + WINDOW).astype(F32)
        for j in range(ATTN_HEADS):
            slope = 2.0 ** (-8.0 * (j + 1) / ATTN_HEADS)
            btab_ref[0, j] = -slope * dist
            btab_ref[1, j] = jnp.where(cur, -slope * dist, NEG_INF)

    @pl.when(j_idx == 0)
    def _():
        kvp_ref[...] = jnp.zeros_like(kvp_ref)
        cst_ref[...] = jnp.zeros_like(cst_ref)
        mst_ref[...] = jnp.zeros_like(mst_ref)

    cur_t = row_c <= col_c
    causal = col_c <= row_c
    tril_b = causal.astype(BF16)
    lane_c = col_c
    top_rows = row_c < HALF
    top_rows_w = lax.broadcasted_iota(jnp.int32, (CHUNK, 2 * LANES), 0) < HALF
    left_lanes = col_c < HALF
    ones_v = jnp.ones((CHUNK, LANES), BF16)
    zero_b = jnp.zeros((), BF16)
    one_b = jnp.ones((), BF16)
    first = (j_idx == 0).astype(jnp.int32)

    def kv_sides(k_blk, v_blk):
        vt = v_blk.astype(F32).T.astype(BF16)
        k_sides = (jnp.where(left_lanes, k_blk, zero_b), jnp.where(left_lanes, zero_b, k_blk))
        vt_sides = (jnp.where(top_rows, vt, one_b), jnp.where(top_rows, one_b, vt))
        return k_sides, vt_sides

    k_off = ATTN_Q_WIDTH
    v_off = ATTN_Q_WIDTH + ATTN_KV_WIDTH
    prev_sides = kv_sides(kvp_ref[0], kvp_ref[1])
    last = slice((nch - 1) * CHUNK, nch * CHUNK)
    kvp_ref[0] = qkv_ref[last, k_off:k_off + ATTN_KV_WIDTH]
    kvp_ref[1] = qkv_ref[last, v_off:v_off + ATTN_KV_WIDTH]

    m_rows = [mst_ref[h:h + 1, :] for h in range(MLSTM_HEADS)]
    c_pairs = [cst_ref[pr] for pr in range(MLSTM_HEADS // 2)]

    for c in range(nch):
        rows = slice(c * CHUNK, (c + 1) * CHUNK)
        variant = first if c == 0 else 0

        cur_sides = kv_sides(qkv_ref[rows, k_off:k_off + ATTN_KV_WIDTH], qkv_ref[rows, v_off:v_off + ATTN_KV_WIDTH])
        q = qkv_ref[rows, 0:ATTN_Q_WIDTH]
        npair = ATTN_GROUP
        q_pairs = [q[:, p * LANES:(p + 1) * LANES] for p in range(npair)]
        s_t = {}
        for side in range(ATTN_KV_HEADS):
            k_side = jnp.concatenate([prev_sides[0][side], cur_sides[0][side]], axis=0)
            for pp in range(npair // 2):
                q2 = jnp.concatenate([q_pairs[2 * pp], q_pairs[2 * pp + 1]], axis=0)
                s2 = _dot_nt(k_side, q2)
                s_t[(side, 2 * pp)] = s2[:, 0:LANES]
                s_t[(side, 2 * pp + 1)] = s2[:, LANES:2 * LANES]
        p_t, mx_rows = {}, {}
        for side in range(ATTN_KV_HEADS):
            for p in range(npair):
                j = side * ATTN_GROUP + p
                s2 = s_t[(side, p)]
                comb = jnp.where(cur_t, s2[CHUNK:2 * CHUNK], s2[0:CHUNK]) + btab_ref[variant, j]
                mx = jnp.maximum(jnp.max(comb, axis=0, keepdims=True), sinks_ref[j])
                e = jnp.exp(comb - mx)
                p_t[(side, p)] = jnp.concatenate([jnp.where(cur_t, 0.0, e), jnp.where(cur_t, e, 0.0)],
                                                 axis=0).astype(BF16)
                mx_rows[(side, p)] = mx
        o_t = {}
        for side in range(ATTN_KV_HEADS):
            vt_side = jnp.concatenate([prev_sides[1][side], cur_sides[1][side]], axis=1)
            for pp in range(npair // 2):
                p2 = jnp.concatenate([p_t[(side, 2 * pp)], p_t[(side, 2 * pp + 1)]], axis=1)
                o2 = _dot(vt_side, p2)
                o_t[(side, 2 * pp)] = o2[:, 0:LANES]
                o_t[(side, 2 * pp + 1)] = o2[:, LANES:2 * LANES]
        for p in range(npair):
            scaled = []
            for side in range(ATTN_KV_HEADS):
                j = side * ATTN_GROUP + p
                o = o_t[(side, p)]
                sums = o[HALF:HALF + 1, :] if side == 0 else o[0:1, :]
                den = sums + jnp.exp(sinks_ref[j] - mx_rows[(side, p)])
                scaled.append(o * (1.0 / den))
            out_t = jnp.where(top_rows, scaled[0], scaled[1])
            ao_ref[rows, p * LANES:(p + 1) * LANES] = out_t.T.astype(BF16)
        prev_sides = cur_sides

        gb = gt_ref[rows, :] + gbias_ref[...]
        f_lanes = (lane_c >= MLSTM_HEADS) & (lane_c < 2 * MLSTM_HEADS)
        lf = jnp.where(f_lanes, _log_sigmoid(gb), 0.0)
        hi, mid, lo = _split3(lf)
        bcum = _dot(tril_b, hi) + _dot(tril_b, mid) + _dot(tril_b, lo)
        gcat_t = jnp.where(lane_c < MLSTM_HEADS, gb, bcum).T
        kt = mm_ref[rows, MLSTM_QK_WIDTH:2 * MLSTM_QK_WIDTH].astype(F32).T

        intra = []
        for pr in range(MLSTM_HEADS // 2):
            qpair = mm_ref[rows, pr * LANES:(pr + 1) * LANES]
            kt_pair = kt[pr * LANES:(pr + 1) * LANES, :]
            kt_h = (jnp.where(top_rows, kt_pair, 0.0), jnp.where(top_rows, 0.0, kt_pair))
            sc2 = _dot(qpair, jnp.concatenate([kt_h[0].astype(BF16), kt_h[1].astype(BF16)], axis=1))
            for half in range(2):
                h = 2 * pr + half
                i_row = gcat_t[h:h + 1, :]
                b_row = gcat_t[MLSTM_HEADS + h:MLSTM_HEADS + h + 1, :]
                b_bc = jnp.broadcast_to(bcum[:, MLSTM_HEADS + h:MLSTM_HEADS + h + 1], (CHUNK, LANES))
                dlog = jnp.where(causal, b_bc + (i_row - b_row), NEG_INF)
                mloc = jnp.broadcast_to(jnp.max(dlog, axis=-1, keepdims=True), (CHUNK, LANES))
                wloc = jnp.exp(dlog - mloc)
                v_h = mm_ref[rows, 2 * MLSTM_QK_WIDTH + h * MLSTM_V_DIM:2 * MLSTM_QK_WIDTH + (h + 1) * MLSTM_V_DIM]
                v_aug = jnp.concatenate([v_h, ones_v], axis=1)
                scw = (sc2[:, half * LANES:(half + 1) * LANES] * wloc).astype(BF16)
                num_intra = _dot(scw, v_aug)
                b_last = b_bc[CHUNK - 1:CHUNK, :]
                mloc_last = mloc[CHUNK - 1:CHUNK, :]
                u_row = jnp.exp(dlog[CHUNK - 1:CHUNK, :] - mloc_last)
                upd = _dot((kt_h[half] * u_row).astype(BF16), v_aug)
                intra.append((qpair, b_bc, mloc, num_intra, b_last, mloc_last, upd))

        for pr in range(MLSTM_HEADS // 2):
            c_pair = c_pairs[pr]
            decs, eus, upds = [], [], []
            for half in range(2):
                h = 2 * pr + half
                qpair, b_bc, mloc, num_intra, b_last, mloc_last, upd = intra[h]
                m_prev = m_rows[h]
                sel_w = top_rows_w if half == 0 else jnp.logical_not(top_rows_w)
                inter = _dot(qpair, jnp.where(sel_w, c_pair, 0.0).astype(BF16))
                a = b_bc + m_prev
                m_t = jnp.maximum(a, mloc)
                e_t = jnp.exp(mloc - m_t)
                w_inter = jnp.exp(a - m_t)
                numv = e_t * num_intra[:, 0:LANES] + w_inter * inter[:, 0:LANES]
                nq = e_t * num_intra[:, LANES:2 * LANES] + w_inter * inter[:, LANES:2 * LANES]
                hv = numv / jnp.maximum(jnp.abs(nq), jnp.exp(-m_t))
                hn = hv * _rms_scale(hv) * normg_ref[:, h * MLSTM_V_DIM:(h + 1) * MLSTM_V_DIM]
                og = mm_ref[rows, 2 * MLSTM_QK_WIDTH + MLSTM_V_WIDTH + h * MLSTM_V_DIM:
                            2 * MLSTM_QK_WIDTH + MLSTM_V_WIDTH + (h + 1) * MLSTM_V_DIM].astype(F32)
                hm_ref[rows, h * MLSTM_V_DIM:(h + 1) * MLSTM_V_DIM] = (hn * jax.nn.sigmoid(og)).astype(BF16)
                m_new = jnp.maximum(b_last + m_prev, mloc_last)
                decs.append(jnp.exp(b_last + m_prev - m_new))
                eus.append(jnp.exp(mloc_last - m_new))
                upds.append(upd)
                m_rows[h] = m_new
            wide = lambda r: jnp.concatenate([r, r], axis=1)
            c_pairs[pr] = (jnp.where(top_rows_w, wide(decs[0]), wide(decs[1])) * c_pair
                           + wide(eus[0]) * upds[0] + wide(eus[1]) * upds[1])

    for h in range(MLSTM_HEADS):
        mst_ref[h:h + 1, :] = m_rows[h]
    for pr in range(MLSTM_HEADS // 2):
        cst_ref[pr] = c_pairs[pr]


def _mixer(sinks, qkv, mm, gt, gbias, normg, batch, seq, nch):
    tm = nch * CHUNK
    nj = seq // tm
    t = batch * seq
    row = lambda w: pl.BlockSpec((tm, w), lambda b, j: (b * nj + j, 0))
    return pl.pallas_call(
        functools.partial(_mixer_kernel, nch=nch),
        grid=(batch, nj),
        in_specs=[pl.BlockSpec(memory_space=pltpu.SMEM),
                  row(QKV_WIDTH), row(MM_WIDTH), row(GATE_WIDTH),
                  _const_spec(gbias.shape), _const_spec(normg.shape)],
        out_specs=[row(ATTN_Q_WIDTH), row(MLSTM_V_WIDTH)],
        out_shape=[jax.ShapeDtypeStruct((t, ATTN_Q_WIDTH), BF16), jax.ShapeDtypeStruct((t, MLSTM_V_WIDTH), BF16)],
        scratch_shapes=[
            pltpu.VMEM((2, ATTN_HEADS, CHUNK, CHUNK), F32),
            pltpu.VMEM((2, CHUNK, ATTN_KV_WIDTH), BF16),
            pltpu.VMEM((MLSTM_HEADS // 2, 2 * MLSTM_QK_DIM, 2 * MLSTM_V_DIM), F32),
            pltpu.VMEM((8, LANES), F32),
        ],
        compiler_params=pltpu.CompilerParams(dimension_semantics=("arbitrary", "arbitrary"),
                                             vmem_limit_bytes=VMEM_LIMIT_BYTES),
        name="mixer",
    )(sinks, qkv, mm, gt, gbias, normg)


def _tail_kernel(x_ref, ao_ref, hm_ref, gg_ref, wab_ref, wmb_ref, wo_ref, g2_ref, wg_ref, wu_ref, wd_ref,
                 gf_ref, out_ref, *, ff_split):
    ya = _dot(ao_ref[...], wab_ref[...])
    ym = _dot(hm_ref[...], wmb_ref[...])
    ga = jax.nn.sigmoid(gg_ref[:, 0:D_MODEL].astype(F32))
    gm = jax.nn.sigmoid(gg_ref[:, D_MODEL:2 * D_MODEL].astype(F32))
    z = (ga * ya + gm * ym).astype(BF16)
    x1 = x_ref[...] + _dot(z, wo_ref[...])
    f = (x1 * _rms_scale(x1) * g2_ref[...]).astype(BF16)
    d_ff = wg_ref.shape[1]
    step = d_ff // ff_split
    x2 = x1
    for s in range(ff_split):
        cols = slice(s * step, (s + 1) * step)
        gte = _dot(f, wg_ref[:, cols])
        up = _dot(f, wu_ref[:, cols])
        hh = (gte * jax.nn.sigmoid(gte) * up).astype(BF16)
        x2 = x2 + _dot(hh, wd_ref[cols, :])
    out_ref[...] = x2 * _rms_scale(x2) * gf_ref[...]


def _tail(x2d, ao, hm, gg, wab, wmb, wo, g2, wg, wu, wd, gf, tm, ff_split):
    t = x2d.shape[0]
    row = lambda w: pl.BlockSpec((tm, w), lambda i: (i, 0))
    consts = (wab, wmb, wo, g2, wg, wu, wd, gf)
    return pl.pallas_call(
        functools.partial(_tail_kernel, ff_split=ff_split),
        grid=(t // tm,),
        in_specs=[row(D_MODEL), row(ATTN_Q_WIDTH), row(MLSTM_V_WIDTH), row(MERGE_WIDTH)]
                 + [_const_spec(c.shape) for c in consts],
        out_specs=row(D_MODEL),
        out_shape=jax.ShapeDtypeStruct((t, D_MODEL), F32),
        compiler_params=pltpu.CompilerParams(dimension_semantics=("arbitrary",),
                                             vmem_limit_bytes=VMEM_LIMIT_BYTES),
        name="tail",
    )(x2d, ao, hm, gg, *consts)


def _layer(x2d, batch, seq, norm1_g, w_in, conv_w, conv_b, i_bias, f_bias, mlstm_norm_g, attn_sinks,
           w_attn_branch, w_mlstm_branch, w_out, norm2_g, w_ffn_gate, w_ffn_up, w_ffn_down, out_g):
    order = jnp.asarray(Q_HEAD_ORDER)
    o = 0
    wq = w_in[:, o:o + ATTN_Q_WIDTH]; o += ATTN_Q_WIDTH
    wkv = w_in[:, o:o + 2 * ATTN_KV_WIDTH]; o += 2 * ATTN_KV_WIDTH
    wm_qk = w_in[:, o:o + 2 * MLSTM_QK_WIDTH]; o += 2 * MLSTM_QK_WIDTH
    wm_vo = w_in[:, o:o + 2 * MLSTM_V_WIDTH]; o += 2 * MLSTM_V_WIDTH
    w_if = w_in[:, o:o + 2 * MLSTM_HEADS]; o += 2 * MLSTM_HEADS
    wgg = w_in[:, o:o + MERGE_WIDTH]
    wq = wq.reshape(D_MODEL, ATTN_HEADS, ATTN_HEAD_DIM)[:, order, :].reshape(D_MODEL, ATTN_Q_WIDTH)
    wa = jnp.concatenate([wq, wkv], axis=1)
    wab = w_attn_branch.reshape(ATTN_HEADS, ATTN_HEAD_DIM, D_MODEL)[order].reshape(ATTN_Q_WIDTH, D_MODEL)
    wgt = jnp.pad(w_if, ((0, 0), (0, GATE_WIDTH - 2 * MLSTM_HEADS)))
    gbias = jnp.pad(jnp.concatenate([i_bias, f_bias]), (0, GATE_WIDTH - 2 * MLSTM_HEADS)).reshape(1, GATE_WIDTH)

    qkv, mm, gt, gg = _inproj(x2d, norm1_g.reshape(1, D_MODEL), wa.astype(BF16), wm_qk.astype(BF16),
                              wm_vo.astype(BF16), wgt.astype(BF16), wgg.astype(BF16), conv_w,
                              conv_b.reshape(1, -1), tm=512, seq=seq)
    ao, hm = _mixer(attn_sinks, qkv, mm, gt, gbias, mlstm_norm_g.reshape(1, MLSTM_V_WIDTH), batch, seq, nch=4)
    return _tail(x2d, ao, hm, gg, wab.astype(BF16), w_mlstm_branch.astype(BF16), w_out.astype(BF16),
                 norm2_g.reshape(1, D_MODEL), w_ffn_gate.astype(BF16), w_ffn_up.astype(BF16),
                 w_ffn_down.astype(BF16), out_g.reshape(1, D_MODEL), tm=512, ff_split=2)


def kernel(x, norm1_g, w_in, conv_w, conv_b, i_bias, f_bias, mlstm_norm_g, attn_sinks, w_attn_branch,
           w_mlstm_branch, w_out, norm2_g, w_ffn_gate, w_ffn_up, w_ffn_down, final_norm_g):
    batch, seq, d = x.shape
    depth = norm1_g.shape[0]
    assert depth == 1 and d == D_MODEL and seq % (4 * CHUNK) == 0
    out = _layer(x.reshape(batch * seq, d), batch, seq, norm1_g[0], w_in[0], conv_w[0], conv_b[0], i_bias[0],
                 f_bias[0], mlstm_norm_g[0], attn_sinks[0], w_attn_branch[0], w_mlstm_branch[0], w_out[0],
                 norm2_g[0], w_ffn_gate[0], w_ffn_up[0], w_ffn_down[0], final_norm_g)
    return out.reshape(batch, seq, d)
```

```python
import functools

import jax
import jax.numpy as jnp
from jax import lax
from jax.experimental import pallas as pl
from jax.experimental.pallas import tpu as pltpu

D_MODEL = 1024
ATTN_HEADS = 8
ATTN_KV_HEADS = 2
ATTN_HEAD_DIM = 64
ATTN_GROUP = ATTN_HEADS // ATTN_KV_HEADS
WINDOW = 128
ATTN_Q_WIDTH = ATTN_HEADS * ATTN_HEAD_DIM
ATTN_KV_WIDTH = ATTN_KV_HEADS * ATTN_HEAD_DIM
MLSTM_HEADS = 4
MLSTM_QK_DIM = 64
MLSTM_V_DIM = 128
MLSTM_QK_WIDTH = MLSTM_HEADS * MLSTM_QK_DIM
MLSTM_V_WIDTH = MLSTM_HEADS * MLSTM_V_DIM
CHUNK = 128
CONV_WIDTH = 4
NORM_EPS = 1e-6

LANES = 128
HALF = LANES // 2
CONV_PAD = 8
GATE_ROWS = 16
VMEM_LIMIT_BYTES = 56 * 1024 * 1024

QKV_WIDTH = ATTN_Q_WIDTH + 2 * ATTN_KV_WIDTH
MM_WIDTH = 2 * MLSTM_QK_WIDTH + 2 * MLSTM_V_WIDTH
GATE_WIDTH = LANES
MERGE_WIDTH = 2 * D_MODEL

Q_HEAD_ORDER = tuple(h * ATTN_GROUP + g for g in range(ATTN_GROUP) for h in range(ATTN_KV_HEADS))

BF16 = jnp.bfloat16
F32 = jnp.float32
NEG_INF = float("-inf")


def _dot(a, b):
    return jnp.dot(a, b, preferred_element_type=F32)


def _dot_nt(a, b):
    return lax.dot_general(a, b, (((1,), (1,)), ((), ())), preferred_element_type=F32)


def _rms_scale(x):
    return lax.rsqrt(jnp.mean(x * x, axis=-1, keepdims=True) + NORM_EPS)


def _const_spec(shape):
    nd = len(shape)
    return pl.BlockSpec(shape, lambda *_: (0,) * nd, pipeline_mode=pl.Buffered(1))


def _inproj_kernel(x_ref, g_ref, wa_ref, wqk_ref, wvo_ref, wgt_ref, wgg_ref, convw_ref, convb_ref,
                   qkv_ref, mm_ref, gt_ref, gg_ref, conv_ref, *, tiles_per_seq):
    tm = x_ref.shape[0]

    @pl.when(pl.program_id(0) % tiles_per_seq == 0)
    def _():
        conv_ref[0:CONV_PAD, :] = jnp.zeros((CONV_PAD, 2 * MLSTM_QK_WIDTH), F32)

    x = x_ref[...]
    u = (x * _rms_scale(x) * g_ref[...]).astype(BF16)
    q_scale = ATTN_HEAD_DIM ** -0.5
    qkv_ref[:, 0:ATTN_Q_WIDTH] = (_dot(u, wa_ref[:, 0:ATTN_Q_WIDTH]) * q_scale).astype(BF16)
    qkv_ref[:, ATTN_Q_WIDTH:QKV_WIDTH] = _dot(u, wa_ref[:, ATTN_Q_WIDTH:QKV_WIDTH]).astype(BF16)
    mm_ref[:, 2 * MLSTM_QK_WIDTH:MM_WIDTH] = _dot(u, wvo_ref[...]).astype(BF16)
    gt_ref[...] = _dot(u, wgt_ref[...])
    gg_ref[...] = _dot(u, wgg_ref[...]).astype(BF16)

    conv_ref[CONV_PAD:CONV_PAD + tm, :] = _dot(u, wqk_ref[...])
    acc = conv_ref[pl.ds(CONV_PAD - (CONV_WIDTH - 1), tm), :] * convw_ref[0:1, :]
    for t in range(1, CONV_WIDTH):
        acc = acc + conv_ref[pl.ds(CONV_PAD - (CONV_WIDTH - 1) + t, tm), :] * convw_ref[t:t + 1, :]
    acc = acc + convb_ref[...]
    lane = lax.broadcasted_iota(jnp.int32, (1, 2 * MLSTM_QK_WIDTH), 1)
    k_scale = jnp.where(lane < MLSTM_QK_WIDTH, 1.0, MLSTM_QK_DIM ** -0.5)
    mm_ref[:, 0:2 * MLSTM_QK_WIDTH] = (acc * jax.nn.sigmoid(acc) * k_scale).astype(BF16)
    conv_ref[0:CONV_PAD, :] = conv_ref[tm:tm + CONV_PAD, :]


def _inproj(x2d, g1, wa, wqk, wvo, wgt, wgg, convw, convb, tm, seq):
    t = x2d.shape[0]
    row = lambda w: pl.BlockSpec((tm, w), lambda i: (i, 0))
    consts = (g1, wa, wqk, wvo, wgt, wgg, convw, convb)
    return pl.pallas_call(
        functools.partial(_inproj_kernel, tiles_per_seq=seq // tm),
        grid=(t // tm,),
        in_specs=[row(D_MODEL)] + [_const_spec(c.shape) for c in consts],
        out_specs=[row(QKV_WIDTH), row(MM_WIDTH), row(GATE_WIDTH), row(MERGE_WIDTH)],
        out_shape=[jax.ShapeDtypeStruct((t, QKV_WIDTH), BF16), jax.ShapeDtypeStruct((t, MM_WIDTH), BF16),
                   jax.ShapeDtypeStruct((t, GATE_WIDTH), F32), jax.ShapeDtypeStruct((t, MERGE_WIDTH), BF16)],
        scratch_shapes=[pltpu.VMEM((CONV_PAD + tm, 2 * MLSTM_QK_WIDTH), F32)],
        compiler_params=pltpu.CompilerParams(dimension_semantics=("arbitrary",),
                                             vmem_limit_bytes=VMEM_LIMIT_BYTES),
        name="inproj",
    )(x2d, *consts)


def _log_sigmoid(x):
    return jnp.minimum(x, 0.0) - jnp.log1p(jnp.exp(-jnp.abs(x)))


def _split3(x):
    hi = x.astype(BF16)
    r1 = x - hi.astype(F32)
    mid = r1.astype(BF16)
    lo = (r1 - mid.astype(F32)).astype(BF16)
    return hi, mid, lo


def _mixer_kernel(sinks_ref, qkv_ref, mm_ref, gt_ref, gbias_ref, normg_ref,
                  ao_ref, hm_ref,
                  btab_ref, kvp_ref, cst_ref, mst_ref, *, nch):
    b_idx = pl.program_id(0)
    j_idx = pl.program_id(1)

    row_c = lax.broadcasted_iota(jnp.int32, (CHUNK, CHUNK), 0)
    col_c = lax.broadcasted_iota(jnp.int32, (CHUNK, CHUNK), 1)

    @pl.when((b_idx == 0) & (j_idx == 0))
    def _():
        cur = row_c <= col_c
        dist = jnp.where(cur, col_c - row_c, col_c - row_c + WINDOW).astype(F32)
        for j in range(ATTN_HEADS):
            slope = 2.0 ** (-8.0 * (j + 1) / ATTN_HEADS)
            btab_ref[0, j] = -slope * dist
            btab_ref[1, j] = jnp.where(cur, -slope * dist, NEG_INF)

    @pl.when(j_idx == 0)
    def _():
        kvp_ref[...] = jnp.zeros_like(kvp_ref)
        cst_ref[...] = jnp.zeros_like(cst_ref)
        mst_ref[...] = jnp.zeros_like(mst_ref)

    cur_t = row_c <= col_c
    causal = col_c <= row_c
    tril_b = causal.astype(BF16)
    triu_b = cur_t.astype(BF16)
    eye_b = (row_c == col_c).astype(BF16)

    def transpose_bf16(x):
        return _dot_nt(eye_b, x)

    top_rows = row_c < HALF
    top_rows_w = lax.broadcasted_iota(jnp.int32, (CHUNK, 2 * LANES), 0) < HALF
    left_lanes = col_c < HALF
    ones_v = jnp.ones((CHUNK, LANES), BF16)
    zero_b = jnp.zeros((), BF16)
    one_b = jnp.ones((), BF16)
    first = (j_idx == 0).astype(jnp.int32)

    def kv_sides(k_blk, v_blk):
        vt = transpose_bf16(v_blk).astype(BF16)
        k_sides = (jnp.where(left_lanes, k_blk, zero_b), jnp.where(left_lanes, zero_b, k_blk))
        vt_sides = (jnp.where(top_rows, vt, one_b), jnp.where(top_rows, one_b, vt))
        return k_sides, vt_sides

    k_off = ATTN_Q_WIDTH
    v_off = ATTN_Q_WIDTH + ATTN_KV_WIDTH
    prev_sides = kv_sides(kvp_ref[0], kvp_ref[1])
    last = slice((nch - 1) * CHUNK, nch * CHUNK)
    kvp_ref[0] = qkv_ref[last, k_off:k_off + ATTN_KV_WIDTH]
    kvp_ref[1] = qkv_ref[last, v_off:v_off + ATTN_KV_WIDTH]

    m_rows = [mst_ref[h:h + 1, :] for h in range(MLSTM_HEADS)]
    c_pairs = [cst_ref[pr] for pr in range(MLSTM_HEADS // 2)]

    npair = ATTN_GROUP
    chunks = range(nch)
    rows_of = lambda c: slice(c * CHUNK, (c + 1) * CHUNK)
    row8 = lax.broadcasted_iota(jnp.int32, (GATE_ROWS, LANES), 0)
    is_f = (row8 >= MLSTM_HEADS) & (row8 < 2 * MLSTM_HEADS)

    sides = [prev_sides] + [kv_sides(qkv_ref[rows_of(c), k_off:k_off + ATTN_KV_WIDTH],
                                     qkv_ref[rows_of(c), v_off:v_off + ATTN_KV_WIDTH]) for c in chunks]
    gates = []
    for c in chunks:
        rows = rows_of(c)
        g8 = (gt_ref[rows, :] + gbias_ref[...]).T[0:GATE_ROWS, :]
        lf_parts = _split3(jnp.where(is_f, _log_sigmoid(g8), 0.0))
        b8 = sum(_dot(part, triu_b) for part in lf_parts)
        b_bc_all = sum(
            _dot_nt(tril_b, jnp.concatenate(
                [jnp.broadcast_to(part[MLSTM_HEADS + h:MLSTM_HEADS + h + 1, :], (CHUNK, LANES))
                 for h in range(MLSTM_HEADS)], axis=0))
            for part in lf_parts)
        mk = mm_ref[rows, MLSTM_QK_WIDTH:2 * MLSTM_QK_WIDTH]
        kt = transpose_bf16(jnp.concatenate([mk[:, 0:LANES], mk[:, LANES:2 * LANES]], axis=0))
        gates.append((g8, b8, b_bc_all, kt))

    s_t, sc2s, kt_hs = {}, {}, {}
    for c in chunks:
        q = qkv_ref[rows_of(c), 0:ATTN_Q_WIDTH]
        q_pairs = [q[:, p * LANES:(p + 1) * LANES] for p in range(npair)]
        for side in range(ATTN_KV_HEADS):
            k_side = jnp.concatenate([sides[c][0][side], sides[c + 1][0][side]], axis=0)
            for pp in range(npair // 2):
                q2 = jnp.concatenate([q_pairs[2 * pp], q_pairs[2 * pp + 1]], axis=0)
                s2 = _dot_nt(k_side, q2)
                s_t[(c, side, 2 * pp)] = s2[:, 0:LANES]
                s_t[(c, side, 2 * pp + 1)] = s2[:, LANES:2 * LANES]
        kt = gates[c][3]
        for pr in range(MLSTM_HEADS // 2):
            qpair = mm_ref[rows_of(c), pr * LANES:(pr + 1) * LANES]
            kt_pair = kt[:, pr * LANES:(pr + 1) * LANES]
            kt_h = (jnp.where(top_rows, kt_pair, 0.0), jnp.where(top_rows, 0.0, kt_pair))
            kt_hs[(c, pr)] = kt_h
            sc2s[(c, pr)] = _dot(qpair, jnp.concatenate([kt_h[0].astype(BF16), kt_h[1].astype(BF16)], axis=1))

    p_t, mx_rows, intra_w = {}, {}, {}
    for c in chunks:
        variant = first if c == 0 else 0
        for side in range(ATTN_KV_HEADS):
            for p in range(npair):
                j = side * ATTN_GROUP + p
                s2 = s_t[(c, side, p)]
                comb = jnp.where(cur_t, s2[CHUNK:2 * CHUNK], s2[0:CHUNK]) + btab_ref[variant, j]
                mx = jnp.maximum(jnp.max(comb, axis=0, keepdims=True), sinks_ref[j])
                e = jnp.exp(comb - mx)
                p_t[(c, side, p)] = jnp.concatenate([jnp.where(cur_t, 0.0, e), jnp.where(cur_t, e, 0.0)],
                                                    axis=0).astype(BF16)
                mx_rows[(c, side, p)] = mx
        g8, b8, b_bc_all, _ = gates[c]
        for h in range(MLSTM_HEADS):
            i_row = g8[h:h + 1, :]
            b_row = b8[MLSTM_HEADS + h:MLSTM_HEADS + h + 1, :]
            b_bc = b_bc_all[:, h * LANES:(h + 1) * LANES]
            dlog = jnp.where(causal, b_bc + (i_row - b_row), NEG_INF)
            mloc = jnp.broadcast_to(jnp.max(dlog, axis=-1, keepdims=True), (CHUNK, LANES))
            wloc = jnp.exp(dlog - mloc)
            scw = (sc2s[(c, h // 2)][:, (h % 2) * LANES:(h % 2 + 1) * LANES] * wloc).astype(BF16)
            b_last = b_bc[CHUNK - 1:CHUNK, :]
            mloc_last = mloc[CHUNK - 1:CHUNK, :]
            u_row = jnp.exp(dlog[CHUNK - 1:CHUNK, :] - mloc_last)
            kw = (kt_hs[(c, h // 2)][h % 2] * u_row).astype(BF16)
            intra_w[(c, h)] = (b_bc, mloc, scw, b_last, mloc_last, kw)

    o_t, intra = {}, {}
    for c in chunks:
        for side in range(ATTN_KV_HEADS):
            vt_side = jnp.concatenate([sides[c][1][side], sides[c + 1][1][side]], axis=1)
            for pp in range(npair // 2):
                p2 = jnp.concatenate([p_t[(c, side, 2 * pp)], p_t[(c, side, 2 * pp + 1)]], axis=1)
                o2 = _dot(vt_side, p2)
                o_t[(c, side, 2 * pp)] = o2[:, 0:LANES]
                o_t[(c, side, 2 * pp + 1)] = o2[:, LANES:2 * LANES]
        for h in range(MLSTM_HEADS):
            b_bc, mloc, scw, b_last, mloc_last, kw = intra_w[(c, h)]
            v_h = mm_ref[rows_of(c), 2 * MLSTM_QK_WIDTH + h * MLSTM_V_DIM:2 * MLSTM_QK_WIDTH + (h + 1) * MLSTM_V_DIM]
            v_aug = jnp.concatenate([v_h, ones_v], axis=1)
            intra[(c, h)] = (b_bc, mloc, _dot(scw, v_aug), b_last, mloc_last, _dot(kw, v_aug))

    for c in chunks:
        out_t = []
        for p in range(npair):
            scaled = []
            for side in range(ATTN_KV_HEADS):
                j = side * ATTN_GROUP + p
                o = o_t[(c, side, p)]
                sums = o[HALF:HALF + 1, :] if side == 0 else o[0:1, :]
                den = sums + jnp.exp(sinks_ref[j] - mx_rows[(c, side, p)])
                scaled.append(o * (1.0 / den))
            out_t.append(jnp.where(top_rows, scaled[0], scaled[1]).astype(BF16))
        for pp in range(npair // 2):
            both = jnp.concatenate([out_t[2 * pp], out_t[2 * pp + 1]], axis=0)
            ao_ref[rows_of(c), 2 * pp * LANES:(2 * pp + 2) * LANES] = transpose_bf16(both).astype(BF16)

    for c in chunks:
        rows = rows_of(c)
        for pr in range(MLSTM_HEADS // 2):
            c_pair = c_pairs[pr]
            qpair = mm_ref[rows, pr * LANES:(pr + 1) * LANES]
            decs, eus, upds = [], [], []
            for half in range(2):
                h = 2 * pr + half
                b_bc, mloc, num_intra, b_last, mloc_last, upd = intra[(c, h)]
                m_prev = m_rows[h]
                sel_w = top_rows_w if half == 0 else jnp.logical_not(top_rows_w)
                inter = _dot(qpair, jnp.where(sel_w, c_pair, 0.0).astype(BF16))
                a = b_bc + m_prev
                m_t = jnp.maximum(a, mloc)
                e_t = jnp.exp(mloc - m_t)
                w_inter = jnp.exp(a - m_t)
                numv = e_t * num_intra[:, 0:LANES] + w_inter * inter[:, 0:LANES]
                nq = e_t * num_intra[:, LANES:2 * LANES] + w_inter * inter[:, LANES:2 * LANES]
                hv = numv / jnp.maximum(jnp.abs(nq), jnp.exp(-m_t))
                hn = hv * _rms_scale(hv) * normg_ref[:, h * MLSTM_V_DIM:(h + 1) * MLSTM_V_DIM]
                og = mm_ref[rows, 2 * MLSTM_QK_WIDTH + MLSTM_V_WIDTH + h * MLSTM_V_DIM:
                            2 * MLSTM_QK_WIDTH + MLSTM_V_WIDTH + (h + 1) * MLSTM_V_DIM].astype(F32)
                hm_ref[rows, h * MLSTM_V_DIM:(h + 1) * MLSTM_V_DIM] = (hn * jax.nn.sigmoid(og)).astype(BF16)
                m_new = jnp.maximum(b_last + m_prev, mloc_last)
                decs.append(jnp.exp(b_last + m_prev - m_new))
                eus.append(jnp.exp(mloc_last - m_new))
                upds.append(upd)
                m_rows[h] = m_new
            wide = lambda r: jnp.concatenate([r, r], axis=1)
            c_pairs[pr] = (jnp.where(top_rows_w, wide(decs[0]), wide(decs[1])) * c_pair
                           + wide(eus[0]) * upds[0] + wide(eus[1]) * upds[1])

    for h in range(MLSTM_HEADS):
        mst_ref[h:h + 1, :] = m_rows[h]
    for pr in range(MLSTM_HEADS // 2):
        cst_ref[pr] = c_pairs[pr]


def _mixer(sinks, qkv, mm, gt, gbias, normg, batch, seq, nch):
    tm = nch * CHUNK
    nj = seq // tm
    t = batch * seq
    row = lambda w: pl.BlockSpec((tm, w), lambda b, j: (b * nj + j, 0))
    return pl.pallas_call(
        functools.partial(_mixer_kernel, nch=nch),
        grid=(batch, nj),
        in_specs=[pl.BlockSpec(memory_space=pltpu.SMEM),
                  row(QKV_WIDTH), row(MM_WIDTH), row(GATE_WIDTH),
                  _const_spec(gbias.shape), _const_spec(normg.shape)],
        out_specs=[row(ATTN_Q_WIDTH), row(MLSTM_V_WIDTH)],
        out_shape=[jax.ShapeDtypeStruct((t, ATTN_Q_WIDTH), BF16), jax.ShapeDtypeStruct((t, MLSTM_V_WIDTH), BF16)],
        scratch_shapes=[
            pltpu.VMEM((2, ATTN_HEADS, CHUNK, CHUNK), F32),
            pltpu.VMEM((2, CHUNK, ATTN_KV_WIDTH), BF16),
            pltpu.VMEM((MLSTM_HEADS // 2, 2 * MLSTM_QK_DIM, 2 * MLSTM_V_DIM), F32),
            pltpu.VMEM((8, LANES), F32),
        ],
        compiler_params=pltpu.CompilerParams(dimension_semantics=("arbitrary", "arbitrary"),
                                             vmem_limit_bytes=VMEM_LIMIT_BYTES),
        name="mixer",
    )(sinks, qkv, mm, gt, gbias, normg)


def _tail_kernel(x_ref, ao_ref, hm_ref, gg_ref, wab_ref, wmb_ref, wo_ref, g2_ref, wg_ref, wu_ref, wd_ref,
                 gf_ref, out_ref, *, ff_split):
    ya = _dot(ao_ref[...], wab_ref[...])
    ym = _dot(hm_ref[...], wmb_ref[...])
    ga = jax.nn.sigmoid(gg_ref[:, 0:D_MODEL].astype(F32))
    gm = jax.nn.sigmoid(gg_ref[:, D_MODEL:2 * D_MODEL].astype(F32))
    z = (ga * ya + gm * ym).astype(BF16)
    x1 = x_ref[...] + _dot(z, wo_ref[...])
    f = (x1 * _rms_scale(x1) * g2_ref[...]).astype(BF16)
    d_ff = wg_ref.shape[1]
    step = d_ff // ff_split
    x2 = x1
    for s in range(ff_split):
        cols = slice(s * step, (s + 1) * step)
        gte = _dot(f, wg_ref[:, cols])
        up = _dot(f, wu_ref[:, cols])
        hh = (gte * jax.nn.sigmoid(gte) * up).astype(BF16)
        x2 = x2 + _dot(hh, wd_ref[cols, :])
    out_ref[...] = x2 * _rms_scale(x2) * gf_ref[...]


def _tail(x2d, ao, hm, gg, wab, wmb, wo, g2, wg, wu, wd, gf, tm, ff_split):
    t = x2d.shape[0]
    row = lambda w: pl.BlockSpec((tm, w), lambda i: (i, 0))
    consts = (wab, wmb, wo, g2, wg, wu, wd, gf)
    return pl.pallas_call(
        functools.partial(_tail_kernel, ff_split=ff_split),
        grid=(t // tm,),
        in_specs=[row(D_MODEL), row(ATTN_Q_WIDTH), row(MLSTM_V_WIDTH), row(MERGE_WIDTH)]
                 + [_const_spec(c.shape) for c in consts],
        out_specs=row(D_MODEL),
        out_shape=jax.ShapeDtypeStruct((t, D_MODEL), F32),
        compiler_params=pltpu.CompilerParams(dimension_semantics=("arbitrary",),
                                             vmem_limit_bytes=VMEM_LIMIT_BYTES),
        name="tail",
    )(x2d, ao, hm, gg, *consts)


def _layer(x2d, batch, seq, norm1_g, w_in, conv_w, conv_b, i_bias, f_bias, mlstm_norm_g, attn_sinks,
           w_attn_branch, w_mlstm_branch, w_out, norm2_g, w_ffn_gate, w_ffn_up, w_ffn_down, out_g):
    order = jnp.asarray(Q_HEAD_ORDER)
    o = 0
    wq = w_in[:, o:o + ATTN_Q_WIDTH]; o += ATTN_Q_WIDTH
    wkv = w_in[:, o:o + 2 * ATTN_KV_WIDTH]; o += 2 * ATTN_KV_WIDTH
    wm_qk = w_in[:, o:o + 2 * MLSTM_QK_WIDTH]; o += 2 * MLSTM_QK_WIDTH
    wm_vo = w_in[:, o:o + 2 * MLSTM_V_WIDTH]; o += 2 * MLSTM_V_WIDTH
    w_if = w_in[:, o:o + 2 * MLSTM_HEADS]; o += 2 * MLSTM_HEADS
    wgg = w_in[:, o:o + MERGE_WIDTH]
    wq = wq.reshape(D_MODEL, ATTN_HEADS, ATTN_HEAD_DIM)[:, order, :].reshape(D_MODEL, ATTN_Q_WIDTH)
    wa = jnp.concatenate([wq, wkv], axis=1)
    wab = w_attn_branch.reshape(ATTN_HEADS, ATTN_HEAD_DIM, D_MODEL)[order].reshape(ATTN_Q_WIDTH, D_MODEL)
    wgt = jnp.pad(w_if, ((0, 0), (0, GATE_WIDTH - 2 * MLSTM_HEADS)))
    gbias = jnp.pad(jnp.concatenate([i_bias, f_bias]), (0, GATE_WIDTH - 2 * MLSTM_HEADS)).reshape(1, GATE_WIDTH)

    qkv, mm, gt, gg = _inproj(x2d, norm1_g.reshape(1, D_MODEL), wa.astype(BF16), wm_qk.astype(BF16),
                              wm_vo.astype(BF16), wgt.astype(BF16), wgg.astype(BF16), conv_w,
                              conv_b.reshape(1, -1), tm=512, seq=seq)
    ao, hm = _mixer(attn_sinks, qkv, mm, gt, gbias, mlstm_norm_g.reshape(1, MLSTM_V_WIDTH), batch, seq, nch=4)
    return _tail(x2d, ao, hm, gg, wab.astype(BF16), w_mlstm_branch.astype(BF16), w_out.astype(BF16),
                 norm2_g.reshape(1, D_MODEL), w_ffn_gate.astype(BF16), w_ffn_up.astype(BF16),
                 w_ffn_down.astype(BF16), out_g.reshape(1, D_MODEL), tm=512, ff_split=2)


def kernel(x, norm1_g, w_in, conv_w, conv_b, i_bias, f_bias, mlstm_norm_g, attn_sinks, w_attn_branch,
           w_mlstm_branch, w_out, norm2_g, w_ffn_gate, w_ffn_up, w_ffn_down, final_norm_g):
    batch, seq, d = x.shape
    depth = norm1_g.shape[0]
    assert depth == 1 and d == D_MODEL and seq % (4 * CHUNK) == 0
    out = _layer(x.reshape(batch * seq, d), batch, seq, norm1_g[0], w_in[0], conv_w[0], conv_b[0], i_bias[0],
                 f_bias[0], mlstm_norm_g[0], attn_sinks[0], w_attn_branch[0], w_mlstm_branch[0], w_out[0],
                 norm2_g[0], w_ffn_gate[0], w_ffn_up[0], w_ffn_down[0], final_norm_g)
    return out.reshape(batch, seq, d)
```

```python
import functools

import jax
import jax.numpy as jnp
from jax import lax
from jax.experimental import pallas as pl
from jax.experimental.pallas import tpu as pltpu

D_MODEL = 1024
ATTN_HEADS = 8
ATTN_KV_HEADS = 2
ATTN_HEAD_DIM = 64
ATTN_GROUP = ATTN_HEADS // ATTN_KV_HEADS
WINDOW = 128
ATTN_Q_WIDTH = ATTN_HEADS * ATTN_HEAD_DIM
ATTN_KV_WIDTH = ATTN_KV_HEADS * ATTN_HEAD_DIM
MLSTM_HEADS = 4
MLSTM_QK_DIM = 64
MLSTM_V_DIM = 128
MLSTM_QK_WIDTH = MLSTM_HEADS * MLSTM_QK_DIM
MLSTM_V_WIDTH = MLSTM_HEADS * MLSTM_V_DIM
CHUNK = 128
CONV_WIDTH = 4
NORM_EPS = 1e-6

LANES = 128
MXU_WIDTH = 256
HALF = LANES // 2
CONV_PAD = 8
GATE_ROWS = 16
VMEM_LIMIT_BYTES = 56 * 1024 * 1024

QKV_WIDTH = ATTN_Q_WIDTH + 2 * ATTN_KV_WIDTH
MM_WIDTH = 2 * MLSTM_QK_WIDTH + 2 * MLSTM_V_WIDTH
GATE_WIDTH = LANES
MERGE_WIDTH = 2 * D_MODEL

Q_HEAD_ORDER = tuple(h * ATTN_GROUP + g for g in range(ATTN_GROUP) for h in range(ATTN_KV_HEADS))

BF16 = jnp.bfloat16
F32 = jnp.float32
NEG_INF = float("-inf")


def _dot(a, b):
    return jnp.dot(a, b, preferred_element_type=F32)


def _dot_nt(a, b):
    return lax.dot_general(a, b, (((1,), (1,)), ((), ())), preferred_element_type=F32)


def _rms_scale(x):
    return lax.rsqrt(jnp.mean(x * x, axis=-1, keepdims=True) + NORM_EPS)


def _const_spec(shape):
    nd = len(shape)
    return pl.BlockSpec(shape, lambda *_: (0,) * nd, pipeline_mode=pl.Buffered(1))


def _inproj_kernel(x_ref, g_ref, wa_ref, wqk_ref, wvo_ref, wgt_ref, wgg_ref, convw_ref, convb_ref,
                   qkv_ref, mm_ref, gt_ref, gg_ref, conv_ref, *, tiles_per_seq):
    tm = x_ref.shape[0]

    @pl.when(pl.program_id(0) % tiles_per_seq == 0)
    def _():
        conv_ref[0:CONV_PAD, :] = jnp.zeros((CONV_PAD, 2 * MLSTM_QK_WIDTH), F32)

    x = x_ref[...]
    u = (x * _rms_scale(x) * g_ref[...]).astype(BF16)
    q_scale = ATTN_HEAD_DIM ** -0.5
    qkv_ref[:, 0:ATTN_Q_WIDTH] = (_dot(u, wa_ref[:, 0:ATTN_Q_WIDTH]) * q_scale).astype(BF16)
    qkv_ref[:, ATTN_Q_WIDTH:QKV_WIDTH] = _dot(u, wa_ref[:, ATTN_Q_WIDTH:QKV_WIDTH]).astype(BF16)
    mm_ref[:, 2 * MLSTM_QK_WIDTH:MM_WIDTH] = _dot(u, wvo_ref[...]).astype(BF16)
    gt_ref[...] = _dot(u, wgt_ref[...])
    gg_ref[...] = _dot(u, wgg_ref[...]).astype(BF16)

    conv_ref[CONV_PAD:CONV_PAD + tm, :] = _dot(u, wqk_ref[...])
    acc = conv_ref[pl.ds(CONV_PAD - (CONV_WIDTH - 1), tm), :] * convw_ref[0:1, :]
    for t in range(1, CONV_WIDTH):
        acc = acc + conv_ref[pl.ds(CONV_PAD - (CONV_WIDTH - 1) + t, tm), :] * convw_ref[t:t + 1, :]
    acc = acc + convb_ref[...]
    lane = lax.broadcasted_iota(jnp.int32, (1, 2 * MLSTM_QK_WIDTH), 1)
    k_scale = jnp.where(lane < MLSTM_QK_WIDTH, 1.0, MLSTM_QK_DIM ** -0.5)
    mm_ref[:, 0:2 * MLSTM_QK_WIDTH] = (acc * jax.nn.sigmoid(acc) * k_scale).astype(BF16)
    conv_ref[0:CONV_PAD, :] = conv_ref[tm:tm + CONV_PAD, :]


def _inproj(x2d, g1, wa, wqk, wvo, wgt, wgg, convw, convb, tm, seq):
    t = x2d.shape[0]
    row = lambda w: pl.BlockSpec((tm, w), lambda i: (i, 0))
    consts = (g1, wa, wqk, wvo, wgt, wgg, convw, convb)
    return pl.pallas_call(
        functools.partial(_inproj_kernel, tiles_per_seq=seq // tm),
        grid=(t // tm,),
        in_specs=[row(D_MODEL)] + [_const_spec(c.shape) for c in consts],
        out_specs=[row(QKV_WIDTH), row(MM_WIDTH), row(GATE_WIDTH), row(MERGE_WIDTH)],
        out_shape=[jax.ShapeDtypeStruct((t, QKV_WIDTH), BF16), jax.ShapeDtypeStruct((t, MM_WIDTH), BF16),
                   jax.ShapeDtypeStruct((t, GATE_WIDTH), F32), jax.ShapeDtypeStruct((t, MERGE_WIDTH), BF16)],
        scratch_shapes=[pltpu.VMEM((CONV_PAD + tm, 2 * MLSTM_QK_WIDTH), F32)],
        compiler_params=pltpu.CompilerParams(dimension_semantics=("arbitrary",),
                                             vmem_limit_bytes=VMEM_LIMIT_BYTES),
        name="inproj",
    )(x2d, *consts)


def _log_sigmoid(x):
    return jnp.minimum(x, 0.0) - jnp.log1p(jnp.exp(-jnp.abs(x)))


def _split3(x):
    hi = x.astype(BF16)
    r1 = x - hi.astype(F32)
    mid = r1.astype(BF16)
    lo = (r1 - mid.astype(F32)).astype(BF16)
    return hi, mid, lo


def _mixer_kernel(sinks_ref, qkv_ref, mm_ref, gt_ref, gbias_ref, normg_ref,
                  ao_ref, hm_ref,
                  btab_ref, kvp_ref, cst_ref, mst_ref, *, nch):
    b_idx = pl.program_id(0)
    j_idx = pl.program_id(1)

    row_c = lax.broadcasted_iota(jnp.int32, (CHUNK, CHUNK), 0)
    col_c = lax.broadcasted_iota(jnp.int32, (CHUNK, CHUNK), 1)

    @pl.when((b_idx == 0) & (j_idx == 0))
    def _():
        cur = row_c <= col_c
        dist = jnp.where(cur, col_c - row_c, col_c - row_c + WINDOW).astype(F32)
        for j in range(ATTN_HEADS):
            slope = 2.0 ** (-8.0 * (j + 1) / ATTN_HEADS)
            btab_ref[0, j] = -slope * dist
            btab_ref[1, j] = jnp.where(cur, -slope * dist, NEG_INF)

    @pl.when(j_idx == 0)
    def _():
        kvp_ref[...] = jnp.zeros_like(kvp_ref)
        cst_ref[...] = jnp.zeros_like(cst_ref)
        mst_ref[...] = jnp.zeros_like(mst_ref)

    cur_t = row_c <= col_c
    causal = col_c <= row_c
    tril_b = causal.astype(BF16)
    triu_b = cur_t.astype(BF16)
    eye_b = (row_c == col_c).astype(BF16)

    def transpose_bf16(x):
        return _dot_nt(eye_b, x)

    top_rows = row_c < HALF
    top_rows_w = lax.broadcasted_iota(jnp.int32, (CHUNK, 2 * LANES), 0) < HALF
    left_lanes = col_c < HALF
    ones_v = jnp.ones((CHUNK, LANES), BF16)
    zero_b = jnp.zeros((), BF16)
    one_b = jnp.ones((), BF16)
    first = (j_idx == 0).astype(jnp.int32)

    def kv_sides(k_blk, v_blk):
        vt = transpose_bf16(v_blk).astype(BF16)
        k_sides = (jnp.where(left_lanes, k_blk, zero_b), jnp.where(left_lanes, zero_b, k_blk))
        vt_sides = (jnp.where(top_rows, vt, one_b), jnp.where(top_rows, one_b, vt))
        return k_sides, vt_sides

    k_off = ATTN_Q_WIDTH
    v_off = ATTN_Q_WIDTH + ATTN_KV_WIDTH
    prev_sides = kv_sides(kvp_ref[0], kvp_ref[1])
    last = slice((nch - 1) * CHUNK, nch * CHUNK)
    kvp_ref[0] = qkv_ref[last, k_off:k_off + ATTN_KV_WIDTH]
    kvp_ref[1] = qkv_ref[last, v_off:v_off + ATTN_KV_WIDTH]

    m_rows = [mst_ref[h:h + 1, :] for h in range(MLSTM_HEADS)]
    c_pairs = [cst_ref[pr] for pr in range(MLSTM_HEADS // 2)]

    npair = ATTN_GROUP
    chunks = range(nch)
    rows_of = lambda c: slice(c * CHUNK, (c + 1) * CHUNK)
    row8 = lax.broadcasted_iota(jnp.int32, (GATE_ROWS, LANES), 0)
    is_f = (row8 >= MLSTM_HEADS) & (row8 < 2 * MLSTM_HEADS)

    sides = [prev_sides] + [kv_sides(qkv_ref[rows_of(c), k_off:k_off + ATTN_KV_WIDTH],
                                     qkv_ref[rows_of(c), v_off:v_off + ATTN_KV_WIDTH]) for c in chunks]
    gates = []
    for c in chunks:
        rows = rows_of(c)
        g8 = (gt_ref[rows, :] + gbias_ref[...]).T[0:GATE_ROWS, :]
        lf_parts = _split3(jnp.where(is_f, _log_sigmoid(g8), 0.0))
        b8 = sum(_dot(part, triu_b) for part in lf_parts)
        b_bc_all = sum(
            _dot_nt(tril_b, jnp.concatenate(
                [jnp.broadcast_to(part[MLSTM_HEADS + h:MLSTM_HEADS + h + 1, :], (CHUNK, LANES))
                 for h in range(MLSTM_HEADS)], axis=0))
            for part in lf_parts)
        mk = mm_ref[rows, MLSTM_QK_WIDTH:2 * MLSTM_QK_WIDTH]
        kt = transpose_bf16(jnp.concatenate([mk[:, 0:LANES], mk[:, LANES:2 * LANES]], axis=0))
        gates.append((g8, b8, b_bc_all, kt))

    s_t, sc2s, kt_hs = {}, {}, {}
    for c in chunks:
        q = qkv_ref[rows_of(c), 0:ATTN_Q_WIDTH]
        q_pairs = [q[:, p * LANES:(p + 1) * LANES] for p in range(npair)]
        for side in range(ATTN_KV_HEADS):
            k_side = jnp.concatenate([sides[c][0][side], sides[c + 1][0][side]], axis=0)
            for pp in range(npair // 2):
                q2 = jnp.concatenate([q_pairs[2 * pp], q_pairs[2 * pp + 1]], axis=0)
                s2 = _dot_nt(k_side, q2)
                s_t[(c, side, 2 * pp)] = s2[:, 0:LANES]
                s_t[(c, side, 2 * pp + 1)] = s2[:, LANES:2 * LANES]
        kt = gates[c][3]
        for pr in range(MLSTM_HEADS // 2):
            qpair = mm_ref[rows_of(c), pr * LANES:(pr + 1) * LANES]
            kt_pair = kt[:, pr * LANES:(pr + 1) * LANES]
            kt_h = (jnp.where(top_rows, kt_pair, 0.0), jnp.where(top_rows, 0.0, kt_pair))
            kt_hs[(c, pr)] = kt_h
            sc2s[(c, pr)] = _dot(qpair, jnp.concatenate([kt_h[0].astype(BF16), kt_h[1].astype(BF16)], axis=1))

    p_t, mx_rows, intra_w = {}, {}, {}
    for c in chunks:
        variant = first if c == 0 else 0
        for side in range(ATTN_KV_HEADS):
            for p in range(npair):
                j = side * ATTN_GROUP + p
                s2 = s_t[(c, side, p)]
                comb = jnp.where(cur_t, s2[CHUNK:2 * CHUNK], s2[0:CHUNK]) + btab_ref[variant, j]
                mx = jnp.maximum(jnp.max(comb, axis=0, keepdims=True), sinks_ref[j])
                e = jnp.exp(comb - mx)
                p_t[(c, side, p)] = jnp.concatenate([jnp.where(cur_t, 0.0, e), jnp.where(cur_t, e, 0.0)],
                                                    axis=0).astype(BF16)
                mx_rows[(c, side, p)] = mx
        g8, b8, b_bc_all, _ = gates[c]
        for h in range(MLSTM_HEADS):
            i_row = g8[h:h + 1, :]
            b_row = b8[MLSTM_HEADS + h:MLSTM_HEADS + h + 1, :]
            b_bc = b_bc_all[:, h * LANES:(h + 1) * LANES]
            dlog = jnp.where(causal, b_bc + (i_row - b_row), NEG_INF)
            mloc = jnp.broadcast_to(jnp.max(dlog, axis=-1, keepdims=True), (CHUNK, LANES))
            wloc = jnp.exp(dlog - mloc)
            scw = (sc2s[(c, h // 2)][:, (h % 2) * LANES:(h % 2 + 1) * LANES] * wloc).astype(BF16)
            b_last = b_bc[CHUNK - 1:CHUNK, :]
            mloc_last = mloc[CHUNK - 1:CHUNK, :]
            u_row = jnp.exp(dlog[CHUNK - 1:CHUNK, :] - mloc_last)
            kw = (kt_hs[(c, h // 2)][h % 2] * u_row).astype(BF16)
            intra_w[(c, h)] = (b_bc, mloc, scw, b_last, mloc_last, kw)

    o_t, intra = {}, {}
    for c in chunks:
        for side in range(ATTN_KV_HEADS):
            vt_side = jnp.concatenate([sides[c][1][side], sides[c + 1][1][side]], axis=1)
            for pp in range(npair // 2):
                p2 = jnp.concatenate([p_t[(c, side, 2 * pp)], p_t[(c, side, 2 * pp + 1)]], axis=1)
                o2 = _dot(vt_side, p2)
                o_t[(c, side, 2 * pp)] = o2[:, 0:LANES]
                o_t[(c, side, 2 * pp + 1)] = o2[:, LANES:2 * LANES]
        for h in range(MLSTM_HEADS):
            b_bc, mloc, scw, b_last, mloc_last, kw = intra_w[(c, h)]
            v_h = mm_ref[rows_of(c), 2 * MLSTM_QK_WIDTH + h * MLSTM_V_DIM:2 * MLSTM_QK_WIDTH + (h + 1) * MLSTM_V_DIM]
            v_aug = jnp.concatenate([v_h, ones_v], axis=1)
            intra[(c, h)] = (b_bc, mloc, _dot(scw, v_aug), b_last, mloc_last, _dot(kw, v_aug))

    for c in chunks:
        out_t = []
        for p in range(npair):
            scaled = []
            for side in range(ATTN_KV_HEADS):
                j = side * ATTN_GROUP + p
                o = o_t[(c, side, p)]
                sums = o[HALF:HALF + 1, :] if side == 0 else o[0:1, :]
                den = sums + jnp.exp(sinks_ref[j] - mx_rows[(c, side, p)])
                scaled.append(o * (1.0 / den))
            out_t.append(jnp.where(top_rows, scaled[0], scaled[1]).astype(BF16))
        for pp in range(npair // 2):
            both = jnp.concatenate([out_t[2 * pp], out_t[2 * pp + 1]], axis=0)
            ao_ref[rows_of(c), 2 * pp * LANES:(2 * pp + 2) * LANES] = transpose_bf16(both).astype(BF16)

    for c in chunks:
        rows = rows_of(c)
        for pr in range(MLSTM_HEADS // 2):
            c_pair = c_pairs[pr]
            qpair = mm_ref[rows, pr * LANES:(pr + 1) * LANES]
            decs, eus, upds = [], [], []
            for half in range(2):
                h = 2 * pr + half
                b_bc, mloc, num_intra, b_last, mloc_last, upd = intra[(c, h)]
                m_prev = m_rows[h]
                sel_w = top_rows_w if half == 0 else jnp.logical_not(top_rows_w)
                inter = _dot(qpair, jnp.where(sel_w, c_pair, 0.0).astype(BF16))
                a = b_bc + m_prev
                m_t = jnp.maximum(a, mloc)
                e_t = jnp.exp(mloc - m_t)
                w_inter = jnp.exp(a - m_t)
                numv = e_t * num_intra[:, 0:LANES] + w_inter * inter[:, 0:LANES]
                nq = e_t * num_intra[:, LANES:2 * LANES] + w_inter * inter[:, LANES:2 * LANES]
                hv = numv / jnp.maximum(jnp.abs(nq), jnp.exp(-m_t))
                hn = hv * _rms_scale(hv) * normg_ref[:, h * MLSTM_V_DIM:(h + 1) * MLSTM_V_DIM]
                og = mm_ref[rows, 2 * MLSTM_QK_WIDTH + MLSTM_V_WIDTH + h * MLSTM_V_DIM:
                            2 * MLSTM_QK_WIDTH + MLSTM_V_WIDTH + (h + 1) * MLSTM_V_DIM].astype(F32)
                hm_ref[rows, h * MLSTM_V_DIM:(h + 1) * MLSTM_V_DIM] = (hn * jax.nn.sigmoid(og)).astype(BF16)
                m_new = jnp.maximum(b_last + m_prev, mloc_last)
                decs.append(jnp.exp(b_last + m_prev - m_new))
                eus.append(jnp.exp(mloc_last - m_new))
                upds.append(upd)
                m_rows[h] = m_new
            wide = lambda r: jnp.concatenate([r, r], axis=1)
            c_pairs[pr] = (jnp.where(top_rows_w, wide(decs[0]), wide(decs[1])) * c_pair
                           + wide(eus[0]) * upds[0] + wide(eus[1]) * upds[1])

    for h in range(MLSTM_HEADS):
        mst_ref[h:h + 1, :] = m_rows[h]
    for pr in range(MLSTM_HEADS // 2):
        cst_ref[pr] = c_pairs[pr]


def _mixer(sinks, qkv, mm, gt, gbias, normg, batch, seq, nch):
    tm = nch * CHUNK
    nj = seq // tm
    t = batch * seq
    row = lambda w: pl.BlockSpec((tm, w), lambda b, j: (b * nj + j, 0))
    return pl.pallas_call(
        functools.partial(_mixer_kernel, nch=nch),
        grid=(batch, nj),
        in_specs=[pl.BlockSpec(memory_space=pltpu.SMEM),
                  row(QKV_WIDTH), row(MM_WIDTH), row(GATE_WIDTH),
                  _const_spec(gbias.shape), _const_spec(normg.shape)],
        out_specs=[row(ATTN_Q_WIDTH), row(MLSTM_V_WIDTH)],
        out_shape=[jax.ShapeDtypeStruct((t, ATTN_Q_WIDTH), BF16), jax.ShapeDtypeStruct((t, MLSTM_V_WIDTH), BF16)],
        scratch_shapes=[
            pltpu.VMEM((2, ATTN_HEADS, CHUNK, CHUNK), F32),
            pltpu.VMEM((2, CHUNK, ATTN_KV_WIDTH), BF16),
            pltpu.VMEM((MLSTM_HEADS // 2, 2 * MLSTM_QK_DIM, 2 * MLSTM_V_DIM), F32),
            pltpu.VMEM((8, LANES), F32),
        ],
        compiler_params=pltpu.CompilerParams(dimension_semantics=("arbitrary", "arbitrary"),
                                             vmem_limit_bytes=VMEM_LIMIT_BYTES),
        name="mixer",
    )(sinks, qkv, mm, gt, gbias, normg)


def _tail_kernel(x_ref, ao_ref, hm_ref, gg_ref, wab_ref, wmb_ref, wo_ref, g2_ref, wg_ref, wu_ref, wd_ref,
                 gf_ref, out_ref, *, ff_split):
    ya = _dot(ao_ref[...], wab_ref[...])
    ym = _dot(hm_ref[...], wmb_ref[...])
    ga = jax.nn.sigmoid(gg_ref[:, 0:D_MODEL].astype(F32))
    gm = jax.nn.sigmoid(gg_ref[:, D_MODEL:2 * D_MODEL].astype(F32))
    z = (ga * ya + gm * ym).astype(BF16)
    x1 = x_ref[...] + _dot(z, wo_ref[...])
    f = (x1 * _rms_scale(x1) * g2_ref[...]).astype(BF16)
    n_tiles = wg_ref.shape[1] // MXU_WIDTH
    bounds = [MXU_WIDTH * ((n_tiles * s + ff_split - 1) // ff_split) for s in range(ff_split + 1)]
    x2 = x1
    for s in range(ff_split):
        cols = slice(bounds[s], bounds[s + 1])
        gte = _dot(f, wg_ref[:, cols])
        up = _dot(f, wu_ref[:, cols])
        hh = (gte * jax.nn.sigmoid(gte) * up).astype(BF16)
        x2 = x2 + _dot(hh, wd_ref[cols, :])
    out_ref[...] = x2 * _rms_scale(x2) * gf_ref[...]


def _tail(x2d, ao, hm, gg, wab, wmb, wo, g2, wg, wu, wd, gf, tm, ff_split):
    t = x2d.shape[0]
    row = lambda w: pl.BlockSpec((tm, w), lambda i: (i, 0))
    consts = (wab, wmb, wo, g2, wg, wu, wd, gf)
    return pl.pallas_call(
        functools.partial(_tail_kernel, ff_split=ff_split),
        grid=(t // tm,),
        in_specs=[row(D_MODEL), row(ATTN_Q_WIDTH), row(MLSTM_V_WIDTH), row(MERGE_WIDTH)]
                 + [_const_spec(c.shape) for c in consts],
        out_specs=row(D_MODEL),
        out_shape=jax.ShapeDtypeStruct((t, D_MODEL), F32),
        compiler_params=pltpu.CompilerParams(dimension_semantics=("arbitrary",),
                                             vmem_limit_bytes=VMEM_LIMIT_BYTES),
        name="tail",
    )(x2d, ao, hm, gg, *consts)


def _layer(x2d, batch, seq, norm1_g, w_in, conv_w, conv_b, i_bias, f_bias, mlstm_norm_g, attn_sinks,
           w_attn_branch, w_mlstm_branch, w_out, norm2_g, w_ffn_gate, w_ffn_up, w_ffn_down, out_g):
    order = jnp.asarray(Q_HEAD_ORDER)
    o = 0
    wq = w_in[:, o:o + ATTN_Q_WIDTH]; o += ATTN_Q_WIDTH
    wkv = w_in[:, o:o + 2 * ATTN_KV_WIDTH]; o += 2 * ATTN_KV_WIDTH
    wm_qk = w_in[:, o:o + 2 * MLSTM_QK_WIDTH]; o += 2 * MLSTM_QK_WIDTH
    wm_vo = w_in[:, o:o + 2 * MLSTM_V_WIDTH]; o += 2 * MLSTM_V_WIDTH
    w_if = w_in[:, o:o + 2 * MLSTM_HEADS]; o += 2 * MLSTM_HEADS
    wgg = w_in[:, o:o + MERGE_WIDTH]
    wq = wq.reshape(D_MODEL, ATTN_HEADS, ATTN_HEAD_DIM)[:, order, :].reshape(D_MODEL, ATTN_Q_WIDTH)
    wa = jnp.concatenate([wq, wkv], axis=1)
    wab = w_attn_branch.reshape(ATTN_HEADS, ATTN_HEAD_DIM, D_MODEL)[order].reshape(ATTN_Q_WIDTH, D_MODEL)
    wgt = jnp.pad(w_if, ((0, 0), (0, GATE_WIDTH - 2 * MLSTM_HEADS)))
    gbias = jnp.pad(jnp.concatenate([i_bias, f_bias]), (0, GATE_WIDTH - 2 * MLSTM_HEADS)).reshape(1, GATE_WIDTH)

    qkv, mm, gt, gg = _inproj(x2d, norm1_g.reshape(1, D_MODEL), wa.astype(BF16), wm_qk.astype(BF16),
                              wm_vo.astype(BF16), wgt.astype(BF16), wgg.astype(BF16), conv_w,
                              conv_b.reshape(1, -1), tm=512, seq=seq)
    ao, hm = _mixer(attn_sinks, qkv, mm, gt, gbias, mlstm_norm_g.reshape(1, MLSTM_V_WIDTH), batch, seq, nch=4)
    return _tail(x2d, ao, hm, gg, wab.astype(BF16), w_mlstm_branch.astype(BF16), w_out.astype(BF16),
                 norm2_g.reshape(1, D_MODEL), w_ffn_gate.astype(BF16), w_ffn_up.astype(BF16),
                 w_ffn_down.astype(BF16), out_g.reshape(1, D_MODEL), tm=512, ff_split=2)


def kernel(x, norm1_g, w_in, conv_w, conv_b, i_bias, f_bias, mlstm_norm_g, attn_sinks, w_attn_branch,
           w_mlstm_branch, w_out, norm2_g, w_ffn_gate, w_ffn_up, w_ffn_down, final_norm_g):
    batch, seq, d = x.shape
    depth = norm1_g.shape[0]
    assert depth == 1 and d == D_MODEL and seq % (4 * CHUNK) == 0
    assert w_ffn_gate.shape[-1] % MXU_WIDTH == 0
    out = _layer(x.reshape(batch * seq, d), batch, seq, norm1_g[0], w_in[0], conv_w[0], conv_b[0], i_bias[0],
                 f_bias[0], mlstm_norm_g[0], attn_sinks[0], w_attn_branch[0], w_mlstm_branch[0], w_out[0],
                 norm2_g[0], w_ffn_gate[0], w_ffn_up[0], w_ffn_down[0], final_norm_g)
    return out.reshape(batch, seq, d)
```

```python
import functools

import jax
import jax.numpy as jnp
from jax import lax
from jax.experimental import pallas as pl
from jax.experimental.pallas import tpu as pltpu

D_MODEL = 1024
ATTN_HEADS = 8
ATTN_KV_HEADS = 2
ATTN_HEAD_DIM = 64
ATTN_GROUP = ATTN_HEADS // ATTN_KV_HEADS
WINDOW = 128
ATTN_Q_WIDTH = ATTN_HEADS * ATTN_HEAD_DIM
ATTN_KV_WIDTH = ATTN_KV_HEADS * ATTN_HEAD_DIM
MLSTM_HEADS = 4
MLSTM_QK_DIM = 64
MLSTM_V_DIM = 128
MLSTM_QK_WIDTH = MLSTM_HEADS * MLSTM_QK_DIM
MLSTM_V_WIDTH = MLSTM_HEADS * MLSTM_V_DIM
CHUNK = 128
CONV_WIDTH = 4
NORM_EPS = 1e-6

LANES = 128
MXU_WIDTH = 256
HALF = LANES // 2
CONV_PAD = 8
GATE_ROWS = 16

INPROJ_TM = 512
INPROJ_ROW_BLOCKS = 2
MIXER_CHUNKS = 8
MIXER_STAGE_LAGS = (0, 1, 2, 3, 4, 4)
TAIL_TM = 512
TAIL_FF_SPLIT = 2
VMEM_LIMIT_BYTES = 56 * 1024 * 1024

QKV_WIDTH = ATTN_Q_WIDTH + 2 * ATTN_KV_WIDTH
MM_WIDTH = 2 * MLSTM_QK_WIDTH + 2 * MLSTM_V_WIDTH
GATE_WIDTH = LANES
MERGE_WIDTH = 2 * D_MODEL

Q_HEAD_ORDER = tuple(h * ATTN_GROUP + g for g in range(ATTN_GROUP) for h in range(ATTN_KV_HEADS))

BF16 = jnp.bfloat16
F32 = jnp.float32
NEG_INF = float("-inf")


def _dot(a, b):
    return jnp.dot(a, b, preferred_element_type=F32)


def _dot_nt(a, b):
    return lax.dot_general(a, b, (((1,), (1,)), ((), ())), preferred_element_type=F32)


def _rms_scale(x):
    return lax.rsqrt(jnp.mean(x * x, axis=-1, keepdims=True) + NORM_EPS)


def _const_spec(shape):
    nd = len(shape)
    return pl.BlockSpec(shape, lambda *_: (0,) * nd, pipeline_mode=pl.Buffered(1))


def _inproj_kernel(x_ref, g_ref, wa_ref, wqk_ref, wvo_ref, wgt_ref, wgg_ref, convw_ref, convb_ref,
                   qkv_ref, mm_ref, gt_ref, gg_ref, conv_ref, *, tiles_per_seq):
    tm = x_ref.shape[0]

    @pl.when(pl.program_id(0) % tiles_per_seq == 0)
    def _():
        conv_ref[0:CONV_PAD, :] = jnp.zeros((CONV_PAD, 2 * MLSTM_QK_WIDTH), F32)

    q_scale = ATTN_HEAD_DIM ** -0.5
    lane = lax.broadcasted_iota(jnp.int32, (1, 2 * MLSTM_QK_WIDTH), 1)
    k_scale = jnp.where(lane < MLSTM_QK_WIDTH, 1.0, MLSTM_QK_DIM ** -0.5)
    rb = tm // INPROJ_ROW_BLOCKS
    for r in range(INPROJ_ROW_BLOCKS):
        rows = slice(r * rb, (r + 1) * rb)
        x = x_ref[rows, :]
        u = (x * _rms_scale(x) * g_ref[...]).astype(BF16)
        conv_ref[CONV_PAD + r * rb:CONV_PAD + (r + 1) * rb, :] = _dot(u, wqk_ref[...])
        qkv_ref[rows, 0:ATTN_Q_WIDTH] = (_dot(u, wa_ref[:, 0:ATTN_Q_WIDTH]) * q_scale).astype(BF16)
        qkv_ref[rows, ATTN_Q_WIDTH:QKV_WIDTH] = _dot(u, wa_ref[:, ATTN_Q_WIDTH:QKV_WIDTH]).astype(BF16)
        mm_ref[rows, 2 * MLSTM_QK_WIDTH:MM_WIDTH] = _dot(u, wvo_ref[...]).astype(BF16)
        gt_ref[rows, :] = _dot(u, wgt_ref[...])
        gg_ref[rows, :] = _dot(u, wgg_ref[...]).astype(BF16)
        xe = conv_ref[r * rb:r * rb + CONV_PAD + rb, :]
        acc = xe * convw_ref[0:1, :]
        for t in range(1, CONV_WIDTH):
            acc = pltpu.roll(acc, 1, axis=0) + xe * convw_ref[t:t + 1, :]
        acc = acc[CONV_PAD:CONV_PAD + rb, :] + convb_ref[...]
        mm_ref[rows, 0:2 * MLSTM_QK_WIDTH] = (acc * jax.nn.sigmoid(acc) * k_scale).astype(BF16)
    conv_ref[0:CONV_PAD, :] = conv_ref[tm:tm + CONV_PAD, :]


def _inproj(x2d, g1, wa, wqk, wvo, wgt, wgg, convw, convb, tm, seq):
    t = x2d.shape[0]
    row = lambda w: pl.BlockSpec((tm, w), lambda i: (i, 0))
    consts = (g1, wa, wqk, wvo, wgt, wgg, convw, convb)
    return pl.pallas_call(
        functools.partial(_inproj_kernel, tiles_per_seq=seq // tm),
        grid=(t // tm,),
        in_specs=[row(D_MODEL)] + [_const_spec(c.shape) for c in consts],
        out_specs=[row(QKV_WIDTH), row(MM_WIDTH), row(GATE_WIDTH), row(MERGE_WIDTH)],
        out_shape=[jax.ShapeDtypeStruct((t, QKV_WIDTH), BF16), jax.ShapeDtypeStruct((t, MM_WIDTH), BF16),
                   jax.ShapeDtypeStruct((t, GATE_WIDTH), F32), jax.ShapeDtypeStruct((t, MERGE_WIDTH), BF16)],
        scratch_shapes=[pltpu.VMEM((CONV_PAD + tm, 2 * MLSTM_QK_WIDTH), F32)],
        compiler_params=pltpu.CompilerParams(dimension_semantics=("arbitrary",),
                                             vmem_limit_bytes=VMEM_LIMIT_BYTES),
        name="inproj",
    )(x2d, *consts)


def _log_sigmoid(x):
    return jnp.minimum(x, 0.0) - jnp.log1p(jnp.exp(-jnp.abs(x)))


def _split3(x):
    hi = x.astype(BF16)
    r1 = x - hi.astype(F32)
    mid = r1.astype(BF16)
    lo = (r1 - mid.astype(F32)).astype(BF16)
    return hi, mid, lo


def _mixer_kernel(sinks_ref, qkv_ref, mm_ref, gt_ref, gbias_ref, normg_ref,
                  ao_ref, hm_ref,
                  btab_ref, kvp_ref, cst_ref, mst_ref, *, nch):
    b_idx = pl.program_id(0)
    j_idx = pl.program_id(1)

    row_c = lax.broadcasted_iota(jnp.int32, (CHUNK, CHUNK), 0)
    col_c = lax.broadcasted_iota(jnp.int32, (CHUNK, CHUNK), 1)

    @pl.when((b_idx == 0) & (j_idx == 0))
    def _():
        cur = row_c <= col_c
        dist = jnp.where(cur, col_c - row_c, col_c - row_c + WINDOW).astype(F32)
        for j in range(ATTN_HEADS):
            slope = 2.0 ** (-8.0 * (j + 1) / ATTN_HEADS)
            btab_ref[0, j] = -slope * dist
            btab_ref[1, j] = jnp.where(cur, -slope * dist, NEG_INF)

    @pl.when(j_idx == 0)
    def _():
        kvp_ref[...] = jnp.zeros_like(kvp_ref)
        cst_ref[...] = jnp.zeros_like(cst_ref)
        mst_ref[...] = jnp.zeros_like(mst_ref)

    cur_t = row_c <= col_c
    causal = col_c <= row_c
    tril_b = causal.astype(BF16)
    triu_b = cur_t.astype(BF16)
    eye_b = (row_c == col_c).astype(BF16)

    def transpose_bf16(x):
        return _dot_nt(eye_b, x)

    top_rows = row_c < HALF
    top_rows_w = lax.broadcasted_iota(jnp.int32, (CHUNK, 2 * LANES), 0) < HALF
    left_lanes = col_c < HALF
    ones_v = jnp.ones((CHUNK, LANES), BF16)
    zero_b = jnp.zeros((), BF16)
    one_b = jnp.ones((), BF16)
    first = (j_idx == 0).astype(jnp.int32)

    def kv_sides(k_blk, v_blk):
        vt = transpose_bf16(v_blk).astype(BF16)
        k_sides = (jnp.where(left_lanes, k_blk, zero_b), jnp.where(left_lanes, zero_b, k_blk))
        vt_sides = (jnp.where(top_rows, vt, one_b), jnp.where(top_rows, one_b, vt))
        return k_sides, vt_sides

    k_off = ATTN_Q_WIDTH
    v_off = ATTN_Q_WIDTH + ATTN_KV_WIDTH
    prev_sides = kv_sides(kvp_ref[0], kvp_ref[1])
    last = slice((nch - 1) * CHUNK, nch * CHUNK)
    kvp_ref[0] = qkv_ref[last, k_off:k_off + ATTN_KV_WIDTH]
    kvp_ref[1] = qkv_ref[last, v_off:v_off + ATTN_KV_WIDTH]

    m_rows = [mst_ref[h:h + 1, :] for h in range(MLSTM_HEADS)]
    c_pairs = [cst_ref[pr] for pr in range(MLSTM_HEADS // 2)]

    npair = ATTN_GROUP
    rows_of = lambda c: slice(c * CHUNK, (c + 1) * CHUNK)
    row8 = lax.broadcasted_iota(jnp.int32, (GATE_ROWS, LANES), 0)
    is_f = (row8 >= MLSTM_HEADS) & (row8 < 2 * MLSTM_HEADS)
    sides = {-1: prev_sides}
    gates, s_t, sc2s, kt_hs, p_t, mx_rows, intra_w, o_t, intra = {}, {}, {}, {}, {}, {}, {}, {}, {}

    def stage_prepare(c):
        rows = rows_of(c)
        sides[c] = kv_sides(qkv_ref[rows, k_off:k_off + ATTN_KV_WIDTH], qkv_ref[rows, v_off:v_off + ATTN_KV_WIDTH])
        g8 = (gt_ref[rows, :] + gbias_ref[...]).T[0:GATE_ROWS, :]
        lf_parts = _split3(jnp.where(is_f, _log_sigmoid(g8), 0.0))
        b8 = sum(_dot(part, triu_b) for part in lf_parts)
        b_bc_all = sum(
            _dot_nt(tril_b, jnp.concatenate(
                [jnp.broadcast_to(part[MLSTM_HEADS + h:MLSTM_HEADS + h + 1, :], (CHUNK, LANES))
                 for h in range(MLSTM_HEADS)], axis=0))
            for part in lf_parts)
        mk = mm_ref[rows, MLSTM_QK_WIDTH:2 * MLSTM_QK_WIDTH]
        kt = transpose_bf16(jnp.concatenate([mk[:, 0:LANES], mk[:, LANES:2 * LANES]], axis=0))
        gates[c] = (g8, b8, b_bc_all, kt)

    def stage_scores(c):
        q = qkv_ref[rows_of(c), 0:ATTN_Q_WIDTH]
        q_pairs = [q[:, p * LANES:(p + 1) * LANES] for p in range(npair)]
        for side in range(ATTN_KV_HEADS):
            k_side = jnp.concatenate([sides[c - 1][0][side], sides[c][0][side]], axis=0)
            for pp in range(npair // 2):
                q2 = jnp.concatenate([q_pairs[2 * pp], q_pairs[2 * pp + 1]], axis=0)
                s2 = _dot_nt(k_side, q2)
                s_t[(c, side, 2 * pp)] = s2[:, 0:LANES]
                s_t[(c, side, 2 * pp + 1)] = s2[:, LANES:2 * LANES]
        kt = gates[c][3]
        for pr in range(MLSTM_HEADS // 2):
            qpair = mm_ref[rows_of(c), pr * LANES:(pr + 1) * LANES]
            kt_pair = kt[:, pr * LANES:(pr + 1) * LANES]
            kt_h = (jnp.where(top_rows, kt_pair, 0.0), jnp.where(top_rows, 0.0, kt_pair))
            kt_hs[(c, pr)] = kt_h
            sc2s[(c, pr)] = _dot(qpair, jnp.concatenate([kt_h[0].astype(BF16), kt_h[1].astype(BF16)], axis=1))

    def stage_weights(c):
        variant = first if c == 0 else 0
        for side in range(ATTN_KV_HEADS):
            for p in range(npair):
                j = side * ATTN_GROUP + p
                s2 = s_t.pop((c, side, p))
                comb = jnp.where(cur_t, s2[CHUNK:2 * CHUNK], s2[0:CHUNK]) + btab_ref[variant, j]
                mx = jnp.maximum(jnp.max(comb, axis=0, keepdims=True), sinks_ref[j])
                e = jnp.exp(comb - mx)
                p_t[(c, side, p)] = jnp.concatenate([jnp.where(cur_t, 0.0, e), jnp.where(cur_t, e, 0.0)],
                                                    axis=0).astype(BF16)
                mx_rows[(c, side, p)] = mx
        g8, b8, b_bc_all, _ = gates[c]
        for h in range(MLSTM_HEADS):
            i_row = g8[h:h + 1, :]
            b_row = b8[MLSTM_HEADS + h:MLSTM_HEADS + h + 1, :]
            b_bc = b_bc_all[:, h * LANES:(h + 1) * LANES]
            dlog = jnp.where(causal, b_bc + (i_row - b_row), NEG_INF)
            mloc = jnp.broadcast_to(jnp.max(dlog, axis=-1, keepdims=True), (CHUNK, LANES))
            wloc = jnp.exp(dlog - mloc)
            scw = (sc2s[(c, h // 2)][:, (h % 2) * LANES:(h % 2 + 1) * LANES] * wloc).astype(BF16)
            b_last = b_bc[CHUNK - 1:CHUNK, :]
            mloc_last = mloc[CHUNK - 1:CHUNK, :]
            u_row = jnp.exp(dlog[CHUNK - 1:CHUNK, :] - mloc_last)
            kw = (kt_hs[(c, h // 2)][h % 2] * u_row).astype(BF16)
            intra_w[(c, h)] = (b_bc, mloc, scw, b_last, mloc_last, kw)

    def stage_values(c):
        for side in range(ATTN_KV_HEADS):
            vt_side = jnp.concatenate([sides[c - 1][1][side], sides[c][1][side]], axis=1)
            for pp in range(npair // 2):
                p2 = jnp.concatenate([p_t.pop((c, side, 2 * pp)), p_t.pop((c, side, 2 * pp + 1))], axis=1)
                o2 = _dot(vt_side, p2)
                o_t[(c, side, 2 * pp)] = o2[:, 0:LANES]
                o_t[(c, side, 2 * pp + 1)] = o2[:, LANES:2 * LANES]
        for h in range(MLSTM_HEADS):
            b_bc, mloc, scw, b_last, mloc_last, kw = intra_w.pop((c, h))
            v_h = mm_ref[rows_of(c), 2 * MLSTM_QK_WIDTH + h * MLSTM_V_DIM:2 * MLSTM_QK_WIDTH + (h + 1) * MLSTM_V_DIM]
            v_aug = jnp.concatenate([v_h, ones_v], axis=1)
            intra[(c, h)] = (b_bc, mloc, _dot(scw, v_aug), b_last, mloc_last, _dot(kw, v_aug))

    def stage_attn_out(c):
        out_t = []
        for p in range(npair):
            scaled = []
            for side in range(ATTN_KV_HEADS):
                j = side * ATTN_GROUP + p
                o = o_t.pop((c, side, p))
                sums = o[HALF:HALF + 1, :] if side == 0 else o[0:1, :]
                den = sums + jnp.exp(sinks_ref[j] - mx_rows.pop((c, side, p)))
                scaled.append(o * (1.0 / den))
            out_t.append(jnp.where(top_rows, scaled[0], scaled[1]).astype(BF16))
        for pp in range(npair // 2):
            both = jnp.concatenate([out_t[2 * pp], out_t[2 * pp + 1]], axis=0)
            ao_ref[rows_of(c), 2 * pp * LANES:(2 * pp + 2) * LANES] = transpose_bf16(both).astype(BF16)

    def stage_recurrence(c):
        rows = rows_of(c)
        for pr in range(MLSTM_HEADS // 2):
            c_pair = c_pairs[pr]
            qpair = mm_ref[rows, pr * LANES:(pr + 1) * LANES]
            decs, eus, upds = [], [], []
            for half in range(2):
                h = 2 * pr + half
                b_bc, mloc, num_intra, b_last, mloc_last, upd = intra.pop((c, h))
                m_prev = m_rows[h]
                sel_w = top_rows_w if half == 0 else jnp.logical_not(top_rows_w)
                inter = _dot(qpair, jnp.where(sel_w, c_pair, 0.0).astype(BF16))
                a = b_bc + m_prev
                m_t = jnp.maximum(a, mloc)
                e_t = jnp.exp(mloc - m_t)
                w_inter = jnp.exp(a - m_t)
                numv = e_t * num_intra[:, 0:LANES] + w_inter * inter[:, 0:LANES]
                nq = e_t * num_intra[:, LANES:2 * LANES] + w_inter * inter[:, LANES:2 * LANES]
                hv = numv / jnp.maximum(jnp.abs(nq), jnp.exp(-m_t))
                hn = hv * _rms_scale(hv) * normg_ref[:, h * MLSTM_V_DIM:(h + 1) * MLSTM_V_DIM]
                og = mm_ref[rows, 2 * MLSTM_QK_WIDTH + MLSTM_V_WIDTH + h * MLSTM_V_DIM:
                            2 * MLSTM_QK_WIDTH + MLSTM_V_WIDTH + (h + 1) * MLSTM_V_DIM].astype(F32)
                hm_ref[rows, h * MLSTM_V_DIM:(h + 1) * MLSTM_V_DIM] = (hn * jax.nn.sigmoid(og)).astype(BF16)
                m_new = jnp.maximum(b_last + m_prev, mloc_last)
                decs.append(jnp.exp(b_last + m_prev - m_new))
                eus.append(jnp.exp(mloc_last - m_new))
                upds.append(upd)
                m_rows[h] = m_new
            wide = lambda r: jnp.concatenate([r, r], axis=1)
            c_pairs[pr] = (jnp.where(top_rows_w, wide(decs[0]), wide(decs[1])) * c_pair
                           + wide(eus[0]) * upds[0] + wide(eus[1]) * upds[1])

    stages = (stage_prepare, stage_scores, stage_weights, stage_values, stage_attn_out, stage_recurrence)
    for tick in range(nch + max(MIXER_STAGE_LAGS)):
        for stage, lag in zip(stages, MIXER_STAGE_LAGS):
            c = tick - lag
            if 0 <= c < nch:
                stage(c)

    for h in range(MLSTM_HEADS):
        mst_ref[h:h + 1, :] = m_rows[h]
    for pr in range(MLSTM_HEADS // 2):
        cst_ref[pr] = c_pairs[pr]


def _mixer(sinks, qkv, mm, gt, gbias, normg, batch, seq, nch):
    tm = nch * CHUNK
    nj = seq // tm
    t = batch * seq
    row = lambda w: pl.BlockSpec((tm, w), lambda b, j: (b * nj + j, 0))
    return pl.pallas_call(
        functools.partial(_mixer_kernel, nch=nch),
        grid=(batch, nj),
        in_specs=[pl.BlockSpec(memory_space=pltpu.SMEM),
                  row(QKV_WIDTH), row(MM_WIDTH), row(GATE_WIDTH),
                  _const_spec(gbias.shape), _const_spec(normg.shape)],
        out_specs=[row(ATTN_Q_WIDTH), row(MLSTM_V_WIDTH)],
        out_shape=[jax.ShapeDtypeStruct((t, ATTN_Q_WIDTH), BF16), jax.ShapeDtypeStruct((t, MLSTM_V_WIDTH), BF16)],
        scratch_shapes=[
            pltpu.VMEM((2, ATTN_HEADS, CHUNK, CHUNK), F32),
            pltpu.VMEM((2, CHUNK, ATTN_KV_WIDTH), BF16),
            pltpu.VMEM((MLSTM_HEADS // 2, 2 * MLSTM_QK_DIM, 2 * MLSTM_V_DIM), F32),
            pltpu.VMEM((8, LANES), F32),
        ],
        compiler_params=pltpu.CompilerParams(dimension_semantics=("arbitrary", "arbitrary"),
                                             vmem_limit_bytes=VMEM_LIMIT_BYTES),
        name="mixer",
    )(sinks, qkv, mm, gt, gbias, normg)


def _tail_kernel(x_ref, ao_ref, hm_ref, gg_ref, wab_ref, wmb_ref, wo_ref, g2_ref, wg_ref, wu_ref, wd_ref,
                 gf_ref, out_ref, *, ff_split):
    ya = _dot(ao_ref[...], wab_ref[...])
    ym = _dot(hm_ref[...], wmb_ref[...])
    ga = jax.nn.sigmoid(gg_ref[:, 0:D_MODEL].astype(F32))
    gm = jax.nn.sigmoid(gg_ref[:, D_MODEL:2 * D_MODEL].astype(F32))
    z = (ga * ya + gm * ym).astype(BF16)
    x1 = x_ref[...] + _dot(z, wo_ref[...])
    f = (x1 * _rms_scale(x1) * g2_ref[...]).astype(BF16)
    n_tiles = wg_ref.shape[1] // MXU_WIDTH
    bounds = [MXU_WIDTH * ((n_tiles * s + ff_split - 1) // ff_split) for s in range(ff_split + 1)]
    x2 = x1
    for s in range(ff_split):
        cols = slice(bounds[s], bounds[s + 1])
        gte = _dot(f, wg_ref[:, cols])
        up = _dot(f, wu_ref[:, cols])
        hh = (gte * jax.nn.sigmoid(gte) * up).astype(BF16)
        x2 = x2 + _dot(hh, wd_ref[cols, :])
    out_ref[...] = x2 * _rms_scale(x2) * gf_ref[...]


def _tail(x2d, ao, hm, gg, wab, wmb, wo, g2, wg, wu, wd, gf, tm, ff_split):
    t = x2d.shape[0]
    row = lambda w: pl.BlockSpec((tm, w), lambda i: (i, 0))
    consts = (wab, wmb, wo, g2, wg, wu, wd, gf)
    return pl.pallas_call(
        functools.partial(_tail_kernel, ff_split=ff_split),
        grid=(t // tm,),
        in_specs=[row(D_MODEL), row(ATTN_Q_WIDTH), row(MLSTM_V_WIDTH), row(MERGE_WIDTH)]
                 + [_const_spec(c.shape) for c in consts],
        out_specs=row(D_MODEL),
        out_shape=jax.ShapeDtypeStruct((t, D_MODEL), F32),
        compiler_params=pltpu.CompilerParams(dimension_semantics=("arbitrary",),
                                             vmem_limit_bytes=VMEM_LIMIT_BYTES),
        name="tail",
    )(x2d, ao, hm, gg, *consts)


def _layer(x2d, batch, seq, norm1_g, w_in, conv_w, conv_b, i_bias, f_bias, mlstm_norm_g, attn_sinks,
           w_attn_branch, w_mlstm_branch, w_out, norm2_g, w_ffn_gate, w_ffn_up, w_ffn_down, out_g):
    order = jnp.asarray(Q_HEAD_ORDER)
    o = 0
    wq = w_in[:, o:o + ATTN_Q_WIDTH]; o += ATTN_Q_WIDTH
    wkv = w_in[:, o:o + 2 * ATTN_KV_WIDTH]; o += 2 * ATTN_KV_WIDTH
    wm_qk = w_in[:, o:o + 2 * MLSTM_QK_WIDTH]; o += 2 * MLSTM_QK_WIDTH
    wm_vo = w_in[:, o:o + 2 * MLSTM_V_WIDTH]; o += 2 * MLSTM_V_WIDTH
    w_if = w_in[:, o:o + 2 * MLSTM_HEADS]; o += 2 * MLSTM_HEADS
    wgg = w_in[:, o:o + MERGE_WIDTH]
    wq = wq.reshape(D_MODEL, ATTN_HEADS, ATTN_HEAD_DIM)[:, order, :].reshape(D_MODEL, ATTN_Q_WIDTH)
    wa = jnp.concatenate([wq, wkv], axis=1)
    wab = w_attn_branch.reshape(ATTN_HEADS, ATTN_HEAD_DIM, D_MODEL)[order].reshape(ATTN_Q_WIDTH, D_MODEL)
    wgt = jnp.pad(w_if, ((0, 0), (0, GATE_WIDTH - 2 * MLSTM_HEADS)))
    gbias = jnp.pad(jnp.concatenate([i_bias, f_bias]), (0, GATE_WIDTH - 2 * MLSTM_HEADS)).reshape(1, GATE_WIDTH)

    qkv, mm, gt, gg = _inproj(x2d, norm1_g.reshape(1, D_MODEL), wa.astype(BF16), wm_qk.astype(BF16),
                              wm_vo.astype(BF16), wgt.astype(BF16), wgg.astype(BF16), conv_w,
                              conv_b.reshape(1, -1), tm=INPROJ_TM, seq=seq)
    ao, hm = _mixer(attn_sinks, qkv, mm, gt, gbias, mlstm_norm_g.reshape(1, MLSTM_V_WIDTH), batch, seq,
                    nch=MIXER_CHUNKS)
    return _tail(x2d, ao, hm, gg, wab.astype(BF16), w_mlstm_branch.astype(BF16), w_out.astype(BF16),
                 norm2_g.reshape(1, D_MODEL), w_ffn_gate.astype(BF16), w_ffn_up.astype(BF16),
                 w_ffn_down.astype(BF16), out_g.reshape(1, D_MODEL), tm=TAIL_TM, ff_split=TAIL_FF_SPLIT)


def kernel(x, norm1_g, w_in, conv_w, conv_b, i_bias, f_bias, mlstm_norm_g, attn_sinks, w_attn_branch,
           w_mlstm_branch, w_out, norm2_g, w_ffn_gate, w_ffn_up, w_ffn_down, final_norm_g):
    batch, seq, d = x.shape
    depth = norm1_g.shape[0]
    assert depth == 1 and d == D_MODEL
    assert seq % (MIXER_CHUNKS * CHUNK) == 0 and seq % INPROJ_TM == 0 and (batch * seq) % TAIL_TM == 0
    assert w_ffn_gate.shape[-1] % MXU_WIDTH == 0
    out = _layer(x.reshape(batch * seq, d), batch, seq, norm1_g[0], w_in[0], conv_w[0], conv_b[0], i_bias[0],
                 f_bias[0], mlstm_norm_g[0], attn_sinks[0], w_attn_branch[0], w_mlstm_branch[0], w_out[0],
                 norm2_g[0], w_ffn_gate[0], w_ffn_up[0], w_ffn_down[0], final_norm_g)
    return out.reshape(batch, seq, d)
```

```python
import functools

import jax
import jax.numpy as jnp
from jax import lax
from jax.experimental import pallas as pl
from jax.experimental.pallas import tpu as pltpu

D_MODEL = 1024
ATTN_HEADS = 8
ATTN_KV_HEADS = 2
ATTN_HEAD_DIM = 64
ATTN_GROUP = ATTN_HEADS // ATTN_KV_HEADS
WINDOW = 128
ATTN_Q_WIDTH = ATTN_HEADS * ATTN_HEAD_DIM
ATTN_KV_WIDTH = ATTN_KV_HEADS * ATTN_HEAD_DIM
MLSTM_HEADS = 4
MLSTM_QK_DIM = 64
MLSTM_V_DIM = 128
MLSTM_QK_WIDTH = MLSTM_HEADS * MLSTM_QK_DIM
MLSTM_V_WIDTH = MLSTM_HEADS * MLSTM_V_DIM
CHUNK = 128
CONV_WIDTH = 4
NORM_EPS = 1e-6

LANES = 128
MXU_WIDTH = 256
HALF = LANES // 2
CONV_PAD = 8
GATE_ROWS = 16

INPROJ_TM = 1024
INPROJ_ROW_BLOCKS = 2
MIXER_CHUNKS = 8
MIXER_STAGE_LAGS = (0, 1, 2, 3, 4, 4)
TAIL_TM = 512
TAIL_FF_SPLIT = 2
VMEM_LIMIT_BYTES = 56 * 1024 * 1024

QKV_WIDTH = ATTN_Q_WIDTH + 2 * ATTN_KV_WIDTH
MM_WIDTH = 2 * MLSTM_QK_WIDTH + 2 * MLSTM_V_WIDTH
GATE_WIDTH = LANES
MERGE_WIDTH = 2 * D_MODEL

Q_HEAD_ORDER = tuple(h * ATTN_GROUP + g for g in range(ATTN_GROUP) for h in range(ATTN_KV_HEADS))

BF16 = jnp.bfloat16
F32 = jnp.float32
NEG_INF = float("-inf")


def _dot(a, b):
    return jnp.dot(a, b, preferred_element_type=F32)


def _dot_nt(a, b):
    return lax.dot_general(a, b, (((1,), (1,)), ((), ())), preferred_element_type=F32)


def _rms_scale(x):
    return lax.rsqrt(jnp.mean(x * x, axis=-1, keepdims=True) + NORM_EPS)


def _const_spec(shape):
    nd = len(shape)
    return pl.BlockSpec(shape, lambda *_: (0,) * nd, pipeline_mode=pl.Buffered(1))


def _inproj_kernel(x_ref, g_ref, wa_ref, wqk_ref, wvo_ref, wgt_ref, wgg_ref, convw_ref, convb_ref,
                   qkv_ref, mm_ref, gt_ref, gg_ref, conv_ref, *, tiles_per_seq):
    tm = x_ref.shape[0]

    @pl.when(pl.program_id(0) % tiles_per_seq == 0)
    def _():
        conv_ref[0:CONV_PAD, :] = jnp.zeros((CONV_PAD, 2 * MLSTM_QK_WIDTH), F32)

    q_scale = ATTN_HEAD_DIM ** -0.5
    lane = lax.broadcasted_iota(jnp.int32, (1, 2 * MLSTM_QK_WIDTH), 1)
    k_scale = jnp.where(lane < MLSTM_QK_WIDTH, 1.0, MLSTM_QK_DIM ** -0.5)
    rb = tm // INPROJ_ROW_BLOCKS
    for r in range(INPROJ_ROW_BLOCKS):
        rows = slice(r * rb, (r + 1) * rb)
        x = x_ref[rows, :]
        u = (x * _rms_scale(x) * g_ref[...]).astype(BF16)
        conv_ref[CONV_PAD + r * rb:CONV_PAD + (r + 1) * rb, :] = _dot(u, wqk_ref[...])
        qkv_ref[rows, 0:ATTN_Q_WIDTH] = (_dot(u, wa_ref[:, 0:ATTN_Q_WIDTH]) * q_scale).astype(BF16)
        qkv_ref[rows, ATTN_Q_WIDTH:QKV_WIDTH] = _dot(u, wa_ref[:, ATTN_Q_WIDTH:QKV_WIDTH]).astype(BF16)
        mm_ref[rows, 2 * MLSTM_QK_WIDTH:MM_WIDTH] = _dot(u, wvo_ref[...]).astype(BF16)
        gt_ref[rows, :] = _dot(u, wgt_ref[...])
        gg_ref[rows, :] = _dot(u, wgg_ref[...]).astype(BF16)
        xe = conv_ref[r * rb:r * rb + CONV_PAD + rb, :]
        acc = xe * convw_ref[0:1, :]
        for t in range(1, CONV_WIDTH):
            acc = pltpu.roll(acc, 1, axis=0) + xe * convw_ref[t:t + 1, :]
        acc = acc[CONV_PAD:CONV_PAD + rb, :] + convb_ref[...]
        mm_ref[rows, 0:2 * MLSTM_QK_WIDTH] = (acc * jax.nn.sigmoid(acc) * k_scale).astype(BF16)
    conv_ref[0:CONV_PAD, :] = conv_ref[tm:tm + CONV_PAD, :]


def _inproj(x2d, g1, wa, wqk, wvo, wgt, wgg, convw, convb, tm, seq):
    t = x2d.shape[0]
    row = lambda w: pl.BlockSpec((tm, w), lambda i: (i, 0))
    consts = (g1, wa, wqk, wvo, wgt, wgg, convw, convb)
    return pl.pallas_call(
        functools.partial(_inproj_kernel, tiles_per_seq=seq // tm),
        grid=(t // tm,),
        in_specs=[row(D_MODEL)] + [_const_spec(c.shape) for c in consts],
        out_specs=[row(QKV_WIDTH), row(MM_WIDTH), row(GATE_WIDTH), row(MERGE_WIDTH)],
        out_shape=[jax.ShapeDtypeStruct((t, QKV_WIDTH), BF16), jax.ShapeDtypeStruct((t, MM_WIDTH), BF16),
                   jax.ShapeDtypeStruct((t, GATE_WIDTH), F32), jax.ShapeDtypeStruct((t, MERGE_WIDTH), BF16)],
        scratch_shapes=[pltpu.VMEM((CONV_PAD + tm, 2 * MLSTM_QK_WIDTH), F32)],
        compiler_params=pltpu.CompilerParams(dimension_semantics=("arbitrary",),
                                             vmem_limit_bytes=VMEM_LIMIT_BYTES),
        name="inproj",
    )(x2d, *consts)


def _log_sigmoid(x):
    return jnp.minimum(x, 0.0) - jnp.log1p(jnp.exp(-jnp.abs(x)))


def _split3(x):
    hi = x.astype(BF16)
    r1 = x - hi.astype(F32)
    mid = r1.astype(BF16)
    lo = (r1 - mid.astype(F32)).astype(BF16)
    return hi, mid, lo


def _mixer_kernel(sinks_ref, qkv_ref, mm_ref, gt_ref, gbias_ref, normg_ref,
                  ao_ref, hm_ref,
                  btab_ref, kvp_ref, cst_ref, mst_ref, *, nch):
    b_idx = pl.program_id(0)
    j_idx = pl.program_id(1)

    row_c = lax.broadcasted_iota(jnp.int32, (CHUNK, CHUNK), 0)
    col_c = lax.broadcasted_iota(jnp.int32, (CHUNK, CHUNK), 1)

    @pl.when((b_idx == 0) & (j_idx == 0))
    def _():
        cur = row_c <= col_c
        dist = jnp.where(cur, col_c - row_c, col_c - row_c + WINDOW).astype(F32)
        for j in range(ATTN_HEADS):
            slope = 2.0 ** (-8.0 * (j + 1) / ATTN_HEADS)
            btab_ref[0, j] = -slope * dist
            btab_ref[1, j] = jnp.where(cur, -slope * dist, NEG_INF)

    @pl.when(j_idx == 0)
    def _():
        kvp_ref[...] = jnp.zeros_like(kvp_ref)
        cst_ref[...] = jnp.zeros_like(cst_ref)
        mst_ref[...] = jnp.zeros_like(mst_ref)

    cur_t = row_c <= col_c
    causal = col_c <= row_c
    tril_b = causal.astype(BF16)
    triu_b = cur_t.astype(BF16)
    eye_b = (row_c == col_c).astype(BF16)

    def transpose_bf16(x):
        return _dot_nt(eye_b, x)

    top_rows = row_c < HALF
    top_rows_w = lax.broadcasted_iota(jnp.int32, (CHUNK, 2 * LANES), 0) < HALF
    left_lanes = col_c < HALF
    ones_v = jnp.ones((CHUNK, LANES), BF16)
    zero_b = jnp.zeros((), BF16)
    one_b = jnp.ones((), BF16)
    first = (j_idx == 0).astype(jnp.int32)

    def kv_sides(k_blk, v_blk):
        vt = transpose_bf16(v_blk).astype(BF16)
        k_sides = (jnp.where(left_lanes, k_blk, zero_b), jnp.where(left_lanes, zero_b, k_blk))
        vt_sides = (jnp.where(top_rows, vt, one_b), jnp.where(top_rows, one_b, vt))
        return k_sides, vt_sides

    k_off = ATTN_Q_WIDTH
    v_off = ATTN_Q_WIDTH + ATTN_KV_WIDTH
    prev_sides = kv_sides(kvp_ref[0], kvp_ref[1])
    last = slice((nch - 1) * CHUNK, nch * CHUNK)
    kvp_ref[0] = qkv_ref[last, k_off:k_off + ATTN_KV_WIDTH]
    kvp_ref[1] = qkv_ref[last, v_off:v_off + ATTN_KV_WIDTH]

    m_rows = [mst_ref[h:h + 1, :] for h in range(MLSTM_HEADS)]
    c_pairs = [cst_ref[pr] for pr in range(MLSTM_HEADS // 2)]

    npair = ATTN_GROUP
    rows_of = lambda c: slice(c * CHUNK, (c + 1) * CHUNK)
    row8 = lax.broadcasted_iota(jnp.int32, (GATE_ROWS, LANES), 0)
    is_f = (row8 >= MLSTM_HEADS) & (row8 < 2 * MLSTM_HEADS)
    sides = {-1: prev_sides}
    gates, s_t, sc2s, kt_hs, p_t, mx_rows, intra_w, o_t, intra = {}, {}, {}, {}, {}, {}, {}, {}, {}

    def stage_prepare(c):
        rows = rows_of(c)
        sides[c] = kv_sides(qkv_ref[rows, k_off:k_off + ATTN_KV_WIDTH], qkv_ref[rows, v_off:v_off + ATTN_KV_WIDTH])
        g8 = (gt_ref[rows, :] + gbias_ref[...]).T[0:GATE_ROWS, :]
        lf_parts = _split3(jnp.where(is_f, _log_sigmoid(g8), 0.0))
        b8 = sum(_dot(part, triu_b) for part in lf_parts)
        b_bc_all = sum(
            _dot_nt(tril_b, jnp.concatenate(
                [jnp.broadcast_to(part[MLSTM_HEADS + h:MLSTM_HEADS + h + 1, :], (CHUNK, LANES))
                 for h in range(MLSTM_HEADS)], axis=0))
            for part in lf_parts)
        mk = mm_ref[rows, MLSTM_QK_WIDTH:2 * MLSTM_QK_WIDTH]
        kt = transpose_bf16(jnp.concatenate([mk[:, 0:LANES], mk[:, LANES:2 * LANES]], axis=0))
        gates[c] = (g8, b8, b_bc_all, kt)

    def stage_scores(c):
        q = qkv_ref[rows_of(c), 0:ATTN_Q_WIDTH]
        q_pairs = [q[:, p * LANES:(p + 1) * LANES] for p in range(npair)]
        for side in range(ATTN_KV_HEADS):
            k_side = jnp.concatenate([sides[c - 1][0][side], sides[c][0][side]], axis=0)
            for pp in range(npair // 2):
                q2 = jnp.concatenate([q_pairs[2 * pp], q_pairs[2 * pp + 1]], axis=0)
                s2 = _dot_nt(k_side, q2)
                s_t[(c, side, 2 * pp)] = s2[:, 0:LANES]
                s_t[(c, side, 2 * pp + 1)] = s2[:, LANES:2 * LANES]
        kt = gates[c][3]
        for pr in range(MLSTM_HEADS // 2):
            qpair = mm_ref[rows_of(c), pr * LANES:(pr + 1) * LANES]
            kt_pair = kt[:, pr * LANES:(pr + 1) * LANES]
            kt_h = (jnp.where(top_rows, kt_pair, 0.0), jnp.where(top_rows, 0.0, kt_pair))
            kt_hs[(c, pr)] = kt_h
            sc2s[(c, pr)] = _dot(qpair, jnp.concatenate([kt_h[0].astype(BF16), kt_h[1].astype(BF16)], axis=1))

    def stage_weights(c):
        variant = first if c == 0 else 0
        for side in range(ATTN_KV_HEADS):
            for p in range(npair):
                j = side * ATTN_GROUP + p
                s2 = s_t.pop((c, side, p))
                comb = jnp.where(cur_t, s2[CHUNK:2 * CHUNK], s2[0:CHUNK]) + btab_ref[variant, j]
                mx = jnp.maximum(jnp.max(comb, axis=0, keepdims=True), sinks_ref[j])
                e = jnp.exp(comb - mx)
                p_t[(c, side, p)] = jnp.concatenate([jnp.where(cur_t, 0.0, e), jnp.where(cur_t, e, 0.0)],
                                                    axis=0).astype(BF16)
                mx_rows[(c, side, p)] = mx
        g8, b8, b_bc_all, _ = gates[c]
        for h in range(MLSTM_HEADS):
            i_row = g8[h:h + 1, :]
            b_row = b8[MLSTM_HEADS + h:MLSTM_HEADS + h + 1, :]
            b_bc = b_bc_all[:, h * LANES:(h + 1) * LANES]
            dlog = jnp.where(causal, b_bc + (i_row - b_row), NEG_INF)
            mloc = jnp.broadcast_to(jnp.max(dlog, axis=-1, keepdims=True), (CHUNK, LANES))
            wloc = jnp.exp(dlog - mloc)
            scw = (sc2s[(c, h // 2)][:, (h % 2) * LANES:(h % 2 + 1) * LANES] * wloc).astype(BF16)
            b_last = b_bc[CHUNK - 1:CHUNK, :]
            mloc_last = mloc[CHUNK - 1:CHUNK, :]
            u_row = jnp.exp(dlog[CHUNK - 1:CHUNK, :] - mloc_last)
            kw = (kt_hs[(c, h // 2)][h % 2] * u_row).astype(BF16)
            intra_w[(c, h)] = (b_bc, mloc, scw, b_last, mloc_last, kw)

    def stage_values(c):
        for side in range(ATTN_KV_HEADS):
            vt_side = jnp.concatenate([sides[c - 1][1][side], sides[c][1][side]], axis=1)
            for pp in range(npair // 2):
                p2 = jnp.concatenate([p_t.pop((c, side, 2 * pp)), p_t.pop((c, side, 2 * pp + 1))], axis=1)
                o2 = _dot(vt_side, p2)
                o_t[(c, side, 2 * pp)] = o2[:, 0:LANES]
                o_t[(c, side, 2 * pp + 1)] = o2[:, LANES:2 * LANES]
        for h in range(MLSTM_HEADS):
            b_bc, mloc, scw, b_last, mloc_last, kw = intra_w.pop((c, h))
            v_h = mm_ref[rows_of(c), 2 * MLSTM_QK_WIDTH + h * MLSTM_V_DIM:2 * MLSTM_QK_WIDTH + (h + 1) * MLSTM_V_DIM]
            v_aug = jnp.concatenate([v_h, ones_v], axis=1)
            intra[(c, h)] = (b_bc, mloc, _dot(scw, v_aug), b_last, mloc_last, _dot(kw, v_aug))

    def stage_attn_out(c):
        out_t = []
        for p in range(npair):
            scaled = []
            for side in range(ATTN_KV_HEADS):
                j = side * ATTN_GROUP + p
                o = o_t.pop((c, side, p))
                sums = o[HALF:HALF + 1, :] if side == 0 else o[0:1, :]
                den = sums + jnp.exp(sinks_ref[j] - mx_rows.pop((c, side, p)))
                scaled.append(o * (1.0 / den))
            out_t.append(jnp.where(top_rows, scaled[0], scaled[1]).astype(BF16))
        for pp in range(npair // 2):
            both = jnp.concatenate([out_t[2 * pp], out_t[2 * pp + 1]], axis=0)
            ao_ref[rows_of(c), 2 * pp * LANES:(2 * pp + 2) * LANES] = transpose_bf16(both).astype(BF16)

    def stage_recurrence(c):
        rows = rows_of(c)
        for pr in range(MLSTM_HEADS // 2):
            c_pair = c_pairs[pr]
            qpair = mm_ref[rows, pr * LANES:(pr + 1) * LANES]
            decs, eus, upds = [], [], []
            for half in range(2):
                h = 2 * pr + half
                b_bc, mloc, num_intra, b_last, mloc_last, upd = intra.pop((c, h))
                m_prev = m_rows[h]
                sel_w = top_rows_w if half == 0 else jnp.logical_not(top_rows_w)
                inter = _dot(qpair, jnp.where(sel_w, c_pair, 0.0).astype(BF16))
                a = b_bc + m_prev
                m_t = jnp.maximum(a, mloc)
                e_t = jnp.exp(mloc - m_t)
                w_inter = jnp.exp(a - m_t)
                numv = e_t * num_intra[:, 0:LANES] + w_inter * inter[:, 0:LANES]
                nq = e_t * num_intra[:, LANES:2 * LANES] + w_inter * inter[:, LANES:2 * LANES]
                hv = numv / jnp.maximum(jnp.abs(nq), jnp.exp(-m_t))
                hn = hv * _rms_scale(hv) * normg_ref[:, h * MLSTM_V_DIM:(h + 1) * MLSTM_V_DIM]
                og = mm_ref[rows, 2 * MLSTM_QK_WIDTH + MLSTM_V_WIDTH + h * MLSTM_V_DIM:
                            2 * MLSTM_QK_WIDTH + MLSTM_V_WIDTH + (h + 1) * MLSTM_V_DIM].astype(F32)
                hm_ref[rows, h * MLSTM_V_DIM:(h + 1) * MLSTM_V_DIM] = (hn * jax.nn.sigmoid(og)).astype(BF16)
                m_new = jnp.maximum(b_last + m_prev, mloc_last)
                decs.append(jnp.exp(b_last + m_prev - m_new))
                eus.append(jnp.exp(mloc_last - m_new))
                upds.append(upd)
                m_rows[h] = m_new
            wide = lambda r: jnp.concatenate([r, r], axis=1)
            c_pairs[pr] = (jnp.where(top_rows_w, wide(decs[0]), wide(decs[1])) * c_pair
                           + wide(eus[0]) * upds[0] + wide(eus[1]) * upds[1])

    stages = (stage_prepare, stage_scores, stage_weights, stage_values, stage_attn_out, stage_recurrence)
    for tick in range(nch + max(MIXER_STAGE_LAGS)):
        for stage, lag in zip(stages, MIXER_STAGE_LAGS):
            c = tick - lag
            if 0 <= c < nch:
                stage(c)

    for h in range(MLSTM_HEADS):
        mst_ref[h:h + 1, :] = m_rows[h]
    for pr in range(MLSTM_HEADS // 2):
        cst_ref[pr] = c_pairs[pr]


def _mixer(sinks, qkv, mm, gt, gbias, normg, batch, seq, nch):
    tm = nch * CHUNK
    nj = seq // tm
    t = batch * seq
    row = lambda w: pl.BlockSpec((tm, w), lambda b, j: (b * nj + j, 0))
    return pl.pallas_call(
        functools.partial(_mixer_kernel, nch=nch),
        grid=(batch, nj),
        in_specs=[pl.BlockSpec(memory_space=pltpu.SMEM),
                  row(QKV_WIDTH), row(MM_WIDTH), row(GATE_WIDTH),
                  _const_spec(gbias.shape), _const_spec(normg.shape)],
        out_specs=[row(ATTN_Q_WIDTH), row(MLSTM_V_WIDTH)],
        out_shape=[jax.ShapeDtypeStruct((t, ATTN_Q_WIDTH), BF16), jax.ShapeDtypeStruct((t, MLSTM_V_WIDTH), BF16)],
        scratch_shapes=[
            pltpu.VMEM((2, ATTN_HEADS, CHUNK, CHUNK), F32),
            pltpu.VMEM((2, CHUNK, ATTN_KV_WIDTH), BF16),
            pltpu.VMEM((MLSTM_HEADS // 2, 2 * MLSTM_QK_DIM, 2 * MLSTM_V_DIM), F32),
            pltpu.VMEM((8, LANES), F32),
        ],
        compiler_params=pltpu.CompilerParams(dimension_semantics=("arbitrary", "arbitrary"),
                                             vmem_limit_bytes=VMEM_LIMIT_BYTES),
        name="mixer",
    )(sinks, qkv, mm, gt, gbias, normg)


def _tail_kernel(x_ref, ao_ref, hm_ref, gg_ref, wab_ref, wmb_ref, wo_ref, g2_ref, wg_ref, wu_ref, wd_ref,
                 gf_ref, out_ref, *, ff_split):
    ya = _dot(ao_ref[...], wab_ref[...])
    ym = _dot(hm_ref[...], wmb_ref[...])
    ga = jax.nn.sigmoid(gg_ref[:, 0:D_MODEL].astype(F32))
    gm = jax.nn.sigmoid(gg_ref[:, D_MODEL:2 * D_MODEL].astype(F32))
    z = (ga * ya + gm * ym).astype(BF16)
    x1 = x_ref[...] + _dot(z, wo_ref[...])
    f = (x1 * _rms_scale(x1) * g2_ref[...]).astype(BF16)
    n_tiles = wg_ref.shape[1] // MXU_WIDTH
    bounds = [MXU_WIDTH * ((n_tiles * s + ff_split - 1) // ff_split) for s in range(ff_split + 1)]
    x2 = x1
    for s in range(ff_split):
        cols = slice(bounds[s], bounds[s + 1])
        gte = _dot(f, wg_ref[:, cols])
        up = _dot(f, wu_ref[:, cols])
        hh = (gte * jax.nn.sigmoid(gte) * up).astype(BF16)
        x2 = x2 + _dot(hh, wd_ref[cols, :])
    out_ref[...] = x2 * _rms_scale(x2) * gf_ref[...]


def _tail(x2d, ao, hm, gg, wab, wmb, wo, g2, wg, wu, wd, gf, tm, ff_split):
    t = x2d.shape[0]
    row = lambda w: pl.BlockSpec((tm, w), lambda i: (i, 0))
    consts = (wab, wmb, wo, g2, wg, wu, wd, gf)
    return pl.pallas_call(
        functools.partial(_tail_kernel, ff_split=ff_split),
        grid=(t // tm,),
        in_specs=[row(D_MODEL), row(ATTN_Q_WIDTH), row(MLSTM_V_WIDTH), row(MERGE_WIDTH)]
                 + [_const_spec(c.shape) for c in consts],
        out_specs=row(D_MODEL),
        out_shape=jax.ShapeDtypeStruct((t, D_MODEL), F32),
        compiler_params=pltpu.CompilerParams(dimension_semantics=("arbitrary",),
                                             vmem_limit_bytes=VMEM_LIMIT_BYTES),
        name="tail",
    )(x2d, ao, hm, gg, *consts)


def _layer(x2d, batch, seq, norm1_g, w_in, conv_w, conv_b, i_bias, f_bias, mlstm_norm_g, attn_sinks,
           w_attn_branch, w_mlstm_branch, w_out, norm2_g, w_ffn_gate, w_ffn_up, w_ffn_down, out_g):
    order = jnp.asarray(Q_HEAD_ORDER)
    o = 0
    wq = w_in[:, o:o + ATTN_Q_WIDTH]; o += ATTN_Q_WIDTH
    wkv = w_in[:, o:o + 2 * ATTN_KV_WIDTH]; o += 2 * ATTN_KV_WIDTH
    wm_qk = w_in[:, o:o + 2 * MLSTM_QK_WIDTH]; o += 2 * MLSTM_QK_WIDTH
    wm_vo = w_in[:, o:o + 2 * MLSTM_V_WIDTH]; o += 2 * MLSTM_V_WIDTH
    w_if = w_in[:, o:o + 2 * MLSTM_HEADS]; o += 2 * MLSTM_HEADS
    wgg = w_in[:, o:o + MERGE_WIDTH]
    wq = wq.reshape(D_MODEL, ATTN_HEADS, ATTN_HEAD_DIM)[:, order, :].reshape(D_MODEL, ATTN_Q_WIDTH)
    wa = jnp.concatenate([wq, wkv], axis=1)
    wab = w_attn_branch.reshape(ATTN_HEADS, ATTN_HEAD_DIM, D_MODEL)[order].reshape(ATTN_Q_WIDTH, D_MODEL)
    wgt = jnp.pad(w_if, ((0, 0), (0, GATE_WIDTH - 2 * MLSTM_HEADS)))
    gbias = jnp.pad(jnp.concatenate([i_bias, f_bias]), (0, GATE_WIDTH - 2 * MLSTM_HEADS)).reshape(1, GATE_WIDTH)

    qkv, mm, gt, gg = _inproj(x2d, norm1_g.reshape(1, D_MODEL), wa.astype(BF16), wm_qk.astype(BF16),
                              wm_vo.astype(BF16), wgt.astype(BF16), wgg.astype(BF16), conv_w,
                              conv_b.reshape(1, -1), tm=INPROJ_TM, seq=seq)
    ao, hm = _mixer(attn_sinks, qkv, mm, gt, gbias, mlstm_norm_g.reshape(1, MLSTM_V_WIDTH), batch, seq,
                    nch=MIXER_CHUNKS)
    return _tail(x2d, ao, hm, gg, wab.astype(BF16), w_mlstm_branch.astype(BF16), w_out.astype(BF16),
                 norm2_g.reshape(1, D_MODEL), w_ffn_gate.astype(BF16), w_ffn_up.astype(BF16),
                 w_ffn_down.astype(BF16), out_g.reshape(1, D_MODEL), tm=TAIL_TM, ff_split=TAIL_FF_SPLIT)


def kernel(x, norm1_g, w_in, conv_w, conv_b, i_bias, f_bias, mlstm_norm_g, attn_sinks, w_attn_branch,
           w_mlstm_branch, w_out, norm2_g, w_ffn_gate, w_ffn_up, w_ffn_down, final_norm_g):
    batch, seq, d = x.shape
    depth = norm1_g.shape[0]
    assert depth == 1 and d == D_MODEL
    assert seq % (MIXER_CHUNKS * CHUNK) == 0 and seq % INPROJ_TM == 0 and (batch * seq) % TAIL_TM == 0
    assert w_ffn_gate.shape[-1] % MXU_WIDTH == 0
    out = _layer(x.reshape(batch * seq, d), batch, seq, norm1_g[0], w_in[0], conv_w[0], conv_b[0], i_bias[0],
                 f_bias[0], mlstm_norm_g[0], attn_sinks[0], w_attn_branch[0], w_mlstm_branch[0], w_out[0],
                 norm2_g[0], w_ffn_gate[0], w_ffn_up[0], w_ffn_down[0], final_norm_g)
    return out.reshape(batch, seq, d)
```

```python
import functools

import jax
import jax.numpy as jnp
from jax import lax
from jax.experimental import pallas as pl
from jax.experimental.pallas import tpu as pltpu

D_MODEL = 1024
ATTN_HEADS = 8
ATTN_KV_HEADS = 2
ATTN_HEAD_DIM = 64
ATTN_GROUP = ATTN_HEADS // ATTN_KV_HEADS
WINDOW = 128
ATTN_Q_WIDTH = ATTN_HEADS * ATTN_HEAD_DIM
ATTN_KV_WIDTH = ATTN_KV_HEADS * ATTN_HEAD_DIM
MLSTM_HEADS = 4
MLSTM_QK_DIM = 64
MLSTM_V_DIM = 128
MLSTM_QK_WIDTH = MLSTM_HEADS * MLSTM_QK_DIM
MLSTM_V_WIDTH = MLSTM_HEADS * MLSTM_V_DIM
CHUNK = 128
CONV_WIDTH = 4
NORM_EPS = 1e-6

LANES = 128
BF16_SUBLANES = 16
MXU_WIDTH = 256
HALF = LANES // 2
CONV_PAD = 8
GATE_ROWS = 16

INPROJ_TM = 1024
INPROJ_ROW_BLOCKS = 2
MIXER_CHUNKS = 8
MIXER_STAGE_LAGS = (0, 1, 2, 3, 4, 4)
TAIL_TM = 512
TAIL_FF_SPLIT = 2
VMEM_LIMIT_BYTES = 56 * 1024 * 1024

QKV_WIDTH = ATTN_Q_WIDTH + 2 * ATTN_KV_WIDTH
MM_WIDTH = 2 * MLSTM_QK_WIDTH + 2 * MLSTM_V_WIDTH
GATE_WIDTH = LANES
MERGE_WIDTH = 2 * D_MODEL

Q_HEAD_ORDER = tuple(h * ATTN_GROUP + g for g in range(ATTN_GROUP) for h in range(ATTN_KV_HEADS))

BF16 = jnp.bfloat16
F32 = jnp.float32
NEG_INF = float("-inf")


def _dot(a, b):
    return jnp.dot(a, b, preferred_element_type=F32)


def _dot_nt(a, b):
    return lax.dot_general(a, b, (((1,), (1,)), ((), ())), preferred_element_type=F32)


def _rms_scale(x):
    return lax.rsqrt(jnp.mean(x * x, axis=-1, keepdims=True) + NORM_EPS)


def _const_spec(shape):
    nd = len(shape)
    return pl.BlockSpec(shape, lambda *_: (0,) * nd, pipeline_mode=pl.Buffered(1))


def _inproj_kernel(x_ref, g_ref, wa_ref, wqk_ref, wvo_ref, wgt_ref, wgg_ref, convw_ref, convb_ref,
                   qkv_ref, mm_ref, gt_ref, gg_ref, conv_ref, *, tiles_per_seq):
    tm = x_ref.shape[0]

    @pl.when(pl.program_id(0) % tiles_per_seq == 0)
    def _():
        conv_ref[0:CONV_PAD, :] = jnp.zeros((CONV_PAD, 2 * MLSTM_QK_WIDTH), F32)

    q_scale = ATTN_HEAD_DIM ** -0.5
    lane = lax.broadcasted_iota(jnp.int32, (1, 2 * MLSTM_QK_WIDTH), 1)
    k_scale = jnp.where(lane < MLSTM_QK_WIDTH, 1.0, MLSTM_QK_DIM ** -0.5)
    rb = tm // INPROJ_ROW_BLOCKS
    for r in range(INPROJ_ROW_BLOCKS):
        rows = slice(r * rb, (r + 1) * rb)
        x = x_ref[rows, :]
        u = (x * _rms_scale(x) * g_ref[...]).astype(BF16)
        conv_ref[CONV_PAD + r * rb:CONV_PAD + (r + 1) * rb, :] = _dot(u, wqk_ref[...])
        qkv_ref[rows, 0:ATTN_Q_WIDTH] = (_dot(u, wa_ref[:, 0:ATTN_Q_WIDTH]) * q_scale).astype(BF16)
        qkv_ref[rows, ATTN_Q_WIDTH:QKV_WIDTH] = _dot(u, wa_ref[:, ATTN_Q_WIDTH:QKV_WIDTH]).astype(BF16)
        mm_ref[rows, 2 * MLSTM_QK_WIDTH:MM_WIDTH] = _dot(u, wvo_ref[...]).astype(BF16)
        gt_ref[rows, :] = _dot(u, wgt_ref[...])
        gg_ref[rows, :] = _dot(u, wgg_ref[...]).astype(BF16)
        xe = conv_ref[r * rb:r * rb + CONV_PAD + rb, :]
        acc = xe * convw_ref[0:1, :]
        for t in range(1, CONV_WIDTH):
            acc = pltpu.roll(acc, 1, axis=0) + xe * convw_ref[t:t + 1, :]
        acc = acc[CONV_PAD:CONV_PAD + rb, :] + convb_ref[...]
        mm_ref[rows, 0:2 * MLSTM_QK_WIDTH] = (acc * jax.nn.sigmoid(acc) * k_scale).astype(BF16)
    conv_ref[0:CONV_PAD, :] = conv_ref[tm:tm + CONV_PAD, :]


def _inproj(x2d, g1, wa, wqk, wvo, wgt, wgg, convw, convb, tm, seq):
    t = x2d.shape[0]
    row = lambda w: pl.BlockSpec((tm, w), lambda i: (i, 0))
    consts = (g1, wa, wqk, wvo, wgt, wgg, convw, convb)
    return pl.pallas_call(
        functools.partial(_inproj_kernel, tiles_per_seq=seq // tm),
        grid=(t // tm,),
        in_specs=[row(D_MODEL)] + [_const_spec(c.shape) for c in consts],
        out_specs=[row(QKV_WIDTH), row(MM_WIDTH), row(GATE_WIDTH), row(MERGE_WIDTH)],
        out_shape=[jax.ShapeDtypeStruct((t, QKV_WIDTH), BF16), jax.ShapeDtypeStruct((t, MM_WIDTH), BF16),
                   jax.ShapeDtypeStruct((t, GATE_WIDTH), F32), jax.ShapeDtypeStruct((t, MERGE_WIDTH), BF16)],
        scratch_shapes=[pltpu.VMEM((CONV_PAD + tm, 2 * MLSTM_QK_WIDTH), F32)],
        compiler_params=pltpu.CompilerParams(dimension_semantics=("arbitrary",),
                                             vmem_limit_bytes=VMEM_LIMIT_BYTES),
        name="inproj",
    )(x2d, *consts)


def _log_sigmoid(x):
    return jnp.minimum(x, 0.0) - jnp.log1p(jnp.exp(-jnp.abs(x)))


def _split3(x):
    hi = x.astype(BF16)
    r1 = x - hi.astype(F32)
    mid = r1.astype(BF16)
    lo = (r1 - mid.astype(F32)).astype(BF16)
    return hi, mid, lo


def _mixer_kernel(sinks_ref, qkv_ref, mm_ref, gt_ref, gbias_ref, normg_ref, *rest, nch, n_cast):
    cast_in, (ao_ref, hm_ref), rest = rest[:n_cast], rest[n_cast:n_cast + 2], rest[n_cast + 2:]
    cast_out, (btab_ref, kvp_ref, cst_ref, mst_ref) = rest[:n_cast], rest[n_cast:]
    b_idx = pl.program_id(0)
    j_idx = pl.program_id(1)

    for src, dst in zip(cast_in, cast_out):
        dst[...] = src[...].astype(BF16)

    row_c = lax.broadcasted_iota(jnp.int32, (CHUNK, CHUNK), 0)
    col_c = lax.broadcasted_iota(jnp.int32, (CHUNK, CHUNK), 1)

    @pl.when((b_idx == 0) & (j_idx == 0))
    def _():
        cur = row_c <= col_c
        dist = jnp.where(cur, col_c - row_c, col_c - row_c + WINDOW).astype(F32)
        for j in range(ATTN_HEADS):
            slope = 2.0 ** (-8.0 * (j + 1) / ATTN_HEADS)
            btab_ref[0, j] = -slope * dist
            btab_ref[1, j] = jnp.where(cur, -slope * dist, NEG_INF)

    @pl.when(j_idx == 0)
    def _():
        kvp_ref[...] = jnp.zeros_like(kvp_ref)
        cst_ref[...] = jnp.zeros_like(cst_ref)
        mst_ref[...] = jnp.zeros_like(mst_ref)

    cur_t = row_c <= col_c
    causal = col_c <= row_c
    tril_b = causal.astype(BF16)
    triu_b = cur_t.astype(BF16)
    eye_b = (row_c == col_c).astype(BF16)

    def transpose_bf16(x):
        return _dot_nt(eye_b, x)

    top_rows = row_c < HALF
    top_rows_w = lax.broadcasted_iota(jnp.int32, (CHUNK, 2 * LANES), 0) < HALF
    left_lanes = col_c < HALF
    ones_v = jnp.ones((CHUNK, LANES), BF16)
    zero_b = jnp.zeros((), BF16)
    one_b = jnp.ones((), BF16)
    first = (j_idx == 0).astype(jnp.int32)

    def kv_sides(k_blk, v_blk):
        vt = transpose_bf16(v_blk).astype(BF16)
        k_sides = (jnp.where(left_lanes, k_blk, zero_b), jnp.where(left_lanes, zero_b, k_blk))
        vt_sides = (jnp.where(top_rows, vt, one_b), jnp.where(top_rows, one_b, vt))
        return k_sides, vt_sides

    k_off = ATTN_Q_WIDTH
    v_off = ATTN_Q_WIDTH + ATTN_KV_WIDTH
    prev_sides = kv_sides(kvp_ref[0], kvp_ref[1])
    last = slice((nch - 1) * CHUNK, nch * CHUNK)
    kvp_ref[0] = qkv_ref[last, k_off:k_off + ATTN_KV_WIDTH]
    kvp_ref[1] = qkv_ref[last, v_off:v_off + ATTN_KV_WIDTH]

    m_rows = [mst_ref[h:h + 1, :] for h in range(MLSTM_HEADS)]
    c_pairs = [cst_ref[pr] for pr in range(MLSTM_HEADS // 2)]

    npair = ATTN_GROUP
    rows_of = lambda c: slice(c * CHUNK, (c + 1) * CHUNK)
    row8 = lax.broadcasted_iota(jnp.int32, (GATE_ROWS, LANES), 0)
    is_f = (row8 >= MLSTM_HEADS) & (row8 < 2 * MLSTM_HEADS)
    sides = {-1: prev_sides}
    gates, s_t, sc2s, kt_hs, p_t, mx_rows, intra_w, o_t, intra = {}, {}, {}, {}, {}, {}, {}, {}, {}

    def stage_prepare(c):
        rows = rows_of(c)
        sides[c] = kv_sides(qkv_ref[rows, k_off:k_off + ATTN_KV_WIDTH], qkv_ref[rows, v_off:v_off + ATTN_KV_WIDTH])
        g8 = (gt_ref[rows, :] + gbias_ref[...]).T[0:GATE_ROWS, :]
        lf_parts = _split3(jnp.where(is_f, _log_sigmoid(g8), 0.0))
        b8 = sum(_dot(part, triu_b) for part in lf_parts)
        b_bc_all = sum(
            _dot_nt(tril_b, jnp.concatenate(
                [jnp.broadcast_to(part[MLSTM_HEADS + h:MLSTM_HEADS + h + 1, :], (CHUNK, LANES))
                 for h in range(MLSTM_HEADS)], axis=0))
            for part in lf_parts)
        mk = mm_ref[rows, MLSTM_QK_WIDTH:2 * MLSTM_QK_WIDTH]
        kt = transpose_bf16(jnp.concatenate([mk[:, 0:LANES], mk[:, LANES:2 * LANES]], axis=0))
        gates[c] = (g8, b8, b_bc_all, kt)

    def stage_scores(c):
        q = qkv_ref[rows_of(c), 0:ATTN_Q_WIDTH]
        q_pairs = [q[:, p * LANES:(p + 1) * LANES] for p in range(npair)]
        for side in range(ATTN_KV_HEADS):
            k_side = jnp.concatenate([sides[c - 1][0][side], sides[c][0][side]], axis=0)
            for pp in range(npair // 2):
                q2 = jnp.concatenate([q_pairs[2 * pp], q_pairs[2 * pp + 1]], axis=0)
                s2 = _dot_nt(k_side, q2)
                s_t[(c, side, 2 * pp)] = s2[:, 0:LANES]
                s_t[(c, side, 2 * pp + 1)] = s2[:, LANES:2 * LANES]
        kt = gates[c][3]
        for pr in range(MLSTM_HEADS // 2):
            qpair = mm_ref[rows_of(c), pr * LANES:(pr + 1) * LANES]
            kt_pair = kt[:, pr * LANES:(pr + 1) * LANES]
            kt_h = (jnp.where(top_rows, kt_pair, 0.0), jnp.where(top_rows, 0.0, kt_pair))
            kt_hs[(c, pr)] = kt_h
            sc2s[(c, pr)] = _dot(qpair, jnp.concatenate([kt_h[0].astype(BF16), kt_h[1].astype(BF16)], axis=1))

    def stage_weights(c):
        variant = first if c == 0 else 0
        for side in range(ATTN_KV_HEADS):
            for p in range(npair):
                j = side * ATTN_GROUP + p
                s2 = s_t.pop((c, side, p))
                comb = jnp.where(cur_t, s2[CHUNK:2 * CHUNK], s2[0:CHUNK]) + btab_ref[variant, j]
                mx = jnp.maximum(jnp.max(comb, axis=0, keepdims=True), sinks_ref[j])
                e = jnp.exp(comb - mx)
                p_t[(c, side, p)] = jnp.concatenate([jnp.where(cur_t, 0.0, e), jnp.where(cur_t, e, 0.0)],
                                                    axis=0).astype(BF16)
                mx_rows[(c, side, p)] = mx
        g8, b8, b_bc_all, _ = gates[c]
        for h in range(MLSTM_HEADS):
            i_row = g8[h:h + 1, :]
            b_row = b8[MLSTM_HEADS + h:MLSTM_HEADS + h + 1, :]
            b_bc = b_bc_all[:, h * LANES:(h + 1) * LANES]
            dlog = jnp.where(causal, b_bc + (i_row - b_row), NEG_INF)
            mloc = jnp.broadcast_to(jnp.max(dlog, axis=-1, keepdims=True), (CHUNK, LANES))
            wloc = jnp.exp(dlog - mloc)
            scw = (sc2s[(c, h // 2)][:, (h % 2) * LANES:(h % 2 + 1) * LANES] * wloc).astype(BF16)
            b_last = b_bc[CHUNK - 1:CHUNK, :]
            mloc_last = mloc[CHUNK - 1:CHUNK, :]
            u_row = jnp.exp(dlog[CHUNK - 1:CHUNK, :] - mloc_last)
            kw = (kt_hs[(c, h // 2)][h % 2] * u_row).astype(BF16)
            intra_w[(c, h)] = (b_bc, mloc, scw, b_last, mloc_last, kw)

    def stage_values(c):
        for side in range(ATTN_KV_HEADS):
            vt_side = jnp.concatenate([sides[c - 1][1][side], sides[c][1][side]], axis=1)
            for pp in range(npair // 2):
                p2 = jnp.concatenate([p_t.pop((c, side, 2 * pp)), p_t.pop((c, side, 2 * pp + 1))], axis=1)
                o2 = _dot(vt_side, p2)
                o_t[(c, side, 2 * pp)] = o2[:, 0:LANES]
                o_t[(c, side, 2 * pp + 1)] = o2[:, LANES:2 * LANES]
        for h in range(MLSTM_HEADS):
            b_bc, mloc, scw, b_last, mloc_last, kw = intra_w.pop((c, h))
            v_h = mm_ref[rows_of(c), 2 * MLSTM_QK_WIDTH + h * MLSTM_V_DIM:2 * MLSTM_QK_WIDTH + (h + 1) * MLSTM_V_DIM]
            v_aug = jnp.concatenate([v_h, ones_v], axis=1)
            intra[(c, h)] = (b_bc, mloc, _dot(scw, v_aug), b_last, mloc_last, _dot(kw, v_aug))

    def stage_attn_out(c):
        out_t = []
        for p in range(npair):
            scaled = []
            for side in range(ATTN_KV_HEADS):
                j = side * ATTN_GROUP + p
                o = o_t.pop((c, side, p))
                sums = o[HALF:HALF + 1, :] if side == 0 else o[0:1, :]
                den = sums + jnp.exp(sinks_ref[j] - mx_rows.pop((c, side, p)))
                scaled.append(o * (1.0 / den))
            out_t.append(jnp.where(top_rows, scaled[0], scaled[1]).astype(BF16))
        for pp in range(npair // 2):
            both = jnp.concatenate([out_t[2 * pp], out_t[2 * pp + 1]], axis=0)
            ao_ref[rows_of(c), 2 * pp * LANES:(2 * pp + 2) * LANES] = transpose_bf16(both).astype(BF16)

    def stage_recurrence(c):
        rows = rows_of(c)
        for pr in range(MLSTM_HEADS // 2):
            c_pair = c_pairs[pr]
            qpair = mm_ref[rows, pr * LANES:(pr + 1) * LANES]
            decs, eus, upds = [], [], []
            for half in range(2):
                h = 2 * pr + half
                b_bc, mloc, num_intra, b_last, mloc_last, upd = intra.pop((c, h))
                m_prev = m_rows[h]
                sel_w = top_rows_w if half == 0 else jnp.logical_not(top_rows_w)
                inter = _dot(qpair, jnp.where(sel_w, c_pair, 0.0).astype(BF16))
                a = b_bc + m_prev
                m_t = jnp.maximum(a, mloc)
                e_t = jnp.exp(mloc - m_t)
                w_inter = jnp.exp(a - m_t)
                numv = e_t * num_intra[:, 0:LANES] + w_inter * inter[:, 0:LANES]
                nq = e_t * num_intra[:, LANES:2 * LANES] + w_inter * inter[:, LANES:2 * LANES]
                hv = numv / jnp.maximum(jnp.abs(nq), jnp.exp(-m_t))
                hn = hv * _rms_scale(hv) * normg_ref[:, h * MLSTM_V_DIM:(h + 1) * MLSTM_V_DIM]
                og = mm_ref[rows, 2 * MLSTM_QK_WIDTH + MLSTM_V_WIDTH + h * MLSTM_V_DIM:
                            2 * MLSTM_QK_WIDTH + MLSTM_V_WIDTH + (h + 1) * MLSTM_V_DIM].astype(F32)
                hm_ref[rows, h * MLSTM_V_DIM:(h + 1) * MLSTM_V_DIM] = (hn * jax.nn.sigmoid(og)).astype(BF16)
                m_new = jnp.maximum(b_last + m_prev, mloc_last)
                decs.append(jnp.exp(b_last + m_prev - m_new))
                eus.append(jnp.exp(mloc_last - m_new))
                upds.append(upd)
                m_rows[h] = m_new
            wide = lambda r: jnp.concatenate([r, r], axis=1)
            c_pairs[pr] = (jnp.where(top_rows_w, wide(decs[0]), wide(decs[1])) * c_pair
                           + wide(eus[0]) * upds[0] + wide(eus[1]) * upds[1])

    stages = (stage_prepare, stage_scores, stage_weights, stage_values, stage_attn_out, stage_recurrence)
    for tick in range(nch + max(MIXER_STAGE_LAGS)):
        for stage, lag in zip(stages, MIXER_STAGE_LAGS):
            c = tick - lag
            if 0 <= c < nch:
                stage(c)

    for h in range(MLSTM_HEADS):
        mst_ref[h:h + 1, :] = m_rows[h]
    for pr in range(MLSTM_HEADS // 2):
        cst_ref[pr] = c_pairs[pr]


def _cast_slab_rows(n_rows, n_steps):
    slab = BF16_SUBLANES * pl.cdiv(pl.cdiv(n_rows, n_steps), BF16_SUBLANES)
    while n_rows % slab:
        slab += BF16_SUBLANES
    return slab


def _mixer(sinks, qkv, mm, gt, gbias, normg, cast_weights, batch, seq, nch):
    tm = nch * CHUNK
    nj = seq // tm
    t = batch * seq
    row = lambda w: pl.BlockSpec((tm, w), lambda b, j: (b * nj + j, 0))
    cast_in_specs, cast_out_specs, cast_shapes = [], [], []
    for w, block_perm in cast_weights:
        n_rows, n_cols = w.shape
        slab = n_rows // len(block_perm) if block_perm else _cast_slab_rows(n_rows, batch * nj)
        assert slab % BF16_SUBLANES == 0 and n_rows // slab <= batch * nj
        last = n_rows // slab - 1
        dst_map = lambda b, j, last=last: (jnp.minimum(b * nj + j, last), 0)
        if block_perm:
            assert tuple(block_perm) == tuple((s % 2) * (len(block_perm) // 2) + s // 2
                                              for s in range(len(block_perm)))
            stride = len(block_perm) // 2
            src_map = lambda b, j, last=last, stride=stride: (
                (jnp.minimum(b * nj + j, last) % 2) * stride + jnp.minimum(b * nj + j, last) // 2, 0)
        else:
            src_map = dst_map
        cast_in_specs.append(pl.BlockSpec((slab, n_cols), src_map))
        cast_out_specs.append(pl.BlockSpec((slab, n_cols), dst_map))
        cast_shapes.append(jax.ShapeDtypeStruct(w.shape, BF16))
    return pl.pallas_call(
        functools.partial(_mixer_kernel, nch=nch, n_cast=len(cast_weights)),
        grid=(batch, nj),
        in_specs=[pl.BlockSpec(memory_space=pltpu.SMEM),
                  row(QKV_WIDTH), row(MM_WIDTH), row(GATE_WIDTH),
                  _const_spec(gbias.shape), _const_spec(normg.shape)] + cast_in_specs,
        out_specs=[row(ATTN_Q_WIDTH), row(MLSTM_V_WIDTH)] + cast_out_specs,
        out_shape=[jax.ShapeDtypeStruct((t, ATTN_Q_WIDTH), BF16), jax.ShapeDtypeStruct((t, MLSTM_V_WIDTH), BF16)]
                  + cast_shapes,
        scratch_shapes=[
            pltpu.VMEM((2, ATTN_HEADS, CHUNK, CHUNK), F32),
            pltpu.VMEM((2, CHUNK, ATTN_KV_WIDTH), BF16),
            pltpu.VMEM((MLSTM_HEADS // 2, 2 * MLSTM_QK_DIM, 2 * MLSTM_V_DIM), F32),
            pltpu.VMEM((8, LANES), F32),
        ],
        compiler_params=pltpu.CompilerParams(dimension_semantics=("arbitrary", "arbitrary"),
                                             vmem_limit_bytes=VMEM_LIMIT_BYTES),
        name="mixer",
    )(sinks, qkv, mm, gt, gbias, normg, *[w for w, _ in cast_weights])


def _tail_kernel(x_ref, ao_ref, hm_ref, gg_ref, wab_ref, wmb_ref, wo_ref, g2_ref, wg_ref, wu_ref, wd_ref,
                 gf_ref, out_ref, *, ff_split):
    ya = _dot(ao_ref[...], wab_ref[...])
    ym = _dot(hm_ref[...], wmb_ref[...])
    ga = jax.nn.sigmoid(gg_ref[:, 0:D_MODEL].astype(F32))
    gm = jax.nn.sigmoid(gg_ref[:, D_MODEL:2 * D_MODEL].astype(F32))
    z = (ga * ya + gm * ym).astype(BF16)
    x1 = x_ref[...] + _dot(z, wo_ref[...])
    f = (x1 * _rms_scale(x1) * g2_ref[...]).astype(BF16)
    n_tiles = wg_ref.shape[1] // MXU_WIDTH
    bounds = [MXU_WIDTH * ((n_tiles * s + ff_split - 1) // ff_split) for s in range(ff_split + 1)]
    x2 = x1
    for s in range(ff_split):
        cols = slice(bounds[s], bounds[s + 1])
        gte = _dot(f, wg_ref[:, cols])
        up = _dot(f, wu_ref[:, cols])
        hh = (gte * jax.nn.sigmoid(gte) * up).astype(BF16)
        x2 = x2 + _dot(hh, wd_ref[cols, :])
    out_ref[...] = x2 * _rms_scale(x2) * gf_ref[...]


def _tail(x2d, ao, hm, gg, wab, wmb, wo, g2, wg, wu, wd, gf, tm, ff_split):
    t = x2d.shape[0]
    row = lambda w: pl.BlockSpec((tm, w), lambda i: (i, 0))
    consts = (wab, wmb, wo, g2, wg, wu, wd, gf)
    return pl.pallas_call(
        functools.partial(_tail_kernel, ff_split=ff_split),
        grid=(t // tm,),
        in_specs=[row(D_MODEL), row(ATTN_Q_WIDTH), row(MLSTM_V_WIDTH), row(MERGE_WIDTH)]
                 + [_const_spec(c.shape) for c in consts],
        out_specs=row(D_MODEL),
        out_shape=jax.ShapeDtypeStruct((t, D_MODEL), F32),
        compiler_params=pltpu.CompilerParams(dimension_semantics=("arbitrary",),
                                             vmem_limit_bytes=VMEM_LIMIT_BYTES),
        name="tail",
    )(x2d, ao, hm, gg, *consts)


def _layer(x2d, batch, seq, norm1_g, w_in, conv_w, conv_b, i_bias, f_bias, mlstm_norm_g, attn_sinks,
           w_attn_branch, w_mlstm_branch, w_out, norm2_g, w_ffn_gate, w_ffn_up, w_ffn_down, out_g):
    order = jnp.asarray(Q_HEAD_ORDER)
    o = 0
    wq = w_in[:, o:o + ATTN_Q_WIDTH]; o += ATTN_Q_WIDTH
    wkv = w_in[:, o:o + 2 * ATTN_KV_WIDTH]; o += 2 * ATTN_KV_WIDTH
    wm_qk = w_in[:, o:o + 2 * MLSTM_QK_WIDTH]; o += 2 * MLSTM_QK_WIDTH
    wm_vo = w_in[:, o:o + 2 * MLSTM_V_WIDTH]; o += 2 * MLSTM_V_WIDTH
    w_if = w_in[:, o:o + 2 * MLSTM_HEADS]; o += 2 * MLSTM_HEADS
    wgg = w_in[:, o:o + MERGE_WIDTH]
    wq = wq.reshape(D_MODEL, ATTN_HEADS, ATTN_HEAD_DIM)[:, order, :].reshape(D_MODEL, ATTN_Q_WIDTH)
    wa = jnp.concatenate([wq, wkv], axis=1)
    wgt = jnp.pad(w_if, ((0, 0), (0, GATE_WIDTH - 2 * MLSTM_HEADS)))
    gbias = jnp.pad(jnp.concatenate([i_bias, f_bias]), (0, GATE_WIDTH - 2 * MLSTM_HEADS)).reshape(1, GATE_WIDTH)

    qkv, mm, gt, gg = _inproj(x2d, norm1_g.reshape(1, D_MODEL), wa.astype(BF16), wm_qk.astype(BF16),
                              wm_vo.astype(BF16), wgt.astype(BF16), wgg.astype(BF16), conv_w,
                              conv_b.reshape(1, -1), tm=INPROJ_TM, seq=seq)
    tail_weights = [(w_attn_branch, Q_HEAD_ORDER), (w_mlstm_branch, None), (w_out, None),
                    (w_ffn_gate, None), (w_ffn_up, None), (w_ffn_down, None)]
    ao, hm, wab, wmb, wo, wg, wu, wd = _mixer(attn_sinks, qkv, mm, gt, gbias,
                                              mlstm_norm_g.reshape(1, MLSTM_V_WIDTH), tail_weights, batch, seq,
                                              nch=MIXER_CHUNKS)
    return _tail(x2d, ao, hm, gg, wab, wmb, wo, norm2_g.reshape(1, D_MODEL), wg, wu, wd,
                 out_g.reshape(1, D_MODEL), tm=TAIL_TM, ff_split=TAIL_FF_SPLIT)


def kernel(x, norm1_g, w_in, conv_w, conv_b, i_bias, f_bias, mlstm_norm_g, attn_sinks, w_attn_branch,
           w_mlstm_branch, w_out, norm2_g, w_ffn_gate, w_ffn_up, w_ffn_down, final_norm_g):
    batch, seq, d = x.shape
    depth = norm1_g.shape[0]
    assert depth == 1 and d == D_MODEL
    assert seq % (MIXER_CHUNKS * CHUNK) == 0 and seq % INPROJ_TM == 0 and (batch * seq) % TAIL_TM == 0
    assert w_ffn_gate.shape[-1] % MXU_WIDTH == 0
    out = _layer(x.reshape(batch * seq, d), batch, seq, norm1_g[0], w_in[0], conv_w[0], conv_b[0], i_bias[0],
                 f_bias[0], mlstm_norm_g[0], attn_sinks[0], w_attn_branch[0], w_mlstm_branch[0], w_out[0],
                 norm2_g[0], w_ffn_gate[0], w_ffn_up[0], w_ffn_down[0], final_norm_g)
    return out.reshape(batch, seq, d)
```

```python
import functools

import jax
import jax.numpy as jnp
from jax import lax
from jax.experimental import pallas as pl
from jax.experimental.pallas import tpu as pltpu

D_MODEL = 1024
ATTN_HEADS = 8
ATTN_KV_HEADS = 2
ATTN_HEAD_DIM = 64
ATTN_GROUP = ATTN_HEADS // ATTN_KV_HEADS
WINDOW = 128
ATTN_Q_WIDTH = ATTN_HEADS * ATTN_HEAD_DIM
ATTN_KV_WIDTH = ATTN_KV_HEADS * ATTN_HEAD_DIM
MLSTM_HEADS = 4
MLSTM_QK_DIM = 64
MLSTM_V_DIM = 128
MLSTM_QK_WIDTH = MLSTM_HEADS * MLSTM_QK_DIM
MLSTM_V_WIDTH = MLSTM_HEADS * MLSTM_V_DIM
CHUNK = 128
CONV_WIDTH = 4
NORM_EPS = 1e-6

LANES = 128
BF16_SUBLANES = 16
MXU_WIDTH = 256
HALF = LANES // 2
CONV_PAD = 8
GATE_ROWS = 16

INPROJ_TM = 1024
INPROJ_ROW_BLOCKS = 2
MIXER_CHUNKS = 8
MIXER_STAGE_LAGS = (0, 1, 2, 3, 4, 4)
TAIL_TM = 512
TAIL_FF_SPLIT = 2
VMEM_LIMIT_BYTES = 56 * 1024 * 1024

QKV_WIDTH = ATTN_Q_WIDTH + 2 * ATTN_KV_WIDTH
MM_WIDTH = 2 * MLSTM_QK_WIDTH + 2 * MLSTM_V_WIDTH
GATE_WIDTH = LANES
MERGE_WIDTH = 2 * D_MODEL

Q_HEAD_ORDER = tuple(h * ATTN_GROUP + g for g in range(ATTN_GROUP) for h in range(ATTN_KV_HEADS))

BF16 = jnp.bfloat16
F32 = jnp.float32
NEG_INF = float("-inf")


def _dot(a, b):
    return jnp.dot(a, b, preferred_element_type=F32)


def _dot_nt(a, b):
    return lax.dot_general(a, b, (((1,), (1,)), ((), ())), preferred_element_type=F32)


def _rms_scale(x):
    return lax.rsqrt(jnp.mean(x * x, axis=-1, keepdims=True) + NORM_EPS)


def _const_spec(shape):
    nd = len(shape)
    return pl.BlockSpec(shape, lambda *_: (0,) * nd, pipeline_mode=pl.Buffered(1))


def _inproj_kernel(x_ref, g_ref, wa_ref, wqk_ref, wvo_ref, wgt_ref, wgg_ref, convw_ref, convb_ref,
                   qkv_ref, mm_ref, gt_ref, gg_ref, conv_ref, *, tiles_per_seq):
    tm = x_ref.shape[0]

    @pl.when(pl.program_id(0) % tiles_per_seq == 0)
    def _():
        conv_ref[0:CONV_PAD, :] = jnp.zeros((CONV_PAD, 2 * MLSTM_QK_WIDTH), F32)

    q_scale = ATTN_HEAD_DIM ** -0.5
    lane = lax.broadcasted_iota(jnp.int32, (1, 2 * MLSTM_QK_WIDTH), 1)
    k_scale = jnp.where(lane < MLSTM_QK_WIDTH, 1.0, MLSTM_QK_DIM ** -0.5)
    rb = tm // INPROJ_ROW_BLOCKS
    for r in range(INPROJ_ROW_BLOCKS):
        rows = slice(r * rb, (r + 1) * rb)
        x = x_ref[rows, :]
        u = (x * _rms_scale(x) * g_ref[...]).astype(BF16)
        conv_ref[CONV_PAD + r * rb:CONV_PAD + (r + 1) * rb, :] = _dot(u, wqk_ref[...])
        qkv_ref[rows, 0:ATTN_Q_WIDTH] = (_dot(u, wa_ref[:, 0:ATTN_Q_WIDTH]) * q_scale).astype(BF16)
        qkv_ref[rows, ATTN_Q_WIDTH:QKV_WIDTH] = _dot(u, wa_ref[:, ATTN_Q_WIDTH:QKV_WIDTH]).astype(BF16)
        mm_ref[rows, 2 * MLSTM_QK_WIDTH:MM_WIDTH] = _dot(u, wvo_ref[...]).astype(BF16)
        gt_ref[rows, :] = _dot(u, wgt_ref[...])
        gg_ref[rows, :] = _dot(u, wgg_ref[...]).astype(BF16)
        xe = conv_ref[r * rb:r * rb + CONV_PAD + rb, :]
        acc = xe * convw_ref[0:1, :]
        for t in range(1, CONV_WIDTH):
            acc = pltpu.roll(acc, 1, axis=0) + xe * convw_ref[t:t + 1, :]
        acc = acc[CONV_PAD:CONV_PAD + rb, :] + convb_ref[...]
        mm_ref[rows, 0:2 * MLSTM_QK_WIDTH] = (acc * jax.nn.sigmoid(acc) * k_scale).astype(BF16)
    conv_ref[0:CONV_PAD, :] = conv_ref[tm:tm + CONV_PAD, :]


def _inproj(x2d, g1, wa, wqk, wvo, wgt, wgg, convw, convb, tm, seq):
    t = x2d.shape[0]
    row = lambda w: pl.BlockSpec((tm, w), lambda i: (i, 0))
    consts = (g1, wa, wqk, wvo, wgt, wgg, convw, convb)
    return pl.pallas_call(
        functools.partial(_inproj_kernel, tiles_per_seq=seq // tm),
        grid=(t // tm,),
        in_specs=[row(D_MODEL)] + [_const_spec(c.shape) for c in consts],
        out_specs=[row(QKV_WIDTH), row(MM_WIDTH), row(GATE_WIDTH), row(MERGE_WIDTH)],
        out_shape=[jax.ShapeDtypeStruct((t, QKV_WIDTH), BF16), jax.ShapeDtypeStruct((t, MM_WIDTH), BF16),
                   jax.ShapeDtypeStruct((t, GATE_WIDTH), F32), jax.ShapeDtypeStruct((t, MERGE_WIDTH), BF16)],
        scratch_shapes=[pltpu.VMEM((CONV_PAD + tm, 2 * MLSTM_QK_WIDTH), F32)],
        compiler_params=pltpu.CompilerParams(dimension_semantics=("arbitrary",),
                                             vmem_limit_bytes=VMEM_LIMIT_BYTES),
        name="inproj",
    )(x2d, *consts)


def _log_sigmoid(x):
    return jnp.minimum(x, 0.0) - jnp.log1p(jnp.exp(-jnp.abs(x)))


def _split3(x):
    hi = x.astype(BF16)
    r1 = x - hi.astype(F32)
    mid = r1.astype(BF16)
    lo = (r1 - mid.astype(F32)).astype(BF16)
    return hi, mid, lo


def _mixer_kernel(sinks_ref, qkv_ref, mm_ref, gt_ref, gbias_ref, normg_ref, *rest, nch, n_cast):
    cast_in, (ao_ref, hm_ref), rest = rest[:n_cast], rest[n_cast:n_cast + 2], rest[n_cast + 2:]
    cast_out, (btab_ref, kvp_ref, cst_ref, mst_ref) = rest[:n_cast], rest[n_cast:]
    b_idx = pl.program_id(0)
    j_idx = pl.program_id(1)

    for src, dst in zip(cast_in, cast_out):
        dst[...] = src[...].astype(BF16)

    row_c = lax.broadcasted_iota(jnp.int32, (CHUNK, CHUNK), 0)
    col_c = lax.broadcasted_iota(jnp.int32, (CHUNK, CHUNK), 1)

    @pl.when((b_idx == 0) & (j_idx == 0))
    def _():
        cur = row_c <= col_c
        dist = jnp.where(cur, col_c - row_c, col_c - row_c + WINDOW).astype(F32)
        for j in range(ATTN_HEADS):
            slope = 2.0 ** (-8.0 * (j + 1) / ATTN_HEADS)
            btab_ref[0, j] = -slope * dist
            btab_ref[1, j] = jnp.where(cur, -slope * dist, NEG_INF)

    @pl.when(j_idx == 0)
    def _():
        kvp_ref[...] = jnp.zeros_like(kvp_ref)
        cst_ref[...] = jnp.zeros_like(cst_ref)
        mst_ref[...] = jnp.zeros_like(mst_ref)

    cur_t = row_c <= col_c
    causal = col_c <= row_c
    triu_b = cur_t.astype(BF16)
    eye_b = (row_c == col_c).astype(BF16)

    def transpose_bf16(x):
        return _dot_nt(eye_b, x)

    top_rows = row_c < HALF
    top_rows_w = lax.broadcasted_iota(jnp.int32, (CHUNK, 2 * LANES), 0) < HALF
    left_lanes = col_c < HALF
    ones_v = jnp.ones((CHUNK, LANES), BF16)
    zero_b = jnp.zeros((), BF16)
    one_b = jnp.ones((), BF16)
    first = (j_idx == 0).astype(jnp.int32)

    def kv_sides(k_blk, v_blk):
        vt = transpose_bf16(v_blk).astype(BF16)
        k_sides = (jnp.where(left_lanes, k_blk, zero_b), jnp.where(left_lanes, zero_b, k_blk))
        vt_sides = (jnp.where(top_rows, vt, one_b), jnp.where(top_rows, one_b, vt))
        return k_sides, vt_sides

    k_off = ATTN_Q_WIDTH
    v_off = ATTN_Q_WIDTH + ATTN_KV_WIDTH
    prev_sides = kv_sides(kvp_ref[0], kvp_ref[1])
    last = slice((nch - 1) * CHUNK, nch * CHUNK)
    kvp_ref[0] = qkv_ref[last, k_off:k_off + ATTN_KV_WIDTH]
    kvp_ref[1] = qkv_ref[last, v_off:v_off + ATTN_KV_WIDTH]

    m_rows = [mst_ref[h:h + 1, :] for h in range(MLSTM_HEADS)]
    c_pairs = [cst_ref[pr] for pr in range(MLSTM_HEADS // 2)]

    npair = ATTN_GROUP
    rows_of = lambda c: slice(c * CHUNK, (c + 1) * CHUNK)
    row8 = lax.broadcasted_iota(jnp.int32, (GATE_ROWS, LANES), 0)
    is_f = (row8 >= MLSTM_HEADS) & (row8 < 2 * MLSTM_HEADS)
    sides = {-1: prev_sides}
    gates, s_t, sc2s, kt_hs, p_t, mx_rows, intra_w, o_t, intra = {}, {}, {}, {}, {}, {}, {}, {}, {}

    def stage_prepare(c):
        rows = rows_of(c)
        sides[c] = kv_sides(qkv_ref[rows, k_off:k_off + ATTN_KV_WIDTH], qkv_ref[rows, v_off:v_off + ATTN_KV_WIDTH])
        g8 = (gt_ref[rows, :] + gbias_ref[...]).T[0:GATE_ROWS, :]
        lf_parts = _split3(jnp.where(is_f, _log_sigmoid(g8), 0.0))
        b8 = sum(_dot(part, triu_b) for part in lf_parts)
        b8_t = jnp.concatenate([b8, jnp.zeros((CHUNK - GATE_ROWS, LANES), F32)], axis=0).T
        b_bc_all = jnp.concatenate(
            [jnp.broadcast_to(b8_t[:, MLSTM_HEADS + h:MLSTM_HEADS + h + 1], (CHUNK, LANES))
             for h in range(MLSTM_HEADS)], axis=1)
        mk = mm_ref[rows, MLSTM_QK_WIDTH:2 * MLSTM_QK_WIDTH]
        kt = transpose_bf16(jnp.concatenate([mk[:, 0:LANES], mk[:, LANES:2 * LANES]], axis=0))
        gates[c] = (g8, b8, b_bc_all, kt)

    def stage_scores(c):
        q = qkv_ref[rows_of(c), 0:ATTN_Q_WIDTH]
        q_pairs = [q[:, p * LANES:(p + 1) * LANES] for p in range(npair)]
        for side in range(ATTN_KV_HEADS):
            k_side = jnp.concatenate([sides[c - 1][0][side], sides[c][0][side]], axis=0)
            for pp in range(npair // 2):
                q2 = jnp.concatenate([q_pairs[2 * pp], q_pairs[2 * pp + 1]], axis=0)
                s2 = _dot_nt(k_side, q2)
                s_t[(c, side, 2 * pp)] = s2[:, 0:LANES]
                s_t[(c, side, 2 * pp + 1)] = s2[:, LANES:2 * LANES]
        kt = gates[c][3]
        for pr in range(MLSTM_HEADS // 2):
            qpair = mm_ref[rows_of(c), pr * LANES:(pr + 1) * LANES]
            kt_pair = kt[:, pr * LANES:(pr + 1) * LANES]
            kt_h = (jnp.where(top_rows, kt_pair, 0.0), jnp.where(top_rows, 0.0, kt_pair))
            kt_hs[(c, pr)] = kt_h
            sc2s[(c, pr)] = _dot(qpair, jnp.concatenate([kt_h[0].astype(BF16), kt_h[1].astype(BF16)], axis=1))

    def stage_weights(c):
        variant = first if c == 0 else 0
        for side in range(ATTN_KV_HEADS):
            for p in range(npair):
                j = side * ATTN_GROUP + p
                s2 = s_t.pop((c, side, p))
                comb = jnp.where(cur_t, s2[CHUNK:2 * CHUNK], s2[0:CHUNK]) + btab_ref[variant, j]
                mx = jnp.maximum(jnp.max(comb, axis=0, keepdims=True), sinks_ref[j])
                e = jnp.exp(comb - mx)
                p_t[(c, side, p)] = jnp.concatenate([jnp.where(cur_t, 0.0, e), jnp.where(cur_t, e, 0.0)],
                                                    axis=0).astype(BF16)
                mx_rows[(c, side, p)] = mx
        g8, b8, b_bc_all, _ = gates[c]
        for h in range(MLSTM_HEADS):
            i_row = g8[h:h + 1, :]
            b_row = b8[MLSTM_HEADS + h:MLSTM_HEADS + h + 1, :]
            b_bc = b_bc_all[:, h * LANES:(h + 1) * LANES]
            dlog = jnp.where(causal, b_bc + (i_row - b_row), NEG_INF)
            mloc = jnp.broadcast_to(jnp.max(dlog, axis=-1, keepdims=True), (CHUNK, LANES))
            wloc = jnp.exp(dlog - mloc)
            scw = (sc2s[(c, h // 2)][:, (h % 2) * LANES:(h % 2 + 1) * LANES] * wloc).astype(BF16)
            b_last = b_bc[CHUNK - 1:CHUNK, :]
            mloc_last = mloc[CHUNK - 1:CHUNK, :]
            u_row = jnp.exp(dlog[CHUNK - 1:CHUNK, :] - mloc_last)
            kw = (kt_hs[(c, h // 2)][h % 2] * u_row).astype(BF16)
            intra_w[(c, h)] = (b_bc, mloc, scw, b_last, mloc_last, kw)

    def stage_values(c):
        for side in range(ATTN_KV_HEADS):
            vt_side = jnp.concatenate([sides[c - 1][1][side], sides[c][1][side]], axis=1)
            for pp in range(npair // 2):
                p2 = jnp.concatenate([p_t.pop((c, side, 2 * pp)), p_t.pop((c, side, 2 * pp + 1))], axis=1)
                o2 = _dot(vt_side, p2)
                o_t[(c, side, 2 * pp)] = o2[:, 0:LANES]
                o_t[(c, side, 2 * pp + 1)] = o2[:, LANES:2 * LANES]
        for h in range(MLSTM_HEADS):
            b_bc, mloc, scw, b_last, mloc_last, kw = intra_w.pop((c, h))
            v_h = mm_ref[rows_of(c), 2 * MLSTM_QK_WIDTH + h * MLSTM_V_DIM:2 * MLSTM_QK_WIDTH + (h + 1) * MLSTM_V_DIM]
            v_aug = jnp.concatenate([v_h, ones_v], axis=1)
            intra[(c, h)] = (b_bc, mloc, _dot(scw, v_aug), b_last, mloc_last, _dot(kw, v_aug))

    def stage_attn_out(c):
        out_t = []
        for p in range(npair):
            scaled = []
            for side in range(ATTN_KV_HEADS):
                j = side * ATTN_GROUP + p
                o = o_t.pop((c, side, p))
                sums = o[HALF:HALF + 1, :] if side == 0 else o[0:1, :]
                den = sums + jnp.exp(sinks_ref[j] - mx_rows.pop((c, side, p)))
                scaled.append(o * (1.0 / den))
            out_t.append(jnp.where(top_rows, scaled[0], scaled[1]).astype(BF16))
        for pp in range(npair // 2):
            both = jnp.concatenate([out_t[2 * pp], out_t[2 * pp + 1]], axis=0)
            ao_ref[rows_of(c), 2 * pp * LANES:(2 * pp + 2) * LANES] = transpose_bf16(both).astype(BF16)

    def stage_recurrence(c):
        rows = rows_of(c)
        for pr in range(MLSTM_HEADS // 2):
            c_pair = c_pairs[pr]
            qpair = mm_ref[rows, pr * LANES:(pr + 1) * LANES]
            decs, eus, upds = [], [], []
            for half in range(2):
                h = 2 * pr + half
                b_bc, mloc, num_intra, b_last, mloc_last, upd = intra.pop((c, h))
                m_prev = m_rows[h]
                sel_w = top_rows_w if half == 0 else jnp.logical_not(top_rows_w)
                inter = _dot(qpair, jnp.where(sel_w, c_pair, 0.0).astype(BF16))
                a = b_bc + m_prev
                m_t = jnp.maximum(a, mloc)
                e_t = jnp.exp(mloc - m_t)
                w_inter = jnp.exp(a - m_t)
                numv = e_t * num_intra[:, 0:LANES] + w_inter * inter[:, 0:LANES]
                nq = e_t * num_intra[:, LANES:2 * LANES] + w_inter * inter[:, LANES:2 * LANES]
                hv = numv / jnp.maximum(jnp.abs(nq), jnp.exp(-m_t))
                hn = hv * _rms_scale(hv) * normg_ref[:, h * MLSTM_V_DIM:(h + 1) * MLSTM_V_DIM]
                og = mm_ref[rows, 2 * MLSTM_QK_WIDTH + MLSTM_V_WIDTH + h * MLSTM_V_DIM:
                            2 * MLSTM_QK_WIDTH + MLSTM_V_WIDTH + (h + 1) * MLSTM_V_DIM].astype(F32)
                hm_ref[rows, h * MLSTM_V_DIM:(h + 1) * MLSTM_V_DIM] = (hn * jax.nn.sigmoid(og)).astype(BF16)
                m_new = jnp.maximum(b_last + m_prev, mloc_last)
                decs.append(jnp.exp(b_last + m_prev - m_new))
                eus.append(jnp.exp(mloc_last - m_new))
                upds.append(upd)
                m_rows[h] = m_new
            wide = lambda r: jnp.concatenate([r, r], axis=1)
            c_pairs[pr] = (jnp.where(top_rows_w, wide(decs[0]), wide(decs[1])) * c_pair
                           + wide(eus[0]) * upds[0] + wide(eus[1]) * upds[1])

    stages = (stage_prepare, stage_scores, stage_weights, stage_values, stage_attn_out, stage_recurrence)
    for tick in range(nch + max(MIXER_STAGE_LAGS)):
        for stage, lag in zip(stages, MIXER_STAGE_LAGS):
            c = tick - lag
            if 0 <= c < nch:
                stage(c)

    for h in range(MLSTM_HEADS):
        mst_ref[h:h + 1, :] = m_rows[h]
    for pr in range(MLSTM_HEADS // 2):
        cst_ref[pr] = c_pairs[pr]


def _cast_slab_rows(n_rows, n_steps):
    slab = BF16_SUBLANES * pl.cdiv(pl.cdiv(n_rows, n_steps), BF16_SUBLANES)
    while n_rows % slab:
        slab += BF16_SUBLANES
    return slab


def _mixer(sinks, qkv, mm, gt, gbias, normg, cast_weights, batch, seq, nch):
    tm = nch * CHUNK
    nj = seq // tm
    t = batch * seq
    row = lambda w: pl.BlockSpec((tm, w), lambda b, j: (b * nj + j, 0))
    cast_in_specs, cast_out_specs, cast_shapes = [], [], []
    for w, block_perm in cast_weights:
        n_rows, n_cols = w.shape
        slab = n_rows // len(block_perm) if block_perm else _cast_slab_rows(n_rows, batch * nj)
        assert slab % BF16_SUBLANES == 0 and n_rows // slab <= batch * nj
        last = n_rows // slab - 1
        dst_map = lambda b, j, last=last: (jnp.minimum(b * nj + j, last), 0)
        if block_perm:
            assert tuple(block_perm) == tuple((s % 2) * (len(block_perm) // 2) + s // 2
                                              for s in range(len(block_perm)))
            stride = len(block_perm) // 2
            src_map = lambda b, j, last=last, stride=stride: (
                (jnp.minimum(b * nj + j, last) % 2) * stride + jnp.minimum(b * nj + j, last) // 2, 0)
        else:
            src_map = dst_map
        cast_in_specs.append(pl.BlockSpec((slab, n_cols), src_map))
        cast_out_specs.append(pl.BlockSpec((slab, n_cols), dst_map))
        cast_shapes.append(jax.ShapeDtypeStruct(w.shape, BF16))
    return pl.pallas_call(
        functools.partial(_mixer_kernel, nch=nch, n_cast=len(cast_weights)),
        grid=(batch, nj),
        in_specs=[pl.BlockSpec(memory_space=pltpu.SMEM),
                  row(QKV_WIDTH), row(MM_WIDTH), row(GATE_WIDTH),
                  _const_spec(gbias.shape), _const_spec(normg.shape)] + cast_in_specs,
        out_specs=[row(ATTN_Q_WIDTH), row(MLSTM_V_WIDTH)] + cast_out_specs,
        out_shape=[jax.ShapeDtypeStruct((t, ATTN_Q_WIDTH), BF16), jax.ShapeDtypeStruct((t, MLSTM_V_WIDTH), BF16)]
                  + cast_shapes,
        scratch_shapes=[
            pltpu.VMEM((2, ATTN_HEADS, CHUNK, CHUNK), F32),
            pltpu.VMEM((2, CHUNK, ATTN_KV_WIDTH), BF16),
            pltpu.VMEM((MLSTM_HEADS // 2, 2 * MLSTM_QK_DIM, 2 * MLSTM_V_DIM), F32),
            pltpu.VMEM((8, LANES), F32),
        ],
        compiler_params=pltpu.CompilerParams(dimension_semantics=("arbitrary", "arbitrary"),
                                             vmem_limit_bytes=VMEM_LIMIT_BYTES),
        name="mixer",
    )(sinks, qkv, mm, gt, gbias, normg, *[w for w, _ in cast_weights])


def _tail_kernel(x_ref, ao_ref, hm_ref, gg_ref, wab_ref, wmb_ref, wo_ref, g2_ref, wg_ref, wu_ref, wd_ref,
                 gf_ref, out_ref, *, ff_split):
    ya = _dot(ao_ref[...], wab_ref[...])
    ym = _dot(hm_ref[...], wmb_ref[...])
    ga = jax.nn.sigmoid(gg_ref[:, 0:D_MODEL].astype(F32))
    gm = jax.nn.sigmoid(gg_ref[:, D_MODEL:2 * D_MODEL].astype(F32))
    z = (ga * ya + gm * ym).astype(BF16)
    x1 = x_ref[...] + _dot(z, wo_ref[...])
    f = (x1 * _rms_scale(x1) * g2_ref[...]).astype(BF16)
    n_tiles = wg_ref.shape[1] // MXU_WIDTH
    bounds = [MXU_WIDTH * ((n_tiles * s + ff_split - 1) // ff_split) for s in range(ff_split + 1)]
    x2 = x1
    for s in range(ff_split):
        cols = slice(bounds[s], bounds[s + 1])
        gte = _dot(f, wg_ref[:, cols])
        up = _dot(f, wu_ref[:, cols])
        hh = (gte * jax.nn.sigmoid(gte) * up).astype(BF16)
        x2 = x2 + _dot(hh, wd_ref[cols, :])
    out_ref[...] = x2 * _rms_scale(x2) * gf_ref[...]


def _tail(x2d, ao, hm, gg, wab, wmb, wo, g2, wg, wu, wd, gf, tm, ff_split):
    t = x2d.shape[0]
    row = lambda w: pl.BlockSpec((tm, w), lambda i: (i, 0))
    consts = (wab, wmb, wo, g2, wg, wu, wd, gf)
    return pl.pallas_call(
        functools.partial(_tail_kernel, ff_split=ff_split),
        grid=(t // tm,),
        in_specs=[row(D_MODEL), row(ATTN_Q_WIDTH), row(MLSTM_V_WIDTH), row(MERGE_WIDTH)]
                 + [_const_spec(c.shape) for c in consts],
        out_specs=row(D_MODEL),
        out_shape=jax.ShapeDtypeStruct((t, D_MODEL), F32),
        compiler_params=pltpu.CompilerParams(dimension_semantics=("arbitrary",),
                                             vmem_limit_bytes=VMEM_LIMIT_BYTES),
        name="tail",
    )(x2d, ao, hm, gg, *consts)


def _layer(x2d, batch, seq, norm1_g, w_in, conv_w, conv_b, i_bias, f_bias, mlstm_norm_g, attn_sinks,
           w_attn_branch, w_mlstm_branch, w_out, norm2_g, w_ffn_gate, w_ffn_up, w_ffn_down, out_g):
    order = jnp.asarray(Q_HEAD_ORDER)
    o = 0
    wq = w_in[:, o:o + ATTN_Q_WIDTH]; o += ATTN_Q_WIDTH
    wkv = w_in[:, o:o + 2 * ATTN_KV_WIDTH]; o += 2 * ATTN_KV_WIDTH
    wm_qk = w_in[:, o:o + 2 * MLSTM_QK_WIDTH]; o += 2 * MLSTM_QK_WIDTH
    wm_vo = w_in[:, o:o + 2 * MLSTM_V_WIDTH]; o += 2 * MLSTM_V_WIDTH
    w_if = w_in[:, o:o + 2 * MLSTM_HEADS]; o += 2 * MLSTM_HEADS
    wgg = w_in[:, o:o + MERGE_WIDTH]
    wq = wq.reshape(D_MODEL, ATTN_HEADS, ATTN_HEAD_DIM)[:, order, :].reshape(D_MODEL, ATTN_Q_WIDTH)
    wa = jnp.concatenate([wq, wkv], axis=1)
    wgt = jnp.pad(w_if, ((0, 0), (0, GATE_WIDTH - 2 * MLSTM_HEADS)))
    gbias = jnp.pad(jnp.concatenate([i_bias, f_bias]), (0, GATE_WIDTH - 2 * MLSTM_HEADS)).reshape(1, GATE_WIDTH)

    qkv, mm, gt, gg = _inproj(x2d, norm1_g.reshape(1, D_MODEL), wa.astype(BF16), wm_qk.astype(BF16),
                              wm_vo.astype(BF16), wgt.astype(BF16), wgg.astype(BF16), conv_w,
                              conv_b.reshape(1, -1), tm=INPROJ_TM, seq=seq)
    tail_weights = [(w_attn_branch, Q_HEAD_ORDER), (w_mlstm_branch, None), (w_out, None),
                    (w_ffn_gate, None), (w_ffn_up, None), (w_ffn_down, None)]
    ao, hm, wab, wmb, wo, wg, wu, wd = _mixer(attn_sinks, qkv, mm, gt, gbias,
                                              mlstm_norm_g.reshape(1, MLSTM_V_WIDTH), tail_weights, batch, seq,
                                              nch=MIXER_CHUNKS)
    return _tail(x2d, ao, hm, gg, wab, wmb, wo, norm2_g.reshape(1, D_MODEL), wg, wu, wd,
                 out_g.reshape(1, D_MODEL), tm=TAIL_TM, ff_split=TAIL_FF_SPLIT)


def kernel(x, norm1_g, w_in, conv_w, conv_b, i_bias, f_bias, mlstm_norm_g, attn_sinks, w_attn_branch,
           w_mlstm_branch, w_out, norm2_g, w_ffn_gate, w_ffn_up, w_ffn_down, final_norm_g):
    batch, seq, d = x.shape
    depth = norm1_g.shape[0]
    assert depth == 1 and d == D_MODEL
    assert seq % (MIXER_CHUNKS * CHUNK) == 0 and seq % INPROJ_TM == 0 and (batch * seq) % TAIL_TM == 0
    assert w_ffn_gate.shape[-1] % MXU_WIDTH == 0
    out = _layer(x.reshape(batch * seq, d), batch, seq, norm1_g[0], w_in[0], conv_w[0], conv_b[0], i_bias[0],
                 f_bias[0], mlstm_norm_g[0], attn_sinks[0], w_attn_branch[0], w_mlstm_branch[0], w_out[0],
                 norm2_g[0], w_ffn_gate[0], w_ffn_up[0], w_ffn_down[0], final_norm_g)
    return out.reshape(batch, seq, d)
```

```python
import functools

import jax
import jax.numpy as jnp
from jax import lax
from jax.experimental import pallas as pl
from jax.experimental.pallas import tpu as pltpu

D_MODEL = 1024
ATTN_HEADS = 8
ATTN_KV_HEADS = 2
ATTN_HEAD_DIM = 64
ATTN_GROUP = ATTN_HEADS // ATTN_KV_HEADS
WINDOW = 128
ATTN_Q_WIDTH = ATTN_HEADS * ATTN_HEAD_DIM
ATTN_KV_WIDTH = ATTN_KV_HEADS * ATTN_HEAD_DIM
MLSTM_HEADS = 4
MLSTM_QK_DIM = 64
MLSTM_V_DIM = 128
MLSTM_QK_WIDTH = MLSTM_HEADS * MLSTM_QK_DIM
MLSTM_V_WIDTH = MLSTM_HEADS * MLSTM_V_DIM
CHUNK = 128
CONV_WIDTH = 4
NORM_EPS = 1e-6

LANES = 128
BF16_SUBLANES = 16
MXU_WIDTH = 256
HALF = LANES // 2
CONV_PAD = 8
GATE_ROWS = 16
NQ_ROWS = BF16_SUBLANES
AUG_ROWS = MLSTM_V_DIM + NQ_ROWS

INPROJ_TM = 1024
INPROJ_ROW_BLOCKS = 2
MIXER_CHUNKS = 8
MIXER_STAGE_LAGS = (0, 1, 2, 3, 4, 4)
TAIL_TM = 512
TAIL_FF_SPLIT = 2
VMEM_LIMIT_BYTES = 56 * 1024 * 1024

QKV_WIDTH = ATTN_Q_WIDTH + 2 * ATTN_KV_WIDTH
MM_WIDTH = 2 * MLSTM_QK_WIDTH + 2 * MLSTM_V_WIDTH
GATE_WIDTH = LANES
MERGE_WIDTH = 2 * D_MODEL

Q_HEAD_ORDER = tuple(h * ATTN_GROUP + g for g in range(ATTN_GROUP) for h in range(ATTN_KV_HEADS))

BF16 = jnp.bfloat16
F32 = jnp.float32
NEG_INF = float("-inf")


def _dot(a, b):
    return jnp.dot(a, b, preferred_element_type=F32)


def _dot_nt(a, b):
    return lax.dot_general(a, b, (((1,), (1,)), ((), ())), preferred_element_type=F32)


def _rms_scale(x):
    return lax.rsqrt(jnp.mean(x * x, axis=-1, keepdims=True) + NORM_EPS)


def _const_spec(shape):
    nd = len(shape)
    return pl.BlockSpec(shape, lambda *_: (0,) * nd, pipeline_mode=pl.Buffered(1))


def _inproj_kernel(x_ref, g_ref, wa_ref, wqk_ref, wvo_ref, wgt_ref, wgg_ref, convw_ref, convb_ref,
                   qkv_ref, mm_ref, gt_ref, gg_ref, conv_ref, *, tiles_per_seq):
    tm = x_ref.shape[0]

    @pl.when(pl.program_id(0) % tiles_per_seq == 0)
    def _():
        conv_ref[0:CONV_PAD, :] = jnp.zeros((CONV_PAD, 2 * MLSTM_QK_WIDTH), F32)

    q_scale = ATTN_HEAD_DIM ** -0.5
    lane = lax.broadcasted_iota(jnp.int32, (1, 2 * MLSTM_QK_WIDTH), 1)
    k_scale = jnp.where(lane < MLSTM_QK_WIDTH, 1.0, MLSTM_QK_DIM ** -0.5)
    rb = tm // INPROJ_ROW_BLOCKS
    for r in range(INPROJ_ROW_BLOCKS):
        rows = slice(r * rb, (r + 1) * rb)
        x = x_ref[rows, :]
        u = (x * _rms_scale(x) * g_ref[...]).astype(BF16)
        conv_ref[CONV_PAD + r * rb:CONV_PAD + (r + 1) * rb, :] = _dot(u, wqk_ref[...])
        qkv_ref[rows, 0:ATTN_Q_WIDTH] = (_dot(u, wa_ref[:, 0:ATTN_Q_WIDTH]) * q_scale).astype(BF16)
        qkv_ref[rows, ATTN_Q_WIDTH:QKV_WIDTH] = _dot(u, wa_ref[:, ATTN_Q_WIDTH:QKV_WIDTH]).astype(BF16)
        mm_ref[rows, 2 * MLSTM_QK_WIDTH:MM_WIDTH] = _dot(u, wvo_ref[...]).astype(BF16)
        gt_ref[rows, :] = _dot(u, wgt_ref[...])
        gg_ref[rows, :] = _dot(u, wgg_ref[...]).astype(BF16)
        xe = conv_ref[r * rb:r * rb + CONV_PAD + rb, :]
        acc = xe * convw_ref[0:1, :]
        for t in range(1, CONV_WIDTH):
            acc = pltpu.roll(acc, 1, axis=0) + xe * convw_ref[t:t + 1, :]
        acc = acc[CONV_PAD:CONV_PAD + rb, :] + convb_ref[...]
        mm_ref[rows, 0:2 * MLSTM_QK_WIDTH] = (acc * jax.nn.sigmoid(acc) * k_scale).astype(BF16)
    conv_ref[0:CONV_PAD, :] = conv_ref[tm:tm + CONV_PAD, :]


def _inproj(x2d, g1, wa, wqk, wvo, wgt, wgg, convw, convb, tm, seq):
    t = x2d.shape[0]
    row = lambda w: pl.BlockSpec((tm, w), lambda i: (i, 0))
    consts = (g1, wa, wqk, wvo, wgt, wgg, convw, convb)
    return pl.pallas_call(
        functools.partial(_inproj_kernel, tiles_per_seq=seq // tm),
        grid=(t // tm,),
        in_specs=[row(D_MODEL)] + [_const_spec(c.shape) for c in consts],
        out_specs=[row(QKV_WIDTH), row(MM_WIDTH), row(GATE_WIDTH), row(MERGE_WIDTH)],
        out_shape=[jax.ShapeDtypeStruct((t, QKV_WIDTH), BF16), jax.ShapeDtypeStruct((t, MM_WIDTH), BF16),
                   jax.ShapeDtypeStruct((t, GATE_WIDTH), F32), jax.ShapeDtypeStruct((t, MERGE_WIDTH), BF16)],
        scratch_shapes=[pltpu.VMEM((CONV_PAD + tm, 2 * MLSTM_QK_WIDTH), F32)],
        compiler_params=pltpu.CompilerParams(dimension_semantics=("arbitrary",),
                                             vmem_limit_bytes=VMEM_LIMIT_BYTES),
        name="inproj",
    )(x2d, *consts)


def _log_sigmoid(x):
    return jnp.minimum(x, 0.0) - jnp.log1p(jnp.exp(-jnp.abs(x)))


def _split3(x):
    hi = x.astype(BF16)
    r1 = x - hi.astype(F32)
    mid = r1.astype(BF16)
    lo = (r1 - mid.astype(F32)).astype(BF16)
    return hi, mid, lo


def _mixer_kernel(sinks_ref, qkv_ref, mm_ref, gt_ref, gbias_ref, normg_ref, *rest, nch, n_cast):
    cast_in, (ao_ref, hm_ref), rest = rest[:n_cast], rest[n_cast:n_cast + 2], rest[n_cast + 2:]
    cast_out, (btab_ref, kvp_ref, cst_ref, mst_ref, gcol_ref) = rest[:n_cast], rest[n_cast:]
    b_idx = pl.program_id(0)
    j_idx = pl.program_id(1)

    for src, dst in zip(cast_in, cast_out):
        dst[...] = src[...].astype(BF16)

    row_c = lax.broadcasted_iota(jnp.int32, (CHUNK, CHUNK), 0)
    col_c = lax.broadcasted_iota(jnp.int32, (CHUNK, CHUNK), 1)

    @pl.when((b_idx == 0) & (j_idx == 0))
    def _():
        cur = row_c <= col_c
        dist = jnp.where(cur, col_c - row_c, col_c - row_c + WINDOW).astype(F32)
        for j in range(ATTN_HEADS):
            slope = 2.0 ** (-8.0 * (j + 1) / ATTN_HEADS)
            btab_ref[0, j] = -slope * dist
            btab_ref[1, j] = jnp.where(cur, -slope * dist, NEG_INF)
        for h in range(MLSTM_HEADS):
            g_row = normg_ref[:, h * MLSTM_V_DIM:(h + 1) * MLSTM_V_DIM]
            gcol_ref[h] = jnp.broadcast_to(g_row, (MLSTM_V_DIM, LANES)).T

    @pl.when(j_idx == 0)
    def _():
        kvp_ref[...] = jnp.zeros_like(kvp_ref)
        cst_ref[...] = jnp.zeros_like(cst_ref)
        mst_ref[...] = jnp.zeros_like(mst_ref)

    cur_t = row_c <= col_c
    triu_b = cur_t.astype(BF16)
    eye_b = (row_c == col_c).astype(BF16)

    def transpose_bf16(x):
        return _dot_nt(eye_b, x)

    top_rows = row_c < HALF
    left_lanes = col_c < HALF
    ones_rows = jnp.ones((NQ_ROWS, LANES), BF16)
    left_row = lax.broadcasted_iota(jnp.int32, (1, LANES), 1) < HALF
    zero_b = jnp.zeros((), BF16)
    one_b = jnp.ones((), BF16)
    first = (j_idx == 0).astype(jnp.int32)

    def kv_sides(k_blk, v_blk):
        vt = transpose_bf16(v_blk).astype(BF16)
        k_sides = (jnp.where(left_lanes, k_blk, zero_b), jnp.where(left_lanes, zero_b, k_blk))
        vt_sides = (jnp.where(top_rows, vt, one_b), jnp.where(top_rows, one_b, vt))
        return k_sides, vt_sides

    k_off = ATTN_Q_WIDTH
    v_off = ATTN_Q_WIDTH + ATTN_KV_WIDTH
    prev_sides = kv_sides(kvp_ref[0], kvp_ref[1])
    last = slice((nch - 1) * CHUNK, nch * CHUNK)
    kvp_ref[0] = qkv_ref[last, k_off:k_off + ATTN_KV_WIDTH]
    kvp_ref[1] = qkv_ref[last, v_off:v_off + ATTN_KV_WIDTH]

    m_rows = [mst_ref[h:h + 1, :] for h in range(MLSTM_HEADS)]
    c_pairs = [cst_ref[pr] for pr in range(MLSTM_HEADS // 2)]

    npair = ATTN_GROUP
    rows_of = lambda c: slice(c * CHUNK, (c + 1) * CHUNK)
    row8 = lax.broadcasted_iota(jnp.int32, (GATE_ROWS, LANES), 0)
    head_rows = row8 < MLSTM_HEADS
    sides = {-1: prev_sides}
    gates, s_t, sc_ts, q_ms, vts, p_t, mx_rows, intra_w, o_t, intra = {}, {}, {}, {}, {}, {}, {}, {}, {}, {}

    def stage_prepare(c):
        rows = rows_of(c)
        sides[c] = kv_sides(qkv_ref[rows, k_off:k_off + ATTN_KV_WIDTH], qkv_ref[rows, v_off:v_off + ATTN_KV_WIDTH])
        g8 = (gt_ref[rows, :] + gbias_ref[...]).T[0:GATE_ROWS, :]
        f8 = pltpu.roll(g8, GATE_ROWS - MLSTM_HEADS, axis=0)
        lf_parts = _split3(jnp.where(head_rows, _log_sigmoid(f8), 0.0))
        b8 = sum(_dot(part, triu_b) for part in lf_parts)
        r8 = jnp.where(head_rows, g8 - b8, 0.0)
        r_t = jnp.concatenate([r8, jnp.zeros((CHUNK - GATE_ROWS, LANES), F32)], axis=0).T
        r_all = jnp.concatenate(
            [jnp.broadcast_to(r_t[:, h:h + 1], (CHUNK, LANES)) for h in range(MLSTM_HEADS)], axis=1)
        gates[c] = (g8, b8, r_all)
        for pr in range(MLSTM_HEADS // 2):
            v0 = 2 * MLSTM_QK_WIDTH + 2 * pr * MLSTM_V_DIM
            v2 = jnp.concatenate([mm_ref[rows, v0:v0 + MLSTM_V_DIM],
                                  mm_ref[rows, v0 + MLSTM_V_DIM:v0 + 2 * MLSTM_V_DIM]], axis=0)
            vts[(c, pr)] = transpose_bf16(v2).astype(BF16)

    def stage_scores(c):
        q = qkv_ref[rows_of(c), 0:ATTN_Q_WIDTH]
        q_pairs = [q[:, p * LANES:(p + 1) * LANES] for p in range(npair)]
        for side in range(ATTN_KV_HEADS):
            k_side = jnp.concatenate([sides[c - 1][0][side], sides[c][0][side]], axis=0)
            for pp in range(npair // 2):
                q2 = jnp.concatenate([q_pairs[2 * pp], q_pairs[2 * pp + 1]], axis=0)
                s2 = _dot_nt(k_side, q2)
                s_t[(c, side, 2 * pp)] = s2[:, 0:LANES]
                s_t[(c, side, 2 * pp + 1)] = s2[:, LANES:2 * LANES]
        for pr in range(MLSTM_HEADS // 2):
            qpair = mm_ref[rows_of(c), pr * LANES:(pr + 1) * LANES]
            k_pair = mm_ref[rows_of(c), MLSTM_QK_WIDTH + pr * LANES:MLSTM_QK_WIDTH + (pr + 1) * LANES]
            q_m = jnp.concatenate([jnp.where(left_lanes, qpair, zero_b), jnp.where(left_lanes, zero_b, qpair)],
                                  axis=0)
            q_ms[(c, pr)] = q_m
            sc_ts[(c, pr)] = _dot_nt(k_pair, q_m)

    def stage_weights(c):
        variant = first if c == 0 else 0
        for side in range(ATTN_KV_HEADS):
            for p in range(npair):
                j = side * ATTN_GROUP + p
                s2 = s_t.pop((c, side, p))
                comb = jnp.where(cur_t, s2[CHUNK:2 * CHUNK], s2[0:CHUNK]) + btab_ref[variant, j]
                mx = jnp.maximum(jnp.max(comb, axis=0, keepdims=True), sinks_ref[j])
                e = jnp.exp(comb - mx)
                p_t[(c, side, p)] = jnp.concatenate([jnp.where(cur_t, 0.0, e), jnp.where(cur_t, e, 0.0)],
                                                    axis=0).astype(BF16)
                mx_rows[(c, side, p)] = mx
        g8, b8, r_all = gates[c]
        for h in range(MLSTM_HEADS):
            b_row = b8[h:h + 1, :]
            d_t = jnp.where(cur_t, r_all[:, h * LANES:(h + 1) * LANES] + b_row, NEG_INF)
            mloc = jnp.max(d_t, axis=0, keepdims=True)
            wloc = jnp.exp(d_t - mloc)
            scw = (sc_ts[(c, h // 2)][:, (h % 2) * LANES:(h % 2 + 1) * LANES] * wloc).astype(BF16)
            b_last = b_row[:, CHUNK - 1:CHUNK]
            mloc_last = mloc[:, CHUNK - 1:CHUNK]
            u_row = jnp.exp((g8[h:h + 1, :] - b_row) + b_last - mloc_last)
            intra_w[(c, h)] = (b_row, mloc, scw, b_last, mloc_last, u_row)

    def stage_values(c):
        for side in range(ATTN_KV_HEADS):
            vt_side = jnp.concatenate([sides[c - 1][1][side], sides[c][1][side]], axis=1)
            for pp in range(npair // 2):
                p2 = jnp.concatenate([p_t.pop((c, side, 2 * pp)), p_t.pop((c, side, 2 * pp + 1))], axis=1)
                o2 = _dot(vt_side, p2)
                o_t[(c, side, 2 * pp)] = o2[:, 0:LANES]
                o_t[(c, side, 2 * pp + 1)] = o2[:, LANES:2 * LANES]
        for h in range(MLSTM_HEADS):
            b_row, mloc, scw, b_last, mloc_last, u_row = intra_w.pop((c, h))
            pr, half = h // 2, h % 2
            vt_aug = jnp.concatenate([vts[(c, pr)][:, half * LANES:(half + 1) * LANES], ones_rows], axis=0)
            k_pair = mm_ref[rows_of(c), MLSTM_QK_WIDTH + pr * LANES:MLSTM_QK_WIDTH + (pr + 1) * LANES]
            k_hm = jnp.where(left_lanes, k_pair, zero_b) if half == 0 else jnp.where(left_lanes, zero_b, k_pair)
            num_t = _dot(vt_aug, scw)
            upd_t = _dot((vt_aug.astype(F32) * u_row).astype(BF16), k_hm)
            intra[(c, h)] = (b_row, mloc, num_t, b_last, mloc_last, upd_t)

    def stage_attn_out(c):
        out_t = []
        for p in range(npair):
            scaled = []
            for side in range(ATTN_KV_HEADS):
                j = side * ATTN_GROUP + p
                o = o_t.pop((c, side, p))
                sums = o[HALF:HALF + 1, :] if side == 0 else o[0:1, :]
                den = sums + jnp.exp(sinks_ref[j] - mx_rows.pop((c, side, p)))
                scaled.append(o * (1.0 / den))
            out_t.append(jnp.where(top_rows, scaled[0], scaled[1]).astype(BF16))
        for pp in range(npair // 2):
            both = jnp.concatenate([out_t[2 * pp], out_t[2 * pp + 1]], axis=0)
            ao_ref[rows_of(c), 2 * pp * LANES:(2 * pp + 2) * LANES] = transpose_bf16(both).astype(BF16)

    def stage_recurrence(c):
        rows = rows_of(c)
        for pr in range(MLSTM_HEADS // 2):
            c_pair = c_pairs[pr]
            inter_t = _dot_nt(c_pair.astype(BF16), q_ms.pop((c, pr)))
            decs, eus, upds = [], [], []
            for half in range(2):
                h = 2 * pr + half
                b_row, mloc, num_t, b_last, mloc_last, upd_t = intra.pop((c, h))
                m_prev = m_rows[h]
                it = inter_t[:, half * LANES:(half + 1) * LANES]
                a = b_row + m_prev
                m_t = jnp.maximum(a, mloc)
                e_t = jnp.exp(mloc - m_t)
                w_inter = jnp.exp(a - m_t)
                numv = e_t * num_t[0:MLSTM_V_DIM] + w_inter * it[0:MLSTM_V_DIM]
                nq = e_t * num_t[MLSTM_V_DIM:MLSTM_V_DIM + 1] + w_inter * it[MLSTM_V_DIM:MLSTM_V_DIM + 1]
                den = jnp.maximum(jnp.abs(nq), jnp.exp(-m_t))
                ssq = jnp.sum(numv * numv, axis=0, keepdims=True)
                scale = lax.rsqrt(ssq * (1.0 / MLSTM_V_DIM) + NORM_EPS * (den * den))
                hn = (numv * scale * gcol_ref[h]).T
                og = mm_ref[rows, 2 * MLSTM_QK_WIDTH + MLSTM_V_WIDTH + h * MLSTM_V_DIM:
                            2 * MLSTM_QK_WIDTH + MLSTM_V_WIDTH + (h + 1) * MLSTM_V_DIM].astype(F32)
                hm_ref[rows, h * MLSTM_V_DIM:(h + 1) * MLSTM_V_DIM] = (hn * jax.nn.sigmoid(og)).astype(BF16)
                m_new = jnp.maximum(b_last + m_prev, mloc_last)
                decs.append(jnp.exp(b_last + m_prev - m_new))
                eus.append(jnp.exp(mloc_last - m_new))
                upds.append(upd_t)
                m_rows[h] = m_new
            c_pairs[pr] = (jnp.where(left_row, decs[0], decs[1]) * c_pair
                           + eus[0] * upds[0] + eus[1] * upds[1])

    stages = (stage_prepare, stage_scores, stage_weights, stage_values, stage_attn_out, stage_recurrence)
    for tick in range(nch + max(MIXER_STAGE_LAGS)):
        for stage, lag in zip(stages, MIXER_STAGE_LAGS):
            c = tick - lag
            if 0 <= c < nch:
                stage(c)

    for h in range(MLSTM_HEADS):
        mst_ref[h:h + 1, :] = m_rows[h]
    for pr in range(MLSTM_HEADS // 2):
        cst_ref[pr] = c_pairs[pr]


def _cast_slab_rows(n_rows, n_steps):
    slab = BF16_SUBLANES * pl.cdiv(pl.cdiv(n_rows, n_steps), BF16_SUBLANES)
    while n_rows % slab:
        slab += BF16_SUBLANES
    return slab


def _mixer(sinks, qkv, mm, gt, gbias, normg, cast_weights, batch, seq, nch):
    tm = nch * CHUNK
    nj = seq // tm
    t = batch * seq
    row = lambda w: pl.BlockSpec((tm, w), lambda b, j: (b * nj + j, 0))
    cast_in_specs, cast_out_specs, cast_shapes = [], [], []
    for w, block_perm in cast_weights:
        n_rows, n_cols = w.shape
        slab = n_rows // len(block_perm) if block_perm else _cast_slab_rows(n_rows, batch * nj)
        assert slab % BF16_SUBLANES == 0 and n_rows // slab <= batch * nj
        last = n_rows // slab - 1
        dst_map = lambda b, j, last=last: (jnp.minimum(b * nj + j, last), 0)
        if block_perm:
            assert tuple(block_perm) == tuple((s % 2) * (len(block_perm) // 2) + s // 2
                                              for s in range(len(block_perm)))
            stride = len(block_perm) // 2
            src_map = lambda b, j, last=last, stride=stride: (
                (jnp.minimum(b * nj + j, last) % 2) * stride + jnp.minimum(b * nj + j, last) // 2, 0)
        else:
            src_map = dst_map
        cast_in_specs.append(pl.BlockSpec((slab, n_cols), src_map))
        cast_out_specs.append(pl.BlockSpec((slab, n_cols), dst_map))
        cast_shapes.append(jax.ShapeDtypeStruct(w.shape, BF16))
    return pl.pallas_call(
        functools.partial(_mixer_kernel, nch=nch, n_cast=len(cast_weights)),
        grid=(batch, nj),
        in_specs=[pl.BlockSpec(memory_space=pltpu.SMEM),
                  row(QKV_WIDTH), row(MM_WIDTH), row(GATE_WIDTH),
                  _const_spec(gbias.shape), _const_spec(normg.shape)] + cast_in_specs,
        out_specs=[row(ATTN_Q_WIDTH), row(MLSTM_V_WIDTH)] + cast_out_specs,
        out_shape=[jax.ShapeDtypeStruct((t, ATTN_Q_WIDTH), BF16), jax.ShapeDtypeStruct((t, MLSTM_V_WIDTH), BF16)]
                  + cast_shapes,
        scratch_shapes=[
            pltpu.VMEM((2, ATTN_HEADS, CHUNK, CHUNK), F32),
            pltpu.VMEM((2, CHUNK, ATTN_KV_WIDTH), BF16),
            pltpu.VMEM((MLSTM_HEADS // 2, AUG_ROWS, 2 * MLSTM_QK_DIM), F32),
            pltpu.VMEM((8, LANES), F32),
            pltpu.VMEM((MLSTM_HEADS, MLSTM_V_DIM, LANES), F32),
        ],
        compiler_params=pltpu.CompilerParams(dimension_semantics=("arbitrary", "arbitrary"),
                                             vmem_limit_bytes=VMEM_LIMIT_BYTES),
        name="mixer",
    )(sinks, qkv, mm, gt, gbias, normg, *[w for w, _ in cast_weights])


def _tail_kernel(x_ref, ao_ref, hm_ref, gg_ref, wab_ref, wmb_ref, wo_ref, g2_ref, wg_ref, wu_ref, wd_ref,
                 gf_ref, out_ref, *, ff_split):
    ya = _dot(ao_ref[...], wab_ref[...])
    ym = _dot(hm_ref[...], wmb_ref[...])
    ga = jax.nn.sigmoid(gg_ref[:, 0:D_MODEL].astype(F32))
    gm = jax.nn.sigmoid(gg_ref[:, D_MODEL:2 * D_MODEL].astype(F32))
    z = (ga * ya + gm * ym).astype(BF16)
    x1 = x_ref[...] + _dot(z, wo_ref[...])
    f = (x1 * _rms_scale(x1) * g2_ref[...]).astype(BF16)
    n_tiles = wg_ref.shape[1] // MXU_WIDTH
    bounds = [MXU_WIDTH * ((n_tiles * s + ff_split - 1) // ff_split) for s in range(ff_split + 1)]
    x2 = x1
    for s in range(ff_split):
        cols = slice(bounds[s], bounds[s + 1])
        gte = _dot(f, wg_ref[:, cols])
        up = _dot(f, wu_ref[:, cols])
        hh = (gte * jax.nn.sigmoid(gte) * up).astype(BF16)
        x2 = x2 + _dot(hh, wd_ref[cols, :])
    out_ref[...] = x2 * _rms_scale(x2) * gf_ref[...]


def _tail(x2d, ao, hm, gg, wab, wmb, wo, g2, wg, wu, wd, gf, tm, ff_split):
    t = x2d.shape[0]
    row = lambda w: pl.BlockSpec((tm, w), lambda i: (i, 0))
    consts = (wab, wmb, wo, g2, wg, wu, wd, gf)
    return pl.pallas_call(
        functools.partial(_tail_kernel, ff_split=ff_split),
        grid=(t // tm,),
        in_specs=[row(D_MODEL), row(ATTN_Q_WIDTH), row(MLSTM_V_WIDTH), row(MERGE_WIDTH)]
                 + [_const_spec(c.shape) for c in consts],
        out_specs=row(D_MODEL),
        out_shape=jax.ShapeDtypeStruct((t, D_MODEL), F32),
        compiler_params=pltpu.CompilerParams(dimension_semantics=("arbitrary",),
                                             vmem_limit_bytes=VMEM_LIMIT_BYTES),
        name="tail",
    )(x2d, ao, hm, gg, *consts)


def _layer(x2d, batch, seq, norm1_g, w_in, conv_w, conv_b, i_bias, f_bias, mlstm_norm_g, attn_sinks,
           w_attn_branch, w_mlstm_branch, w_out, norm2_g, w_ffn_gate, w_ffn_up, w_ffn_down, out_g):
    order = jnp.asarray(Q_HEAD_ORDER)
    o = 0
    wq = w_in[:, o:o + ATTN_Q_WIDTH]; o += ATTN_Q_WIDTH
    wkv = w_in[:, o:o + 2 * ATTN_KV_WIDTH]; o += 2 * ATTN_KV_WIDTH
    wm_qk = w_in[:, o:o + 2 * MLSTM_QK_WIDTH]; o += 2 * MLSTM_QK_WIDTH
    wm_vo = w_in[:, o:o + 2 * MLSTM_V_WIDTH]; o += 2 * MLSTM_V_WIDTH
    w_if = w_in[:, o:o + 2 * MLSTM_HEADS]; o += 2 * MLSTM_HEADS
    wgg = w_in[:, o:o + MERGE_WIDTH]
    wq = wq.reshape(D_MODEL, ATTN_HEADS, ATTN_HEAD_DIM)[:, order, :].reshape(D_MODEL, ATTN_Q_WIDTH)
    wa = jnp.concatenate([wq, wkv], axis=1)
    wgt = jnp.pad(w_if, ((0, 0), (0, GATE_WIDTH - 2 * MLSTM_HEADS)))
    gbias = jnp.pad(jnp.concatenate([i_bias, f_bias]), (0, GATE_WIDTH - 2 * MLSTM_HEADS)).reshape(1, GATE_WIDTH)

    qkv, mm, gt, gg = _inproj(x2d, norm1_g.reshape(1, D_MODEL), wa.astype(BF16), wm_qk.astype(BF16),
                              wm_vo.astype(BF16), wgt.astype(BF16), wgg.astype(BF16), conv_w,
                              conv_b.reshape(1, -1), tm=INPROJ_TM, seq=seq)
    tail_weights = [(w_attn_branch, Q_HEAD_ORDER), (w_mlstm_branch, None), (w_out, None),
                    (w_ffn_gate, None), (w_ffn_up, None), (w_ffn_down, None)]
    ao, hm, wab, wmb, wo, wg, wu, wd = _mixer(attn_sinks, qkv, mm, gt, gbias,
                                              mlstm_norm_g.reshape(1, MLSTM_V_WIDTH), tail_weights, batch, seq,
                                              nch=MIXER_CHUNKS)
    return _tail(x2d, ao, hm, gg, wab, wmb, wo, norm2_g.reshape(1, D_MODEL), wg, wu, wd,
                 out_g.reshape(1, D_MODEL), tm=TAIL_TM, ff_split=TAIL_FF_SPLIT)


def kernel(x, norm1_g, w_in, conv_w, conv_b, i_bias, f_bias, mlstm_norm_g, attn_sinks, w_attn_branch,
           w_mlstm_branch, w_out, norm2_g, w_ffn_gate, w_ffn_up, w_ffn_down, final_norm_g):
    batch, seq, d = x.shape
    depth = norm1_g.shape[0]
    assert depth == 1 and d == D_MODEL
    assert seq % (MIXER_CHUNKS * CHUNK) == 0 and seq % INPROJ_TM == 0 and (batch * seq) % TAIL_TM == 0
    assert w_ffn_gate.shape[-1] % MXU_WIDTH == 0
    out = _layer(x.reshape(batch * seq, d), batch, seq, norm1_g[0], w_in[0], conv_w[0], conv_b[0], i_bias[0],
                 f_bias[0], mlstm_norm_g[0], attn_sinks[0], w_attn_branch[0], w_mlstm_branch[0], w_out[0],
                 norm2_g[0], w_ffn_gate[0], w_ffn_up[0], w_ffn_down[0], final_norm_g)
    return out.reshape(batch, seq, d)
```

```python
import functools

import jax
import jax.numpy as jnp
from jax import lax
from jax.experimental import pallas as pl
from jax.experimental.pallas import tpu as pltpu

D_MODEL = 1024
ATTN_HEADS = 8
ATTN_KV_HEADS = 2
ATTN_HEAD_DIM = 64
ATTN_GROUP = ATTN_HEADS // ATTN_KV_HEADS
WINDOW = 128
ATTN_Q_WIDTH = ATTN_HEADS * ATTN_HEAD_DIM
ATTN_KV_WIDTH = ATTN_KV_HEADS * ATTN_HEAD_DIM
MLSTM_HEADS = 4
MLSTM_QK_DIM = 64
MLSTM_V_DIM = 128
MLSTM_QK_WIDTH = MLSTM_HEADS * MLSTM_QK_DIM
MLSTM_V_WIDTH = MLSTM_HEADS * MLSTM_V_DIM
CHUNK = 128
CONV_WIDTH = 4
NORM_EPS = 1e-6

LANES = 128
BF16_SUBLANES = 16
MXU_WIDTH = 256
HALF = LANES // 2
CONV_PAD = 8
GATE_ROWS = 16
NQ_ROWS = BF16_SUBLANES
AUG_ROWS = MLSTM_V_DIM + NQ_ROWS

INPROJ_TM = 1024
INPROJ_ROW_BLOCKS = 2
MIXER_CHUNKS = 8
MIXER_STAGE_LAGS = (0, 1, 2, 3, 4, 4)
TAIL_TM = 512
TAIL_FF_SPLIT = 2
VMEM_LIMIT_BYTES = 56 * 1024 * 1024

QKV_WIDTH = ATTN_Q_WIDTH + 2 * ATTN_KV_WIDTH
MM_WIDTH = 2 * MLSTM_QK_WIDTH + 2 * MLSTM_V_WIDTH
GATE_WIDTH = LANES
MERGE_WIDTH = 2 * D_MODEL

Q_HEAD_ORDER = tuple(h * ATTN_GROUP + g for g in range(ATTN_GROUP) for h in range(ATTN_KV_HEADS))

BF16 = jnp.bfloat16
F32 = jnp.float32
NEG_INF = float("-inf")


def _dot(a, b):
    return jnp.dot(a, b, preferred_element_type=F32)


def _dot_nt(a, b):
    return lax.dot_general(a, b, (((1,), (1,)), ((), ())), preferred_element_type=F32)


def _rms_scale(x):
    return lax.rsqrt(jnp.mean(x * x, axis=-1, keepdims=True) + NORM_EPS)


def _const_spec(shape):
    nd = len(shape)
    return pl.BlockSpec(shape, lambda *_: (0,) * nd, pipeline_mode=pl.Buffered(1))


def _inproj_kernel(x_ref, g_ref, wa_ref, wqk_ref, wvo_ref, wgt_ref, wgg_ref, convw_ref, convb_ref,
                   qkv_ref, mm_ref, gt_ref, gg_ref, conv_ref, *, tiles_per_seq):
    tm = x_ref.shape[0]

    @pl.when(pl.program_id(0) % tiles_per_seq == 0)
    def _():
        conv_ref[0:CONV_PAD, :] = jnp.zeros((CONV_PAD, 2 * MLSTM_QK_WIDTH), F32)

    q_scale = ATTN_HEAD_DIM ** -0.5
    lane = lax.broadcasted_iota(jnp.int32, (1, 2 * MLSTM_QK_WIDTH), 1)
    k_scale = jnp.where(lane < MLSTM_QK_WIDTH, 1.0, MLSTM_QK_DIM ** -0.5)
    rb = tm // INPROJ_ROW_BLOCKS
    for r in range(INPROJ_ROW_BLOCKS):
        rows = slice(r * rb, (r + 1) * rb)
        x = x_ref[rows, :]
        u = (x * _rms_scale(x) * g_ref[...]).astype(BF16)
        conv_ref[CONV_PAD + r * rb:CONV_PAD + (r + 1) * rb, :] = _dot(u, wqk_ref[...])
        qkv_ref[rows, 0:ATTN_Q_WIDTH] = (_dot(u, wa_ref[:, 0:ATTN_Q_WIDTH]) * q_scale).astype(BF16)
        qkv_ref[rows, ATTN_Q_WIDTH:QKV_WIDTH] = _dot(u, wa_ref[:, ATTN_Q_WIDTH:QKV_WIDTH]).astype(BF16)
        mm_ref[rows, 2 * MLSTM_QK_WIDTH:MM_WIDTH] = _dot(u, wvo_ref[...]).astype(BF16)
        gt_ref[rows, :] = _dot(u, wgt_ref[...])
        gg_ref[rows, :] = _dot(u, wgg_ref[...]).astype(BF16)
        xe = conv_ref[r * rb:r * rb + CONV_PAD + rb, :]
        acc = xe * convw_ref[0:1, :]
        for t in range(1, CONV_WIDTH):
            acc = pltpu.roll(acc, 1, axis=0) + xe * convw_ref[t:t + 1, :]
        acc = acc[CONV_PAD:CONV_PAD + rb, :] + convb_ref[...]
        mm_ref[rows, 0:2 * MLSTM_QK_WIDTH] = (acc * jax.nn.sigmoid(acc) * k_scale).astype(BF16)
    conv_ref[0:CONV_PAD, :] = conv_ref[tm:tm + CONV_PAD, :]


def _inproj(x2d, g1, wa, wqk, wvo, wgt, wgg, convw, convb, tm, seq):
    t = x2d.shape[0]
    row = lambda w: pl.BlockSpec((tm, w), lambda i: (i, 0))
    consts = (g1, wa, wqk, wvo, wgt, wgg, convw, convb)
    return pl.pallas_call(
        functools.partial(_inproj_kernel, tiles_per_seq=seq // tm),
        grid=(t // tm,),
        in_specs=[row(D_MODEL)] + [_const_spec(c.shape) for c in consts],
        out_specs=[row(QKV_WIDTH), row(MM_WIDTH), row(GATE_WIDTH), row(MERGE_WIDTH)],
        out_shape=[jax.ShapeDtypeStruct((t, QKV_WIDTH), BF16), jax.ShapeDtypeStruct((t, MM_WIDTH), BF16),
                   jax.ShapeDtypeStruct((t, GATE_WIDTH), F32), jax.ShapeDtypeStruct((t, MERGE_WIDTH), BF16)],
        scratch_shapes=[pltpu.VMEM((CONV_PAD + tm, 2 * MLSTM_QK_WIDTH), F32)],
        compiler_params=pltpu.CompilerParams(dimension_semantics=("arbitrary",),
                                             vmem_limit_bytes=VMEM_LIMIT_BYTES),
        name="inproj",
    )(x2d, *consts)


def _log_sigmoid(x):
    return jnp.minimum(x, 0.0) - jnp.log1p(jnp.exp(-jnp.abs(x)))


def _split3(x):
    hi = x.astype(BF16)
    r1 = x - hi.astype(F32)
    mid = r1.astype(BF16)
    lo = (r1 - mid.astype(F32)).astype(BF16)
    return hi, mid, lo


def _mixer_kernel(sinks_ref, qkv_ref, mm_ref, gt_ref, gbias_ref, normg_ref, *rest, nch, n_cast):
    cast_in, (ao_ref, hm_ref), rest = rest[:n_cast], rest[n_cast:n_cast + 2], rest[n_cast + 2:]
    cast_out, (btab_ref, kvp_ref, cst_ref, mst_ref, gcol_ref) = rest[:n_cast], rest[n_cast:]
    b_idx = pl.program_id(0)
    j_idx = pl.program_id(1)

    for src, dst in zip(cast_in, cast_out):
        dst[...] = src[...].astype(BF16)

    row_c = lax.broadcasted_iota(jnp.int32, (CHUNK, CHUNK), 0)
    col_c = lax.broadcasted_iota(jnp.int32, (CHUNK, CHUNK), 1)

    @pl.when((b_idx == 0) & (j_idx == 0))
    def _():
        cur = row_c <= col_c
        dist = jnp.where(cur, col_c - row_c, col_c - row_c + WINDOW).astype(F32)
        for j in range(ATTN_HEADS):
            slope = 2.0 ** (-8.0 * (j + 1) / ATTN_HEADS)
            btab_ref[0, j] = -slope * dist
            btab_ref[1, j] = jnp.where(cur, -slope * dist, NEG_INF)
        for h in range(MLSTM_HEADS):
            g_row = normg_ref[:, h * MLSTM_V_DIM:(h + 1) * MLSTM_V_DIM]
            gcol_ref[h] = jnp.broadcast_to(g_row, (MLSTM_V_DIM, LANES)).T

    @pl.when(j_idx == 0)
    def _():
        kvp_ref[...] = jnp.zeros_like(kvp_ref)
        cst_ref[...] = jnp.zeros_like(cst_ref)
        mst_ref[...] = jnp.zeros_like(mst_ref)

    cur_t = row_c <= col_c
    triu_b = cur_t.astype(BF16)
    eye_b = (row_c == col_c).astype(BF16)

    def transpose_bf16(x):
        return _dot_nt(eye_b, x)

    left_lanes = col_c < HALF
    ones_rows = jnp.ones((NQ_ROWS, LANES), BF16)
    left_row = lax.broadcasted_iota(jnp.int32, (1, LANES), 1) < HALF
    zero_b = jnp.zeros((), BF16)
    first = (j_idx == 0).astype(jnp.int32)

    def kv_sides(k_blk, v_blk):
        vt = transpose_bf16(v_blk).astype(BF16)
        k_sides = (jnp.where(left_lanes, k_blk, zero_b), jnp.where(left_lanes, zero_b, k_blk))
        vt_sides = tuple(jnp.concatenate([vt[s * HALF:(s + 1) * HALF], ones_rows], axis=0)
                         for s in range(ATTN_KV_HEADS))
        return k_sides, vt_sides

    k_off = ATTN_Q_WIDTH
    v_off = ATTN_Q_WIDTH + ATTN_KV_WIDTH
    prev_sides = kv_sides(kvp_ref[0], kvp_ref[1])
    last = slice((nch - 1) * CHUNK, nch * CHUNK)
    kvp_ref[0] = qkv_ref[last, k_off:k_off + ATTN_KV_WIDTH]
    kvp_ref[1] = qkv_ref[last, v_off:v_off + ATTN_KV_WIDTH]

    m_rows = [mst_ref[h:h + 1, :] for h in range(MLSTM_HEADS)]
    c_pairs = [cst_ref[pr] for pr in range(MLSTM_HEADS // 2)]

    npair = ATTN_GROUP
    rows_of = lambda c: slice(c * CHUNK, (c + 1) * CHUNK)
    row8 = lax.broadcasted_iota(jnp.int32, (GATE_ROWS, LANES), 0)
    head_rows = row8 < MLSTM_HEADS
    sides = {-1: prev_sides}
    gates, s_t, sc_ts, q_ms, vts, p_t, mx_rows, intra_w, o_t, intra = {}, {}, {}, {}, {}, {}, {}, {}, {}, {}

    def stage_prepare(c):
        rows = rows_of(c)
        sides[c] = kv_sides(qkv_ref[rows, k_off:k_off + ATTN_KV_WIDTH], qkv_ref[rows, v_off:v_off + ATTN_KV_WIDTH])
        g8 = (gt_ref[rows, :] + gbias_ref[...]).T[0:GATE_ROWS, :]
        f8 = pltpu.roll(g8, GATE_ROWS - MLSTM_HEADS, axis=0)
        lf_parts = _split3(jnp.where(head_rows, _log_sigmoid(f8), 0.0))
        b8 = sum(_dot(part, triu_b) for part in lf_parts)
        r8 = jnp.where(head_rows, g8 - b8, 0.0)
        r_t = jnp.concatenate([r8, jnp.zeros((CHUNK - GATE_ROWS, LANES), F32)], axis=0).T
        r_all = jnp.concatenate(
            [jnp.broadcast_to(r_t[:, h:h + 1], (CHUNK, LANES)) for h in range(MLSTM_HEADS)], axis=1)
        gates[c] = (g8, b8, r_all)
        for pr in range(MLSTM_HEADS // 2):
            v0 = 2 * MLSTM_QK_WIDTH + 2 * pr * MLSTM_V_DIM
            v2 = jnp.concatenate([mm_ref[rows, v0:v0 + MLSTM_V_DIM],
                                  mm_ref[rows, v0 + MLSTM_V_DIM:v0 + 2 * MLSTM_V_DIM]], axis=0)
            vts[(c, pr)] = transpose_bf16(v2).astype(BF16)

    def stage_scores(c):
        q = qkv_ref[rows_of(c), 0:ATTN_Q_WIDTH]
        q_pairs = [q[:, p * LANES:(p + 1) * LANES] for p in range(npair)]
        for side in range(ATTN_KV_HEADS):
            k_side = jnp.concatenate([sides[c - 1][0][side], sides[c][0][side]], axis=0)
            for pp in range(npair // 2):
                q2 = jnp.concatenate([q_pairs[2 * pp], q_pairs[2 * pp + 1]], axis=0)
                s2 = _dot_nt(k_side, q2)
                s_t[(c, side, 2 * pp)] = s2[:, 0:LANES]
                s_t[(c, side, 2 * pp + 1)] = s2[:, LANES:2 * LANES]
        for pr in range(MLSTM_HEADS // 2):
            qpair = mm_ref[rows_of(c), pr * LANES:(pr + 1) * LANES]
            k_pair = mm_ref[rows_of(c), MLSTM_QK_WIDTH + pr * LANES:MLSTM_QK_WIDTH + (pr + 1) * LANES]
            q_m = jnp.concatenate([jnp.where(left_lanes, qpair, zero_b), jnp.where(left_lanes, zero_b, qpair)],
                                  axis=0)
            q_ms[(c, pr)] = q_m
            sc_ts[(c, pr)] = _dot_nt(k_pair, q_m)

    def stage_weights(c):
        variant = first if c == 0 else 0
        for side in range(ATTN_KV_HEADS):
            for p in range(npair):
                j = side * ATTN_GROUP + p
                s2 = s_t.pop((c, side, p))
                comb = jnp.where(cur_t, s2[CHUNK:2 * CHUNK], s2[0:CHUNK]) + btab_ref[variant, j]
                mx = jnp.maximum(jnp.max(comb, axis=0, keepdims=True), sinks_ref[j])
                e = jnp.exp(comb - mx)
                p_t[(c, side, p)] = jnp.concatenate([jnp.where(cur_t, 0.0, e), jnp.where(cur_t, e, 0.0)],
                                                    axis=0).astype(BF16)
                mx_rows[(c, side, p)] = mx
        g8, b8, r_all = gates[c]
        for h in range(MLSTM_HEADS):
            b_row = b8[h:h + 1, :]
            d_t = jnp.where(cur_t, r_all[:, h * LANES:(h + 1) * LANES] + b_row, NEG_INF)
            mloc = jnp.max(d_t, axis=0, keepdims=True)
            wloc = jnp.exp(d_t - mloc)
            scw = (sc_ts[(c, h // 2)][:, (h % 2) * LANES:(h % 2 + 1) * LANES] * wloc).astype(BF16)
            b_last = b_row[:, CHUNK - 1:CHUNK]
            mloc_last = mloc[:, CHUNK - 1:CHUNK]
            u_row = jnp.exp((g8[h:h + 1, :] - b_row) + b_last - mloc_last)
            intra_w[(c, h)] = (b_row, mloc, scw, b_last, mloc_last, u_row)

    def stage_values(c):
        for side in range(ATTN_KV_HEADS):
            vt_side = jnp.concatenate([sides[c - 1][1][side], sides[c][1][side]], axis=1)
            for pp in range(npair // 2):
                p2 = jnp.concatenate([p_t.pop((c, side, 2 * pp)), p_t.pop((c, side, 2 * pp + 1))], axis=1)
                o2 = _dot(vt_side, p2)
                o_t[(c, side, 2 * pp)] = o2[:, 0:LANES]
                o_t[(c, side, 2 * pp + 1)] = o2[:, LANES:2 * LANES]
        for h in range(MLSTM_HEADS):
            b_row, mloc, scw, b_last, mloc_last, u_row = intra_w.pop((c, h))
            pr, half = h // 2, h % 2
            vt_aug = jnp.concatenate([vts[(c, pr)][:, half * LANES:(half + 1) * LANES], ones_rows], axis=0)
            k_pair = mm_ref[rows_of(c), MLSTM_QK_WIDTH + pr * LANES:MLSTM_QK_WIDTH + (pr + 1) * LANES]
            k_hm = jnp.where(left_lanes, k_pair, zero_b) if half == 0 else jnp.where(left_lanes, zero_b, k_pair)
            num_t = _dot(vt_aug, scw)
            upd_t = _dot((vt_aug.astype(F32) * u_row).astype(BF16), k_hm)
            intra[(c, h)] = (b_row, mloc, num_t, b_last, mloc_last, upd_t)

    def stage_attn_out(c):
        out_t = []
        for p in range(npair):
            scaled = []
            for side in range(ATTN_KV_HEADS):
                j = side * ATTN_GROUP + p
                o = o_t.pop((c, side, p))
                den = o[HALF:HALF + 1, :] + jnp.exp(sinks_ref[j] - mx_rows.pop((c, side, p)))
                scaled.append(o[0:HALF] * (1.0 / den))
            out_t.append(jnp.concatenate(scaled, axis=0).astype(BF16))
        for pp in range(npair // 2):
            both = jnp.concatenate([out_t[2 * pp], out_t[2 * pp + 1]], axis=0)
            ao_ref[rows_of(c), 2 * pp * LANES:(2 * pp + 2) * LANES] = transpose_bf16(both).astype(BF16)

    def stage_recurrence(c):
        rows = rows_of(c)
        for pr in range(MLSTM_HEADS // 2):
            c_pair = c_pairs[pr]
            inter_t = _dot_nt(c_pair.astype(BF16), q_ms.pop((c, pr)))
            decs, eus, upds = [], [], []
            for half in range(2):
                h = 2 * pr + half
                b_row, mloc, num_t, b_last, mloc_last, upd_t = intra.pop((c, h))
                m_prev = m_rows[h]
                it = inter_t[:, half * LANES:(half + 1) * LANES]
                a = b_row + m_prev
                m_t = jnp.maximum(a, mloc)
                e_t = jnp.exp(mloc - m_t)
                w_inter = jnp.exp(a - m_t)
                numv = e_t * num_t[0:MLSTM_V_DIM] + w_inter * it[0:MLSTM_V_DIM]
                nq = e_t * num_t[MLSTM_V_DIM:MLSTM_V_DIM + 1] + w_inter * it[MLSTM_V_DIM:MLSTM_V_DIM + 1]
                den = jnp.maximum(jnp.abs(nq), jnp.exp(-m_t))
                ssq = jnp.sum(numv * numv, axis=0, keepdims=True)
                scale = lax.rsqrt(ssq * (1.0 / MLSTM_V_DIM) + NORM_EPS * (den * den))
                hn = (numv * scale * gcol_ref[h]).T
                og = mm_ref[rows, 2 * MLSTM_QK_WIDTH + MLSTM_V_WIDTH + h * MLSTM_V_DIM:
                            2 * MLSTM_QK_WIDTH + MLSTM_V_WIDTH + (h + 1) * MLSTM_V_DIM].astype(F32)
                gate = 0.5 * jnp.tanh(0.5 * og) + 0.5
                hm_ref[rows, h * MLSTM_V_DIM:(h + 1) * MLSTM_V_DIM] = (hn * gate).astype(BF16)
                m_new = jnp.maximum(b_last + m_prev, mloc_last)
                decs.append(jnp.exp(b_last + m_prev - m_new))
                eus.append(jnp.exp(mloc_last - m_new))
                upds.append(upd_t)
                m_rows[h] = m_new
            c_pairs[pr] = (jnp.where(left_row, decs[0], decs[1]) * c_pair
                           + eus[0] * upds[0] + eus[1] * upds[1])

    stages = (stage_prepare, stage_scores, stage_weights, stage_values, stage_attn_out, stage_recurrence)
    for tick in range(nch + max(MIXER_STAGE_LAGS)):
        for stage, lag in zip(stages, MIXER_STAGE_LAGS):
            c = tick - lag
            if 0 <= c < nch:
                stage(c)

    for h in range(MLSTM_HEADS):
        mst_ref[h:h + 1, :] = m_rows[h]
    for pr in range(MLSTM_HEADS // 2):
        cst_ref[pr] = c_pairs[pr]


def _cast_slab_rows(n_rows, n_steps):
    slab = BF16_SUBLANES * pl.cdiv(pl.cdiv(n_rows, n_steps), BF16_SUBLANES)
    while n_rows % slab:
        slab += BF16_SUBLANES
    return slab


def _mixer(sinks, qkv, mm, gt, gbias, normg, cast_weights, batch, seq, nch):
    tm = nch * CHUNK
    nj = seq // tm
    t = batch * seq
    row = lambda w: pl.BlockSpec((tm, w), lambda b, j: (b * nj + j, 0))
    cast_in_specs, cast_out_specs, cast_shapes = [], [], []
    for w, block_perm in cast_weights:
        n_rows, n_cols = w.shape
        slab = n_rows // len(block_perm) if block_perm else _cast_slab_rows(n_rows, batch * nj)
        assert slab % BF16_SUBLANES == 0 and n_rows // slab <= batch * nj
        last = n_rows // slab - 1
        dst_map = lambda b, j, last=last: (jnp.minimum(b * nj + j, last), 0)
        if block_perm:
            assert tuple(block_perm) == tuple((s % 2) * (len(block_perm) // 2) + s // 2
                                              for s in range(len(block_perm)))
            stride = len(block_perm) // 2
            src_map = lambda b, j, last=last, stride=stride: (
                (jnp.minimum(b * nj + j, last) % 2) * stride + jnp.minimum(b * nj + j, last) // 2, 0)
        else:
            src_map = dst_map
        cast_in_specs.append(pl.BlockSpec((slab, n_cols), src_map))
        cast_out_specs.append(pl.BlockSpec((slab, n_cols), dst_map))
        cast_shapes.append(jax.ShapeDtypeStruct(w.shape, BF16))
    return pl.pallas_call(
        functools.partial(_mixer_kernel, nch=nch, n_cast=len(cast_weights)),
        grid=(batch, nj),
        in_specs=[pl.BlockSpec(memory_space=pltpu.SMEM),
                  row(QKV_WIDTH), row(MM_WIDTH), row(GATE_WIDTH),
                  _const_spec(gbias.shape), _const_spec(normg.shape)] + cast_in_specs,
        out_specs=[row(ATTN_Q_WIDTH), row(MLSTM_V_WIDTH)] + cast_out_specs,
        out_shape=[jax.ShapeDtypeStruct((t, ATTN_Q_WIDTH), BF16), jax.ShapeDtypeStruct((t, MLSTM_V_WIDTH), BF16)]
                  + cast_shapes,
        scratch_shapes=[
            pltpu.VMEM((2, ATTN_HEADS, CHUNK, CHUNK), F32),
            pltpu.VMEM((2, CHUNK, ATTN_KV_WIDTH), BF16),
            pltpu.VMEM((MLSTM_HEADS // 2, AUG_ROWS, 2 * MLSTM_QK_DIM), F32),
            pltpu.VMEM((8, LANES), F32),
            pltpu.VMEM((MLSTM_HEADS, MLSTM_V_DIM, LANES), F32),
        ],
        compiler_params=pltpu.CompilerParams(dimension_semantics=("arbitrary", "arbitrary"),
                                             vmem_limit_bytes=VMEM_LIMIT_BYTES),
        name="mixer",
    )(sinks, qkv, mm, gt, gbias, normg, *[w for w, _ in cast_weights])


def _tail_kernel(x_ref, ao_ref, hm_ref, gg_ref, wab_ref, wmb_ref, wo_ref, g2_ref, wg_ref, wu_ref, wd_ref,
                 gf_ref, out_ref, *, ff_split):
    ya = _dot(ao_ref[...], wab_ref[...])
    ym = _dot(hm_ref[...], wmb_ref[...])
    ga = jax.nn.sigmoid(gg_ref[:, 0:D_MODEL].astype(F32))
    gm = jax.nn.sigmoid(gg_ref[:, D_MODEL:2 * D_MODEL].astype(F32))
    z = (ga * ya + gm * ym).astype(BF16)
    x1 = x_ref[...] + _dot(z, wo_ref[...])
    f = (x1 * _rms_scale(x1) * g2_ref[...]).astype(BF16)
    n_tiles = wg_ref.shape[1] // MXU_WIDTH
    bounds = [MXU_WIDTH * ((n_tiles * s + ff_split - 1) // ff_split) for s in range(ff_split + 1)]
    x2 = x1
    for s in range(ff_split):
        cols = slice(bounds[s], bounds[s + 1])
        gte = _dot(f, wg_ref[:, cols])
        up = _dot(f, wu_ref[:, cols])
        hh = (gte * jax.nn.sigmoid(gte) * up).astype(BF16)
        x2 = x2 + _dot(hh, wd_ref[cols, :])
    out_ref[...] = x2 * _rms_scale(x2) * gf_ref[...]


def _tail(x2d, ao, hm, gg, wab, wmb, wo, g2, wg, wu, wd, gf, tm, ff_split):
    t = x2d.shape[0]
    row = lambda w: pl.BlockSpec((tm, w), lambda i: (i, 0))
    consts = (wab, wmb, wo, g2, wg, wu, wd, gf)
    return pl.pallas_call(
        functools.partial(_tail_kernel, ff_split=ff_split),
        grid=(t // tm,),
        in_specs=[row(D_MODEL), row(ATTN_Q_WIDTH), row(MLSTM_V_WIDTH), row(MERGE_WIDTH)]
                 + [_const_spec(c.shape) for c in consts],
        out_specs=row(D_MODEL),
        out_shape=jax.ShapeDtypeStruct((t, D_MODEL), F32),
        compiler_params=pltpu.CompilerParams(dimension_semantics=("arbitrary",),
                                             vmem_limit_bytes=VMEM_LIMIT_BYTES),
        name="tail",
    )(x2d, ao, hm, gg, *consts)


def _layer(x2d, batch, seq, norm1_g, w_in, conv_w, conv_b, i_bias, f_bias, mlstm_norm_g, attn_sinks,
           w_attn_branch, w_mlstm_branch, w_out, norm2_g, w_ffn_gate, w_ffn_up, w_ffn_down, out_g):
    order = jnp.asarray(Q_HEAD_ORDER)
    o = 0
    wq = w_in[:, o:o + ATTN_Q_WIDTH]; o += ATTN_Q_WIDTH
    wkv = w_in[:, o:o + 2 * ATTN_KV_WIDTH]; o += 2 * ATTN_KV_WIDTH
    wm_qk = w_in[:, o:o + 2 * MLSTM_QK_WIDTH]; o += 2 * MLSTM_QK_WIDTH
    wm_vo = w_in[:, o:o + 2 * MLSTM_V_WIDTH]; o += 2 * MLSTM_V_WIDTH
    w_if = w_in[:, o:o + 2 * MLSTM_HEADS]; o += 2 * MLSTM_HEADS
    wgg = w_in[:, o:o + MERGE_WIDTH]
    wq = wq.reshape(D_MODEL, ATTN_HEADS, ATTN_HEAD_DIM)[:, order, :].reshape(D_MODEL, ATTN_Q_WIDTH)
    wa = jnp.concatenate([wq, wkv], axis=1)
    wgt = jnp.pad(w_if, ((0, 0), (0, GATE_WIDTH - 2 * MLSTM_HEADS)))
    gbias = jnp.pad(jnp.concatenate([i_bias, f_bias]), (0, GATE_WIDTH - 2 * MLSTM_HEADS)).reshape(1, GATE_WIDTH)

    qkv, mm, gt, gg = _inproj(x2d, norm1_g.reshape(1, D_MODEL), wa.astype(BF16), wm_qk.astype(BF16),
                              wm_vo.astype(BF16), wgt.astype(BF16), wgg.astype(BF16), conv_w,
                              conv_b.reshape(1, -1), tm=INPROJ_TM, seq=seq)
    tail_weights = [(w_attn_branch, Q_HEAD_ORDER), (w_mlstm_branch, None), (w_out, None),
                    (w_ffn_gate, None), (w_ffn_up, None), (w_ffn_down, None)]
    ao, hm, wab, wmb, wo, wg, wu, wd = _mixer(attn_sinks, qkv, mm, gt, gbias,
                                              mlstm_norm_g.reshape(1, MLSTM_V_WIDTH), tail_weights, batch, seq,
                                              nch=MIXER_CHUNKS)
    return _tail(x2d, ao, hm, gg, wab, wmb, wo, norm2_g.reshape(1, D_MODEL), wg, wu, wd,
                 out_g.reshape(1, D_MODEL), tm=TAIL_TM, ff_split=TAIL_FF_SPLIT)


def kernel(x, norm1_g, w_in, conv_w, conv_b, i_bias, f_bias, mlstm_norm_g, attn_sinks, w_attn_branch,
           w_mlstm_branch, w_out, norm2_g, w_ffn_gate, w_ffn_up, w_ffn_down, final_norm_g):
    batch, seq, d = x.shape
    depth = norm1_g.shape[0]
    assert depth == 1 and d == D_MODEL
    assert seq % (MIXER_CHUNKS * CHUNK) == 0 and seq % INPROJ_TM == 0 and (batch * seq) % TAIL_TM == 0
    assert w_ffn_gate.shape[-1] % MXU_WIDTH == 0
    out = _layer(x.reshape(batch * seq, d), batch, seq, norm1_g[0], w_in[0], conv_w[0], conv_b[0], i_bias[0],
                 f_bias[0], mlstm_norm_g[0], attn_sinks[0], w_attn_branch[0], w_mlstm_branch[0], w_out[0],
                 norm2_g[0], w_ffn_gate[0], w_ffn_up[0], w_ffn_down[0], final_norm_g)
    return out.reshape(batch, seq, d)
```

```python
import functools

import jax
import jax.numpy as jnp
from jax import lax
from jax.experimental import pallas as pl
from jax.experimental.pallas import tpu as pltpu

D_MODEL = 1024
ATTN_HEADS = 8
ATTN_KV_HEADS = 2
ATTN_HEAD_DIM = 64
ATTN_GROUP = ATTN_HEADS // ATTN_KV_HEADS
WINDOW = 128
ATTN_Q_WIDTH = ATTN_HEADS * ATTN_HEAD_DIM
ATTN_KV_WIDTH = ATTN_KV_HEADS * ATTN_HEAD_DIM
MLSTM_HEADS = 4
MLSTM_QK_DIM = 64
MLSTM_V_DIM = 128
MLSTM_QK_WIDTH = MLSTM_HEADS * MLSTM_QK_DIM
MLSTM_V_WIDTH = MLSTM_HEADS * MLSTM_V_DIM
CHUNK = 128
CONV_WIDTH = 4
NORM_EPS = 1e-6

LANES = 128
BF16_SUBLANES = 16
MXU_WIDTH = 256
HALF = LANES // 2
CONV_PAD = 8
GATE_ROWS = 16
NQ_ROWS = BF16_SUBLANES
AUG_ROWS = MLSTM_V_DIM + NQ_ROWS

INPROJ_TM = 1024
INPROJ_ROW_BLOCKS = 2
MIXER_CHUNKS = 8
MIXER_STAGE_LAGS = (0, 1, 2, 3, 4, 4)
TAIL_TM = 512
TAIL_FF_SPLIT = 2
VMEM_LIMIT_BYTES = 56 * 1024 * 1024

QKV_WIDTH = ATTN_Q_WIDTH + 2 * ATTN_KV_WIDTH
MM_WIDTH = 2 * MLSTM_QK_WIDTH + 2 * MLSTM_V_WIDTH
GATE_WIDTH = LANES
MERGE_WIDTH = 2 * D_MODEL
W_MQK_OFF = QKV_WIDTH
W_MVO_OFF = W_MQK_OFF + 2 * MLSTM_QK_WIDTH
W_GATE_OFF = W_MVO_OFF + 2 * MLSTM_V_WIDTH
W_MERGE_OFF = W_GATE_OFF + 2 * MLSTM_HEADS

Q_HEAD_ORDER = tuple(h * ATTN_GROUP + g for g in range(ATTN_GROUP) for h in range(ATTN_KV_HEADS))

BF16 = jnp.bfloat16
F32 = jnp.float32
NEG_INF = float("-inf")


def _dot(a, b):
    return jnp.dot(a, b, preferred_element_type=F32)


def _dot_nt(a, b):
    return lax.dot_general(a, b, (((1,), (1,)), ((), ())), preferred_element_type=F32)


def _rms_scale(x):
    return lax.rsqrt(jnp.mean(x * x, axis=-1, keepdims=True) + NORM_EPS)


def _const_spec(shape):
    nd = len(shape)
    return pl.BlockSpec(shape, lambda *_: (0,) * nd, pipeline_mode=pl.Buffered(1))


def _inproj_kernel(x_ref, g_ref, w_ref, wgt_ref, convw_ref, convb_ref,
                   qkv_ref, mm_ref, gt_ref, gg_ref, conv_ref, wq_ref, wgg_ref, *, tiles_per_seq):
    tm = x_ref.shape[0]

    @pl.when(pl.program_id(0) == 0)
    def _():
        src = lax.broadcasted_iota(jnp.int32, (ATTN_Q_WIDTH, ATTN_Q_WIDTH), 0)
        dst = lax.broadcasted_iota(jnp.int32, (ATTN_Q_WIDTH, ATTN_Q_WIDTH), 1)
        head = ((dst % LANES) // ATTN_HEAD_DIM) * ATTN_GROUP + dst // LANES
        perm = (src == head * ATTN_HEAD_DIM + dst % ATTN_HEAD_DIM).astype(BF16)
        wq_ref[...] = _dot(w_ref[:, 0:ATTN_Q_WIDTH], perm).astype(BF16)
        wgg_ref[...] = w_ref[:, W_MERGE_OFF:W_MERGE_OFF + MERGE_WIDTH]

    @pl.when(pl.program_id(0) % tiles_per_seq == 0)
    def _():
        conv_ref[0:CONV_PAD, :] = jnp.zeros((CONV_PAD, 2 * MLSTM_QK_WIDTH), F32)

    q_scale = ATTN_HEAD_DIM ** -0.5
    lane = lax.broadcasted_iota(jnp.int32, (1, 2 * MLSTM_QK_WIDTH), 1)
    k_scale = jnp.where(lane < MLSTM_QK_WIDTH, 1.0, MLSTM_QK_DIM ** -0.5)
    rb = tm // INPROJ_ROW_BLOCKS
    for r in range(INPROJ_ROW_BLOCKS):
        rows = slice(r * rb, (r + 1) * rb)
        x = x_ref[rows, :]
        u = (x * _rms_scale(x) * g_ref[...]).astype(BF16)
        conv_ref[CONV_PAD + r * rb:CONV_PAD + (r + 1) * rb, :] = _dot(
            u, w_ref[:, W_MQK_OFF:W_MQK_OFF + 2 * MLSTM_QK_WIDTH])
        qkv_ref[rows, 0:ATTN_Q_WIDTH] = (_dot(u, wq_ref[...]) * q_scale).astype(BF16)
        qkv_ref[rows, ATTN_Q_WIDTH:QKV_WIDTH] = _dot(u, w_ref[:, ATTN_Q_WIDTH:QKV_WIDTH]).astype(BF16)
        mm_ref[rows, 2 * MLSTM_QK_WIDTH:MM_WIDTH] = _dot(
            u, w_ref[:, W_MVO_OFF:W_MVO_OFF + 2 * MLSTM_V_WIDTH]).astype(BF16)
        gt_ref[rows, :] = _dot(u, wgt_ref[...])
        gg_ref[rows, :] = _dot(u, wgg_ref[...]).astype(BF16)
        xe = conv_ref[r * rb:r * rb + CONV_PAD + rb, :]
        acc = xe * convw_ref[0:1, :]
        for t in range(1, CONV_WIDTH):
            acc = pltpu.roll(acc, 1, axis=0) + xe * convw_ref[t:t + 1, :]
        acc = acc[CONV_PAD:CONV_PAD + rb, :] + convb_ref[...]
        mm_ref[rows, 0:2 * MLSTM_QK_WIDTH] = (acc * jax.nn.sigmoid(acc) * k_scale).astype(BF16)
    conv_ref[0:CONV_PAD, :] = conv_ref[tm:tm + CONV_PAD, :]


def _inproj(x2d, g1, w_all, wgt, convw, convb, tm, seq):
    t = x2d.shape[0]
    row = lambda w: pl.BlockSpec((tm, w), lambda i: (i, 0))
    consts = (g1, w_all, wgt, convw, convb)
    return pl.pallas_call(
        functools.partial(_inproj_kernel, tiles_per_seq=seq // tm),
        grid=(t // tm,),
        in_specs=[row(D_MODEL)] + [_const_spec(c.shape) for c in consts],
        out_specs=[row(QKV_WIDTH), row(MM_WIDTH), row(GATE_WIDTH), row(MERGE_WIDTH)],
        out_shape=[jax.ShapeDtypeStruct((t, QKV_WIDTH), BF16), jax.ShapeDtypeStruct((t, MM_WIDTH), BF16),
                   jax.ShapeDtypeStruct((t, GATE_WIDTH), F32), jax.ShapeDtypeStruct((t, MERGE_WIDTH), BF16)],
        scratch_shapes=[pltpu.VMEM((CONV_PAD + tm, 2 * MLSTM_QK_WIDTH), F32),
                        pltpu.VMEM((D_MODEL, ATTN_Q_WIDTH), BF16), pltpu.VMEM((D_MODEL, MERGE_WIDTH), BF16)],
        compiler_params=pltpu.CompilerParams(dimension_semantics=("arbitrary",),
                                             vmem_limit_bytes=VMEM_LIMIT_BYTES),
        name="inproj",
    )(x2d, *consts)


def _log_sigmoid(x):
    return jnp.minimum(x, 0.0) - jnp.log1p(jnp.exp(-jnp.abs(x)))


def _split3(x):
    hi = x.astype(BF16)
    r1 = x - hi.astype(F32)
    mid = r1.astype(BF16)
    lo = (r1 - mid.astype(F32)).astype(BF16)
    return hi, mid, lo


def _mixer_kernel(sinks_ref, qkv_ref, mm_ref, gt_ref, gbias_ref, normg_ref, *rest, nch, n_cast):
    cast_in, (ao_ref, hm_ref), rest = rest[:n_cast], rest[n_cast:n_cast + 2], rest[n_cast + 2:]
    cast_out, (btab_ref, kvp_ref, cst_ref, mst_ref, gcol_ref) = rest[:n_cast], rest[n_cast:]
    b_idx = pl.program_id(0)
    j_idx = pl.program_id(1)

    for src, dst in zip(cast_in, cast_out):
        dst[...] = src[...].astype(BF16)

    row_c = lax.broadcasted_iota(jnp.int32, (CHUNK, CHUNK), 0)
    col_c = lax.broadcasted_iota(jnp.int32, (CHUNK, CHUNK), 1)

    @pl.when((b_idx == 0) & (j_idx == 0))
    def _():
        cur = row_c <= col_c
        dist = jnp.where(cur, col_c - row_c, col_c - row_c + WINDOW).astype(F32)
        for j in range(ATTN_HEADS):
            slope = 2.0 ** (-8.0 * (j + 1) / ATTN_HEADS)
            btab_ref[0, j] = -slope * dist
            btab_ref[1, j] = jnp.where(cur, -slope * dist, NEG_INF)
        for h in range(MLSTM_HEADS):
            g_row = normg_ref[:, h * MLSTM_V_DIM:(h + 1) * MLSTM_V_DIM]
            gcol_ref[h] = jnp.broadcast_to(g_row, (MLSTM_V_DIM, LANES)).T

    @pl.when(j_idx == 0)
    def _():
        kvp_ref[...] = jnp.zeros_like(kvp_ref)
        cst_ref[...] = jnp.zeros_like(cst_ref)
        mst_ref[...] = jnp.zeros_like(mst_ref)

    cur_t = row_c <= col_c
    triu_b = cur_t.astype(BF16)
    eye_b = (row_c == col_c).astype(BF16)

    def transpose_bf16(x):
        return _dot_nt(eye_b, x)

    left_lanes = col_c < HALF
    ones_rows = jnp.ones((NQ_ROWS, LANES), BF16)
    left_row = lax.broadcasted_iota(jnp.int32, (1, LANES), 1) < HALF
    zero_b = jnp.zeros((), BF16)
    first = (j_idx == 0).astype(jnp.int32)

    def kv_sides(k_blk, v_blk):
        vt = transpose_bf16(v_blk).astype(BF16)
        k_sides = (jnp.where(left_lanes, k_blk, zero_b), jnp.where(left_lanes, zero_b, k_blk))
        vt_sides = tuple(jnp.concatenate([vt[s * HALF:(s + 1) * HALF], ones_rows], axis=0)
                         for s in range(ATTN_KV_HEADS))
        return k_sides, vt_sides

    k_off = ATTN_Q_WIDTH
    v_off = ATTN_Q_WIDTH + ATTN_KV_WIDTH
    prev_sides = kv_sides(kvp_ref[0], kvp_ref[1])
    last = slice((nch - 1) * CHUNK, nch * CHUNK)
    kvp_ref[0] = qkv_ref[last, k_off:k_off + ATTN_KV_WIDTH]
    kvp_ref[1] = qkv_ref[last, v_off:v_off + ATTN_KV_WIDTH]

    m_rows = [mst_ref[h:h + 1, :] for h in range(MLSTM_HEADS)]
    c_pairs = [cst_ref[pr] for pr in range(MLSTM_HEADS // 2)]

    npair = ATTN_GROUP
    rows_of = lambda c: slice(c * CHUNK, (c + 1) * CHUNK)
    row8 = lax.broadcasted_iota(jnp.int32, (GATE_ROWS, LANES), 0)
    head_rows = row8 < MLSTM_HEADS
    sides = {-1: prev_sides}
    gates, s_t, sc_ts, q_ms, vts, p_t, mx_rows, intra_w, o_t, intra = {}, {}, {}, {}, {}, {}, {}, {}, {}, {}

    def stage_prepare(c):
        rows = rows_of(c)
        sides[c] = kv_sides(qkv_ref[rows, k_off:k_off + ATTN_KV_WIDTH], qkv_ref[rows, v_off:v_off + ATTN_KV_WIDTH])
        g8 = (gt_ref[rows, :] + gbias_ref[...]).T[0:GATE_ROWS, :]
        f8 = pltpu.roll(g8, GATE_ROWS - MLSTM_HEADS, axis=0)
        lf_parts = _split3(jnp.where(head_rows, _log_sigmoid(f8), 0.0))
        b8 = sum(_dot(part, triu_b) for part in lf_parts)
        r8 = jnp.where(head_rows, g8 - b8, 0.0)
        r_t = jnp.concatenate([r8, jnp.zeros((CHUNK - GATE_ROWS, LANES), F32)], axis=0).T
        r_all = jnp.concatenate(
            [jnp.broadcast_to(r_t[:, h:h + 1], (CHUNK, LANES)) for h in range(MLSTM_HEADS)], axis=1)
        gates[c] = (g8, b8, r_all)
        for pr in range(MLSTM_HEADS // 2):
            v0 = 2 * MLSTM_QK_WIDTH + 2 * pr * MLSTM_V_DIM
            v2 = jnp.concatenate([mm_ref[rows, v0:v0 + MLSTM_V_DIM],
                                  mm_ref[rows, v0 + MLSTM_V_DIM:v0 + 2 * MLSTM_V_DIM]], axis=0)
            vts[(c, pr)] = transpose_bf16(v2).astype(BF16)

    def stage_scores(c):
        q = qkv_ref[rows_of(c), 0:ATTN_Q_WIDTH]
        q_pairs = [q[:, p * LANES:(p + 1) * LANES] for p in range(npair)]
        for side in range(ATTN_KV_HEADS):
            k_side = jnp.concatenate([sides[c - 1][0][side], sides[c][0][side]], axis=0)
            for pp in range(npair // 2):
                q2 = jnp.concatenate([q_pairs[2 * pp], q_pairs[2 * pp + 1]], axis=0)
                s2 = _dot_nt(k_side, q2)
                s_t[(c, side, 2 * pp)] = s2[:, 0:LANES]
                s_t[(c, side, 2 * pp + 1)] = s2[:, LANES:2 * LANES]
        for pr in range(MLSTM_HEADS // 2):
            qpair = mm_ref[rows_of(c), pr * LANES:(pr + 1) * LANES]
            k_pair = mm_ref[rows_of(c), MLSTM_QK_WIDTH + pr * LANES:MLSTM_QK_WIDTH + (pr + 1) * LANES]
            q_m = jnp.concatenate([jnp.where(left_lanes, qpair, zero_b), jnp.where(left_lanes, zero_b, qpair)],
                                  axis=0)
            q_ms[(c, pr)] = q_m
            sc_ts[(c, pr)] = _dot_nt(k_pair, q_m)

    def stage_weights(c):
        variant = first if c == 0 else 0
        for side in range(ATTN_KV_HEADS):
            for p in range(npair):
                j = side * ATTN_GROUP + p
                s2 = s_t.pop((c, side, p))
                comb = jnp.where(cur_t, s2[CHUNK:2 * CHUNK], s2[0:CHUNK]) + btab_ref[variant, j]
                mx = jnp.maximum(jnp.max(comb, axis=0, keepdims=True), sinks_ref[j])
                e = jnp.exp(comb - mx)
                p_t[(c, side, p)] = jnp.concatenate([jnp.where(cur_t, 0.0, e), jnp.where(cur_t, e, 0.0)],
                                                    axis=0).astype(BF16)
                mx_rows[(c, side, p)] = mx
        g8, b8, r_all = gates[c]
        for h in range(MLSTM_HEADS):
            b_row = b8[h:h + 1, :]
            d_t = jnp.where(cur_t, r_all[:, h * LANES:(h + 1) * LANES] + b_row, NEG_INF)
            mloc = jnp.max(d_t, axis=0, keepdims=True)
            wloc = jnp.exp(d_t - mloc)
            scw = (sc_ts[(c, h // 2)][:, (h % 2) * LANES:(h % 2 + 1) * LANES] * wloc).astype(BF16)
            b_last = b_row[:, CHUNK - 1:CHUNK]
            mloc_last = mloc[:, CHUNK - 1:CHUNK]
            u_row = jnp.exp((g8[h:h + 1, :] - b_row) + b_last - mloc_last)
            intra_w[(c, h)] = (b_row, mloc, scw, b_last, mloc_last, u_row)

    def stage_values(c):
        for side in range(ATTN_KV_HEADS):
            vt_side = jnp.concatenate([sides[c - 1][1][side], sides[c][1][side]], axis=1)
            for pp in range(npair // 2):
                p2 = jnp.concatenate([p_t.pop((c, side, 2 * pp)), p_t.pop((c, side, 2 * pp + 1))], axis=1)
                o2 = _dot(vt_side, p2)
                o_t[(c, side, 2 * pp)] = o2[:, 0:LANES]
                o_t[(c, side, 2 * pp + 1)] = o2[:, LANES:2 * LANES]
        for h in range(MLSTM_HEADS):
            b_row, mloc, scw, b_last, mloc_last, u_row = intra_w.pop((c, h))
            pr, half = h // 2, h % 2
            vt_aug = jnp.concatenate([vts[(c, pr)][:, half * LANES:(half + 1) * LANES], ones_rows], axis=0)
            k_pair = mm_ref[rows_of(c), MLSTM_QK_WIDTH + pr * LANES:MLSTM_QK_WIDTH + (pr + 1) * LANES]
            k_hm = jnp.where(left_lanes, k_pair, zero_b) if half == 0 else jnp.where(left_lanes, zero_b, k_pair)
            num_t = _dot(vt_aug, scw)
            upd_t = _dot((vt_aug.astype(F32) * u_row).astype(BF16), k_hm)
            intra[(c, h)] = (b_row, mloc, num_t, b_last, mloc_last, upd_t)

    def stage_attn_out(c):
        out_t = []
        for p in range(npair):
            scaled = []
            for side in range(ATTN_KV_HEADS):
                j = side * ATTN_GROUP + p
                o = o_t.pop((c, side, p))
                den = o[HALF:HALF + 1, :] + jnp.exp(sinks_ref[j] - mx_rows.pop((c, side, p)))
                scaled.append(o[0:HALF] * (1.0 / den))
            out_t.append(jnp.concatenate(scaled, axis=0).astype(BF16))
        for pp in range(npair // 2):
            both = jnp.concatenate([out_t[2 * pp], out_t[2 * pp + 1]], axis=0)
            ao_ref[rows_of(c), 2 * pp * LANES:(2 * pp + 2) * LANES] = transpose_bf16(both).astype(BF16)

    def stage_recurrence(c):
        rows = rows_of(c)
        for pr in range(MLSTM_HEADS // 2):
            c_pair = c_pairs[pr]
            inter_t = _dot_nt(c_pair.astype(BF16), q_ms.pop((c, pr)))
            decs, eus, upds = [], [], []
            for half in range(2):
                h = 2 * pr + half
                b_row, mloc, num_t, b_last, mloc_last, upd_t = intra.pop((c, h))
                m_prev = m_rows[h]
                it = inter_t[:, half * LANES:(half + 1) * LANES]
                a = b_row + m_prev
                m_t = jnp.maximum(a, mloc)
                e_t = jnp.exp(mloc - m_t)
                w_inter = jnp.exp(a - m_t)
                numv = e_t * num_t[0:MLSTM_V_DIM] + w_inter * it[0:MLSTM_V_DIM]
                nq = e_t * num_t[MLSTM_V_DIM:MLSTM_V_DIM + 1] + w_inter * it[MLSTM_V_DIM:MLSTM_V_DIM + 1]
                den = jnp.maximum(jnp.abs(nq), jnp.exp(-m_t))
                ssq = jnp.sum(numv * numv, axis=0, keepdims=True)
                scale = lax.rsqrt(ssq * (1.0 / MLSTM_V_DIM) + NORM_EPS * (den * den))
                hn = (numv * scale * gcol_ref[h]).T
                og = mm_ref[rows, 2 * MLSTM_QK_WIDTH + MLSTM_V_WIDTH + h * MLSTM_V_DIM:
                            2 * MLSTM_QK_WIDTH + MLSTM_V_WIDTH + (h + 1) * MLSTM_V_DIM].astype(F32)
                gate = 0.5 * jnp.tanh(0.5 * og) + 0.5
                hm_ref[rows, h * MLSTM_V_DIM:(h + 1) * MLSTM_V_DIM] = (hn * gate).astype(BF16)
                m_new = jnp.maximum(b_last + m_prev, mloc_last)
                decs.append(jnp.exp(b_last + m_prev - m_new))
                eus.append(jnp.exp(mloc_last - m_new))
                upds.append(upd_t)
                m_rows[h] = m_new
            c_pairs[pr] = (jnp.where(left_row, decs[0], decs[1]) * c_pair
                           + eus[0] * upds[0] + eus[1] * upds[1])

    stages = (stage_prepare, stage_scores, stage_weights, stage_values, stage_attn_out, stage_recurrence)
    for tick in range(nch + max(MIXER_STAGE_LAGS)):
        for stage, lag in zip(stages, MIXER_STAGE_LAGS):
            c = tick - lag
            if 0 <= c < nch:
                stage(c)

    for h in range(MLSTM_HEADS):
        mst_ref[h:h + 1, :] = m_rows[h]
    for pr in range(MLSTM_HEADS // 2):
        cst_ref[pr] = c_pairs[pr]


def _cast_slab_rows(n_rows, n_steps):
    slab = BF16_SUBLANES * pl.cdiv(pl.cdiv(n_rows, n_steps), BF16_SUBLANES)
    while n_rows % slab:
        slab += BF16_SUBLANES
    return slab


def _mixer(sinks, qkv, mm, gt, gbias, normg, cast_weights, batch, seq, nch):
    tm = nch * CHUNK
    nj = seq // tm
    t = batch * seq
    row = lambda w: pl.BlockSpec((tm, w), lambda b, j: (b * nj + j, 0))
    cast_in_specs, cast_out_specs, cast_shapes = [], [], []
    for w, block_perm in cast_weights:
        n_rows, n_cols = w.shape
        slab = n_rows // len(block_perm) if block_perm else _cast_slab_rows(n_rows, batch * nj)
        assert slab % BF16_SUBLANES == 0 and n_rows // slab <= batch * nj
        last = n_rows // slab - 1
        dst_map = lambda b, j, last=last: (jnp.minimum(b * nj + j, last), 0)
        if block_perm:
            assert tuple(block_perm) == tuple((s % 2) * (len(block_perm) // 2) + s // 2
                                              for s in range(len(block_perm)))
            stride = len(block_perm) // 2
            src_map = lambda b, j, last=last, stride=stride: (
                (jnp.minimum(b * nj + j, last) % 2) * stride + jnp.minimum(b * nj + j, last) // 2, 0)
        else:
            src_map = dst_map
        cast_in_specs.append(pl.BlockSpec((slab, n_cols), src_map))
        cast_out_specs.append(pl.BlockSpec((slab, n_cols), dst_map))
        cast_shapes.append(jax.ShapeDtypeStruct(w.shape, BF16))
    return pl.pallas_call(
        functools.partial(_mixer_kernel, nch=nch, n_cast=len(cast_weights)),
        grid=(batch, nj),
        in_specs=[pl.BlockSpec(memory_space=pltpu.SMEM),
                  row(QKV_WIDTH), row(MM_WIDTH), row(GATE_WIDTH),
                  _const_spec(gbias.shape), _const_spec(normg.shape)] + cast_in_specs,
        out_specs=[row(ATTN_Q_WIDTH), row(MLSTM_V_WIDTH)] + cast_out_specs,
        out_shape=[jax.ShapeDtypeStruct((t, ATTN_Q_WIDTH), BF16), jax.ShapeDtypeStruct((t, MLSTM_V_WIDTH), BF16)]
                  + cast_shapes,
        scratch_shapes=[
            pltpu.VMEM((2, ATTN_HEADS, CHUNK, CHUNK), F32),
            pltpu.VMEM((2, CHUNK, ATTN_KV_WIDTH), BF16),
            pltpu.VMEM((MLSTM_HEADS // 2, AUG_ROWS, 2 * MLSTM_QK_DIM), F32),
            pltpu.VMEM((8, LANES), F32),
            pltpu.VMEM((MLSTM_HEADS, MLSTM_V_DIM, LANES), F32),
        ],
        compiler_params=pltpu.CompilerParams(dimension_semantics=("arbitrary", "arbitrary"),
                                             vmem_limit_bytes=VMEM_LIMIT_BYTES),
        name="mixer",
    )(sinks, qkv, mm, gt, gbias, normg, *[w for w, _ in cast_weights])


def _tail_kernel(x_ref, ao_ref, hm_ref, gg_ref, wab_ref, wmb_ref, wo_ref, g2_ref, wg_ref, wu_ref, wd_ref,
                 gf_ref, out_ref, *, ff_split):
    ya = _dot(ao_ref[...], wab_ref[...])
    ym = _dot(hm_ref[...], wmb_ref[...])
    ga = jax.nn.sigmoid(gg_ref[:, 0:D_MODEL].astype(F32))
    gm = jax.nn.sigmoid(gg_ref[:, D_MODEL:2 * D_MODEL].astype(F32))
    z = (ga * ya + gm * ym).astype(BF16)
    x1 = x_ref[...] + _dot(z, wo_ref[...])
    f = (x1 * _rms_scale(x1) * g2_ref[...]).astype(BF16)
    n_tiles = wg_ref.shape[1] // MXU_WIDTH
    bounds = [MXU_WIDTH * ((n_tiles * s + ff_split - 1) // ff_split) for s in range(ff_split + 1)]
    x2 = x1
    for s in range(ff_split):
        cols = slice(bounds[s], bounds[s + 1])
        gte = _dot(f, wg_ref[:, cols])
        up = _dot(f, wu_ref[:, cols])
        hh = (gte * jax.nn.sigmoid(gte) * up).astype(BF16)
        x2 = x2 + _dot(hh, wd_ref[cols, :])
    out_ref[...] = x2 * _rms_scale(x2) * gf_ref[...]


def _tail(x2d, ao, hm, gg, wab, wmb, wo, g2, wg, wu, wd, gf, tm, ff_split):
    t = x2d.shape[0]
    row = lambda w: pl.BlockSpec((tm, w), lambda i: (i, 0))
    consts = (wab, wmb, wo, g2, wg, wu, wd, gf)
    return pl.pallas_call(
        functools.partial(_tail_kernel, ff_split=ff_split),
        grid=(t // tm,),
        in_specs=[row(D_MODEL), row(ATTN_Q_WIDTH), row(MLSTM_V_WIDTH), row(MERGE_WIDTH)]
                 + [_const_spec(c.shape) for c in consts],
        out_specs=row(D_MODEL),
        out_shape=jax.ShapeDtypeStruct((t, D_MODEL), F32),
        compiler_params=pltpu.CompilerParams(dimension_semantics=("arbitrary",),
                                             vmem_limit_bytes=VMEM_LIMIT_BYTES),
        name="tail",
    )(x2d, ao, hm, gg, *consts)


def _layer(x2d, batch, seq, norm1_g, w_in, conv_w, conv_b, i_bias, f_bias, mlstm_norm_g, attn_sinks,
           w_attn_branch, w_mlstm_branch, w_out, norm2_g, w_ffn_gate, w_ffn_up, w_ffn_down, out_g):
    w_if = w_in[:, W_GATE_OFF:W_GATE_OFF + 2 * MLSTM_HEADS]
    wgt = jnp.pad(w_if, ((0, 0), (0, GATE_WIDTH - 2 * MLSTM_HEADS)))
    gbias = jnp.pad(jnp.concatenate([i_bias, f_bias]), (0, GATE_WIDTH - 2 * MLSTM_HEADS)).reshape(1, GATE_WIDTH)

    qkv, mm, gt, gg = _inproj(x2d, norm1_g.reshape(1, D_MODEL), w_in.astype(BF16), wgt.astype(BF16), conv_w,
                              conv_b.reshape(1, -1), tm=INPROJ_TM, seq=seq)
    tail_weights = [(w_attn_branch, Q_HEAD_ORDER), (w_mlstm_branch, None), (w_out, None),
                    (w_ffn_gate, None), (w_ffn_up, None), (w_ffn_down, None)]
    ao, hm, wab, wmb, wo, wg, wu, wd = _mixer(attn_sinks, qkv, mm, gt, gbias,
                                              mlstm_norm_g.reshape(1, MLSTM_V_WIDTH), tail_weights, batch, seq,
                                              nch=MIXER_CHUNKS)
    return _tail(x2d, ao, hm, gg, wab, wmb, wo, norm2_g.reshape(1, D_MODEL), wg, wu, wd,
                 out_g.reshape(1, D_MODEL), tm=TAIL_TM, ff_split=TAIL_FF_SPLIT)


def kernel(x, norm1_g, w_in, conv_w, conv_b, i_bias, f_bias, mlstm_norm_g, attn_sinks, w_attn_branch,
           w_mlstm_branch, w_out, norm2_g, w_ffn_gate, w_ffn_up, w_ffn_down, final_norm_g):
    batch, seq, d = x.shape
    depth = norm1_g.shape[0]
    assert depth == 1 and d == D_MODEL
    assert seq % (MIXER_CHUNKS * CHUNK) == 0 and seq % INPROJ_TM == 0 and (batch * seq) % TAIL_TM == 0
    assert w_ffn_gate.shape[-1] % MXU_WIDTH == 0
    out = _layer(x.reshape(batch * seq, d), batch, seq, norm1_g[0], w_in[0], conv_w[0], conv_b[0], i_bias[0],
                 f_bias[0], mlstm_norm_g[0], attn_sinks[0], w_attn_branch[0], w_mlstm_branch[0], w_out[0],
                 norm2_g[0], w_ffn_gate[0], w_ffn_up[0], w_ffn_down[0], final_norm_g)
    return out.reshape(batch, seq, d)
```

```python
import functools

import jax
import jax.numpy as jnp
from jax import lax
from jax.experimental import pallas as pl
from jax.experimental.pallas import tpu as pltpu

D_MODEL = 1024
ATTN_HEADS = 8
ATTN_KV_HEADS = 2
ATTN_HEAD_DIM = 64
ATTN_GROUP = ATTN_HEADS // ATTN_KV_HEADS
WINDOW = 128
ATTN_Q_WIDTH = ATTN_HEADS * ATTN_HEAD_DIM
ATTN_KV_WIDTH = ATTN_KV_HEADS * ATTN_HEAD_DIM
MLSTM_HEADS = 4
MLSTM_QK_DIM = 64
MLSTM_V_DIM = 128
MLSTM_QK_WIDTH = MLSTM_HEADS * MLSTM_QK_DIM
MLSTM_V_WIDTH = MLSTM_HEADS * MLSTM_V_DIM
CHUNK = 128
CONV_WIDTH = 4
NORM_EPS = 1e-6

LANES = 128
BF16_SUBLANES = 16
MXU_WIDTH = 256
HALF = LANES // 2
CONV_PAD = 8
GATE_ROWS = 16
NQ_ROWS = BF16_SUBLANES
AUG_ROWS = MLSTM_V_DIM + NQ_ROWS

INPROJ_TM = 1024
INPROJ_ROW_BLOCKS = 2
MIXER_CHUNKS = 16
MIXER_STAGE_LAGS = (0, 1, 2, 3, 4, 4)
TAIL_TM = 512
TAIL_FF_SPLIT = 1
VMEM_LIMIT_BYTES = 56 * 1024 * 1024

QKV_WIDTH = ATTN_Q_WIDTH + 2 * ATTN_KV_WIDTH
MM_WIDTH = 2 * MLSTM_QK_WIDTH + 2 * MLSTM_V_WIDTH
GATE_WIDTH = LANES
MERGE_WIDTH = 2 * D_MODEL
W_MQK_OFF = QKV_WIDTH
W_MVO_OFF = W_MQK_OFF + 2 * MLSTM_QK_WIDTH
W_GATE_OFF = W_MVO_OFF + 2 * MLSTM_V_WIDTH
W_MERGE_OFF = W_GATE_OFF + 2 * MLSTM_HEADS

Q_HEAD_ORDER = tuple(h * ATTN_GROUP + g for g in range(ATTN_GROUP) for h in range(ATTN_KV_HEADS))

BF16 = jnp.bfloat16
F32 = jnp.float32
NEG_INF = float("-inf")


def _dot(a, b):
    return jnp.dot(a, b, preferred_element_type=F32)


def _dot_nt(a, b):
    return lax.dot_general(a, b, (((1,), (1,)), ((), ())), preferred_element_type=F32)


def _rms_scale(x):
    return lax.rsqrt(jnp.mean(x * x, axis=-1, keepdims=True) + NORM_EPS)


def _const_spec(shape):
    nd = len(shape)
    return pl.BlockSpec(shape, lambda *_: (0,) * nd, pipeline_mode=pl.Buffered(1))


def _inproj_kernel(x_ref, g_ref, w_ref, wgt_ref, convw_ref, convb_ref,
                   qkv_ref, mm_ref, gt_ref, gg_ref, conv_ref, wq_ref, wgg_ref, *, tiles_per_seq):
    tm = x_ref.shape[0]

    @pl.when(pl.program_id(0) == 0)
    def _():
        src = lax.broadcasted_iota(jnp.int32, (ATTN_Q_WIDTH, ATTN_Q_WIDTH), 0)
        dst = lax.broadcasted_iota(jnp.int32, (ATTN_Q_WIDTH, ATTN_Q_WIDTH), 1)
        head = ((dst % LANES) // ATTN_HEAD_DIM) * ATTN_GROUP + dst // LANES
        perm = (src == head * ATTN_HEAD_DIM + dst % ATTN_HEAD_DIM).astype(BF16)
        wq_ref[...] = _dot(w_ref[:, 0:ATTN_Q_WIDTH], perm).astype(BF16)
        wgg_ref[...] = w_ref[:, W_MERGE_OFF:W_MERGE_OFF + MERGE_WIDTH]

    @pl.when(pl.program_id(0) % tiles_per_seq == 0)
    def _():
        conv_ref[0:CONV_PAD, :] = jnp.zeros((CONV_PAD, 2 * MLSTM_QK_WIDTH), F32)

    q_scale = ATTN_HEAD_DIM ** -0.5
    lane = lax.broadcasted_iota(jnp.int32, (1, 2 * MLSTM_QK_WIDTH), 1)
    k_scale = jnp.where(lane < MLSTM_QK_WIDTH, 1.0, MLSTM_QK_DIM ** -0.5)
    rb = tm // INPROJ_ROW_BLOCKS
    for r in range(INPROJ_ROW_BLOCKS):
        rows = slice(r * rb, (r + 1) * rb)
        x = x_ref[rows, :]
        u = (x * _rms_scale(x) * g_ref[...]).astype(BF16)
        conv_ref[CONV_PAD + r * rb:CONV_PAD + (r + 1) * rb, :] = _dot(
            u, w_ref[:, W_MQK_OFF:W_MQK_OFF + 2 * MLSTM_QK_WIDTH])
        qkv_ref[rows, 0:ATTN_Q_WIDTH] = (_dot(u, wq_ref[...]) * q_scale).astype(BF16)
        qkv_ref[rows, ATTN_Q_WIDTH:QKV_WIDTH] = _dot(u, w_ref[:, ATTN_Q_WIDTH:QKV_WIDTH]).astype(BF16)
        mm_ref[rows, 2 * MLSTM_QK_WIDTH:MM_WIDTH] = _dot(
            u, w_ref[:, W_MVO_OFF:W_MVO_OFF + 2 * MLSTM_V_WIDTH]).astype(BF16)
        gt_ref[rows, :] = _dot(u, wgt_ref[...])
        gg_ref[rows, :] = _dot(u, wgg_ref[...]).astype(BF16)
        xe = conv_ref[r * rb:r * rb + CONV_PAD + rb, :]
        acc = xe * convw_ref[0:1, :]
        for t in range(1, CONV_WIDTH):
            acc = pltpu.roll(acc, 1, axis=0) + xe * convw_ref[t:t + 1, :]
        acc = acc[CONV_PAD:CONV_PAD + rb, :] + convb_ref[...]
        mm_ref[rows, 0:2 * MLSTM_QK_WIDTH] = (acc * jax.nn.sigmoid(acc) * k_scale).astype(BF16)
    conv_ref[0:CONV_PAD, :] = conv_ref[tm:tm + CONV_PAD, :]


def _inproj(x2d, g1, w_all, wgt, convw, convb, tm, seq):
    t = x2d.shape[0]
    row = lambda w: pl.BlockSpec((tm, w), lambda i: (i, 0))
    consts = (g1, w_all, wgt, convw, convb)
    return pl.pallas_call(
        functools.partial(_inproj_kernel, tiles_per_seq=seq // tm),
        grid=(t // tm,),
        in_specs=[row(D_MODEL)] + [_const_spec(c.shape) for c in consts],
        out_specs=[row(QKV_WIDTH), row(MM_WIDTH), row(GATE_WIDTH), row(MERGE_WIDTH)],
        out_shape=[jax.ShapeDtypeStruct((t, QKV_WIDTH), BF16), jax.ShapeDtypeStruct((t, MM_WIDTH), BF16),
                   jax.ShapeDtypeStruct((t, GATE_WIDTH), F32), jax.ShapeDtypeStruct((t, MERGE_WIDTH), BF16)],
        scratch_shapes=[pltpu.VMEM((CONV_PAD + tm, 2 * MLSTM_QK_WIDTH), F32),
                        pltpu.VMEM((D_MODEL, ATTN_Q_WIDTH), BF16), pltpu.VMEM((D_MODEL, MERGE_WIDTH), BF16)],
        compiler_params=pltpu.CompilerParams(dimension_semantics=("arbitrary",),
                                             vmem_limit_bytes=VMEM_LIMIT_BYTES),
        name="inproj",
    )(x2d, *consts)


def _log_sigmoid(x):
    return jnp.minimum(x, 0.0) - jnp.log1p(jnp.exp(-jnp.abs(x)))


def _split3(x):
    hi = x.astype(BF16)
    r1 = x - hi.astype(F32)
    mid = r1.astype(BF16)
    lo = (r1 - mid.astype(F32)).astype(BF16)
    return hi, mid, lo


def _mixer_kernel(sinks_ref, qkv_ref, mm_ref, gt_ref, gbias_ref, normg_ref, *rest, nch, n_cast):
    cast_in, (ao_ref, hm_ref), rest = rest[:n_cast], rest[n_cast:n_cast + 2], rest[n_cast + 2:]
    cast_out, (btab_ref, kvp_ref, cst_ref, mst_ref, gcol_ref) = rest[:n_cast], rest[n_cast:]
    b_idx = pl.program_id(0)
    j_idx = pl.program_id(1)

    for src, dst in zip(cast_in, cast_out):
        dst[...] = src[...].astype(BF16)

    row_c = lax.broadcasted_iota(jnp.int32, (CHUNK, CHUNK), 0)
    col_c = lax.broadcasted_iota(jnp.int32, (CHUNK, CHUNK), 1)

    @pl.when((b_idx == 0) & (j_idx == 0))
    def _():
        cur = row_c <= col_c
        dist = jnp.where(cur, col_c - row_c, col_c - row_c + WINDOW).astype(F32)
        for j in range(ATTN_HEADS):
            slope = 2.0 ** (-8.0 * (j + 1) / ATTN_HEADS)
            btab_ref[0, j] = -slope * dist
            btab_ref[1, j] = jnp.where(cur, -slope * dist, NEG_INF)
        for h in range(MLSTM_HEADS):
            g_row = normg_ref[:, h * MLSTM_V_DIM:(h + 1) * MLSTM_V_DIM]
            gcol_ref[h] = jnp.broadcast_to(g_row, (MLSTM_V_DIM, LANES)).T

    @pl.when(j_idx == 0)
    def _():
        kvp_ref[...] = jnp.zeros_like(kvp_ref)
        cst_ref[...] = jnp.zeros_like(cst_ref)
        mst_ref[...] = jnp.zeros_like(mst_ref)

    cur_t = row_c <= col_c
    triu_b = cur_t.astype(BF16)
    eye_b = (row_c == col_c).astype(BF16)

    def transpose_bf16(x):
        return _dot_nt(eye_b, x)

    left_lanes = col_c < HALF
    ones_rows = jnp.ones((NQ_ROWS, LANES), BF16)
    left_row = lax.broadcasted_iota(jnp.int32, (1, LANES), 1) < HALF
    zero_b = jnp.zeros((), BF16)
    first = (j_idx == 0).astype(jnp.int32)

    def kv_sides(k_blk, v_blk):
        vt = transpose_bf16(v_blk).astype(BF16)
        k_sides = (jnp.where(left_lanes, k_blk, zero_b), jnp.where(left_lanes, zero_b, k_blk))
        vt_sides = tuple(jnp.concatenate([vt[s * HALF:(s + 1) * HALF], ones_rows], axis=0)
                         for s in range(ATTN_KV_HEADS))
        return k_sides, vt_sides

    k_off = ATTN_Q_WIDTH
    v_off = ATTN_Q_WIDTH + ATTN_KV_WIDTH
    prev_sides = kv_sides(kvp_ref[0], kvp_ref[1])
    last = slice((nch - 1) * CHUNK, nch * CHUNK)
    kvp_ref[0] = qkv_ref[last, k_off:k_off + ATTN_KV_WIDTH]
    kvp_ref[1] = qkv_ref[last, v_off:v_off + ATTN_KV_WIDTH]

    m_rows = [mst_ref[h:h + 1, :] for h in range(MLSTM_HEADS)]
    c_pairs = [cst_ref[pr] for pr in range(MLSTM_HEADS // 2)]

    npair = ATTN_GROUP
    rows_of = lambda c: slice(c * CHUNK, (c + 1) * CHUNK)
    row8 = lax.broadcasted_iota(jnp.int32, (GATE_ROWS, LANES), 0)
    head_rows = row8 < MLSTM_HEADS
    sides = {-1: prev_sides}
    gates, s_t, sc_ts, q_ms, vts, p_t, mx_rows, intra_w, o_t, intra = {}, {}, {}, {}, {}, {}, {}, {}, {}, {}

    def stage_prepare(c):
        rows = rows_of(c)
        sides[c] = kv_sides(qkv_ref[rows, k_off:k_off + ATTN_KV_WIDTH], qkv_ref[rows, v_off:v_off + ATTN_KV_WIDTH])
        g8 = (gt_ref[rows, :] + gbias_ref[...]).T[0:GATE_ROWS, :]
        f8 = pltpu.roll(g8, GATE_ROWS - MLSTM_HEADS, axis=0)
        lf_parts = _split3(jnp.where(head_rows, _log_sigmoid(f8), 0.0))
        b8 = sum(_dot(part, triu_b) for part in lf_parts)
        r8 = jnp.where(head_rows, g8 - b8, 0.0)
        r_t = jnp.concatenate([r8, jnp.zeros((CHUNK - GATE_ROWS, LANES), F32)], axis=0).T
        r_all = jnp.concatenate(
            [jnp.broadcast_to(r_t[:, h:h + 1], (CHUNK, LANES)) for h in range(MLSTM_HEADS)], axis=1)
        gates[c] = (g8, b8, r_all)
        for pr in range(MLSTM_HEADS // 2):
            v0 = 2 * MLSTM_QK_WIDTH + 2 * pr * MLSTM_V_DIM
            v2 = jnp.concatenate([mm_ref[rows, v0:v0 + MLSTM_V_DIM],
                                  mm_ref[rows, v0 + MLSTM_V_DIM:v0 + 2 * MLSTM_V_DIM]], axis=0)
            vts[(c, pr)] = transpose_bf16(v2).astype(BF16)

    def stage_scores(c):
        q = qkv_ref[rows_of(c), 0:ATTN_Q_WIDTH]
        q_pairs = [q[:, p * LANES:(p + 1) * LANES] for p in range(npair)]
        for side in range(ATTN_KV_HEADS):
            k_side = jnp.concatenate([sides[c - 1][0][side], sides[c][0][side]], axis=0)
            for pp in range(npair // 2):
                q2 = jnp.concatenate([q_pairs[2 * pp], q_pairs[2 * pp + 1]], axis=0)
                s2 = _dot_nt(k_side, q2)
                s_t[(c, side, 2 * pp)] = s2[:, 0:LANES]
                s_t[(c, side, 2 * pp + 1)] = s2[:, LANES:2 * LANES]
        for pr in range(MLSTM_HEADS // 2):
            qpair = mm_ref[rows_of(c), pr * LANES:(pr + 1) * LANES]
            k_pair = mm_ref[rows_of(c), MLSTM_QK_WIDTH + pr * LANES:MLSTM_QK_WIDTH + (pr + 1) * LANES]
            q_m = jnp.concatenate([jnp.where(left_lanes, qpair, zero_b), jnp.where(left_lanes, zero_b, qpair)],
                                  axis=0)
            q_ms[(c, pr)] = q_m
            sc_ts[(c, pr)] = _dot_nt(k_pair, q_m)

    def stage_weights(c):
        variant = first if c == 0 else 0
        for side in range(ATTN_KV_HEADS):
            for p in range(npair):
                j = side * ATTN_GROUP + p
                s2 = s_t.pop((c, side, p))
                comb = jnp.where(cur_t, s2[CHUNK:2 * CHUNK], s2[0:CHUNK]) + btab_ref[variant, j]
                mx = jnp.maximum(jnp.max(comb, axis=0, keepdims=True), sinks_ref[j])
                e = jnp.exp(comb - mx)
                p_t[(c, side, p)] = jnp.concatenate([jnp.where(cur_t, 0.0, e), jnp.where(cur_t, e, 0.0)],
                                                    axis=0).astype(BF16)
                mx_rows[(c, side, p)] = mx
        g8, b8, r_all = gates[c]
        for h in range(MLSTM_HEADS):
            b_row = b8[h:h + 1, :]
            d_t = jnp.where(cur_t, r_all[:, h * LANES:(h + 1) * LANES] + b_row, NEG_INF)
            mloc = jnp.max(d_t, axis=0, keepdims=True)
            wloc = jnp.exp(d_t - mloc)
            scw = (sc_ts[(c, h // 2)][:, (h % 2) * LANES:(h % 2 + 1) * LANES] * wloc).astype(BF16)
            b_last = b_row[:, CHUNK - 1:CHUNK]
            mloc_last = mloc[:, CHUNK - 1:CHUNK]
            u_row = jnp.exp((g8[h:h + 1, :] - b_row) + b_last - mloc_last)
            intra_w[(c, h)] = (b_row, mloc, scw, b_last, mloc_last, u_row)

    def stage_values(c):
        for side in range(ATTN_KV_HEADS):
            vt_side = jnp.concatenate([sides[c - 1][1][side], sides[c][1][side]], axis=1)
            for pp in range(npair // 2):
                p2 = jnp.concatenate([p_t.pop((c, side, 2 * pp)), p_t.pop((c, side, 2 * pp + 1))], axis=1)
                o2 = _dot(vt_side, p2)
                o_t[(c, side, 2 * pp)] = o2[:, 0:LANES]
                o_t[(c, side, 2 * pp + 1)] = o2[:, LANES:2 * LANES]
        for h in range(MLSTM_HEADS):
            b_row, mloc, scw, b_last, mloc_last, u_row = intra_w.pop((c, h))
            pr, half = h // 2, h % 2
            vt_aug = jnp.concatenate([vts[(c, pr)][:, half * LANES:(half + 1) * LANES], ones_rows], axis=0)
            k_pair = mm_ref[rows_of(c), MLSTM_QK_WIDTH + pr * LANES:MLSTM_QK_WIDTH + (pr + 1) * LANES]
            k_hm = jnp.where(left_lanes, k_pair, zero_b) if half == 0 else jnp.where(left_lanes, zero_b, k_pair)
            num_t = _dot(vt_aug, scw)
            upd_t = _dot((vt_aug.astype(F32) * u_row).astype(BF16), k_hm)
            intra[(c, h)] = (b_row, mloc, num_t, b_last, mloc_last, upd_t)

    def stage_attn_out(c):
        out_t = []
        for p in range(npair):
            scaled = []
            for side in range(ATTN_KV_HEADS):
                j = side * ATTN_GROUP + p
                o = o_t.pop((c, side, p))
                den = o[HALF:HALF + 1, :] + jnp.exp(sinks_ref[j] - mx_rows.pop((c, side, p)))
                scaled.append(o[0:HALF] * (1.0 / den))
            out_t.append(jnp.concatenate(scaled, axis=0).astype(BF16))
        for pp in range(npair // 2):
            both = jnp.concatenate([out_t[2 * pp], out_t[2 * pp + 1]], axis=0)
            ao_ref[rows_of(c), 2 * pp * LANES:(2 * pp + 2) * LANES] = transpose_bf16(both).astype(BF16)

    def stage_recurrence(c):
        rows = rows_of(c)
        for pr in range(MLSTM_HEADS // 2):
            c_pair = c_pairs[pr]
            inter_t = _dot_nt(c_pair.astype(BF16), q_ms.pop((c, pr)))
            decs, eus, upds = [], [], []
            for half in range(2):
                h = 2 * pr + half
                b_row, mloc, num_t, b_last, mloc_last, upd_t = intra.pop((c, h))
                m_prev = m_rows[h]
                it = inter_t[:, half * LANES:(half + 1) * LANES]
                a = b_row + m_prev
                m_t = jnp.maximum(a, mloc)
                e_t = jnp.exp(mloc - m_t)
                w_inter = jnp.exp(a - m_t)
                numv = e_t * num_t[0:MLSTM_V_DIM] + w_inter * it[0:MLSTM_V_DIM]
                nq = e_t * num_t[MLSTM_V_DIM:MLSTM_V_DIM + 1] + w_inter * it[MLSTM_V_DIM:MLSTM_V_DIM + 1]
                den = jnp.maximum(jnp.abs(nq), jnp.exp(-m_t))
                ssq = jnp.sum(numv * numv, axis=0, keepdims=True)
                scale = lax.rsqrt(ssq * (1.0 / MLSTM_V_DIM) + NORM_EPS * (den * den))
                hn = (numv * scale * gcol_ref[h]).T
                og = mm_ref[rows, 2 * MLSTM_QK_WIDTH + MLSTM_V_WIDTH + h * MLSTM_V_DIM:
                            2 * MLSTM_QK_WIDTH + MLSTM_V_WIDTH + (h + 1) * MLSTM_V_DIM].astype(F32)
                gate = 0.5 * jnp.tanh(0.5 * og) + 0.5
                hm_ref[rows, h * MLSTM_V_DIM:(h + 1) * MLSTM_V_DIM] = (hn * gate).astype(BF16)
                m_new = jnp.maximum(b_last + m_prev, mloc_last)
                decs.append(jnp.exp(b_last + m_prev - m_new))
                eus.append(jnp.exp(mloc_last - m_new))
                upds.append(upd_t)
                m_rows[h] = m_new
            c_pairs[pr] = (jnp.where(left_row, decs[0], decs[1]) * c_pair
                           + eus[0] * upds[0] + eus[1] * upds[1])

    stages = (stage_prepare, stage_scores, stage_weights, stage_values, stage_attn_out, stage_recurrence)
    for tick in range(nch + max(MIXER_STAGE_LAGS)):
        for stage, lag in zip(stages, MIXER_STAGE_LAGS):
            c = tick - lag
            if 0 <= c < nch:
                stage(c)

    for h in range(MLSTM_HEADS):
        mst_ref[h:h + 1, :] = m_rows[h]
    for pr in range(MLSTM_HEADS // 2):
        cst_ref[pr] = c_pairs[pr]


def _cast_slab_rows(n_rows, n_steps):
    slab = BF16_SUBLANES * pl.cdiv(pl.cdiv(n_rows, n_steps), BF16_SUBLANES)
    while n_rows % slab:
        slab += BF16_SUBLANES
    return slab


def _mixer(sinks, qkv, mm, gt, gbias, normg, cast_weights, batch, seq, nch):
    tm = nch * CHUNK
    nj = seq // tm
    t = batch * seq
    row = lambda w: pl.BlockSpec((tm, w), lambda b, j: (b * nj + j, 0))
    cast_in_specs, cast_out_specs, cast_shapes = [], [], []
    for w, block_perm in cast_weights:
        n_rows, n_cols = w.shape
        slab = n_rows // len(block_perm) if block_perm else _cast_slab_rows(n_rows, batch * nj)
        assert slab % BF16_SUBLANES == 0 and n_rows // slab <= batch * nj
        last = n_rows // slab - 1
        dst_map = lambda b, j, last=last: (jnp.minimum(b * nj + j, last), 0)
        if block_perm:
            assert tuple(block_perm) == tuple((s % 2) * (len(block_perm) // 2) + s // 2
                                              for s in range(len(block_perm)))
            stride = len(block_perm) // 2
            src_map = lambda b, j, last=last, stride=stride: (
                (jnp.minimum(b * nj + j, last) % 2) * stride + jnp.minimum(b * nj + j, last) // 2, 0)
        else:
            src_map = dst_map
        cast_in_specs.append(pl.BlockSpec((slab, n_cols), src_map))
        cast_out_specs.append(pl.BlockSpec((slab, n_cols), dst_map))
        cast_shapes.append(jax.ShapeDtypeStruct(w.shape, BF16))
    return pl.pallas_call(
        functools.partial(_mixer_kernel, nch=nch, n_cast=len(cast_weights)),
        grid=(batch, nj),
        in_specs=[pl.BlockSpec(memory_space=pltpu.SMEM),
                  row(QKV_WIDTH), row(MM_WIDTH), row(GATE_WIDTH),
                  _const_spec(gbias.shape), _const_spec(normg.shape)] + cast_in_specs,
        out_specs=[row(ATTN_Q_WIDTH), row(MLSTM_V_WIDTH)] + cast_out_specs,
        out_shape=[jax.ShapeDtypeStruct((t, ATTN_Q_WIDTH), BF16), jax.ShapeDtypeStruct((t, MLSTM_V_WIDTH), BF16)]
                  + cast_shapes,
        scratch_shapes=[
            pltpu.VMEM((2, ATTN_HEADS, CHUNK, CHUNK), F32),
            pltpu.VMEM((2, CHUNK, ATTN_KV_WIDTH), BF16),
            pltpu.VMEM((MLSTM_HEADS // 2, AUG_ROWS, 2 * MLSTM_QK_DIM), F32),
            pltpu.VMEM((8, LANES), F32),
            pltpu.VMEM((MLSTM_HEADS, MLSTM_V_DIM, LANES), F32),
        ],
        compiler_params=pltpu.CompilerParams(dimension_semantics=("arbitrary", "arbitrary"),
                                             vmem_limit_bytes=VMEM_LIMIT_BYTES),
        name="mixer",
    )(sinks, qkv, mm, gt, gbias, normg, *[w for w, _ in cast_weights])


def _tail_kernel(x_ref, ao_ref, hm_ref, gg_ref, wab_ref, wmb_ref, wo_ref, g2_ref, wg_ref, wu_ref, wd_ref,
                 gf_ref, out_ref, *, ff_split):
    ya = _dot(ao_ref[...], wab_ref[...])
    ym = _dot(hm_ref[...], wmb_ref[...])
    ga = jax.nn.sigmoid(gg_ref[:, 0:D_MODEL].astype(F32))
    gm = jax.nn.sigmoid(gg_ref[:, D_MODEL:2 * D_MODEL].astype(F32))
    z = (ga * ya + gm * ym).astype(BF16)
    x1 = x_ref[...] + _dot(z, wo_ref[...])
    f = (x1 * _rms_scale(x1) * g2_ref[...]).astype(BF16)
    n_tiles = wg_ref.shape[1] // MXU_WIDTH
    bounds = [MXU_WIDTH * ((n_tiles * s + ff_split - 1) // ff_split) for s in range(ff_split + 1)]
    x2 = x1
    for s in range(ff_split):
        cols = slice(bounds[s], bounds[s + 1])
        gte = _dot(f, wg_ref[:, cols])
        up = _dot(f, wu_ref[:, cols])
        hh = (gte * jax.nn.sigmoid(gte) * up).astype(BF16)
        x2 = x2 + _dot(hh, wd_ref[cols, :])
    out_ref[...] = x2 * _rms_scale(x2) * gf_ref[...]


def _tail(x2d, ao, hm, gg, wab, wmb, wo, g2, wg, wu, wd, gf, tm, ff_split):
    t = x2d.shape[0]
    row = lambda w: pl.BlockSpec((tm, w), lambda i: (i, 0))
    consts = (wab, wmb, wo, g2, wg, wu, wd, gf)
    return pl.pallas_call(
        functools.partial(_tail_kernel, ff_split=ff_split),
        grid=(t // tm,),
        in_specs=[row(D_MODEL), row(ATTN_Q_WIDTH), row(MLSTM_V_WIDTH), row(MERGE_WIDTH)]
                 + [_const_spec(c.shape) for c in consts],
        out_specs=row(D_MODEL),
        out_shape=jax.ShapeDtypeStruct((t, D_MODEL), F32),
        compiler_params=pltpu.CompilerParams(dimension_semantics=("arbitrary",),
                                             vmem_limit_bytes=VMEM_LIMIT_BYTES),
        name="tail",
    )(x2d, ao, hm, gg, *consts)


def _layer(x2d, batch, seq, norm1_g, w_in, conv_w, conv_b, i_bias, f_bias, mlstm_norm_g, attn_sinks,
           w_attn_branch, w_mlstm_branch, w_out, norm2_g, w_ffn_gate, w_ffn_up, w_ffn_down, out_g):
    w_if = w_in[:, W_GATE_OFF:W_GATE_OFF + 2 * MLSTM_HEADS]
    wgt = jnp.pad(w_if, ((0, 0), (0, GATE_WIDTH - 2 * MLSTM_HEADS)))
    gbias = jnp.pad(jnp.concatenate([i_bias, f_bias]), (0, GATE_WIDTH - 2 * MLSTM_HEADS)).reshape(1, GATE_WIDTH)

    qkv, mm, gt, gg = _inproj(x2d, norm1_g.reshape(1, D_MODEL), w_in.astype(BF16), wgt.astype(BF16), conv_w,
                              conv_b.reshape(1, -1), tm=INPROJ_TM, seq=seq)
    tail_weights = [(w_attn_branch, Q_HEAD_ORDER), (w_mlstm_branch, None), (w_out, None),
                    (w_ffn_gate, None), (w_ffn_up, None), (w_ffn_down, None)]
    ao, hm, wab, wmb, wo, wg, wu, wd = _mixer(attn_sinks, qkv, mm, gt, gbias,
                                              mlstm_norm_g.reshape(1, MLSTM_V_WIDTH), tail_weights, batch, seq,
                                              nch=MIXER_CHUNKS)
    return _tail(x2d, ao, hm, gg, wab, wmb, wo, norm2_g.reshape(1, D_MODEL), wg, wu, wd,
                 out_g.reshape(1, D_MODEL), tm=TAIL_TM, ff_split=TAIL_FF_SPLIT)


def kernel(x, norm1_g, w_in, conv_w, conv_b, i_bias, f_bias, mlstm_norm_g, attn_sinks, w_attn_branch,
           w_mlstm_branch, w_out, norm2_g, w_ffn_gate, w_ffn_up, w_ffn_down, final_norm_g):
    batch, seq, d = x.shape
    depth = norm1_g.shape[0]
    assert depth == 1 and d == D_MODEL
    assert seq % (MIXER_CHUNKS * CHUNK) == 0 and seq % INPROJ_TM == 0 and (batch * seq) % TAIL_TM == 0
    assert w_ffn_gate.shape[-1] % MXU_WIDTH == 0
    out = _layer(x.reshape(batch * seq, d), batch, seq, norm1_g[0], w_in[0], conv_w[0], conv_b[0], i_bias[0],
                 f_bias[0], mlstm_norm_g[0], attn_sinks[0], w_attn_branch[0], w_mlstm_branch[0], w_out[0],
                 norm2_g[0], w_ffn_gate[0], w_ffn_up[0], w_ffn_down[0], final_norm_g)
    return out.reshape(batch, seq, d)
```

```python
import functools

import jax
import jax.numpy as jnp
from jax import lax
from jax.experimental import pallas as pl
from jax.experimental.pallas import tpu as pltpu

D_MODEL = 1024
ATTN_HEADS = 8
ATTN_KV_HEADS = 2
ATTN_HEAD_DIM = 64
ATTN_GROUP = ATTN_HEADS // ATTN_KV_HEADS
WINDOW = 128
ATTN_Q_WIDTH = ATTN_HEADS * ATTN_HEAD_DIM
ATTN_KV_WIDTH = ATTN_KV_HEADS * ATTN_HEAD_DIM
MLSTM_HEADS = 4
MLSTM_QK_DIM = 64
MLSTM_V_DIM = 128
MLSTM_QK_WIDTH = MLSTM_HEADS * MLSTM_QK_DIM
MLSTM_V_WIDTH = MLSTM_HEADS * MLSTM_V_DIM
CHUNK = 128
CONV_WIDTH = 4
NORM_EPS = 1e-6

LANES = 128
BF16_SUBLANES = 16
MXU_WIDTH = 256
HALF = LANES // 2
CONV_PAD = 8
GATE_ROWS = 16
NQ_ROWS = BF16_SUBLANES
AUG_ROWS = MLSTM_V_DIM + NQ_ROWS

INPROJ_TM = 1024
INPROJ_ROW_BLOCKS = 2
MIXER_CHUNKS = 16
MIXER_STAGE_LAGS = (0, 1, 2, 3, 4, 4)
TAIL_TM = 512
TAIL_FF_SPLIT = 1
VMEM_LIMIT_BYTES = 56 * 1024 * 1024

QKV_WIDTH = ATTN_Q_WIDTH + 2 * ATTN_KV_WIDTH
MM_WIDTH = 2 * MLSTM_QK_WIDTH + 2 * MLSTM_V_WIDTH
GATE_WIDTH = LANES
MERGE_WIDTH = 2 * D_MODEL
W_MQK_OFF = QKV_WIDTH
W_MVO_OFF = W_MQK_OFF + 2 * MLSTM_QK_WIDTH
W_GATE_OFF = W_MVO_OFF + 2 * MLSTM_V_WIDTH
W_MERGE_OFF = W_GATE_OFF + 2 * MLSTM_HEADS

Q_HEAD_ORDER = tuple(h * ATTN_GROUP + g for g in range(ATTN_GROUP) for h in range(ATTN_KV_HEADS))

BF16 = jnp.bfloat16
F32 = jnp.float32
NEG_INF = float("-inf")


def _dot(a, b):
    return jnp.dot(a, b, preferred_element_type=F32)


def _dot_nt(a, b):
    return lax.dot_general(a, b, (((1,), (1,)), ((), ())), preferred_element_type=F32)


def _rms_scale(x):
    return lax.rsqrt(jnp.mean(x * x, axis=-1, keepdims=True) + NORM_EPS)


def _const_spec(shape):
    nd = len(shape)
    return pl.BlockSpec(shape, lambda *_: (0,) * nd, pipeline_mode=pl.Buffered(1))


def _inproj_kernel(x_ref, g_ref, w_ref, wgt_ref, convw_ref, convb_ref,
                   qkv_ref, mm_ref, gt_ref, gg_ref, conv_ref, wq_ref, wgg_ref, *, tiles_per_seq):
    tm = x_ref.shape[0]

    @pl.when(pl.program_id(0) == 0)
    def _():
        src = lax.broadcasted_iota(jnp.int32, (ATTN_Q_WIDTH, ATTN_Q_WIDTH), 0)
        dst = lax.broadcasted_iota(jnp.int32, (ATTN_Q_WIDTH, ATTN_Q_WIDTH), 1)
        head = ((dst % LANES) // ATTN_HEAD_DIM) * ATTN_GROUP + dst // LANES
        perm = (src == head * ATTN_HEAD_DIM + dst % ATTN_HEAD_DIM).astype(BF16)
        wq_ref[...] = _dot(w_ref[:, 0:ATTN_Q_WIDTH], perm).astype(BF16)
        wgg_ref[...] = w_ref[:, W_MERGE_OFF:W_MERGE_OFF + MERGE_WIDTH]

    @pl.when(pl.program_id(0) % tiles_per_seq == 0)
    def _():
        conv_ref[0:CONV_PAD, :] = jnp.zeros((CONV_PAD, 2 * MLSTM_QK_WIDTH), F32)

    q_scale = ATTN_HEAD_DIM ** -0.5
    lane = lax.broadcasted_iota(jnp.int32, (1, 2 * MLSTM_QK_WIDTH), 1)
    k_scale = jnp.where(lane < MLSTM_QK_WIDTH, 1.0, MLSTM_QK_DIM ** -0.5)
    rb = tm // INPROJ_ROW_BLOCKS
    for r in range(INPROJ_ROW_BLOCKS):
        rows = slice(r * rb, (r + 1) * rb)
        x = x_ref[rows, :]
        u = (x * _rms_scale(x) * g_ref[...]).astype(BF16)
        conv_ref[CONV_PAD + r * rb:CONV_PAD + (r + 1) * rb, :] = _dot(
            u, w_ref[:, W_MQK_OFF:W_MQK_OFF + 2 * MLSTM_QK_WIDTH])
        qkv_ref[rows, 0:ATTN_Q_WIDTH] = (_dot(u, wq_ref[...]) * q_scale).astype(BF16)
        qkv_ref[rows, ATTN_Q_WIDTH:QKV_WIDTH] = _dot(u, w_ref[:, ATTN_Q_WIDTH:QKV_WIDTH]).astype(BF16)
        mm_ref[rows, 2 * MLSTM_QK_WIDTH:MM_WIDTH] = _dot(
            u, w_ref[:, W_MVO_OFF:W_MVO_OFF + 2 * MLSTM_V_WIDTH]).astype(BF16)
        gt_ref[rows, :] = _dot(u, wgt_ref[...])
        gg_ref[rows, :] = _dot(u, wgg_ref[...]).astype(BF16)
        xe = conv_ref[r * rb:r * rb + CONV_PAD + rb, :]
        acc = xe * convw_ref[0:1, :]
        for t in range(1, CONV_WIDTH):
            acc = pltpu.roll(acc, 1, axis=0) + xe * convw_ref[t:t + 1, :]
        acc = acc[CONV_PAD:CONV_PAD + rb, :] + convb_ref[...]
        mm_ref[rows, 0:2 * MLSTM_QK_WIDTH] = (acc * jax.nn.sigmoid(acc) * k_scale).astype(BF16)
    conv_ref[0:CONV_PAD, :] = conv_ref[tm:tm + CONV_PAD, :]


def _inproj(x2d, g1, w_all, wgt, convw, convb, tm, seq):
    t = x2d.shape[0]
    row = lambda w: pl.BlockSpec((tm, w), lambda i: (i, 0))
    consts = (g1, w_all, wgt, convw, convb)
    return pl.pallas_call(
        functools.partial(_inproj_kernel, tiles_per_seq=seq // tm),
        grid=(t // tm,),
        in_specs=[row(D_MODEL)] + [_const_spec(c.shape) for c in consts],
        out_specs=[row(QKV_WIDTH), row(MM_WIDTH), row(GATE_WIDTH), row(MERGE_WIDTH)],
        out_shape=[jax.ShapeDtypeStruct((t, QKV_WIDTH), BF16), jax.ShapeDtypeStruct((t, MM_WIDTH), BF16),
                   jax.ShapeDtypeStruct((t, GATE_WIDTH), F32), jax.ShapeDtypeStruct((t, MERGE_WIDTH), BF16)],
        scratch_shapes=[pltpu.VMEM((CONV_PAD + tm, 2 * MLSTM_QK_WIDTH), F32),
                        pltpu.VMEM((D_MODEL, ATTN_Q_WIDTH), BF16), pltpu.VMEM((D_MODEL, MERGE_WIDTH), BF16)],
        compiler_params=pltpu.CompilerParams(dimension_semantics=("arbitrary",),
                                             vmem_limit_bytes=VMEM_LIMIT_BYTES),
        name="inproj",
    )(x2d, *consts)


def _log_sigmoid(x):
    return jnp.minimum(x, 0.0) - jnp.log1p(jnp.exp(-jnp.abs(x)))


def _split3(x):
    hi = x.astype(BF16)
    r1 = x - hi.astype(F32)
    mid = r1.astype(BF16)
    lo = (r1 - mid.astype(F32)).astype(BF16)
    return hi, mid, lo


def _mixer_kernel(sinks_ref, qkv_ref, mm_ref, gt_ref, gbias_ref, normg_ref, *rest, nch, n_cast):
    cast_in, (ao_ref, hm_ref), rest = rest[:n_cast], rest[n_cast:n_cast + 2], rest[n_cast + 2:]
    cast_out, (btab_ref, kvp_ref, cst_ref, mst_ref, gcol_ref) = rest[:n_cast], rest[n_cast:]
    b_idx = pl.program_id(0)
    j_idx = pl.program_id(1)

    for src, dst in zip(cast_in, cast_out):
        dst[...] = src[...].astype(BF16)

    row_c = lax.broadcasted_iota(jnp.int32, (CHUNK, CHUNK), 0)
    col_c = lax.broadcasted_iota(jnp.int32, (CHUNK, CHUNK), 1)

    @pl.when((b_idx == 0) & (j_idx == 0))
    def _():
        cur = row_c <= col_c
        dist = jnp.where(cur, col_c - row_c, col_c - row_c + WINDOW).astype(F32)
        for j in range(ATTN_HEADS):
            slope = 2.0 ** (-8.0 * (j + 1) / ATTN_HEADS)
            btab_ref[0, j] = -slope * dist
            btab_ref[1, j] = jnp.where(cur, -slope * dist, NEG_INF)
        for h in range(MLSTM_HEADS):
            g_row = normg_ref[:, h * MLSTM_V_DIM:(h + 1) * MLSTM_V_DIM]
            gcol_ref[h] = jnp.broadcast_to(g_row, (MLSTM_V_DIM, LANES)).T

    @pl.when(j_idx == 0)
    def _():
        kvp_ref[...] = jnp.zeros_like(kvp_ref)
        cst_ref[...] = jnp.zeros_like(cst_ref)
        mst_ref[...] = jnp.zeros_like(mst_ref)

    cur_t = row_c <= col_c
    triu_b = cur_t.astype(BF16)
    eye_b = (row_c == col_c).astype(BF16)

    def transpose_bf16(x):
        return _dot_nt(eye_b, x)

    left_lanes = col_c < HALF
    ones_rows = jnp.ones((NQ_ROWS, LANES), BF16)
    left_row = lax.broadcasted_iota(jnp.int32, (1, LANES), 1) < HALF
    zero_b = jnp.zeros((), BF16)
    first = (j_idx == 0).astype(jnp.int32)

    def kv_sides(k_blk, v_blk):
        vt = transpose_bf16(v_blk).astype(BF16)
        k_sides = (jnp.where(left_lanes, k_blk, zero_b), jnp.where(left_lanes, zero_b, k_blk))
        vt_sides = tuple(jnp.concatenate([vt[s * HALF:(s + 1) * HALF], ones_rows], axis=0)
                         for s in range(ATTN_KV_HEADS))
        return k_sides, vt_sides

    k_off = ATTN_Q_WIDTH
    v_off = ATTN_Q_WIDTH + ATTN_KV_WIDTH
    prev_sides = kv_sides(kvp_ref[0], kvp_ref[1])
    last = slice((nch - 1) * CHUNK, nch * CHUNK)
    kvp_ref[0] = qkv_ref[last, k_off:k_off + ATTN_KV_WIDTH]
    kvp_ref[1] = qkv_ref[last, v_off:v_off + ATTN_KV_WIDTH]

    m_rows = [mst_ref[h:h + 1, :] for h in range(MLSTM_HEADS)]
    c_pairs = [cst_ref[pr] for pr in range(MLSTM_HEADS // 2)]

    npair = ATTN_GROUP
    rows_of = lambda c: slice(c * CHUNK, (c + 1) * CHUNK)
    row8 = lax.broadcasted_iota(jnp.int32, (GATE_ROWS, LANES), 0)
    head_rows = row8 < MLSTM_HEADS
    sides = {-1: prev_sides}
    gates, s_t, sc_ts, q_ms, vts, p_t, mx_rows, intra_w, o_t, intra = {}, {}, {}, {}, {}, {}, {}, {}, {}, {}

    def stage_prepare(c):
        rows = rows_of(c)
        sides[c] = kv_sides(qkv_ref[rows, k_off:k_off + ATTN_KV_WIDTH], qkv_ref[rows, v_off:v_off + ATTN_KV_WIDTH])
        g8 = (gt_ref[rows, :] + gbias_ref[...]).T[0:GATE_ROWS, :]
        f8 = pltpu.roll(g8, GATE_ROWS - MLSTM_HEADS, axis=0)
        lf_parts = _split3(jnp.where(head_rows, _log_sigmoid(f8), 0.0))
        b8 = sum(_dot(part, triu_b) for part in lf_parts)
        r8 = jnp.where(head_rows, g8 - b8, 0.0)
        r_t = jnp.concatenate([r8, jnp.zeros((CHUNK - GATE_ROWS, LANES), F32)], axis=0).T
        r_all = jnp.concatenate(
            [jnp.broadcast_to(r_t[:, h:h + 1], (CHUNK, LANES)) for h in range(MLSTM_HEADS)], axis=1)
        gates[c] = (g8, b8, r_all)
        for pr in range(MLSTM_HEADS // 2):
            v0 = 2 * MLSTM_QK_WIDTH + 2 * pr * MLSTM_V_DIM
            v2 = jnp.concatenate([mm_ref[rows, v0:v0 + MLSTM_V_DIM],
                                  mm_ref[rows, v0 + MLSTM_V_DIM:v0 + 2 * MLSTM_V_DIM]], axis=0)
            vts[(c, pr)] = transpose_bf16(v2).astype(BF16)

    def stage_scores(c):
        q = qkv_ref[rows_of(c), 0:ATTN_Q_WIDTH]
        q_pairs = [q[:, p * LANES:(p + 1) * LANES] for p in range(npair)]
        for side in range(ATTN_KV_HEADS):
            k_side = jnp.concatenate([sides[c - 1][0][side], sides[c][0][side]], axis=0)
            for pp in range(npair // 2):
                q2 = jnp.concatenate([q_pairs[2 * pp], q_pairs[2 * pp + 1]], axis=0)
                s2 = _dot_nt(k_side, q2)
                s_t[(c, side, 2 * pp)] = s2[:, 0:LANES]
                s_t[(c, side, 2 * pp + 1)] = s2[:, LANES:2 * LANES]
        for pr in range(MLSTM_HEADS // 2):
            qpair = mm_ref[rows_of(c), pr * LANES:(pr + 1) * LANES]
            k_pair = mm_ref[rows_of(c), MLSTM_QK_WIDTH + pr * LANES:MLSTM_QK_WIDTH + (pr + 1) * LANES]
            q_m = jnp.concatenate([jnp.where(left_lanes, qpair, zero_b), jnp.where(left_lanes, zero_b, qpair)],
                                  axis=0)
            q_ms[(c, pr)] = q_m
            sc_ts[(c, pr)] = _dot_nt(k_pair, q_m)

    def stage_weights(c):
        variant = first if c == 0 else 0
        for side in range(ATTN_KV_HEADS):
            for p in range(npair):
                j = side * ATTN_GROUP + p
                s2 = s_t.pop((c, side, p))
                comb = jnp.where(cur_t, s2[CHUNK:2 * CHUNK], s2[0:CHUNK]) + btab_ref[variant, j]
                mx = jnp.maximum(jnp.max(comb, axis=0, keepdims=True), sinks_ref[j])
                e = jnp.exp(comb - mx)
                p_t[(c, side, p)] = jnp.concatenate([jnp.where(cur_t, 0.0, e), jnp.where(cur_t, e, 0.0)],
                                                    axis=0).astype(BF16)
                mx_rows[(c, side, p)] = mx
        g8, b8, r_all = gates[c]
        for h in range(MLSTM_HEADS):
            b_row = b8[h:h + 1, :]
            d_t = jnp.where(cur_t, r_all[:, h * LANES:(h + 1) * LANES] + b_row, NEG_INF)
            mloc = jnp.max(d_t, axis=0, keepdims=True)
            wloc = jnp.exp(d_t - mloc)
            scw = (sc_ts[(c, h // 2)][:, (h % 2) * LANES:(h % 2 + 1) * LANES] * wloc).astype(BF16)
            b_last = b_row[:, CHUNK - 1:CHUNK]
            mloc_last = mloc[:, CHUNK - 1:CHUNK]
            u_row = jnp.exp((g8[h:h + 1, :] - b_row) + b_last - mloc_last)
            intra_w[(c, h)] = (b_row, mloc, scw, b_last, mloc_last, u_row)

    def stage_values(c):
        for side in range(ATTN_KV_HEADS):
            vt_side = jnp.concatenate([sides[c - 1][1][side], sides[c][1][side]], axis=1)
            for pp in range(npair // 2):
                p2 = jnp.concatenate([p_t.pop((c, side, 2 * pp)), p_t.pop((c, side, 2 * pp + 1))], axis=1)
                o2 = _dot(vt_side, p2)
                o_t[(c, side, 2 * pp)] = o2[:, 0:LANES]
                o_t[(c, side, 2 * pp + 1)] = o2[:, LANES:2 * LANES]
        for h in range(MLSTM_HEADS):
            b_row, mloc, scw, b_last, mloc_last, u_row = intra_w.pop((c, h))
            pr, half = h // 2, h % 2
            vt_aug = jnp.concatenate([vts[(c, pr)][:, half * LANES:(half + 1) * LANES], ones_rows], axis=0)
            k_pair = mm_ref[rows_of(c), MLSTM_QK_WIDTH + pr * LANES:MLSTM_QK_WIDTH + (pr + 1) * LANES]
            k_hm = jnp.where(left_lanes, k_pair, zero_b) if half == 0 else jnp.where(left_lanes, zero_b, k_pair)
            num_t = _dot(vt_aug, scw)
            upd_t = _dot((vt_aug.astype(F32) * u_row).astype(BF16), k_hm)
            intra[(c, h)] = (b_row, mloc, num_t, b_last, mloc_last, upd_t)

    def stage_attn_out(c):
        out_t = []
        for p in range(npair):
            scaled = []
            for side in range(ATTN_KV_HEADS):
                j = side * ATTN_GROUP + p
                o = o_t.pop((c, side, p))
                den = o[HALF:HALF + 1, :] + jnp.exp(sinks_ref[j] - mx_rows.pop((c, side, p)))
                scaled.append(o[0:HALF] * (1.0 / den))
            out_t.append(jnp.concatenate(scaled, axis=0).astype(BF16))
        for pp in range(npair // 2):
            both = jnp.concatenate([out_t[2 * pp], out_t[2 * pp + 1]], axis=0)
            ao_ref[rows_of(c), 2 * pp * LANES:(2 * pp + 2) * LANES] = transpose_bf16(both).astype(BF16)

    def stage_recurrence(c):
        rows = rows_of(c)
        for pr in range(MLSTM_HEADS // 2):
            c_pair = c_pairs[pr]
            inter_t = _dot_nt(c_pair.astype(BF16), q_ms.pop((c, pr)))
            decs, eus, upds = [], [], []
            for half in range(2):
                h = 2 * pr + half
                b_row, mloc, num_t, b_last, mloc_last, upd_t = intra.pop((c, h))
                m_prev = m_rows[h]
                it = inter_t[:, half * LANES:(half + 1) * LANES]
                a = b_row + m_prev
                m_t = jnp.maximum(a, mloc)
                e_t = jnp.exp(mloc - m_t)
                w_inter = jnp.exp(a - m_t)
                numv = e_t * num_t[0:MLSTM_V_DIM] + w_inter * it[0:MLSTM_V_DIM]
                nq = e_t * num_t[MLSTM_V_DIM:MLSTM_V_DIM + 1] + w_inter * it[MLSTM_V_DIM:MLSTM_V_DIM + 1]
                den = jnp.maximum(jnp.abs(nq), jnp.exp(-m_t))
                ssq = jnp.sum(numv * numv, axis=0, keepdims=True)
                scale = lax.rsqrt(ssq * (1.0 / MLSTM_V_DIM) + NORM_EPS * (den * den))
                hn = (numv * scale * gcol_ref[h]).T
                og = mm_ref[rows, 2 * MLSTM_QK_WIDTH + MLSTM_V_WIDTH + h * MLSTM_V_DIM:
                            2 * MLSTM_QK_WIDTH + MLSTM_V_WIDTH + (h + 1) * MLSTM_V_DIM].astype(F32)
                gate = 0.5 * jnp.tanh(0.5 * og) + 0.5
                hm_ref[rows, h * MLSTM_V_DIM:(h + 1) * MLSTM_V_DIM] = (hn * gate).astype(BF16)
                m_new = jnp.maximum(b_last + m_prev, mloc_last)
                decs.append(jnp.exp(b_last + m_prev - m_new))
                eus.append(jnp.exp(mloc_last - m_new))
                upds.append(upd_t)
                m_rows[h] = m_new
            c_pairs[pr] = (jnp.where(left_row, decs[0], decs[1]) * c_pair
                           + eus[0] * upds[0] + eus[1] * upds[1])

    stages = (stage_prepare, stage_scores, stage_weights, stage_values, stage_attn_out, stage_recurrence)
    for tick in range(nch + max(MIXER_STAGE_LAGS)):
        for stage, lag in zip(stages, MIXER_STAGE_LAGS):
            c = tick - lag
            if 0 <= c < nch:
                stage(c)

    for h in range(MLSTM_HEADS):
        mst_ref[h:h + 1, :] = m_rows[h]
    for pr in range(MLSTM_HEADS // 2):
        cst_ref[pr] = c_pairs[pr]


def _cast_slab_rows(n_rows, n_steps):
    slab = BF16_SUBLANES * pl.cdiv(pl.cdiv(n_rows, n_steps), BF16_SUBLANES)
    while n_rows % slab:
        slab += BF16_SUBLANES
    return slab


def _mixer(sinks, qkv, mm, gt, gbias, normg, cast_weights, batch, seq, nch):
    tm = nch * CHUNK
    nj = seq // tm
    t = batch * seq
    row = lambda w: pl.BlockSpec((tm, w), lambda b, j: (b * nj + j, 0))
    cast_in_specs, cast_out_specs, cast_shapes = [], [], []
    for w, block_perm in cast_weights:
        n_rows, n_cols = w.shape
        slab = n_rows // len(block_perm) if block_perm else _cast_slab_rows(n_rows, batch * nj)
        assert slab % BF16_SUBLANES == 0 and n_rows // slab <= batch * nj
        last = n_rows // slab - 1
        dst_map = lambda b, j, last=last: (jnp.minimum(b * nj + j, last), 0)
        if block_perm:
            assert tuple(block_perm) == tuple((s % 2) * (len(block_perm) // 2) + s // 2
                                              for s in range(len(block_perm)))
            stride = len(block_perm) // 2
            src_map = lambda b, j, last=last, stride=stride: (
                (jnp.minimum(b * nj + j, last) % 2) * stride + jnp.minimum(b * nj + j, last) // 2, 0)
        else:
            src_map = dst_map
        cast_in_specs.append(pl.BlockSpec((slab, n_cols), src_map))
        cast_out_specs.append(pl.BlockSpec((slab, n_cols), dst_map))
        cast_shapes.append(jax.ShapeDtypeStruct(w.shape, BF16))
    return pl.pallas_call(
        functools.partial(_mixer_kernel, nch=nch, n_cast=len(cast_weights)),
        grid=(batch, nj),
        in_specs=[pl.BlockSpec(memory_space=pltpu.SMEM),
                  row(QKV_WIDTH), row(MM_WIDTH), row(GATE_WIDTH),
                  _const_spec(gbias.shape), _const_spec(normg.shape)] + cast_in_specs,
        out_specs=[row(ATTN_Q_WIDTH), row(MLSTM_V_WIDTH)] + cast_out_specs,
        out_shape=[jax.ShapeDtypeStruct((t, ATTN_Q_WIDTH), BF16), jax.ShapeDtypeStruct((t, MLSTM_V_WIDTH), BF16)]
                  + cast_shapes,
        scratch_shapes=[
            pltpu.VMEM((2, ATTN_HEADS, CHUNK, CHUNK), F32),
            pltpu.VMEM((2, CHUNK, ATTN_KV_WIDTH), BF16),
            pltpu.VMEM((MLSTM_HEADS // 2, AUG_ROWS, 2 * MLSTM_QK_DIM), F32),
            pltpu.VMEM((8, LANES), F32),
            pltpu.VMEM((MLSTM_HEADS, MLSTM_V_DIM, LANES), F32),
        ],
        compiler_params=pltpu.CompilerParams(dimension_semantics=("arbitrary", "arbitrary"),
                                             vmem_limit_bytes=VMEM_LIMIT_BYTES),
        name="mixer",
    )(sinks, qkv, mm, gt, gbias, normg, *[w for w, _ in cast_weights])


def _tail_kernel(x_ref, ao_ref, hm_ref, gg_ref, wab_ref, wmb_ref, wo_ref, g2_ref, wg_ref, wu_ref, wd_ref,
                 gf_ref, out_ref, *, ff_split):
    ya = _dot(ao_ref[...], wab_ref[...])
    ym = _dot(hm_ref[...], wmb_ref[...])
    ga = jax.nn.sigmoid(gg_ref[:, 0:D_MODEL].astype(F32))
    gm = jax.nn.sigmoid(gg_ref[:, D_MODEL:2 * D_MODEL].astype(F32))
    z = (ga * ya + gm * ym).astype(BF16)
    x1 = x_ref[...] + _dot(z, wo_ref[...])
    f = (x1 * g2_ref[...]).astype(BF16)
    rs = _rms_scale(x1)
    n_tiles = wg_ref.shape[1] // MXU_WIDTH
    bounds = [MXU_WIDTH * ((n_tiles * s + ff_split - 1) // ff_split) for s in range(ff_split + 1)]
    x2 = x1
    for s in range(ff_split):
        cols = slice(bounds[s], bounds[s + 1])
        gte = _dot(f, wg_ref[:, cols]) * rs
        up = _dot(f, wu_ref[:, cols]) * rs
        hh = (gte * jax.nn.sigmoid(gte) * up).astype(BF16)
        x2 = x2 + _dot(hh, wd_ref[cols, :])
    out_ref[...] = x2 * _rms_scale(x2) * gf_ref[...]


def _tail(x2d, ao, hm, gg, wab, wmb, wo, g2, wg, wu, wd, gf, tm, ff_split):
    t = x2d.shape[0]
    row = lambda w: pl.BlockSpec((tm, w), lambda i: (i, 0))
    consts = (wab, wmb, wo, g2, wg, wu, wd, gf)
    return pl.pallas_call(
        functools.partial(_tail_kernel, ff_split=ff_split),
        grid=(t // tm,),
        in_specs=[row(D_MODEL), row(ATTN_Q_WIDTH), row(MLSTM_V_WIDTH), row(MERGE_WIDTH)]
                 + [_const_spec(c.shape) for c in consts],
        out_specs=row(D_MODEL),
        out_shape=jax.ShapeDtypeStruct((t, D_MODEL), F32),
        compiler_params=pltpu.CompilerParams(dimension_semantics=("arbitrary",),
                                             vmem_limit_bytes=VMEM_LIMIT_BYTES),
        name="tail",
    )(x2d, ao, hm, gg, *consts)


def _layer(x2d, batch, seq, norm1_g, w_in, conv_w, conv_b, i_bias, f_bias, mlstm_norm_g, attn_sinks,
           w_attn_branch, w_mlstm_branch, w_out, norm2_g, w_ffn_gate, w_ffn_up, w_ffn_down, out_g):
    w_if = w_in[:, W_GATE_OFF:W_GATE_OFF + 2 * MLSTM_HEADS]
    wgt = jnp.pad(w_if, ((0, 0), (0, GATE_WIDTH - 2 * MLSTM_HEADS)))
    gbias = jnp.pad(jnp.concatenate([i_bias, f_bias]), (0, GATE_WIDTH - 2 * MLSTM_HEADS)).reshape(1, GATE_WIDTH)

    qkv, mm, gt, gg = _inproj(x2d, norm1_g.reshape(1, D_MODEL), w_in.astype(BF16), wgt.astype(BF16), conv_w,
                              conv_b.reshape(1, -1), tm=INPROJ_TM, seq=seq)
    tail_weights = [(w_attn_branch, Q_HEAD_ORDER), (w_mlstm_branch, None), (w_out, None),
                    (w_ffn_gate, None), (w_ffn_up, None), (w_ffn_down, None)]
    ao, hm, wab, wmb, wo, wg, wu, wd = _mixer(attn_sinks, qkv, mm, gt, gbias,
                                              mlstm_norm_g.reshape(1, MLSTM_V_WIDTH), tail_weights, batch, seq,
                                              nch=MIXER_CHUNKS)
    return _tail(x2d, ao, hm, gg, wab, wmb, wo, norm2_g.reshape(1, D_MODEL), wg, wu, wd,
                 out_g.reshape(1, D_MODEL), tm=TAIL_TM, ff_split=TAIL_FF_SPLIT)


def kernel(x, norm1_g, w_in, conv_w, conv_b, i_bias, f_bias, mlstm_norm_g, attn_sinks, w_attn_branch,
           w_mlstm_branch, w_out, norm2_g, w_ffn_gate, w_ffn_up, w_ffn_down, final_norm_g):
    batch, seq, d = x.shape
    depth = norm1_g.shape[0]
    assert depth == 1 and d == D_MODEL
    assert seq % (MIXER_CHUNKS * CHUNK) == 0 and seq % INPROJ_TM == 0 and (batch * seq) % TAIL_TM == 0
    assert w_ffn_gate.shape[-1] % MXU_WIDTH == 0
    out = _layer(x.reshape(batch * seq, d), batch, seq, norm1_g[0], w_in[0], conv_w[0], conv_b[0], i_bias[0],
                 f_bias[0], mlstm_norm_g[0], attn_sinks[0], w_attn_branch[0], w_mlstm_branch[0], w_out[0],
                 norm2_g[0], w_ffn_gate[0], w_ffn_up[0], w_ffn_down[0], final_norm_g)
    return out.reshape(batch, seq, d)
```

```python
import functools

import jax
import jax.numpy as jnp
from jax import lax
from jax.experimental import pallas as pl
from jax.experimental.pallas import tpu as pltpu

D_MODEL = 1024
ATTN_HEADS = 8
ATTN_KV_HEADS = 2
ATTN_HEAD_DIM = 64
ATTN_GROUP = ATTN_HEADS // ATTN_KV_HEADS
WINDOW = 128
ATTN_Q_WIDTH = ATTN_HEADS * ATTN_HEAD_DIM
ATTN_KV_WIDTH = ATTN_KV_HEADS * ATTN_HEAD_DIM
MLSTM_HEADS = 4
MLSTM_QK_DIM = 64
MLSTM_V_DIM = 128
MLSTM_QK_WIDTH = MLSTM_HEADS * MLSTM_QK_DIM
MLSTM_V_WIDTH = MLSTM_HEADS * MLSTM_V_DIM
CHUNK = 128
CONV_WIDTH = 4
NORM_EPS = 1e-6

LANES = 128
BF16_SUBLANES = 16
MXU_WIDTH = 256
HALF = LANES // 2
CONV_PAD = 8
GATE_ROWS = 16
NQ_ROWS = BF16_SUBLANES
AUG_ROWS = MLSTM_V_DIM + NQ_ROWS

INPROJ_TM = 1024
INPROJ_ROW_BLOCKS = 2
MIXER_CHUNKS = 16
MIXER_STAGE_LAGS = (0, 1, 2, 3, 4, 4)
TAIL_TM = 512
TAIL_FF_SPLIT = 1
VMEM_LIMIT_BYTES = 56 * 1024 * 1024

QKV_WIDTH = ATTN_Q_WIDTH + 2 * ATTN_KV_WIDTH
MM_WIDTH = 2 * MLSTM_QK_WIDTH + MLSTM_V_WIDTH
GATE_WIDTH = LANES
MERGE_WIDTH = 2 * D_MODEL
W_MQK_OFF = QKV_WIDTH
W_MVO_OFF = W_MQK_OFF + 2 * MLSTM_QK_WIDTH
W_GATE_OFF = W_MVO_OFF + 2 * MLSTM_V_WIDTH
W_MERGE_OFF = W_GATE_OFF + 2 * MLSTM_HEADS

Q_HEAD_ORDER = tuple(h * ATTN_GROUP + g for g in range(ATTN_GROUP) for h in range(ATTN_KV_HEADS))

BF16 = jnp.bfloat16
F32 = jnp.float32
NEG_INF = float("-inf")


def _dot(a, b):
    return jnp.dot(a, b, preferred_element_type=F32)


def _dot_nt(a, b):
    return lax.dot_general(a, b, (((1,), (1,)), ((), ())), preferred_element_type=F32)


def _rms_scale(x):
    return lax.rsqrt(jnp.mean(x * x, axis=-1, keepdims=True) + NORM_EPS)


def _const_spec(shape):
    nd = len(shape)
    return pl.BlockSpec(shape, lambda *_: (0,) * nd, pipeline_mode=pl.Buffered(1))


def _inproj_kernel(x_ref, g_ref, w_ref, wgt_ref, convw_ref, convb_ref,
                   qkv_ref, mm_ref, gt_ref, gg_ref, vt_ref, conv_ref, wq_ref, wgg_ref, wvt_ref, *, tiles_per_seq):
    tm = x_ref.shape[0]

    @pl.when(pl.program_id(0) == 0)
    def _():
        src = lax.broadcasted_iota(jnp.int32, (ATTN_Q_WIDTH, ATTN_Q_WIDTH), 0)
        dst = lax.broadcasted_iota(jnp.int32, (ATTN_Q_WIDTH, ATTN_Q_WIDTH), 1)
        head = ((dst % LANES) // ATTN_HEAD_DIM) * ATTN_GROUP + dst // LANES
        perm = (src == head * ATTN_HEAD_DIM + dst % ATTN_HEAD_DIM).astype(BF16)
        wq_ref[...] = _dot(w_ref[:, 0:ATTN_Q_WIDTH], perm).astype(BF16)
        wgg_ref[...] = w_ref[:, W_MERGE_OFF:W_MERGE_OFF + MERGE_WIDTH]
        eye = (lax.broadcasted_iota(jnp.int32, (MLSTM_V_WIDTH, MLSTM_V_WIDTH), 0)
               == lax.broadcasted_iota(jnp.int32, (MLSTM_V_WIDTH, MLSTM_V_WIDTH), 1)).astype(BF16)
        for k0 in range(0, D_MODEL, MLSTM_V_WIDTH):
            wvt_ref[:, k0:k0 + MLSTM_V_WIDTH] = _dot_nt(
                eye, w_ref[k0:k0 + MLSTM_V_WIDTH, W_MVO_OFF:W_MVO_OFF + MLSTM_V_WIDTH]).astype(BF16)

    @pl.when(pl.program_id(0) % tiles_per_seq == 0)
    def _():
        conv_ref[0:CONV_PAD, :] = jnp.zeros((CONV_PAD, 2 * MLSTM_QK_WIDTH), F32)

    q_scale = ATTN_HEAD_DIM ** -0.5
    lane = lax.broadcasted_iota(jnp.int32, (1, 2 * MLSTM_QK_WIDTH), 1)
    k_scale = jnp.where(lane < MLSTM_QK_WIDTH, 1.0, MLSTM_QK_DIM ** -0.5)
    rb = tm // INPROJ_ROW_BLOCKS
    for r in range(INPROJ_ROW_BLOCKS):
        rows = slice(r * rb, (r + 1) * rb)
        x = x_ref[rows, :]
        u = (x * _rms_scale(x) * g_ref[...]).astype(BF16)
        conv_ref[CONV_PAD + r * rb:CONV_PAD + (r + 1) * rb, :] = _dot(
            u, w_ref[:, W_MQK_OFF:W_MQK_OFF + 2 * MLSTM_QK_WIDTH])
        qkv_ref[rows, 0:ATTN_Q_WIDTH] = (_dot(u, wq_ref[...]) * q_scale).astype(BF16)
        qkv_ref[rows, ATTN_Q_WIDTH:QKV_WIDTH] = _dot(u, w_ref[:, ATTN_Q_WIDTH:QKV_WIDTH]).astype(BF16)
        mm_ref[rows, 2 * MLSTM_QK_WIDTH:MM_WIDTH] = _dot(
            u, w_ref[:, W_MVO_OFF + MLSTM_V_WIDTH:W_MVO_OFF + 2 * MLSTM_V_WIDTH]).astype(BF16)
        vt_ref[:, rows] = _dot_nt(wvt_ref[...], u).astype(BF16)
        gt_ref[rows, :] = _dot(u, wgt_ref[...])
        gg_ref[rows, :] = _dot(u, wgg_ref[...]).astype(BF16)
        xe = conv_ref[r * rb:r * rb + CONV_PAD + rb, :]
        acc = xe * convw_ref[0:1, :]
        for t in range(1, CONV_WIDTH):
            acc = pltpu.roll(acc, 1, axis=0) + xe * convw_ref[t:t + 1, :]
        acc = acc[CONV_PAD:CONV_PAD + rb, :] + convb_ref[...]
        mm_ref[rows, 0:2 * MLSTM_QK_WIDTH] = (acc * jax.nn.sigmoid(acc) * k_scale).astype(BF16)
    conv_ref[0:CONV_PAD, :] = conv_ref[tm:tm + CONV_PAD, :]


def _inproj(x2d, g1, w_all, wgt, convw, convb, tm, seq):
    t = x2d.shape[0]
    row = lambda w: pl.BlockSpec((tm, w), lambda i: (i, 0))
    consts = (g1, w_all, wgt, convw, convb)
    return pl.pallas_call(
        functools.partial(_inproj_kernel, tiles_per_seq=seq // tm),
        grid=(t // tm,),
        in_specs=[row(D_MODEL)] + [_const_spec(c.shape) for c in consts],
        out_specs=[row(QKV_WIDTH), row(MM_WIDTH), row(GATE_WIDTH), row(MERGE_WIDTH),
                   pl.BlockSpec((MLSTM_V_WIDTH, tm), lambda i: (0, i))],
        out_shape=[jax.ShapeDtypeStruct((t, QKV_WIDTH), BF16), jax.ShapeDtypeStruct((t, MM_WIDTH), BF16),
                   jax.ShapeDtypeStruct((t, GATE_WIDTH), F32), jax.ShapeDtypeStruct((t, MERGE_WIDTH), BF16),
                   jax.ShapeDtypeStruct((MLSTM_V_WIDTH, t), BF16)],
        scratch_shapes=[pltpu.VMEM((CONV_PAD + tm, 2 * MLSTM_QK_WIDTH), F32),
                        pltpu.VMEM((D_MODEL, ATTN_Q_WIDTH), BF16), pltpu.VMEM((D_MODEL, MERGE_WIDTH), BF16),
                        pltpu.VMEM((MLSTM_V_WIDTH, D_MODEL), BF16)],
        compiler_params=pltpu.CompilerParams(dimension_semantics=("arbitrary",),
                                             vmem_limit_bytes=VMEM_LIMIT_BYTES),
        name="inproj",
    )(x2d, *consts)


def _log_sigmoid(x):
    return jnp.minimum(x, 0.0) - jnp.log1p(jnp.exp(-jnp.abs(x)))


def _split3(x):
    hi = x.astype(BF16)
    r1 = x - hi.astype(F32)
    mid = r1.astype(BF16)
    lo = (r1 - mid.astype(F32)).astype(BF16)
    return hi, mid, lo


def _mixer_kernel(sinks_ref, qkv_ref, mm_ref, gt_ref, vt_ref, gbias_ref, normg_ref, *rest, nch, n_cast):
    cast_in, (ao_ref, hm_ref), rest = rest[:n_cast], rest[n_cast:n_cast + 2], rest[n_cast + 2:]
    cast_out, (btab_ref, kvp_ref, cst_ref, mst_ref, gcol_ref) = rest[:n_cast], rest[n_cast:]
    b_idx = pl.program_id(0)
    j_idx = pl.program_id(1)

    for src, dst in zip(cast_in, cast_out):
        dst[...] = src[...].astype(BF16)

    row_c = lax.broadcasted_iota(jnp.int32, (CHUNK, CHUNK), 0)
    col_c = lax.broadcasted_iota(jnp.int32, (CHUNK, CHUNK), 1)

    @pl.when((b_idx == 0) & (j_idx == 0))
    def _():
        cur = row_c <= col_c
        dist = jnp.where(cur, col_c - row_c, col_c - row_c + WINDOW).astype(F32)
        for j in range(ATTN_HEADS):
            slope = 2.0 ** (-8.0 * (j + 1) / ATTN_HEADS)
            btab_ref[0, j] = -slope * dist
            btab_ref[1, j] = jnp.where(cur, -slope * dist, NEG_INF)
        for h in range(MLSTM_HEADS):
            g_row = normg_ref[:, h * MLSTM_V_DIM:(h + 1) * MLSTM_V_DIM]
            gcol_ref[h] = jnp.broadcast_to(g_row, (MLSTM_V_DIM, LANES)).T

    @pl.when(j_idx == 0)
    def _():
        kvp_ref[...] = jnp.zeros_like(kvp_ref)
        cst_ref[...] = jnp.zeros_like(cst_ref)
        mst_ref[...] = jnp.zeros_like(mst_ref)

    cur_t = row_c <= col_c
    triu_b = cur_t.astype(BF16)
    eye_b = (row_c == col_c).astype(BF16)

    def transpose_bf16(x):
        return _dot_nt(eye_b, x)

    left_lanes = col_c < HALF
    ones_rows = jnp.ones((NQ_ROWS, LANES), BF16)
    left_row = lax.broadcasted_iota(jnp.int32, (1, LANES), 1) < HALF
    zero_b = jnp.zeros((), BF16)
    first = (j_idx == 0).astype(jnp.int32)

    def kv_sides(k_blk, v_blk):
        vt = transpose_bf16(v_blk).astype(BF16)
        k_sides = (jnp.where(left_lanes, k_blk, zero_b), jnp.where(left_lanes, zero_b, k_blk))
        vt_sides = tuple(jnp.concatenate([vt[s * HALF:(s + 1) * HALF], ones_rows], axis=0)
                         for s in range(ATTN_KV_HEADS))
        return k_sides, vt_sides

    k_off = ATTN_Q_WIDTH
    v_off = ATTN_Q_WIDTH + ATTN_KV_WIDTH
    prev_sides = kv_sides(kvp_ref[0], kvp_ref[1])
    last = slice((nch - 1) * CHUNK, nch * CHUNK)
    kvp_ref[0] = qkv_ref[last, k_off:k_off + ATTN_KV_WIDTH]
    kvp_ref[1] = qkv_ref[last, v_off:v_off + ATTN_KV_WIDTH]

    m_rows = [mst_ref[h:h + 1, :] for h in range(MLSTM_HEADS)]
    c_pairs = [cst_ref[pr] for pr in range(MLSTM_HEADS // 2)]

    npair = ATTN_GROUP
    rows_of = lambda c: slice(c * CHUNK, (c + 1) * CHUNK)
    row8 = lax.broadcasted_iota(jnp.int32, (GATE_ROWS, LANES), 0)
    head_rows = row8 < MLSTM_HEADS
    sides = {-1: prev_sides}
    gates, s_t, sc_ts, q_ms, p_t, mx_rows, intra_w, o_t, intra = {}, {}, {}, {}, {}, {}, {}, {}, {}

    def stage_prepare(c):
        rows = rows_of(c)
        sides[c] = kv_sides(qkv_ref[rows, k_off:k_off + ATTN_KV_WIDTH], qkv_ref[rows, v_off:v_off + ATTN_KV_WIDTH])
        g8 = (gt_ref[rows, :] + gbias_ref[...]).T[0:GATE_ROWS, :]
        f8 = pltpu.roll(g8, GATE_ROWS - MLSTM_HEADS, axis=0)
        lf_parts = _split3(jnp.where(head_rows, _log_sigmoid(f8), 0.0))
        b8 = sum(_dot(part, triu_b) for part in lf_parts)
        r8 = jnp.where(head_rows, g8 - b8, 0.0)
        r_t = jnp.concatenate([r8, jnp.zeros((CHUNK - GATE_ROWS, LANES), F32)], axis=0).T
        r_all = jnp.concatenate(
            [jnp.broadcast_to(r_t[:, h:h + 1], (CHUNK, LANES)) for h in range(MLSTM_HEADS)], axis=1)
        gates[c] = (g8, b8, r_all)

    def stage_scores(c):
        q = qkv_ref[rows_of(c), 0:ATTN_Q_WIDTH]
        q_pairs = [q[:, p * LANES:(p + 1) * LANES] for p in range(npair)]
        for side in range(ATTN_KV_HEADS):
            k_side = jnp.concatenate([sides[c - 1][0][side], sides[c][0][side]], axis=0)
            for pp in range(npair // 2):
                q2 = jnp.concatenate([q_pairs[2 * pp], q_pairs[2 * pp + 1]], axis=0)
                s2 = _dot_nt(k_side, q2)
                s_t[(c, side, 2 * pp)] = s2[:, 0:LANES]
                s_t[(c, side, 2 * pp + 1)] = s2[:, LANES:2 * LANES]
        for pr in range(MLSTM_HEADS // 2):
            qpair = mm_ref[rows_of(c), pr * LANES:(pr + 1) * LANES]
            k_pair = mm_ref[rows_of(c), MLSTM_QK_WIDTH + pr * LANES:MLSTM_QK_WIDTH + (pr + 1) * LANES]
            q_m = jnp.concatenate([jnp.where(left_lanes, qpair, zero_b), jnp.where(left_lanes, zero_b, qpair)],
                                  axis=0)
            q_ms[(c, pr)] = q_m
            sc_ts[(c, pr)] = _dot_nt(k_pair, q_m)

    def stage_weights(c):
        variant = first if c == 0 else 0
        for side in range(ATTN_KV_HEADS):
            for p in range(npair):
                j = side * ATTN_GROUP + p
                s2 = s_t.pop((c, side, p))
                comb = jnp.where(cur_t, s2[CHUNK:2 * CHUNK], s2[0:CHUNK]) + btab_ref[variant, j]
                mx = jnp.maximum(jnp.max(comb, axis=0, keepdims=True), sinks_ref[j])
                e = jnp.exp(comb - mx)
                p_t[(c, side, p)] = jnp.concatenate([jnp.where(cur_t, 0.0, e), jnp.where(cur_t, e, 0.0)],
                                                    axis=0).astype(BF16)
                mx_rows[(c, side, p)] = mx
        g8, b8, r_all = gates[c]
        for h in range(MLSTM_HEADS):
            b_row = b8[h:h + 1, :]
            d_t = jnp.where(cur_t, r_all[:, h * LANES:(h + 1) * LANES] + b_row, NEG_INF)
            mloc = jnp.max(d_t, axis=0, keepdims=True)
            wloc = jnp.exp(d_t - mloc)
            scw = (sc_ts[(c, h // 2)][:, (h % 2) * LANES:(h % 2 + 1) * LANES] * wloc).astype(BF16)
            b_last = b_row[:, CHUNK - 1:CHUNK]
            mloc_last = mloc[:, CHUNK - 1:CHUNK]
            u_row = jnp.exp((g8[h:h + 1, :] - b_row) + b_last - mloc_last)
            intra_w[(c, h)] = (b_row, mloc, scw, b_last, mloc_last, u_row)

    def stage_values(c):
        for side in range(ATTN_KV_HEADS):
            vt_side = jnp.concatenate([sides[c - 1][1][side], sides[c][1][side]], axis=1)
            for pp in range(npair // 2):
                p2 = jnp.concatenate([p_t.pop((c, side, 2 * pp)), p_t.pop((c, side, 2 * pp + 1))], axis=1)
                o2 = _dot(vt_side, p2)
                o_t[(c, side, 2 * pp)] = o2[:, 0:LANES]
                o_t[(c, side, 2 * pp + 1)] = o2[:, LANES:2 * LANES]
        for h in range(MLSTM_HEADS):
            b_row, mloc, scw, b_last, mloc_last, u_row = intra_w.pop((c, h))
            pr, half = h // 2, h % 2
            vt_aug = jnp.concatenate([vt_ref[h * MLSTM_V_DIM:(h + 1) * MLSTM_V_DIM, rows_of(c)], ones_rows], axis=0)
            k_pair = mm_ref[rows_of(c), MLSTM_QK_WIDTH + pr * LANES:MLSTM_QK_WIDTH + (pr + 1) * LANES]
            k_hm = jnp.where(left_lanes, k_pair, zero_b) if half == 0 else jnp.where(left_lanes, zero_b, k_pair)
            num_t = _dot(vt_aug, scw)
            upd_t = _dot((vt_aug.astype(F32) * u_row).astype(BF16), k_hm)
            intra[(c, h)] = (b_row, mloc, num_t, b_last, mloc_last, upd_t)

    def stage_attn_out(c):
        out_t = []
        for p in range(npair):
            scaled = []
            for side in range(ATTN_KV_HEADS):
                j = side * ATTN_GROUP + p
                o = o_t.pop((c, side, p))
                den = o[HALF:HALF + 1, :] + jnp.exp(sinks_ref[j] - mx_rows.pop((c, side, p)))
                scaled.append(o[0:HALF] * (1.0 / den))
            out_t.append(jnp.concatenate(scaled, axis=0).astype(BF16))
        for pp in range(npair // 2):
            both = jnp.concatenate([out_t[2 * pp], out_t[2 * pp + 1]], axis=0)
            ao_ref[rows_of(c), 2 * pp * LANES:(2 * pp + 2) * LANES] = transpose_bf16(both).astype(BF16)

    def stage_recurrence(c):
        rows = rows_of(c)
        for pr in range(MLSTM_HEADS // 2):
            c_pair = c_pairs[pr]
            inter_t = _dot_nt(c_pair.astype(BF16), q_ms.pop((c, pr)))
            decs, eus, upds = [], [], []
            for half in range(2):
                h = 2 * pr + half
                b_row, mloc, num_t, b_last, mloc_last, upd_t = intra.pop((c, h))
                m_prev = m_rows[h]
                it = inter_t[:, half * LANES:(half + 1) * LANES]
                a = b_row + m_prev
                m_t = jnp.maximum(a, mloc)
                e_t = jnp.exp(mloc - m_t)
                w_inter = jnp.exp(a - m_t)
                numv = e_t * num_t[0:MLSTM_V_DIM] + w_inter * it[0:MLSTM_V_DIM]
                nq = e_t * num_t[MLSTM_V_DIM:MLSTM_V_DIM + 1] + w_inter * it[MLSTM_V_DIM:MLSTM_V_DIM + 1]
                den = jnp.maximum(jnp.abs(nq), jnp.exp(-m_t))
                ssq = jnp.sum(numv * numv, axis=0, keepdims=True)
                scale = lax.rsqrt(ssq * (1.0 / MLSTM_V_DIM) + NORM_EPS * (den * den))
                hn = (numv * scale * gcol_ref[h]).T
                og = mm_ref[rows, 2 * MLSTM_QK_WIDTH + h * MLSTM_V_DIM:
                            2 * MLSTM_QK_WIDTH + (h + 1) * MLSTM_V_DIM].astype(F32)
                gate = 0.5 * jnp.tanh(0.5 * og) + 0.5
                hm_ref[rows, h * MLSTM_V_DIM:(h + 1) * MLSTM_V_DIM] = (hn * gate).astype(BF16)
                m_new = jnp.maximum(b_last + m_prev, mloc_last)
                decs.append(jnp.exp(b_last + m_prev - m_new))
                eus.append(jnp.exp(mloc_last - m_new))
                upds.append(upd_t)
                m_rows[h] = m_new
            c_pairs[pr] = (jnp.where(left_row, decs[0], decs[1]) * c_pair
                           + eus[0] * upds[0] + eus[1] * upds[1])

    stages = (stage_prepare, stage_scores, stage_weights, stage_values, stage_attn_out, stage_recurrence)
    for tick in range(nch + max(MIXER_STAGE_LAGS)):
        for stage, lag in zip(stages, MIXER_STAGE_LAGS):
            c = tick - lag
            if 0 <= c < nch:
                stage(c)

    for h in range(MLSTM_HEADS):
        mst_ref[h:h + 1, :] = m_rows[h]
    for pr in range(MLSTM_HEADS // 2):
        cst_ref[pr] = c_pairs[pr]


def _cast_slab_rows(n_rows, n_steps):
    slab = BF16_SUBLANES * pl.cdiv(pl.cdiv(n_rows, n_steps), BF16_SUBLANES)
    while n_rows % slab:
        slab += BF16_SUBLANES
    return slab


def _mixer(sinks, qkv, mm, gt, vt, gbias, normg, cast_weights, batch, seq, nch):
    tm = nch * CHUNK
    nj = seq // tm
    t = batch * seq
    row = lambda w: pl.BlockSpec((tm, w), lambda b, j: (b * nj + j, 0))
    cast_in_specs, cast_out_specs, cast_shapes = [], [], []
    for w, block_perm in cast_weights:
        n_rows, n_cols = w.shape
        slab = n_rows // len(block_perm) if block_perm else _cast_slab_rows(n_rows, batch * nj)
        assert slab % BF16_SUBLANES == 0 and n_rows // slab <= batch * nj
        last = n_rows // slab - 1
        dst_map = lambda b, j, last=last: (jnp.minimum(b * nj + j, last), 0)
        if block_perm:
            assert tuple(block_perm) == tuple((s % 2) * (len(block_perm) // 2) + s // 2
                                              for s in range(len(block_perm)))
            stride = len(block_perm) // 2
            src_map = lambda b, j, last=last, stride=stride: (
                (jnp.minimum(b * nj + j, last) % 2) * stride + jnp.minimum(b * nj + j, last) // 2, 0)
        else:
            src_map = dst_map
        cast_in_specs.append(pl.BlockSpec((slab, n_cols), src_map))
        cast_out_specs.append(pl.BlockSpec((slab, n_cols), dst_map))
        cast_shapes.append(jax.ShapeDtypeStruct(w.shape, BF16))
    return pl.pallas_call(
        functools.partial(_mixer_kernel, nch=nch, n_cast=len(cast_weights)),
        grid=(batch, nj),
        in_specs=[pl.BlockSpec(memory_space=pltpu.SMEM),
                  row(QKV_WIDTH), row(MM_WIDTH), row(GATE_WIDTH),
                  pl.BlockSpec((MLSTM_V_WIDTH, tm), lambda b, j: (0, b * nj + j)),
                  _const_spec(gbias.shape), _const_spec(normg.shape)] + cast_in_specs,
        out_specs=[row(ATTN_Q_WIDTH), row(MLSTM_V_WIDTH)] + cast_out_specs,
        out_shape=[jax.ShapeDtypeStruct((t, ATTN_Q_WIDTH), BF16), jax.ShapeDtypeStruct((t, MLSTM_V_WIDTH), BF16)]
                  + cast_shapes,
        scratch_shapes=[
            pltpu.VMEM((2, ATTN_HEADS, CHUNK, CHUNK), F32),
            pltpu.VMEM((2, CHUNK, ATTN_KV_WIDTH), BF16),
            pltpu.VMEM((MLSTM_HEADS // 2, AUG_ROWS, 2 * MLSTM_QK_DIM), F32),
            pltpu.VMEM((8, LANES), F32),
            pltpu.VMEM((MLSTM_HEADS, MLSTM_V_DIM, LANES), F32),
        ],
        compiler_params=pltpu.CompilerParams(dimension_semantics=("arbitrary", "arbitrary"),
                                             vmem_limit_bytes=VMEM_LIMIT_BYTES),
        name="mixer",
    )(sinks, qkv, mm, gt, vt, gbias, normg, *[w for w, _ in cast_weights])


def _tail_kernel(x_ref, ao_ref, hm_ref, gg_ref, wab_ref, wmb_ref, wo_ref, g2_ref, wg_ref, wu_ref, wd_ref,
                 gf_ref, out_ref, *, ff_split):
    ya = _dot(ao_ref[...], wab_ref[...])
    ym = _dot(hm_ref[...], wmb_ref[...])
    ga = jax.nn.sigmoid(gg_ref[:, 0:D_MODEL].astype(F32))
    gm = jax.nn.sigmoid(gg_ref[:, D_MODEL:2 * D_MODEL].astype(F32))
    z = (ga * ya + gm * ym).astype(BF16)
    x1 = x_ref[...] + _dot(z, wo_ref[...])
    f = (x1 * g2_ref[...]).astype(BF16)
    rs = _rms_scale(x1)
    n_tiles = wg_ref.shape[1] // MXU_WIDTH
    bounds = [MXU_WIDTH * ((n_tiles * s + ff_split - 1) // ff_split) for s in range(ff_split + 1)]
    x2 = x1
    for s in range(ff_split):
        cols = slice(bounds[s], bounds[s + 1])
        gte = _dot(f, wg_ref[:, cols]) * rs
        up = _dot(f, wu_ref[:, cols]) * rs
        hh = (gte * jax.nn.sigmoid(gte) * up).astype(BF16)
        x2 = x2 + _dot(hh, wd_ref[cols, :])
    out_ref[...] = x2 * _rms_scale(x2) * gf_ref[...]


def _tail(x2d, ao, hm, gg, wab, wmb, wo, g2, wg, wu, wd, gf, tm, ff_split):
    t = x2d.shape[0]
    row = lambda w: pl.BlockSpec((tm, w), lambda i: (i, 0))
    consts = (wab, wmb, wo, g2, wg, wu, wd, gf)
    return pl.pallas_call(
        functools.partial(_tail_kernel, ff_split=ff_split),
        grid=(t // tm,),
        in_specs=[row(D_MODEL), row(ATTN_Q_WIDTH), row(MLSTM_V_WIDTH), row(MERGE_WIDTH)]
                 + [_const_spec(c.shape) for c in consts],
        out_specs=row(D_MODEL),
        out_shape=jax.ShapeDtypeStruct((t, D_MODEL), F32),
        compiler_params=pltpu.CompilerParams(dimension_semantics=("arbitrary",),
                                             vmem_limit_bytes=VMEM_LIMIT_BYTES),
        name="tail",
    )(x2d, ao, hm, gg, *consts)


def _layer(x2d, batch, seq, norm1_g, w_in, conv_w, conv_b, i_bias, f_bias, mlstm_norm_g, attn_sinks,
           w_attn_branch, w_mlstm_branch, w_out, norm2_g, w_ffn_gate, w_ffn_up, w_ffn_down, out_g):
    w_if = w_in[:, W_GATE_OFF:W_GATE_OFF + 2 * MLSTM_HEADS]
    wgt = jnp.pad(w_if, ((0, 0), (0, GATE_WIDTH - 2 * MLSTM_HEADS)))
    gbias = jnp.pad(jnp.concatenate([i_bias, f_bias]), (0, GATE_WIDTH - 2 * MLSTM_HEADS)).reshape(1, GATE_WIDTH)

    qkv, mm, gt, gg, vt_all = _inproj(x2d, norm1_g.reshape(1, D_MODEL), w_in.astype(BF16), wgt.astype(BF16), conv_w,
                              conv_b.reshape(1, -1), tm=INPROJ_TM, seq=seq)
    tail_weights = [(w_attn_branch, Q_HEAD_ORDER), (w_mlstm_branch, None), (w_out, None),
                    (w_ffn_gate, None), (w_ffn_up, None), (w_ffn_down, None)]
    ao, hm, wab, wmb, wo, wg, wu, wd = _mixer(attn_sinks, qkv, mm, gt, vt_all, gbias,
                                              mlstm_norm_g.reshape(1, MLSTM_V_WIDTH), tail_weights, batch, seq,
                                              nch=MIXER_CHUNKS)
    return _tail(x2d, ao, hm, gg, wab, wmb, wo, norm2_g.reshape(1, D_MODEL), wg, wu, wd,
                 out_g.reshape(1, D_MODEL), tm=TAIL_TM, ff_split=TAIL_FF_SPLIT)


def kernel(x, norm1_g, w_in, conv_w, conv_b, i_bias, f_bias, mlstm_norm_g, attn_sinks, w_attn_branch,
           w_mlstm_branch, w_out, norm2_g, w_ffn_gate, w_ffn_up, w_ffn_down, final_norm_g):
    batch, seq, d = x.shape
    depth = norm1_g.shape[0]
    assert depth == 1 and d == D_MODEL
    assert seq % (MIXER_CHUNKS * CHUNK) == 0 and seq % INPROJ_TM == 0 and (batch * seq) % TAIL_TM == 0
    assert w_ffn_gate.shape[-1] % MXU_WIDTH == 0
    out = _layer(x.reshape(batch * seq, d), batch, seq, norm1_g[0], w_in[0], conv_w[0], conv_b[0], i_bias[0],
                 f_bias[0], mlstm_norm_g[0], attn_sinks[0], w_attn_branch[0], w_mlstm_branch[0], w_out[0],
                 norm2_g[0], w_ffn_gate[0], w_ffn_up[0], w_ffn_down[0], final_norm_g)
    return out.reshape(batch, seq, d)
```

```python
import functools

import jax
import jax.numpy as jnp
from jax import lax
from jax.experimental import pallas as pl
from jax.experimental.pallas import tpu as pltpu

D_MODEL = 1024
ATTN_HEADS = 8
ATTN_KV_HEADS = 2
ATTN_HEAD_DIM = 64
ATTN_GROUP = ATTN_HEADS // ATTN_KV_HEADS
WINDOW = 128
ATTN_Q_WIDTH = ATTN_HEADS * ATTN_HEAD_DIM
ATTN_KV_WIDTH = ATTN_KV_HEADS * ATTN_HEAD_DIM
MLSTM_HEADS = 4
MLSTM_QK_DIM = 64
MLSTM_V_DIM = 128
MLSTM_QK_WIDTH = MLSTM_HEADS * MLSTM_QK_DIM
MLSTM_V_WIDTH = MLSTM_HEADS * MLSTM_V_DIM
CHUNK = 128
CONV_WIDTH = 4
NORM_EPS = 1e-6

LANES = 128
BF16_SUBLANES = 16
MXU_WIDTH = 256
HALF = LANES // 2
CONV_PAD = 8
GATE_ROWS = 16
NQ_ROWS = BF16_SUBLANES
AUG_ROWS = MLSTM_V_DIM + NQ_ROWS

INPROJ_TM = 1024
INPROJ_ROW_BLOCKS = 2
MIXER_CHUNKS = 16
MIXER_STAGE_LAGS = (0, 1, 2, 3, 4, 4)
TAIL_TM = 512
TAIL_FF_SPLIT = 1
VMEM_LIMIT_BYTES = 56 * 1024 * 1024

QKV_WIDTH = ATTN_Q_WIDTH + 2 * ATTN_KV_WIDTH
MM_WIDTH = 2 * MLSTM_QK_WIDTH
GATE_WIDTH = LANES
MERGE_WIDTH = 2 * D_MODEL
W_MQK_OFF = QKV_WIDTH
W_MVO_OFF = W_MQK_OFF + 2 * MLSTM_QK_WIDTH
W_GATE_OFF = W_MVO_OFF + 2 * MLSTM_V_WIDTH
W_MERGE_OFF = W_GATE_OFF + 2 * MLSTM_HEADS

Q_HEAD_ORDER = tuple(h * ATTN_GROUP + g for g in range(ATTN_GROUP) for h in range(ATTN_KV_HEADS))

BF16 = jnp.bfloat16
F32 = jnp.float32
NEG_INF = float("-inf")


def _dot(a, b):
    return jnp.dot(a, b, preferred_element_type=F32)


def _dot_nt(a, b):
    return lax.dot_general(a, b, (((1,), (1,)), ((), ())), preferred_element_type=F32)


def _rms_scale(x):
    return lax.rsqrt(jnp.mean(x * x, axis=-1, keepdims=True) + NORM_EPS)


def _const_spec(shape):
    nd = len(shape)
    return pl.BlockSpec(shape, lambda *_: (0,) * nd, pipeline_mode=pl.Buffered(1))


def _inproj_kernel(x_ref, g_ref, w_ref, wgt_ref, convw_ref, convb_ref,
                   qkv_ref, mm_ref, gt_ref, gg_ref, vt_ref, ogt_ref, conv_ref, wq_ref, wgg_ref, wvt_ref, *,
                   tiles_per_seq):
    tm = x_ref.shape[0]

    @pl.when(pl.program_id(0) == 0)
    def _():
        src = lax.broadcasted_iota(jnp.int32, (ATTN_Q_WIDTH, ATTN_Q_WIDTH), 0)
        dst = lax.broadcasted_iota(jnp.int32, (ATTN_Q_WIDTH, ATTN_Q_WIDTH), 1)
        head = ((dst % LANES) // ATTN_HEAD_DIM) * ATTN_GROUP + dst // LANES
        perm = (src == head * ATTN_HEAD_DIM + dst % ATTN_HEAD_DIM).astype(BF16)
        wq_ref[...] = _dot(w_ref[:, 0:ATTN_Q_WIDTH], perm).astype(BF16)
        wgg_ref[...] = w_ref[:, W_MERGE_OFF:W_MERGE_OFF + MERGE_WIDTH]
        eye = (lax.broadcasted_iota(jnp.int32, (MLSTM_V_WIDTH, MLSTM_V_WIDTH), 0)
               == lax.broadcasted_iota(jnp.int32, (MLSTM_V_WIDTH, MLSTM_V_WIDTH), 1)).astype(BF16)
        for half in range(2):
            for k0 in range(0, D_MODEL, MLSTM_V_WIDTH):
                c0 = W_MVO_OFF + half * MLSTM_V_WIDTH
                wvt_ref[half * MLSTM_V_WIDTH:(half + 1) * MLSTM_V_WIDTH, k0:k0 + MLSTM_V_WIDTH] = _dot_nt(
                    eye, w_ref[k0:k0 + MLSTM_V_WIDTH, c0:c0 + MLSTM_V_WIDTH]).astype(BF16)

    @pl.when(pl.program_id(0) % tiles_per_seq == 0)
    def _():
        conv_ref[0:CONV_PAD, :] = jnp.zeros((CONV_PAD, 2 * MLSTM_QK_WIDTH), F32)

    q_scale = ATTN_HEAD_DIM ** -0.5
    lane = lax.broadcasted_iota(jnp.int32, (1, 2 * MLSTM_QK_WIDTH), 1)
    k_scale = jnp.where(lane < MLSTM_QK_WIDTH, 1.0, MLSTM_QK_DIM ** -0.5)
    rb = tm // INPROJ_ROW_BLOCKS
    for r in range(INPROJ_ROW_BLOCKS):
        rows = slice(r * rb, (r + 1) * rb)
        x = x_ref[rows, :]
        u = (x * _rms_scale(x) * g_ref[...]).astype(BF16)
        conv_ref[CONV_PAD + r * rb:CONV_PAD + (r + 1) * rb, :] = _dot(
            u, w_ref[:, W_MQK_OFF:W_MQK_OFF + 2 * MLSTM_QK_WIDTH])
        qkv_ref[rows, 0:ATTN_Q_WIDTH] = (_dot(u, wq_ref[...]) * q_scale).astype(BF16)
        qkv_ref[rows, ATTN_Q_WIDTH:QKV_WIDTH] = _dot(u, w_ref[:, ATTN_Q_WIDTH:QKV_WIDTH]).astype(BF16)
        vo_t = _dot_nt(wvt_ref[...], u).astype(BF16)
        vt_ref[:, rows] = vo_t[0:MLSTM_V_WIDTH]
        ogt_ref[:, rows] = vo_t[MLSTM_V_WIDTH:2 * MLSTM_V_WIDTH]
        gt_ref[rows, :] = _dot(u, wgt_ref[...])
        gg_ref[rows, :] = _dot(u, wgg_ref[...]).astype(BF16)
        xe = conv_ref[r * rb:r * rb + CONV_PAD + rb, :]
        acc = xe * convw_ref[0:1, :]
        for t in range(1, CONV_WIDTH):
            acc = pltpu.roll(acc, 1, axis=0) + xe * convw_ref[t:t + 1, :]
        acc = acc[CONV_PAD:CONV_PAD + rb, :] + convb_ref[...]
        mm_ref[rows, 0:2 * MLSTM_QK_WIDTH] = (acc * jax.nn.sigmoid(acc) * k_scale).astype(BF16)
    conv_ref[0:CONV_PAD, :] = conv_ref[tm:tm + CONV_PAD, :]


def _inproj(x2d, g1, w_all, wgt, convw, convb, tm, seq):
    t = x2d.shape[0]
    row = lambda w: pl.BlockSpec((tm, w), lambda i: (i, 0))
    consts = (g1, w_all, wgt, convw, convb)
    return pl.pallas_call(
        functools.partial(_inproj_kernel, tiles_per_seq=seq // tm),
        grid=(t // tm,),
        in_specs=[row(D_MODEL)] + [_const_spec(c.shape) for c in consts],
        out_specs=[row(QKV_WIDTH), row(MM_WIDTH), row(GATE_WIDTH), row(MERGE_WIDTH),
                   pl.BlockSpec((MLSTM_V_WIDTH, tm), lambda i: (0, i)),
                   pl.BlockSpec((MLSTM_V_WIDTH, tm), lambda i: (0, i))],
        out_shape=[jax.ShapeDtypeStruct((t, QKV_WIDTH), BF16), jax.ShapeDtypeStruct((t, MM_WIDTH), BF16),
                   jax.ShapeDtypeStruct((t, GATE_WIDTH), F32), jax.ShapeDtypeStruct((t, MERGE_WIDTH), BF16),
                   jax.ShapeDtypeStruct((MLSTM_V_WIDTH, t), BF16), jax.ShapeDtypeStruct((MLSTM_V_WIDTH, t), BF16)],
        scratch_shapes=[pltpu.VMEM((CONV_PAD + tm, 2 * MLSTM_QK_WIDTH), F32),
                        pltpu.VMEM((D_MODEL, ATTN_Q_WIDTH), BF16), pltpu.VMEM((D_MODEL, MERGE_WIDTH), BF16),
                        pltpu.VMEM((2 * MLSTM_V_WIDTH, D_MODEL), BF16)],
        compiler_params=pltpu.CompilerParams(dimension_semantics=("arbitrary",),
                                             vmem_limit_bytes=VMEM_LIMIT_BYTES),
        name="inproj",
    )(x2d, *consts)


def _log_sigmoid(x):
    return jnp.minimum(x, 0.0) - jnp.log1p(jnp.exp(-jnp.abs(x)))


def _split3(x):
    hi = x.astype(BF16)
    r1 = x - hi.astype(F32)
    mid = r1.astype(BF16)
    lo = (r1 - mid.astype(F32)).astype(BF16)
    return hi, mid, lo


def _mixer_kernel(sinks_ref, qkv_ref, mm_ref, gt_ref, vt_ref, ogt_ref, gbias_ref, normg_ref, *rest, nch, n_cast):
    cast_in, (ao_ref, hm_ref), rest = rest[:n_cast], rest[n_cast:n_cast + 2], rest[n_cast + 2:]
    cast_out, (btab_ref, kvp_ref, cst_ref, mst_ref, gcol_ref) = rest[:n_cast], rest[n_cast:]
    b_idx = pl.program_id(0)
    j_idx = pl.program_id(1)

    for src, dst in zip(cast_in, cast_out):
        dst[...] = src[...].astype(BF16)

    row_c = lax.broadcasted_iota(jnp.int32, (CHUNK, CHUNK), 0)
    col_c = lax.broadcasted_iota(jnp.int32, (CHUNK, CHUNK), 1)

    @pl.when((b_idx == 0) & (j_idx == 0))
    def _():
        cur = row_c <= col_c
        dist = jnp.where(cur, col_c - row_c, col_c - row_c + WINDOW).astype(F32)
        for j in range(ATTN_HEADS):
            slope = 2.0 ** (-8.0 * (j + 1) / ATTN_HEADS)
            btab_ref[0, j] = -slope * dist
            btab_ref[1, j] = jnp.where(cur, -slope * dist, NEG_INF)
        for h in range(MLSTM_HEADS):
            g_row = normg_ref[:, h * MLSTM_V_DIM:(h + 1) * MLSTM_V_DIM]
            gcol_ref[h] = jnp.broadcast_to(g_row, (MLSTM_V_DIM, LANES)).T

    @pl.when(j_idx == 0)
    def _():
        kvp_ref[...] = jnp.zeros_like(kvp_ref)
        cst_ref[...] = jnp.zeros_like(cst_ref)
        mst_ref[...] = jnp.zeros_like(mst_ref)

    cur_t = row_c <= col_c
    triu_b = cur_t.astype(BF16)
    eye_b = (row_c == col_c).astype(BF16)

    def transpose_bf16(x):
        return _dot_nt(eye_b, x)

    left_lanes = col_c < HALF
    ones_rows = jnp.ones((NQ_ROWS, LANES), BF16)
    left_row = lax.broadcasted_iota(jnp.int32, (1, LANES), 1) < HALF
    zero_b = jnp.zeros((), BF16)
    first = (j_idx == 0).astype(jnp.int32)

    def kv_sides(k_blk, v_blk):
        vt = transpose_bf16(v_blk).astype(BF16)
        k_sides = (jnp.where(left_lanes, k_blk, zero_b), jnp.where(left_lanes, zero_b, k_blk))
        vt_sides = tuple(jnp.concatenate([vt[s * HALF:(s + 1) * HALF], ones_rows], axis=0)
                         for s in range(ATTN_KV_HEADS))
        return k_sides, vt_sides

    k_off = ATTN_Q_WIDTH
    v_off = ATTN_Q_WIDTH + ATTN_KV_WIDTH
    prev_sides = kv_sides(kvp_ref[0], kvp_ref[1])
    last = slice((nch - 1) * CHUNK, nch * CHUNK)
    kvp_ref[0] = qkv_ref[last, k_off:k_off + ATTN_KV_WIDTH]
    kvp_ref[1] = qkv_ref[last, v_off:v_off + ATTN_KV_WIDTH]

    m_rows = [mst_ref[h:h + 1, :] for h in range(MLSTM_HEADS)]
    c_pairs = [cst_ref[pr] for pr in range(MLSTM_HEADS // 2)]

    npair = ATTN_GROUP
    rows_of = lambda c: slice(c * CHUNK, (c + 1) * CHUNK)
    row8 = lax.broadcasted_iota(jnp.int32, (GATE_ROWS, LANES), 0)
    head_rows = row8 < MLSTM_HEADS
    sides = {-1: prev_sides}
    gates, s_t, sc_ts, q_ms, p_t, mx_rows, intra_w, o_t, intra = {}, {}, {}, {}, {}, {}, {}, {}, {}

    def stage_prepare(c):
        rows = rows_of(c)
        sides[c] = kv_sides(qkv_ref[rows, k_off:k_off + ATTN_KV_WIDTH], qkv_ref[rows, v_off:v_off + ATTN_KV_WIDTH])
        g8 = (gt_ref[rows, :] + gbias_ref[...]).T[0:GATE_ROWS, :]
        f8 = pltpu.roll(g8, GATE_ROWS - MLSTM_HEADS, axis=0)
        lf_parts = _split3(jnp.where(head_rows, _log_sigmoid(f8), 0.0))
        b8 = sum(_dot(part, triu_b) for part in lf_parts)
        r8 = jnp.where(head_rows, g8 - b8, 0.0)
        r_t = jnp.concatenate([r8, jnp.zeros((CHUNK - GATE_ROWS, LANES), F32)], axis=0).T
        r_all = jnp.concatenate(
            [jnp.broadcast_to(r_t[:, h:h + 1], (CHUNK, LANES)) for h in range(MLSTM_HEADS)], axis=1)
        gates[c] = (g8, b8, r_all)

    def stage_scores(c):
        q = qkv_ref[rows_of(c), 0:ATTN_Q_WIDTH]
        q_pairs = [q[:, p * LANES:(p + 1) * LANES] for p in range(npair)]
        for side in range(ATTN_KV_HEADS):
            k_side = jnp.concatenate([sides[c - 1][0][side], sides[c][0][side]], axis=0)
            for pp in range(npair // 2):
                q2 = jnp.concatenate([q_pairs[2 * pp], q_pairs[2 * pp + 1]], axis=0)
                s2 = _dot_nt(k_side, q2)
                s_t[(c, side, 2 * pp)] = s2[:, 0:LANES]
                s_t[(c, side, 2 * pp + 1)] = s2[:, LANES:2 * LANES]
        for pr in range(MLSTM_HEADS // 2):
            qpair = mm_ref[rows_of(c), pr * LANES:(pr + 1) * LANES]
            k_pair = mm_ref[rows_of(c), MLSTM_QK_WIDTH + pr * LANES:MLSTM_QK_WIDTH + (pr + 1) * LANES]
            q_m = jnp.concatenate([jnp.where(left_lanes, qpair, zero_b), jnp.where(left_lanes, zero_b, qpair)],
                                  axis=0)
            q_ms[(c, pr)] = q_m
            sc_ts[(c, pr)] = _dot_nt(k_pair, q_m)

    def stage_weights(c):
        variant = first if c == 0 else 0
        for side in range(ATTN_KV_HEADS):
            for p in range(npair):
                j = side * ATTN_GROUP + p
                s2 = s_t.pop((c, side, p))
                comb = jnp.where(cur_t, s2[CHUNK:2 * CHUNK], s2[0:CHUNK]) + btab_ref[variant, j]
                mx = jnp.maximum(jnp.max(comb, axis=0, keepdims=True), sinks_ref[j])
                e = jnp.exp(comb - mx)
                p_t[(c, side, p)] = jnp.concatenate([jnp.where(cur_t, 0.0, e), jnp.where(cur_t, e, 0.0)],
                                                    axis=0).astype(BF16)
                mx_rows[(c, side, p)] = mx
        g8, b8, r_all = gates[c]
        for h in range(MLSTM_HEADS):
            b_row = b8[h:h + 1, :]
            d_t = jnp.where(cur_t, r_all[:, h * LANES:(h + 1) * LANES] + b_row, NEG_INF)
            mloc = jnp.max(d_t, axis=0, keepdims=True)
            wloc = jnp.exp(d_t - mloc)
            scw = (sc_ts[(c, h // 2)][:, (h % 2) * LANES:(h % 2 + 1) * LANES] * wloc).astype(BF16)
            b_last = b_row[:, CHUNK - 1:CHUNK]
            mloc_last = mloc[:, CHUNK - 1:CHUNK]
            u_row = jnp.exp((g8[h:h + 1, :] - b_row) + b_last - mloc_last)
            intra_w[(c, h)] = (b_row, mloc, scw, b_last, mloc_last, u_row)

    def stage_values(c):
        for side in range(ATTN_KV_HEADS):
            vt_side = jnp.concatenate([sides[c - 1][1][side], sides[c][1][side]], axis=1)
            for pp in range(npair // 2):
                p2 = jnp.concatenate([p_t.pop((c, side, 2 * pp)), p_t.pop((c, side, 2 * pp + 1))], axis=1)
                o2 = _dot(vt_side, p2)
                o_t[(c, side, 2 * pp)] = o2[:, 0:LANES]
                o_t[(c, side, 2 * pp + 1)] = o2[:, LANES:2 * LANES]
        for h in range(MLSTM_HEADS):
            b_row, mloc, scw, b_last, mloc_last, u_row = intra_w.pop((c, h))
            pr, half = h // 2, h % 2
            vt_aug = jnp.concatenate([vt_ref[h * MLSTM_V_DIM:(h + 1) * MLSTM_V_DIM, rows_of(c)], ones_rows], axis=0)
            k_pair = mm_ref[rows_of(c), MLSTM_QK_WIDTH + pr * LANES:MLSTM_QK_WIDTH + (pr + 1) * LANES]
            k_hm = jnp.where(left_lanes, k_pair, zero_b) if half == 0 else jnp.where(left_lanes, zero_b, k_pair)
            num_t = _dot(vt_aug, scw)
            upd_t = _dot((vt_aug.astype(F32) * u_row).astype(BF16), k_hm)
            intra[(c, h)] = (b_row, mloc, num_t, b_last, mloc_last, upd_t)

    def stage_attn_out(c):
        for p in range(npair):
            scaled = []
            for side in range(ATTN_KV_HEADS):
                j = side * ATTN_GROUP + p
                o = o_t.pop((c, side, p))
                den = o[HALF:HALF + 1, :] + jnp.exp(sinks_ref[j] - mx_rows.pop((c, side, p)))
                scaled.append(o[0:HALF] * (1.0 / den))
            ao_ref[p * LANES:(p + 1) * LANES, rows_of(c)] = jnp.concatenate(scaled, axis=0).astype(BF16)

    def stage_recurrence(c):
        rows = rows_of(c)
        for pr in range(MLSTM_HEADS // 2):
            c_pair = c_pairs[pr]
            inter_t = _dot_nt(c_pair.astype(BF16), q_ms.pop((c, pr)))
            decs, eus, upds = [], [], []
            for half in range(2):
                h = 2 * pr + half
                b_row, mloc, num_t, b_last, mloc_last, upd_t = intra.pop((c, h))
                m_prev = m_rows[h]
                it = inter_t[:, half * LANES:(half + 1) * LANES]
                a = b_row + m_prev
                m_t = jnp.maximum(a, mloc)
                e_t = jnp.exp(mloc - m_t)
                w_inter = jnp.exp(a - m_t)
                numv = e_t * num_t[0:MLSTM_V_DIM] + w_inter * it[0:MLSTM_V_DIM]
                nq = e_t * num_t[MLSTM_V_DIM:MLSTM_V_DIM + 1] + w_inter * it[MLSTM_V_DIM:MLSTM_V_DIM + 1]
                den = jnp.maximum(jnp.abs(nq), jnp.exp(-m_t))
                ssq = jnp.sum(numv * numv, axis=0, keepdims=True)
                scale = lax.rsqrt(ssq * (1.0 / MLSTM_V_DIM) + NORM_EPS * (den * den))
                og = ogt_ref[h * MLSTM_V_DIM:(h + 1) * MLSTM_V_DIM, rows].astype(F32)
                gate = 0.5 * jnp.tanh(0.5 * og) + 0.5
                hm_ref[h * MLSTM_V_DIM:(h + 1) * MLSTM_V_DIM, rows] = (numv * scale * gcol_ref[h] * gate).astype(BF16)
                m_new = jnp.maximum(b_last + m_prev, mloc_last)
                decs.append(jnp.exp(b_last + m_prev - m_new))
                eus.append(jnp.exp(mloc_last - m_new))
                upds.append(upd_t)
                m_rows[h] = m_new
            c_pairs[pr] = (jnp.where(left_row, decs[0], decs[1]) * c_pair
                           + eus[0] * upds[0] + eus[1] * upds[1])

    stages = (stage_prepare, stage_scores, stage_weights, stage_values, stage_attn_out, stage_recurrence)
    for tick in range(nch + max(MIXER_STAGE_LAGS)):
        for stage, lag in zip(stages, MIXER_STAGE_LAGS):
            c = tick - lag
            if 0 <= c < nch:
                stage(c)

    for h in range(MLSTM_HEADS):
        mst_ref[h:h + 1, :] = m_rows[h]
    for pr in range(MLSTM_HEADS // 2):
        cst_ref[pr] = c_pairs[pr]


def _cast_slab_rows(n_rows, n_steps):
    slab = BF16_SUBLANES * pl.cdiv(pl.cdiv(n_rows, n_steps), BF16_SUBLANES)
    while n_rows % slab:
        slab += BF16_SUBLANES
    return slab


def _mixer(sinks, qkv, mm, gt, vt, ogt, gbias, normg, cast_weights, batch, seq, nch):
    tm = nch * CHUNK
    nj = seq // tm
    t = batch * seq
    row = lambda w: pl.BlockSpec((tm, w), lambda b, j: (b * nj + j, 0))
    col = lambda h: pl.BlockSpec((h, tm), lambda b, j: (0, b * nj + j))
    cast_in_specs, cast_out_specs, cast_shapes = [], [], []
    for w, block_perm in cast_weights:
        n_rows, n_cols = w.shape
        slab = n_rows // len(block_perm) if block_perm else _cast_slab_rows(n_rows, batch * nj)
        assert slab % BF16_SUBLANES == 0 and n_rows // slab <= batch * nj
        last = n_rows // slab - 1
        dst_map = lambda b, j, last=last: (jnp.minimum(b * nj + j, last), 0)
        if block_perm:
            assert tuple(block_perm) == tuple((s % 2) * (len(block_perm) // 2) + s // 2
                                              for s in range(len(block_perm)))
            stride = len(block_perm) // 2
            src_map = lambda b, j, last=last, stride=stride: (
                (jnp.minimum(b * nj + j, last) % 2) * stride + jnp.minimum(b * nj + j, last) // 2, 0)
        else:
            src_map = dst_map
        cast_in_specs.append(pl.BlockSpec((slab, n_cols), src_map))
        cast_out_specs.append(pl.BlockSpec((slab, n_cols), dst_map))
        cast_shapes.append(jax.ShapeDtypeStruct(w.shape, BF16))
    return pl.pallas_call(
        functools.partial(_mixer_kernel, nch=nch, n_cast=len(cast_weights)),
        grid=(batch, nj),
        in_specs=[pl.BlockSpec(memory_space=pltpu.SMEM),
                  row(QKV_WIDTH), row(MM_WIDTH), row(GATE_WIDTH),
                  col(MLSTM_V_WIDTH), col(MLSTM_V_WIDTH),
                  _const_spec(gbias.shape), _const_spec(normg.shape)] + cast_in_specs,
        out_specs=[col(ATTN_Q_WIDTH), col(MLSTM_V_WIDTH)] + cast_out_specs,
        out_shape=[jax.ShapeDtypeStruct((ATTN_Q_WIDTH, t), BF16), jax.ShapeDtypeStruct((MLSTM_V_WIDTH, t), BF16)]
                  + cast_shapes,
        scratch_shapes=[
            pltpu.VMEM((2, ATTN_HEADS, CHUNK, CHUNK), F32),
            pltpu.VMEM((2, CHUNK, ATTN_KV_WIDTH), BF16),
            pltpu.VMEM((MLSTM_HEADS // 2, AUG_ROWS, 2 * MLSTM_QK_DIM), F32),
            pltpu.VMEM((8, LANES), F32),
            pltpu.VMEM((MLSTM_HEADS, MLSTM_V_DIM, LANES), F32),
        ],
        compiler_params=pltpu.CompilerParams(dimension_semantics=("arbitrary", "arbitrary"),
                                             vmem_limit_bytes=VMEM_LIMIT_BYTES),
        name="mixer",
    )(sinks, qkv, mm, gt, vt, ogt, gbias, normg, *[w for w, _ in cast_weights])


def _tail_kernel(x_ref, ao_ref, hm_ref, gg_ref, wab_ref, wmb_ref, wo_ref, g2_ref, wg_ref, wu_ref, wd_ref,
                 gf_ref, out_ref, *, ff_split):
    tn = (((0,), (0,)), ((), ()))
    ya = lax.dot_general(ao_ref[...], wab_ref[...], tn, preferred_element_type=F32)
    ym = lax.dot_general(hm_ref[...], wmb_ref[...], tn, preferred_element_type=F32)
    ga = jax.nn.sigmoid(gg_ref[:, 0:D_MODEL].astype(F32))
    gm = jax.nn.sigmoid(gg_ref[:, D_MODEL:2 * D_MODEL].astype(F32))
    z = (ga * ya + gm * ym).astype(BF16)
    x1 = x_ref[...] + _dot(z, wo_ref[...])
    f = (x1 * g2_ref[...]).astype(BF16)
    rs = _rms_scale(x1)
    n_tiles = wg_ref.shape[1] // MXU_WIDTH
    bounds = [MXU_WIDTH * ((n_tiles * s + ff_split - 1) // ff_split) for s in range(ff_split + 1)]
    x2 = x1
    for s in range(ff_split):
        cols = slice(bounds[s], bounds[s + 1])
        gte = _dot(f, wg_ref[:, cols]) * rs
        up = _dot(f, wu_ref[:, cols]) * rs
        hh = (gte * jax.nn.sigmoid(gte) * up).astype(BF16)
        x2 = x2 + _dot(hh, wd_ref[cols, :])
    out_ref[...] = x2 * _rms_scale(x2) * gf_ref[...]


def _tail(x2d, ao, hm, gg, wab, wmb, wo, g2, wg, wu, wd, gf, tm, ff_split):
    t = x2d.shape[0]
    row = lambda w: pl.BlockSpec((tm, w), lambda i: (i, 0))
    consts = (wab, wmb, wo, g2, wg, wu, wd, gf)
    return pl.pallas_call(
        functools.partial(_tail_kernel, ff_split=ff_split),
        grid=(t // tm,),
        in_specs=[row(D_MODEL), pl.BlockSpec((ATTN_Q_WIDTH, tm), lambda i: (0, i)),
                  pl.BlockSpec((MLSTM_V_WIDTH, tm), lambda i: (0, i)), row(MERGE_WIDTH)]
                 + [_const_spec(c.shape) for c in consts],
        out_specs=row(D_MODEL),
        out_shape=jax.ShapeDtypeStruct((t, D_MODEL), F32),
        compiler_params=pltpu.CompilerParams(dimension_semantics=("arbitrary",),
                                             vmem_limit_bytes=VMEM_LIMIT_BYTES),
        name="tail",
    )(x2d, ao, hm, gg, *consts)


def _layer(x2d, batch, seq, norm1_g, w_in, conv_w, conv_b, i_bias, f_bias, mlstm_norm_g, attn_sinks,
           w_attn_branch, w_mlstm_branch, w_out, norm2_g, w_ffn_gate, w_ffn_up, w_ffn_down, out_g):
    w_if = w_in[:, W_GATE_OFF:W_GATE_OFF + 2 * MLSTM_HEADS]
    wgt = jnp.pad(w_if, ((0, 0), (0, GATE_WIDTH - 2 * MLSTM_HEADS)))
    gbias = jnp.pad(jnp.concatenate([i_bias, f_bias]), (0, GATE_WIDTH - 2 * MLSTM_HEADS)).reshape(1, GATE_WIDTH)

    qkv, mm, gt, gg, vt_all, ogt_all = _inproj(x2d, norm1_g.reshape(1, D_MODEL), w_in.astype(BF16),
                                               wgt.astype(BF16), conv_w, conv_b.reshape(1, -1), tm=INPROJ_TM,
                                               seq=seq)
    tail_weights = [(w_attn_branch, Q_HEAD_ORDER), (w_mlstm_branch, None), (w_out, None),
                    (w_ffn_gate, None), (w_ffn_up, None), (w_ffn_down, None)]
    ao, hm, wab, wmb, wo, wg, wu, wd = _mixer(attn_sinks, qkv, mm, gt, vt_all, ogt_all, gbias,
                                              mlstm_norm_g.reshape(1, MLSTM_V_WIDTH), tail_weights, batch, seq,
                                              nch=MIXER_CHUNKS)
    return _tail(x2d, ao, hm, gg, wab, wmb, wo, norm2_g.reshape(1, D_MODEL), wg, wu, wd,
                 out_g.reshape(1, D_MODEL), tm=TAIL_TM, ff_split=TAIL_FF_SPLIT)


def kernel(x, norm1_g, w_in, conv_w, conv_b, i_bias, f_bias, mlstm_norm_g, attn_sinks, w_attn_branch,
           w_mlstm_branch, w_out, norm2_g, w_ffn_gate, w_ffn_up, w_ffn_down, final_norm_g):
    batch, seq, d = x.shape
    depth = norm1_g.shape[0]
    assert depth == 1 and d == D_MODEL
    assert seq % (MIXER_CHUNKS * CHUNK) == 0 and seq % INPROJ_TM == 0 and (batch * seq) % TAIL_TM == 0
    assert w_ffn_gate.shape[-1] % MXU_WIDTH == 0
    out = _layer(x.reshape(batch * seq, d), batch, seq, norm1_g[0], w_in[0], conv_w[0], conv_b[0], i_bias[0],
                 f_bias[0], mlstm_norm_g[0], attn_sinks[0], w_attn_branch[0], w_mlstm_branch[0], w_out[0],
                 norm2_g[0], w_ffn_gate[0], w_ffn_up[0], w_ffn_down[0], final_norm_g)
    return out.reshape(batch, seq, d)
```

```python
import functools

import jax
import jax.numpy as jnp
from jax import lax
from jax.experimental import pallas as pl
from jax.experimental.pallas import tpu as pltpu

D_MODEL = 1024
ATTN_HEADS = 8
ATTN_KV_HEADS = 2
ATTN_HEAD_DIM = 64
ATTN_GROUP = ATTN_HEADS // ATTN_KV_HEADS
WINDOW = 128
ATTN_Q_WIDTH = ATTN_HEADS * ATTN_HEAD_DIM
ATTN_KV_WIDTH = ATTN_KV_HEADS * ATTN_HEAD_DIM
MLSTM_HEADS = 4
MLSTM_QK_DIM = 64
MLSTM_V_DIM = 128
MLSTM_QK_WIDTH = MLSTM_HEADS * MLSTM_QK_DIM
MLSTM_V_WIDTH = MLSTM_HEADS * MLSTM_V_DIM
CHUNK = 128
CONV_WIDTH = 4
NORM_EPS = 1e-6

LANES = 128
BF16_SUBLANES = 16
MXU_WIDTH = 256
HALF = LANES // 2
CONV_PAD = 8
GATE_ROWS = 16
NQ_ROWS = BF16_SUBLANES
AUG_ROWS = MLSTM_V_DIM + NQ_ROWS

INPROJ_TM = 1024
INPROJ_ROW_BLOCKS = 2
MIXER_CHUNKS = 16
MIXER_STAGE_LAGS = (0, 1, 2, 3, 4, 4)
TAIL_TM = 512
TAIL_FF_SPLIT = 1
VMEM_LIMIT_BYTES = 56 * 1024 * 1024

QKV_WIDTH = ATTN_Q_WIDTH + 2 * ATTN_KV_WIDTH
MM_WIDTH = 2 * MLSTM_QK_WIDTH
GATE_WIDTH = LANES
MERGE_WIDTH = 2 * D_MODEL
W_MQK_OFF = QKV_WIDTH
W_MVO_OFF = W_MQK_OFF + 2 * MLSTM_QK_WIDTH
W_GATE_OFF = W_MVO_OFF + 2 * MLSTM_V_WIDTH
W_MERGE_OFF = W_GATE_OFF + 2 * MLSTM_HEADS

Q_HEAD_ORDER = tuple(h * ATTN_GROUP + g for g in range(ATTN_GROUP) for h in range(ATTN_KV_HEADS))

BF16 = jnp.bfloat16
F32 = jnp.float32
NEG_INF = float("-inf")


def _dot(a, b):
    return jnp.dot(a, b, preferred_element_type=F32)


def _dot_nt(a, b):
    return lax.dot_general(a, b, (((1,), (1,)), ((), ())), preferred_element_type=F32)


def _rms_scale(x):
    return lax.rsqrt(jnp.mean(x * x, axis=-1, keepdims=True) + NORM_EPS)


def _const_spec(shape):
    nd = len(shape)
    return pl.BlockSpec(shape, lambda *_: (0,) * nd, pipeline_mode=pl.Buffered(1))


def _inproj_kernel(x_ref, g_ref, w_ref, wgt_ref, convw_ref, convb_ref,
                   qkv_ref, mm_ref, gt_ref, gg_ref, vt_ref, ogt_ref, conv_ref, wq_ref, wgg_ref, wvt_ref, *,
                   tiles_per_seq):
    tm = x_ref.shape[0]

    @pl.when(pl.program_id(0) == 0)
    def _():
        src = lax.broadcasted_iota(jnp.int32, (ATTN_Q_WIDTH, ATTN_Q_WIDTH), 0)
        dst = lax.broadcasted_iota(jnp.int32, (ATTN_Q_WIDTH, ATTN_Q_WIDTH), 1)
        head = ((dst % LANES) // ATTN_HEAD_DIM) * ATTN_GROUP + dst // LANES
        perm = (src == head * ATTN_HEAD_DIM + dst % ATTN_HEAD_DIM).astype(BF16)
        wq_ref[...] = _dot(w_ref[:, 0:ATTN_Q_WIDTH], perm).astype(BF16)
        wgg_ref[...] = w_ref[:, W_MERGE_OFF:W_MERGE_OFF + MERGE_WIDTH]
        eye = (lax.broadcasted_iota(jnp.int32, (MLSTM_V_WIDTH, MLSTM_V_WIDTH), 0)
               == lax.broadcasted_iota(jnp.int32, (MLSTM_V_WIDTH, MLSTM_V_WIDTH), 1)).astype(BF16)
        for half in range(2):
            for k0 in range(0, D_MODEL, MLSTM_V_WIDTH):
                c0 = W_MVO_OFF + half * MLSTM_V_WIDTH
                wvt_ref[half * MLSTM_V_WIDTH:(half + 1) * MLSTM_V_WIDTH, k0:k0 + MLSTM_V_WIDTH] = _dot_nt(
                    eye, w_ref[k0:k0 + MLSTM_V_WIDTH, c0:c0 + MLSTM_V_WIDTH]).astype(BF16)

    @pl.when(pl.program_id(0) % tiles_per_seq == 0)
    def _():
        conv_ref[0:CONV_PAD, :] = jnp.zeros((CONV_PAD, 2 * MLSTM_QK_WIDTH), F32)

    q_scale = ATTN_HEAD_DIM ** -0.5
    lane = lax.broadcasted_iota(jnp.int32, (1, 2 * MLSTM_QK_WIDTH), 1)
    k_scale = jnp.where(lane < MLSTM_QK_WIDTH, 1.0, MLSTM_QK_DIM ** -0.5)
    rb = tm // INPROJ_ROW_BLOCKS
    for r in range(INPROJ_ROW_BLOCKS):
        rows = slice(r * rb, (r + 1) * rb)
        x = x_ref[rows, :]
        u = (x * _rms_scale(x) * g_ref[...]).astype(BF16)
        conv_ref[CONV_PAD + r * rb:CONV_PAD + (r + 1) * rb, :] = _dot(
            u, w_ref[:, W_MQK_OFF:W_MQK_OFF + 2 * MLSTM_QK_WIDTH])
        qkv_ref[rows, 0:ATTN_Q_WIDTH] = (_dot(u, wq_ref[...]) * q_scale).astype(BF16)
        qkv_ref[rows, ATTN_Q_WIDTH:QKV_WIDTH] = _dot(u, w_ref[:, ATTN_Q_WIDTH:QKV_WIDTH]).astype(BF16)
        vo_t = _dot_nt(wvt_ref[...], u).astype(BF16)
        for k in range(rb // CHUNK):
            ck = r * (rb // CHUNK) + k
            vt_ref[ck] = vo_t[0:MLSTM_V_WIDTH, k * CHUNK:(k + 1) * CHUNK]
            ogt_ref[ck] = vo_t[MLSTM_V_WIDTH:2 * MLSTM_V_WIDTH, k * CHUNK:(k + 1) * CHUNK]
        gt_ref[rows, :] = _dot(u, wgt_ref[...])
        gg_ref[rows, :] = _dot(u, wgg_ref[...]).astype(BF16)
        xe = conv_ref[r * rb:r * rb + CONV_PAD + rb, :]
        acc = xe * convw_ref[0:1, :]
        for t in range(1, CONV_WIDTH):
            acc = pltpu.roll(acc, 1, axis=0) + xe * convw_ref[t:t + 1, :]
        acc = acc[CONV_PAD:CONV_PAD + rb, :] + convb_ref[...]
        mm_ref[rows, 0:2 * MLSTM_QK_WIDTH] = (acc * jax.nn.sigmoid(acc) * k_scale).astype(BF16)
    conv_ref[0:CONV_PAD, :] = conv_ref[tm:tm + CONV_PAD, :]


def _inproj(x2d, g1, w_all, wgt, convw, convb, tm, seq):
    t = x2d.shape[0]
    row = lambda w: pl.BlockSpec((tm, w), lambda i: (i, 0))
    consts = (g1, w_all, wgt, convw, convb)
    return pl.pallas_call(
        functools.partial(_inproj_kernel, tiles_per_seq=seq // tm),
        grid=(t // tm,),
        in_specs=[row(D_MODEL)] + [_const_spec(c.shape) for c in consts],
        out_specs=[row(QKV_WIDTH), row(MM_WIDTH), row(GATE_WIDTH), row(MERGE_WIDTH),
                   pl.BlockSpec((tm // CHUNK, MLSTM_V_WIDTH, CHUNK), lambda i: (i, 0, 0)),
                   pl.BlockSpec((tm // CHUNK, MLSTM_V_WIDTH, CHUNK), lambda i: (i, 0, 0))],
        out_shape=[jax.ShapeDtypeStruct((t, QKV_WIDTH), BF16), jax.ShapeDtypeStruct((t, MM_WIDTH), BF16),
                   jax.ShapeDtypeStruct((t, GATE_WIDTH), F32), jax.ShapeDtypeStruct((t, MERGE_WIDTH), BF16),
                   jax.ShapeDtypeStruct((t // CHUNK, MLSTM_V_WIDTH, CHUNK), BF16),
                   jax.ShapeDtypeStruct((t // CHUNK, MLSTM_V_WIDTH, CHUNK), BF16)],
        scratch_shapes=[pltpu.VMEM((CONV_PAD + tm, 2 * MLSTM_QK_WIDTH), F32),
                        pltpu.VMEM((D_MODEL, ATTN_Q_WIDTH), BF16), pltpu.VMEM((D_MODEL, MERGE_WIDTH), BF16),
                        pltpu.VMEM((2 * MLSTM_V_WIDTH, D_MODEL), BF16)],
        compiler_params=pltpu.CompilerParams(dimension_semantics=("arbitrary",),
                                             vmem_limit_bytes=VMEM_LIMIT_BYTES),
        name="inproj",
    )(x2d, *consts)


def _log_sigmoid(x):
    return jnp.minimum(x, 0.0) - jnp.log1p(jnp.exp(-jnp.abs(x)))


def _split3(x):
    hi = x.astype(BF16)
    r1 = x - hi.astype(F32)
    mid = r1.astype(BF16)
    lo = (r1 - mid.astype(F32)).astype(BF16)
    return hi, mid, lo


def _mixer_kernel(sinks_ref, qkv_ref, mm_ref, gt_ref, vt_ref, ogt_ref, gbias_ref, normg_ref, *rest, nch, n_cast):
    cast_in, (ao_ref, hm_ref), rest = rest[:n_cast], rest[n_cast:n_cast + 2], rest[n_cast + 2:]
    cast_out, (btab_ref, kvp_ref, cst_ref, mst_ref, gcol_ref) = rest[:n_cast], rest[n_cast:]
    b_idx = pl.program_id(0)
    j_idx = pl.program_id(1)

    for src, dst in zip(cast_in, cast_out):
        dst[...] = src[...].astype(BF16)

    row_c = lax.broadcasted_iota(jnp.int32, (CHUNK, CHUNK), 0)
    col_c = lax.broadcasted_iota(jnp.int32, (CHUNK, CHUNK), 1)

    @pl.when((b_idx == 0) & (j_idx == 0))
    def _():
        cur = row_c <= col_c
        dist = jnp.where(cur, col_c - row_c, col_c - row_c + WINDOW).astype(F32)
        for j in range(ATTN_HEADS):
            slope = 2.0 ** (-8.0 * (j + 1) / ATTN_HEADS)
            btab_ref[0, j] = -slope * dist
            btab_ref[1, j] = jnp.where(cur, -slope * dist, NEG_INF)
        for h in range(MLSTM_HEADS):
            g_row = normg_ref[:, h * MLSTM_V_DIM:(h + 1) * MLSTM_V_DIM]
            gcol_ref[h] = jnp.broadcast_to(g_row, (MLSTM_V_DIM, LANES)).T

    @pl.when(j_idx == 0)
    def _():
        kvp_ref[...] = jnp.zeros_like(kvp_ref)
        cst_ref[...] = jnp.zeros_like(cst_ref)
        mst_ref[...] = jnp.zeros_like(mst_ref)

    cur_t = row_c <= col_c
    triu_b = cur_t.astype(BF16)
    eye_b = (row_c == col_c).astype(BF16)

    def transpose_bf16(x):
        return _dot_nt(eye_b, x)

    left_lanes = col_c < HALF
    ones_rows = jnp.ones((NQ_ROWS, LANES), BF16)
    left_row = lax.broadcasted_iota(jnp.int32, (1, LANES), 1) < HALF
    zero_b = jnp.zeros((), BF16)
    first = (j_idx == 0).astype(jnp.int32)

    def kv_sides(k_blk, v_blk):
        vt = transpose_bf16(v_blk).astype(BF16)
        k_sides = (jnp.where(left_lanes, k_blk, zero_b), jnp.where(left_lanes, zero_b, k_blk))
        vt_sides = tuple(jnp.concatenate([vt[s * HALF:(s + 1) * HALF], ones_rows], axis=0)
                         for s in range(ATTN_KV_HEADS))
        return k_sides, vt_sides

    k_off = ATTN_Q_WIDTH
    v_off = ATTN_Q_WIDTH + ATTN_KV_WIDTH
    prev_sides = kv_sides(kvp_ref[0], kvp_ref[1])
    last = slice((nch - 1) * CHUNK, nch * CHUNK)
    kvp_ref[0] = qkv_ref[last, k_off:k_off + ATTN_KV_WIDTH]
    kvp_ref[1] = qkv_ref[last, v_off:v_off + ATTN_KV_WIDTH]

    m_rows = [mst_ref[h:h + 1, :] for h in range(MLSTM_HEADS)]
    c_pairs = [cst_ref[pr] for pr in range(MLSTM_HEADS // 2)]

    npair = ATTN_GROUP
    rows_of = lambda c: slice(c * CHUNK, (c + 1) * CHUNK)
    row8 = lax.broadcasted_iota(jnp.int32, (GATE_ROWS, LANES), 0)
    head_rows = row8 < MLSTM_HEADS
    sides = {-1: prev_sides}
    gates, s_t, sc_ts, q_ms, p_t, mx_rows, intra_w, o_t, intra = {}, {}, {}, {}, {}, {}, {}, {}, {}

    def stage_prepare(c):
        rows = rows_of(c)
        sides[c] = kv_sides(qkv_ref[rows, k_off:k_off + ATTN_KV_WIDTH], qkv_ref[rows, v_off:v_off + ATTN_KV_WIDTH])
        g8 = (gt_ref[rows, :] + gbias_ref[...]).T[0:GATE_ROWS, :]
        f8 = pltpu.roll(g8, GATE_ROWS - MLSTM_HEADS, axis=0)
        lf_parts = _split3(jnp.where(head_rows, _log_sigmoid(f8), 0.0))
        b8 = sum(_dot(part, triu_b) for part in lf_parts)
        r8 = jnp.where(head_rows, g8 - b8, 0.0)
        r_t = jnp.concatenate([r8, jnp.zeros((CHUNK - GATE_ROWS, LANES), F32)], axis=0).T
        r_all = jnp.concatenate(
            [jnp.broadcast_to(r_t[:, h:h + 1], (CHUNK, LANES)) for h in range(MLSTM_HEADS)], axis=1)
        gates[c] = (g8, b8, r_all)

    def stage_scores(c):
        q = qkv_ref[rows_of(c), 0:ATTN_Q_WIDTH]
        q_pairs = [q[:, p * LANES:(p + 1) * LANES] for p in range(npair)]
        for side in range(ATTN_KV_HEADS):
            k_side = jnp.concatenate([sides[c - 1][0][side], sides[c][0][side]], axis=0)
            for pp in range(npair // 2):
                q2 = jnp.concatenate([q_pairs[2 * pp], q_pairs[2 * pp + 1]], axis=0)
                s2 = _dot_nt(k_side, q2)
                s_t[(c, side, 2 * pp)] = s2[:, 0:LANES]
                s_t[(c, side, 2 * pp + 1)] = s2[:, LANES:2 * LANES]
        for pr in range(MLSTM_HEADS // 2):
            qpair = mm_ref[rows_of(c), pr * LANES:(pr + 1) * LANES]
            k_pair = mm_ref[rows_of(c), MLSTM_QK_WIDTH + pr * LANES:MLSTM_QK_WIDTH + (pr + 1) * LANES]
            q_m = jnp.concatenate([jnp.where(left_lanes, qpair, zero_b), jnp.where(left_lanes, zero_b, qpair)],
                                  axis=0)
            q_ms[(c, pr)] = q_m
            sc_ts[(c, pr)] = _dot_nt(k_pair, q_m)

    def stage_weights(c):
        variant = first if c == 0 else 0
        for side in range(ATTN_KV_HEADS):
            for p in range(npair):
                j = side * ATTN_GROUP + p
                s2 = s_t.pop((c, side, p))
                comb = jnp.where(cur_t, s2[CHUNK:2 * CHUNK], s2[0:CHUNK]) + btab_ref[variant, j]
                mx = jnp.maximum(jnp.max(comb, axis=0, keepdims=True), sinks_ref[j])
                e = jnp.exp(comb - mx)
                p_t[(c, side, p)] = jnp.concatenate([jnp.where(cur_t, 0.0, e), jnp.where(cur_t, e, 0.0)],
                                                    axis=0).astype(BF16)
                mx_rows[(c, side, p)] = mx
        g8, b8, r_all = gates[c]
        for h in range(MLSTM_HEADS):
            b_row = b8[h:h + 1, :]
            d_t = jnp.where(cur_t, r_all[:, h * LANES:(h + 1) * LANES] + b_row, NEG_INF)
            mloc = jnp.max(d_t, axis=0, keepdims=True)
            wloc = jnp.exp(d_t - mloc)
            scw = (sc_ts[(c, h // 2)][:, (h % 2) * LANES:(h % 2 + 1) * LANES] * wloc).astype(BF16)
            b_last = b_row[:, CHUNK - 1:CHUNK]
            mloc_last = mloc[:, CHUNK - 1:CHUNK]
            u_row = jnp.exp((g8[h:h + 1, :] - b_row) + b_last - mloc_last)
            intra_w[(c, h)] = (b_row, mloc, scw, b_last, mloc_last, u_row)

    def stage_values(c):
        for side in range(ATTN_KV_HEADS):
            vt_side = jnp.concatenate([sides[c - 1][1][side], sides[c][1][side]], axis=1)
            for pp in range(npair // 2):
                p2 = jnp.concatenate([p_t.pop((c, side, 2 * pp)), p_t.pop((c, side, 2 * pp + 1))], axis=1)
                o2 = _dot(vt_side, p2)
                o_t[(c, side, 2 * pp)] = o2[:, 0:LANES]
                o_t[(c, side, 2 * pp + 1)] = o2[:, LANES:2 * LANES]
        for h in range(MLSTM_HEADS):
            b_row, mloc, scw, b_last, mloc_last, u_row = intra_w.pop((c, h))
            pr, half = h // 2, h % 2
            vt_aug = jnp.concatenate([vt_ref[c, h * MLSTM_V_DIM:(h + 1) * MLSTM_V_DIM, :], ones_rows], axis=0)
            k_pair = mm_ref[rows_of(c), MLSTM_QK_WIDTH + pr * LANES:MLSTM_QK_WIDTH + (pr + 1) * LANES]
            k_hm = jnp.where(left_lanes, k_pair, zero_b) if half == 0 else jnp.where(left_lanes, zero_b, k_pair)
            num_t = _dot(vt_aug, scw)
            upd_t = _dot((vt_aug.astype(F32) * u_row).astype(BF16), k_hm)
            intra[(c, h)] = (b_row, mloc, num_t, b_last, mloc_last, upd_t)

    def stage_attn_out(c):
        for p in range(npair):
            scaled = []
            for side in range(ATTN_KV_HEADS):
                j = side * ATTN_GROUP + p
                o = o_t.pop((c, side, p))
                den = o[HALF:HALF + 1, :] + jnp.exp(sinks_ref[j] - mx_rows.pop((c, side, p)))
                scaled.append(o[0:HALF] * (1.0 / den))
            ao_ref[c, p * LANES:(p + 1) * LANES, :] = jnp.concatenate(scaled, axis=0).astype(BF16)

    def stage_recurrence(c):
        rows = rows_of(c)
        for pr in range(MLSTM_HEADS // 2):
            c_pair = c_pairs[pr]
            inter_t = _dot_nt(c_pair.astype(BF16), q_ms.pop((c, pr)))
            decs, eus, upds = [], [], []
            for half in range(2):
                h = 2 * pr + half
                b_row, mloc, num_t, b_last, mloc_last, upd_t = intra.pop((c, h))
                m_prev = m_rows[h]
                it = inter_t[:, half * LANES:(half + 1) * LANES]
                a = b_row + m_prev
                m_t = jnp.maximum(a, mloc)
                e_t = jnp.exp(mloc - m_t)
                w_inter = jnp.exp(a - m_t)
                numv = e_t * num_t[0:MLSTM_V_DIM] + w_inter * it[0:MLSTM_V_DIM]
                nq = e_t * num_t[MLSTM_V_DIM:MLSTM_V_DIM + 1] + w_inter * it[MLSTM_V_DIM:MLSTM_V_DIM + 1]
                den = jnp.maximum(jnp.abs(nq), jnp.exp(-m_t))
                ssq = jnp.sum(numv * numv, axis=0, keepdims=True)
                scale = lax.rsqrt(ssq * (1.0 / MLSTM_V_DIM) + NORM_EPS * (den * den))
                og = ogt_ref[c, h * MLSTM_V_DIM:(h + 1) * MLSTM_V_DIM, :].astype(F32)
                gate = 0.5 * jnp.tanh(0.5 * og) + 0.5
                hm_ref[c, h * MLSTM_V_DIM:(h + 1) * MLSTM_V_DIM, :] = (numv * scale * gcol_ref[h] * gate).astype(BF16)
                m_new = jnp.maximum(b_last + m_prev, mloc_last)
                decs.append(jnp.exp(b_last + m_prev - m_new))
                eus.append(jnp.exp(mloc_last - m_new))
                upds.append(upd_t)
                m_rows[h] = m_new
            c_pairs[pr] = (jnp.where(left_row, decs[0], decs[1]) * c_pair
                           + eus[0] * upds[0] + eus[1] * upds[1])

    stages = (stage_prepare, stage_scores, stage_weights, stage_values, stage_attn_out, stage_recurrence)
    for tick in range(nch + max(MIXER_STAGE_LAGS)):
        for stage, lag in zip(stages, MIXER_STAGE_LAGS):
            c = tick - lag
            if 0 <= c < nch:
                stage(c)

    for h in range(MLSTM_HEADS):
        mst_ref[h:h + 1, :] = m_rows[h]
    for pr in range(MLSTM_HEADS // 2):
        cst_ref[pr] = c_pairs[pr]


def _cast_slab_rows(n_rows, n_steps):
    slab = BF16_SUBLANES * pl.cdiv(pl.cdiv(n_rows, n_steps), BF16_SUBLANES)
    while n_rows % slab:
        slab += BF16_SUBLANES
    return slab


def _mixer(sinks, qkv, mm, gt, vt, ogt, gbias, normg, cast_weights, batch, seq, nch):
    tm = nch * CHUNK
    nj = seq // tm
    t = batch * seq
    row = lambda w: pl.BlockSpec((tm, w), lambda b, j: (b * nj + j, 0))
    col = lambda h: pl.BlockSpec((nch, h, CHUNK), lambda b, j: (b * nj + j, 0, 0))
    cast_in_specs, cast_out_specs, cast_shapes = [], [], []
    for w, block_perm in cast_weights:
        n_rows, n_cols = w.shape
        slab = n_rows // len(block_perm) if block_perm else _cast_slab_rows(n_rows, batch * nj)
        assert slab % BF16_SUBLANES == 0 and n_rows // slab <= batch * nj
        last = n_rows // slab - 1
        dst_map = lambda b, j, last=last: (jnp.minimum(b * nj + j, last), 0)
        if block_perm:
            assert tuple(block_perm) == tuple((s % 2) * (len(block_perm) // 2) + s // 2
                                              for s in range(len(block_perm)))
            stride = len(block_perm) // 2
            src_map = lambda b, j, last=last, stride=stride: (
                (jnp.minimum(b * nj + j, last) % 2) * stride + jnp.minimum(b * nj + j, last) // 2, 0)
        else:
            src_map = dst_map
        cast_in_specs.append(pl.BlockSpec((slab, n_cols), src_map))
        cast_out_specs.append(pl.BlockSpec((slab, n_cols), dst_map))
        cast_shapes.append(jax.ShapeDtypeStruct(w.shape, BF16))
    return pl.pallas_call(
        functools.partial(_mixer_kernel, nch=nch, n_cast=len(cast_weights)),
        grid=(batch, nj),
        in_specs=[pl.BlockSpec(memory_space=pltpu.SMEM),
                  row(QKV_WIDTH), row(MM_WIDTH), row(GATE_WIDTH),
                  col(MLSTM_V_WIDTH), col(MLSTM_V_WIDTH),
                  _const_spec(gbias.shape), _const_spec(normg.shape)] + cast_in_specs,
        out_specs=[col(ATTN_Q_WIDTH), col(MLSTM_V_WIDTH)] + cast_out_specs,
        out_shape=[jax.ShapeDtypeStruct((t // CHUNK, ATTN_Q_WIDTH, CHUNK), BF16),
                   jax.ShapeDtypeStruct((t // CHUNK, MLSTM_V_WIDTH, CHUNK), BF16)]
                  + cast_shapes,
        scratch_shapes=[
            pltpu.VMEM((2, ATTN_HEADS, CHUNK, CHUNK), F32),
            pltpu.VMEM((2, CHUNK, ATTN_KV_WIDTH), BF16),
            pltpu.VMEM((MLSTM_HEADS // 2, AUG_ROWS, 2 * MLSTM_QK_DIM), F32),
            pltpu.VMEM((8, LANES), F32),
            pltpu.VMEM((MLSTM_HEADS, MLSTM_V_DIM, LANES), F32),
        ],
        compiler_params=pltpu.CompilerParams(dimension_semantics=("arbitrary", "arbitrary"),
                                             vmem_limit_bytes=VMEM_LIMIT_BYTES),
        name="mixer",
    )(sinks, qkv, mm, gt, vt, ogt, gbias, normg, *[w for w, _ in cast_weights])


def _tail_kernel(x_ref, ao_ref, hm_ref, gg_ref, wab_ref, wmb_ref, wo_ref, g2_ref, wg_ref, wu_ref, wd_ref,
                 gf_ref, out_ref, *, ff_split):
    def tokens_major(ref):
        return jnp.concatenate([ref[k].astype(F32).T for k in range(ref.shape[0])], axis=0).astype(BF16)

    ya = _dot(tokens_major(ao_ref), wab_ref[...])
    ym = _dot(tokens_major(hm_ref), wmb_ref[...])
    ga = jax.nn.sigmoid(gg_ref[:, 0:D_MODEL].astype(F32))
    gm = jax.nn.sigmoid(gg_ref[:, D_MODEL:2 * D_MODEL].astype(F32))
    z = (ga * ya + gm * ym).astype(BF16)
    x1 = x_ref[...] + _dot(z, wo_ref[...])
    f = (x1 * g2_ref[...]).astype(BF16)
    rs = _rms_scale(x1)
    n_tiles = wg_ref.shape[1] // MXU_WIDTH
    bounds = [MXU_WIDTH * ((n_tiles * s + ff_split - 1) // ff_split) for s in range(ff_split + 1)]
    x2 = x1
    for s in range(ff_split):
        cols = slice(bounds[s], bounds[s + 1])
        gte = _dot(f, wg_ref[:, cols]) * rs
        up = _dot(f, wu_ref[:, cols]) * rs
        hh = (gte * jax.nn.sigmoid(gte) * up).astype(BF16)
        x2 = x2 + _dot(hh, wd_ref[cols, :])
    out_ref[...] = x2 * _rms_scale(x2) * gf_ref[...]


def _tail(x2d, ao, hm, gg, wab, wmb, wo, g2, wg, wu, wd, gf, tm, ff_split):
    t = x2d.shape[0]
    row = lambda w: pl.BlockSpec((tm, w), lambda i: (i, 0))
    consts = (wab, wmb, wo, g2, wg, wu, wd, gf)
    return pl.pallas_call(
        functools.partial(_tail_kernel, ff_split=ff_split),
        grid=(t // tm,),
        in_specs=[row(D_MODEL), pl.BlockSpec((tm // CHUNK, ATTN_Q_WIDTH, CHUNK), lambda i: (i, 0, 0)),
                  pl.BlockSpec((tm // CHUNK, MLSTM_V_WIDTH, CHUNK), lambda i: (i, 0, 0)), row(MERGE_WIDTH)]
                 + [_const_spec(c.shape) for c in consts],
        out_specs=row(D_MODEL),
        out_shape=jax.ShapeDtypeStruct((t, D_MODEL), F32),
        compiler_params=pltpu.CompilerParams(dimension_semantics=("arbitrary",),
                                             vmem_limit_bytes=VMEM_LIMIT_BYTES),
        name="tail",
    )(x2d, ao, hm, gg, *consts)


def _layer(x2d, batch, seq, norm1_g, w_in, conv_w, conv_b, i_bias, f_bias, mlstm_norm_g, attn_sinks,
           w_attn_branch, w_mlstm_branch, w_out, norm2_g, w_ffn_gate, w_ffn_up, w_ffn_down, out_g):
    w_if = w_in[:, W_GATE_OFF:W_GATE_OFF + 2 * MLSTM_HEADS]
    wgt = jnp.pad(w_if, ((0, 0), (0, GATE_WIDTH - 2 * MLSTM_HEADS)))
    gbias = jnp.pad(jnp.concatenate([i_bias, f_bias]), (0, GATE_WIDTH - 2 * MLSTM_HEADS)).reshape(1, GATE_WIDTH)

    qkv, mm, gt, gg, vt_all, ogt_all = _inproj(x2d, norm1_g.reshape(1, D_MODEL), w_in.astype(BF16),
                                               wgt.astype(BF16), conv_w, conv_b.reshape(1, -1), tm=INPROJ_TM,
                                               seq=seq)
    tail_weights = [(w_attn_branch, Q_HEAD_ORDER), (w_mlstm_branch, None), (w_out, None),
                    (w_ffn_gate, None), (w_ffn_up, None), (w_ffn_down, None)]
    ao, hm, wab, wmb, wo, wg, wu, wd = _mixer(attn_sinks, qkv, mm, gt, vt_all, ogt_all, gbias,
                                              mlstm_norm_g.reshape(1, MLSTM_V_WIDTH), tail_weights, batch, seq,
                                              nch=MIXER_CHUNKS)
    return _tail(x2d, ao, hm, gg, wab, wmb, wo, norm2_g.reshape(1, D_MODEL), wg, wu, wd,
                 out_g.reshape(1, D_MODEL), tm=TAIL_TM, ff_split=TAIL_FF_SPLIT)


def kernel(x, norm1_g, w_in, conv_w, conv_b, i_bias, f_bias, mlstm_norm_g, attn_sinks, w_attn_branch,
           w_mlstm_branch, w_out, norm2_g, w_ffn_gate, w_ffn_up, w_ffn_down, final_norm_g):
    batch, seq, d = x.shape
    depth = norm1_g.shape[0]
    assert depth == 1 and d == D_MODEL
    assert seq % (MIXER_CHUNKS * CHUNK) == 0 and seq % INPROJ_TM == 0 and (batch * seq) % TAIL_TM == 0
    assert w_ffn_gate.shape[-1] % MXU_WIDTH == 0
    out = _layer(x.reshape(batch * seq, d), batch, seq, norm1_g[0], w_in[0], conv_w[0], conv_b[0], i_bias[0],
                 f_bias[0], mlstm_norm_g[0], attn_sinks[0], w_attn_branch[0], w_mlstm_branch[0], w_out[0],
                 norm2_g[0], w_ffn_gate[0], w_ffn_up[0], w_ffn_down[0], final_norm_g)
    return out.reshape(batch, seq, d)
```

```python
import functools

import jax
import jax.numpy as jnp
from jax import lax
from jax.experimental import pallas as pl
from jax.experimental.pallas import tpu as pltpu

D_MODEL = 1024
ATTN_HEADS = 8
ATTN_KV_HEADS = 2
ATTN_HEAD_DIM = 64
ATTN_GROUP = ATTN_HEADS // ATTN_KV_HEADS
WINDOW = 128
ATTN_Q_WIDTH = ATTN_HEADS * ATTN_HEAD_DIM
ATTN_KV_WIDTH = ATTN_KV_HEADS * ATTN_HEAD_DIM
MLSTM_HEADS = 4
MLSTM_QK_DIM = 64
MLSTM_V_DIM = 128
MLSTM_QK_WIDTH = MLSTM_HEADS * MLSTM_QK_DIM
MLSTM_V_WIDTH = MLSTM_HEADS * MLSTM_V_DIM
CHUNK = 128
CONV_WIDTH = 4
NORM_EPS = 1e-6

LANES = 128
BF16_SUBLANES = 16
MXU_WIDTH = 256
HALF = LANES // 2
CONV_PAD = 8
GATE_ROWS = 16
NQ_ROWS = BF16_SUBLANES
AUG_ROWS = MLSTM_V_DIM + NQ_ROWS

INPROJ_TM = 1024
INPROJ_ROW_BLOCKS = 2
MIXER_CHUNKS = 16
MIXER_STAGE_LAGS = (0, 1, 2, 3, 4, 4)
TAIL_TM = 512
TAIL_FF_SPLIT = 1
VMEM_LIMIT_BYTES = 56 * 1024 * 1024

QKV_WIDTH = ATTN_Q_WIDTH + 2 * ATTN_KV_WIDTH
MM_WIDTH = 2 * MLSTM_QK_WIDTH
GATE_WIDTH = LANES
MERGE_WIDTH = 2 * D_MODEL
W_MQK_OFF = QKV_WIDTH
W_MVO_OFF = W_MQK_OFF + 2 * MLSTM_QK_WIDTH
W_GATE_OFF = W_MVO_OFF + 2 * MLSTM_V_WIDTH
W_MERGE_OFF = W_GATE_OFF + 2 * MLSTM_HEADS

Q_HEAD_ORDER = tuple(h * ATTN_GROUP + g for g in range(ATTN_GROUP) for h in range(ATTN_KV_HEADS))

BF16 = jnp.bfloat16
F32 = jnp.float32
NEG_INF = float("-inf")


def _dot(a, b):
    return jnp.dot(a, b, preferred_element_type=F32)


def _dot_nt(a, b):
    return lax.dot_general(a, b, (((1,), (1,)), ((), ())), preferred_element_type=F32)


def _rms_scale(x):
    return lax.rsqrt(jnp.mean(x * x, axis=-1, keepdims=True) + NORM_EPS)


def _const_spec(shape):
    nd = len(shape)
    return pl.BlockSpec(shape, lambda *_: (0,) * nd, pipeline_mode=pl.Buffered(1))


def _inproj_kernel(x_ref, g_ref, w_ref, wgt_ref, convw_ref, convb_ref,
                   qkv_ref, mm_ref, gt_ref, gg_ref, vt_ref, ogt_ref, conv_ref, wq_ref, wgg_ref, wvt_ref, *,
                   tiles_per_seq):
    tm = x_ref.shape[0]

    @pl.when(pl.program_id(0) == 0)
    def _():
        src = lax.broadcasted_iota(jnp.int32, (ATTN_Q_WIDTH, ATTN_Q_WIDTH), 0)
        dst = lax.broadcasted_iota(jnp.int32, (ATTN_Q_WIDTH, ATTN_Q_WIDTH), 1)
        head = ((dst % LANES) // ATTN_HEAD_DIM) * ATTN_GROUP + dst // LANES
        perm = (src == head * ATTN_HEAD_DIM + dst % ATTN_HEAD_DIM).astype(BF16)
        wq_ref[...] = _dot(w_ref[:, 0:ATTN_Q_WIDTH], perm).astype(BF16)
        wgg_ref[...] = w_ref[:, W_MERGE_OFF:W_MERGE_OFF + MERGE_WIDTH]
        eye = (lax.broadcasted_iota(jnp.int32, (MLSTM_V_WIDTH, MLSTM_V_WIDTH), 0)
               == lax.broadcasted_iota(jnp.int32, (MLSTM_V_WIDTH, MLSTM_V_WIDTH), 1)).astype(BF16)
        for half in range(2):
            for k0 in range(0, D_MODEL, MLSTM_V_WIDTH):
                c0 = W_MVO_OFF + half * MLSTM_V_WIDTH
                wvt_ref[half * MLSTM_V_WIDTH:(half + 1) * MLSTM_V_WIDTH, k0:k0 + MLSTM_V_WIDTH] = _dot_nt(
                    eye, w_ref[k0:k0 + MLSTM_V_WIDTH, c0:c0 + MLSTM_V_WIDTH]).astype(BF16)

    @pl.when(pl.program_id(0) % tiles_per_seq == 0)
    def _():
        conv_ref[0:CONV_PAD, :] = jnp.zeros((CONV_PAD, 2 * MLSTM_QK_WIDTH), F32)

    q_scale = ATTN_HEAD_DIM ** -0.5
    lane = lax.broadcasted_iota(jnp.int32, (1, 2 * MLSTM_QK_WIDTH), 1)
    k_scale = jnp.where(lane < MLSTM_QK_WIDTH, 1.0, MLSTM_QK_DIM ** -0.5)
    rb = tm // INPROJ_ROW_BLOCKS
    for r in range(INPROJ_ROW_BLOCKS):
        rows = slice(r * rb, (r + 1) * rb)
        x = x_ref[rows, :]
        u = (x * _rms_scale(x) * g_ref[...]).astype(BF16)
        conv_ref[CONV_PAD + r * rb:CONV_PAD + (r + 1) * rb, :] = _dot(
            u, w_ref[:, W_MQK_OFF:W_MQK_OFF + 2 * MLSTM_QK_WIDTH])
        qkv_ref[rows, 0:ATTN_Q_WIDTH] = (_dot(u, wq_ref[...]) * q_scale).astype(BF16)
        qkv_ref[rows, ATTN_Q_WIDTH:QKV_WIDTH] = _dot(u, w_ref[:, ATTN_Q_WIDTH:QKV_WIDTH]).astype(BF16)
        vo_t = _dot_nt(wvt_ref[...], u).astype(BF16)
        for k in range(rb // CHUNK):
            ck = r * (rb // CHUNK) + k
            vt_ref[ck] = vo_t[0:MLSTM_V_WIDTH, k * CHUNK:(k + 1) * CHUNK]
            ogt_ref[ck] = vo_t[MLSTM_V_WIDTH:2 * MLSTM_V_WIDTH, k * CHUNK:(k + 1) * CHUNK]
        gt_ref[rows, :] = _dot(u, wgt_ref[...])
        gg_ref[rows, :] = _dot(u, wgg_ref[...]).astype(BF16)
        xe = conv_ref[r * rb:r * rb + CONV_PAD + rb, :]
        acc = xe * convw_ref[0:1, :]
        for t in range(1, CONV_WIDTH):
            acc = pltpu.roll(acc, 1, axis=0) + xe * convw_ref[t:t + 1, :]
        acc = acc[CONV_PAD:CONV_PAD + rb, :] + convb_ref[...]
        mm_ref[rows, 0:2 * MLSTM_QK_WIDTH] = (acc * jax.nn.sigmoid(acc) * k_scale).astype(BF16)
    conv_ref[0:CONV_PAD, :] = conv_ref[tm:tm + CONV_PAD, :]


def _inproj(x2d, g1, w_all, wgt, convw, convb, tm, seq):
    t = x2d.shape[0]
    row = lambda w: pl.BlockSpec((tm, w), lambda i: (i, 0))
    consts = (g1, w_all, wgt, convw, convb)
    return pl.pallas_call(
        functools.partial(_inproj_kernel, tiles_per_seq=seq // tm),
        grid=(t // tm,),
        in_specs=[row(D_MODEL)] + [_const_spec(c.shape) for c in consts],
        out_specs=[row(QKV_WIDTH), row(MM_WIDTH), row(GATE_WIDTH), row(MERGE_WIDTH),
                   pl.BlockSpec((tm // CHUNK, MLSTM_V_WIDTH, CHUNK), lambda i: (i, 0, 0)),
                   pl.BlockSpec((tm // CHUNK, MLSTM_V_WIDTH, CHUNK), lambda i: (i, 0, 0))],
        out_shape=[jax.ShapeDtypeStruct((t, QKV_WIDTH), BF16), jax.ShapeDtypeStruct((t, MM_WIDTH), BF16),
                   jax.ShapeDtypeStruct((t, GATE_WIDTH), F32), jax.ShapeDtypeStruct((t, MERGE_WIDTH), BF16),
                   jax.ShapeDtypeStruct((t // CHUNK, MLSTM_V_WIDTH, CHUNK), BF16),
                   jax.ShapeDtypeStruct((t // CHUNK, MLSTM_V_WIDTH, CHUNK), BF16)],
        scratch_shapes=[pltpu.VMEM((CONV_PAD + tm, 2 * MLSTM_QK_WIDTH), F32),
                        pltpu.VMEM((D_MODEL, ATTN_Q_WIDTH), BF16), pltpu.VMEM((D_MODEL, MERGE_WIDTH), BF16),
                        pltpu.VMEM((2 * MLSTM_V_WIDTH, D_MODEL), BF16)],
        compiler_params=pltpu.CompilerParams(dimension_semantics=("arbitrary",),
                                             vmem_limit_bytes=VMEM_LIMIT_BYTES),
        name="inproj",
    )(x2d, *consts)


def _log_sigmoid(x):
    return jnp.minimum(x, 0.0) - jnp.log1p(jnp.exp(-jnp.abs(x)))


def _split3(x):
    hi = x.astype(BF16)
    r1 = x - hi.astype(F32)
    mid = r1.astype(BF16)
    lo = (r1 - mid.astype(F32)).astype(BF16)
    return hi, mid, lo


def _mixer_kernel(sinks_ref, qkv_ref, mm_ref, gt_ref, vt_ref, ogt_ref, gbias_ref, normg_ref, *rest, nch, n_cast):
    cast_in, (ao_ref, hm_ref), rest = rest[:n_cast], rest[n_cast:n_cast + 2], rest[n_cast + 2:]
    cast_out, (btab_ref, kvp_ref, cst_ref, mst_ref, gcol_ref) = rest[:n_cast], rest[n_cast:]
    b_idx = pl.program_id(0)
    j_idx = pl.program_id(1)

    for src, dst in zip(cast_in, cast_out):
        dst[...] = src[...].astype(BF16)

    row_c = lax.broadcasted_iota(jnp.int32, (CHUNK, CHUNK), 0)
    col_c = lax.broadcasted_iota(jnp.int32, (CHUNK, CHUNK), 1)

    @pl.when((b_idx == 0) & (j_idx == 0))
    def _():
        cur = row_c <= col_c
        dist = jnp.where(cur, col_c - row_c, col_c - row_c + WINDOW).astype(F32)
        for j in range(ATTN_HEADS):
            slope = 2.0 ** (-8.0 * (j + 1) / ATTN_HEADS)
            btab_ref[0, j] = -slope * dist
            btab_ref[1, j] = jnp.where(cur, -slope * dist, NEG_INF)
        for h in range(MLSTM_HEADS):
            g_row = normg_ref[:, h * MLSTM_V_DIM:(h + 1) * MLSTM_V_DIM]
            gcol_ref[h] = jnp.broadcast_to(g_row, (MLSTM_V_DIM, LANES)).T

    @pl.when(j_idx == 0)
    def _():
        kvp_ref[...] = jnp.zeros_like(kvp_ref)
        cst_ref[...] = jnp.zeros_like(cst_ref)
        mst_ref[...] = jnp.zeros_like(mst_ref)

    cur_t = row_c <= col_c
    triu_b = cur_t.astype(BF16)
    eye_b = (row_c == col_c).astype(BF16)

    def transpose_bf16(x):
        return _dot_nt(eye_b, x)

    left_lanes = col_c < HALF
    ones_rows = jnp.ones((NQ_ROWS, LANES), BF16)
    left_row = lax.broadcasted_iota(jnp.int32, (1, LANES), 1) < HALF
    zero_b = jnp.zeros((), BF16)
    first = (j_idx == 0).astype(jnp.int32)

    def kv_sides(k_blk, v_blk):
        vt = transpose_bf16(v_blk).astype(BF16)
        k_sides = (jnp.where(left_lanes, k_blk, zero_b), jnp.where(left_lanes, zero_b, k_blk))
        vt_sides = tuple(jnp.concatenate([vt[s * HALF:(s + 1) * HALF], ones_rows], axis=0)
                         for s in range(ATTN_KV_HEADS))
        return k_sides, vt_sides

    k_off = ATTN_Q_WIDTH
    v_off = ATTN_Q_WIDTH + ATTN_KV_WIDTH
    prev_sides = kv_sides(kvp_ref[0], kvp_ref[1])
    last = slice((nch - 1) * CHUNK, nch * CHUNK)
    kvp_ref[0] = qkv_ref[last, k_off:k_off + ATTN_KV_WIDTH]
    kvp_ref[1] = qkv_ref[last, v_off:v_off + ATTN_KV_WIDTH]

    m_rows = [mst_ref[h:h + 1, :] for h in range(MLSTM_HEADS)]
    c_pairs = [cst_ref[pr] for pr in range(MLSTM_HEADS // 2)]

    npair = ATTN_GROUP
    rows_of = lambda c: slice(c * CHUNK, (c + 1) * CHUNK)
    row8 = lax.broadcasted_iota(jnp.int32, (GATE_ROWS, LANES), 0)
    head_rows = row8 < MLSTM_HEADS
    sides = {-1: prev_sides}
    gates, s_t, sc_ts, q_ms, p_t, mx_rows, intra_w, o_t, intra = {}, {}, {}, {}, {}, {}, {}, {}, {}

    def stage_prepare(c):
        rows = rows_of(c)
        sides[c] = kv_sides(qkv_ref[rows, k_off:k_off + ATTN_KV_WIDTH], qkv_ref[rows, v_off:v_off + ATTN_KV_WIDTH])
        g8 = (gt_ref[rows, :] + gbias_ref[...]).T[0:GATE_ROWS, :]
        f8 = pltpu.roll(g8, GATE_ROWS - MLSTM_HEADS, axis=0)
        lf_parts = _split3(jnp.where(head_rows, _log_sigmoid(f8), 0.0))
        b8 = sum(_dot(part, triu_b) for part in lf_parts)
        r8 = jnp.where(head_rows, g8 - b8, 0.0)
        r_t = jnp.concatenate([r8, jnp.zeros((CHUNK - GATE_ROWS, LANES), F32)], axis=0).T
        r_all = jnp.concatenate(
            [jnp.broadcast_to(r_t[:, h:h + 1], (CHUNK, LANES)) for h in range(MLSTM_HEADS)], axis=1)
        gates[c] = (g8, b8, r_all)

    def stage_scores(c):
        q = qkv_ref[rows_of(c), 0:ATTN_Q_WIDTH]
        q_pairs = [q[:, p * LANES:(p + 1) * LANES] for p in range(npair)]
        for side in range(ATTN_KV_HEADS):
            k_side = jnp.concatenate([sides[c - 1][0][side], sides[c][0][side]], axis=0)
            for pp in range(npair // 2):
                q2 = jnp.concatenate([q_pairs[2 * pp], q_pairs[2 * pp + 1]], axis=0)
                s2 = _dot_nt(k_side, q2)
                s_t[(c, side, 2 * pp)] = s2[:, 0:LANES]
                s_t[(c, side, 2 * pp + 1)] = s2[:, LANES:2 * LANES]
        for pr in range(MLSTM_HEADS // 2):
            qpair = mm_ref[rows_of(c), pr * LANES:(pr + 1) * LANES]
            k_pair = mm_ref[rows_of(c), MLSTM_QK_WIDTH + pr * LANES:MLSTM_QK_WIDTH + (pr + 1) * LANES]
            q_m = jnp.concatenate([jnp.where(left_lanes, qpair, zero_b), jnp.where(left_lanes, zero_b, qpair)],
                                  axis=0)
            q_ms[(c, pr)] = q_m
            sc_ts[(c, pr)] = _dot_nt(k_pair, q_m)

    def stage_weights(c):
        variant = first if c == 0 else 0
        for side in range(ATTN_KV_HEADS):
            for p in range(npair):
                j = side * ATTN_GROUP + p
                s2 = s_t.pop((c, side, p))
                comb = jnp.where(cur_t, s2[CHUNK:2 * CHUNK], s2[0:CHUNK]) + btab_ref[variant, j]
                mx = jnp.maximum(jnp.max(comb, axis=0, keepdims=True), sinks_ref[j])
                e = jnp.exp(comb - mx)
                p_t[(c, side, p)] = jnp.concatenate([jnp.where(cur_t, 0.0, e), jnp.where(cur_t, e, 0.0)],
                                                    axis=0).astype(BF16)
                mx_rows[(c, side, p)] = mx
        g8, b8, r_all = gates[c]
        for h in range(MLSTM_HEADS):
            b_row = b8[h:h + 1, :]
            d_t = jnp.where(cur_t, r_all[:, h * LANES:(h + 1) * LANES] + b_row, NEG_INF)
            mloc = jnp.max(d_t, axis=0, keepdims=True)
            wloc = jnp.exp(d_t - mloc)
            scw = (sc_ts[(c, h // 2)][:, (h % 2) * LANES:(h % 2 + 1) * LANES] * wloc).astype(BF16)
            b_last = b_row[:, CHUNK - 1:CHUNK]
            mloc_last = mloc[:, CHUNK - 1:CHUNK]
            u_row = jnp.exp((g8[h:h + 1, :] - b_row) + b_last - mloc_last)
            intra_w[(c, h)] = (b_row, mloc, scw, b_last, mloc_last, u_row)

    def stage_values(c):
        for side in range(ATTN_KV_HEADS):
            vt_side = jnp.concatenate([sides[c - 1][1][side], sides[c][1][side]], axis=1)
            for pp in range(npair // 2):
                p2 = jnp.concatenate([p_t.pop((c, side, 2 * pp)), p_t.pop((c, side, 2 * pp + 1))], axis=1)
                o2 = _dot(vt_side, p2)
                o_t[(c, side, 2 * pp)] = o2[:, 0:LANES]
                o_t[(c, side, 2 * pp + 1)] = o2[:, LANES:2 * LANES]
        for h in range(MLSTM_HEADS):
            b_row, mloc, scw, b_last, mloc_last, u_row = intra_w.pop((c, h))
            pr, half = h // 2, h % 2
            vt_aug = jnp.concatenate([vt_ref[c, h * MLSTM_V_DIM:(h + 1) * MLSTM_V_DIM, :], ones_rows], axis=0)
            k_pair = mm_ref[rows_of(c), MLSTM_QK_WIDTH + pr * LANES:MLSTM_QK_WIDTH + (pr + 1) * LANES]
            k_hm = jnp.where(left_lanes, k_pair, zero_b) if half == 0 else jnp.where(left_lanes, zero_b, k_pair)
            num_t = _dot(vt_aug, scw)
            upd_t = _dot((vt_aug.astype(F32) * u_row).astype(BF16), k_hm)
            intra[(c, h)] = (b_row, mloc, num_t, b_last, mloc_last, upd_t)

    def stage_attn_out(c):
        for p in range(npair):
            scaled = []
            for side in range(ATTN_KV_HEADS):
                j = side * ATTN_GROUP + p
                o = o_t.pop((c, side, p))
                den = o[HALF:HALF + 1, :] + jnp.exp(sinks_ref[j] - mx_rows.pop((c, side, p)))
                scaled.append(o[0:HALF] * (1.0 / den))
            ao_ref[c, p * LANES:(p + 1) * LANES, :] = jnp.concatenate(scaled, axis=0).astype(BF16)

    def stage_recurrence(c):
        rows = rows_of(c)
        for pr in range(MLSTM_HEADS // 2):
            c_pair = c_pairs[pr]
            inter_t = _dot_nt(c_pair.astype(BF16), q_ms.pop((c, pr)))
            decs, eus, upds = [], [], []
            for half in range(2):
                h = 2 * pr + half
                b_row, mloc, num_t, b_last, mloc_last, upd_t = intra.pop((c, h))
                m_prev = m_rows[h]
                it = inter_t[:, half * LANES:(half + 1) * LANES]
                a = b_row + m_prev
                m_t = jnp.maximum(a, mloc)
                e_t = jnp.exp(mloc - m_t)
                w_inter = jnp.exp(a - m_t)
                numv = e_t * num_t[0:MLSTM_V_DIM] + w_inter * it[0:MLSTM_V_DIM]
                nq = e_t * num_t[MLSTM_V_DIM:MLSTM_V_DIM + 1] + w_inter * it[MLSTM_V_DIM:MLSTM_V_DIM + 1]
                den = jnp.maximum(jnp.abs(nq), jnp.exp(-m_t))
                ssq = jnp.sum(numv * numv, axis=0, keepdims=True)
                scale = lax.rsqrt(ssq * (1.0 / MLSTM_V_DIM) + NORM_EPS * (den * den))
                og = ogt_ref[c, h * MLSTM_V_DIM:(h + 1) * MLSTM_V_DIM, :].astype(F32)
                gate = 0.5 * jnp.tanh(0.5 * og) + 0.5
                hm_ref[c, h * MLSTM_V_DIM:(h + 1) * MLSTM_V_DIM, :] = (numv * scale * gcol_ref[h] * gate).astype(BF16)
                m_new = jnp.maximum(b_last + m_prev, mloc_last)
                decs.append(jnp.exp(b_last + m_prev - m_new))
                eus.append(jnp.exp(mloc_last - m_new))
                upds.append(upd_t)
                m_rows[h] = m_new
            c_pairs[pr] = (jnp.where(left_row, decs[0], decs[1]) * c_pair
                           + eus[0] * upds[0] + eus[1] * upds[1])

    stages = (stage_prepare, stage_scores, stage_weights, stage_values, stage_attn_out, stage_recurrence)
    for tick in range(nch + max(MIXER_STAGE_LAGS)):
        for stage, lag in zip(stages, MIXER_STAGE_LAGS):
            c = tick - lag
            if 0 <= c < nch:
                stage(c)

    for h in range(MLSTM_HEADS):
        mst_ref[h:h + 1, :] = m_rows[h]
    for pr in range(MLSTM_HEADS // 2):
        cst_ref[pr] = c_pairs[pr]


def _cast_slab_rows(n_rows, n_steps):
    slab = BF16_SUBLANES * pl.cdiv(pl.cdiv(n_rows, n_steps), BF16_SUBLANES)
    while n_rows % slab:
        slab += BF16_SUBLANES
    return slab


def _mixer(sinks, qkv, mm, gt, vt, ogt, gbias, normg, cast_weights, batch, seq, nch):
    tm = nch * CHUNK
    nj = seq // tm
    t = batch * seq
    row = lambda w: pl.BlockSpec((tm, w), lambda b, j: (b * nj + j, 0))
    col = lambda h: pl.BlockSpec((nch, h, CHUNK), lambda b, j: (b * nj + j, 0, 0))
    cast_in_specs, cast_out_specs, cast_shapes = [], [], []
    for w, block_perm in cast_weights:
        n_rows, n_cols = w.shape
        slab = n_rows // len(block_perm) if block_perm else _cast_slab_rows(n_rows, batch * nj)
        assert slab % BF16_SUBLANES == 0 and n_rows // slab <= batch * nj
        last = n_rows // slab - 1
        dst_map = lambda b, j, last=last: (jnp.minimum(b * nj + j, last), 0)
        if block_perm:
            assert tuple(block_perm) == tuple((s % 2) * (len(block_perm) // 2) + s // 2
                                              for s in range(len(block_perm)))
            stride = len(block_perm) // 2
            src_map = lambda b, j, last=last, stride=stride: (
                (jnp.minimum(b * nj + j, last) % 2) * stride + jnp.minimum(b * nj + j, last) // 2, 0)
        else:
            src_map = dst_map
        cast_in_specs.append(pl.BlockSpec((slab, n_cols), src_map))
        cast_out_specs.append(pl.BlockSpec((slab, n_cols), dst_map))
        cast_shapes.append(jax.ShapeDtypeStruct(w.shape, BF16))
    return pl.pallas_call(
        functools.partial(_mixer_kernel, nch=nch, n_cast=len(cast_weights)),
        grid=(batch, nj),
        in_specs=[pl.BlockSpec(memory_space=pltpu.SMEM),
                  row(QKV_WIDTH), row(MM_WIDTH), row(GATE_WIDTH),
                  col(MLSTM_V_WIDTH), col(MLSTM_V_WIDTH),
                  _const_spec(gbias.shape), _const_spec(normg.shape)] + cast_in_specs,
        out_specs=[col(ATTN_Q_WIDTH), col(MLSTM_V_WIDTH)] + cast_out_specs,
        out_shape=[jax.ShapeDtypeStruct((t // CHUNK, ATTN_Q_WIDTH, CHUNK), BF16),
                   jax.ShapeDtypeStruct((t // CHUNK, MLSTM_V_WIDTH, CHUNK), BF16)]
                  + cast_shapes,
        scratch_shapes=[
            pltpu.VMEM((2, ATTN_HEADS, CHUNK, CHUNK), F32),
            pltpu.VMEM((2, CHUNK, ATTN_KV_WIDTH), BF16),
            pltpu.VMEM((MLSTM_HEADS // 2, AUG_ROWS, 2 * MLSTM_QK_DIM), F32),
            pltpu.VMEM((8, LANES), F32),
            pltpu.VMEM((MLSTM_HEADS, MLSTM_V_DIM, LANES), F32),
        ],
        compiler_params=pltpu.CompilerParams(dimension_semantics=("arbitrary", "arbitrary"),
                                             vmem_limit_bytes=VMEM_LIMIT_BYTES),
        name="mixer",
    )(sinks, qkv, mm, gt, vt, ogt, gbias, normg, *[w for w, _ in cast_weights])


def _tail_kernel(x_ref, ao_ref, hm_ref, gg_ref, wab_ref, wmb_ref, wo_ref, g2_ref, wg_ref, wu_ref, wd_ref,
                 gf_ref, out_ref, *, ff_split):
    def features_major(ref):
        return jnp.concatenate([ref[k] for k in range(ref.shape[0])], axis=1)

    tn = (((0,), (0,)), ((), ()))
    ya = lax.dot_general(features_major(ao_ref), wab_ref[...], tn, preferred_element_type=F32)
    ym = lax.dot_general(features_major(hm_ref), wmb_ref[...], tn, preferred_element_type=F32)
    ga = jax.nn.sigmoid(gg_ref[:, 0:D_MODEL].astype(F32))
    gm = jax.nn.sigmoid(gg_ref[:, D_MODEL:2 * D_MODEL].astype(F32))
    z = (ga * ya + gm * ym).astype(BF16)
    x1 = x_ref[...] + _dot(z, wo_ref[...])
    f = (x1 * g2_ref[...]).astype(BF16)
    rs = _rms_scale(x1)
    n_tiles = wg_ref.shape[1] // MXU_WIDTH
    bounds = [MXU_WIDTH * ((n_tiles * s + ff_split - 1) // ff_split) for s in range(ff_split + 1)]
    x2 = x1
    for s in range(ff_split):
        cols = slice(bounds[s], bounds[s + 1])
        gte = _dot(f, wg_ref[:, cols]) * rs
        up = _dot(f, wu_ref[:, cols]) * rs
        hh = (gte * jax.nn.sigmoid(gte) * up).astype(BF16)
        x2 = x2 + _dot(hh, wd_ref[cols, :])
    out_ref[...] = x2 * _rms_scale(x2) * gf_ref[...]


def _tail(x2d, ao, hm, gg, wab, wmb, wo, g2, wg, wu, wd, gf, tm, ff_split):
    t = x2d.shape[0]
    row = lambda w: pl.BlockSpec((tm, w), lambda i: (i, 0))
    consts = (wab, wmb, wo, g2, wg, wu, wd, gf)
    return pl.pallas_call(
        functools.partial(_tail_kernel, ff_split=ff_split),
        grid=(t // tm,),
        in_specs=[row(D_MODEL), pl.BlockSpec((tm // CHUNK, ATTN_Q_WIDTH, CHUNK), lambda i: (i, 0, 0)),
                  pl.BlockSpec((tm // CHUNK, MLSTM_V_WIDTH, CHUNK), lambda i: (i, 0, 0)), row(MERGE_WIDTH)]
                 + [_const_spec(c.shape) for c in consts],
        out_specs=row(D_MODEL),
        out_shape=jax.ShapeDtypeStruct((t, D_MODEL), F32),
        compiler_params=pltpu.CompilerParams(dimension_semantics=("arbitrary",),
                                             vmem_limit_bytes=VMEM_LIMIT_BYTES),
        name="tail",
    )(x2d, ao, hm, gg, *consts)


def _layer(x2d, batch, seq, norm1_g, w_in, conv_w, conv_b, i_bias, f_bias, mlstm_norm_g, attn_sinks,
           w_attn_branch, w_mlstm_branch, w_out, norm2_g, w_ffn_gate, w_ffn_up, w_ffn_down, out_g):
    w_if = w_in[:, W_GATE_OFF:W_GATE_OFF + 2 * MLSTM_HEADS]
    wgt = jnp.pad(w_if, ((0, 0), (0, GATE_WIDTH - 2 * MLSTM_HEADS)))
    gbias = jnp.pad(jnp.concatenate([i_bias, f_bias]), (0, GATE_WIDTH - 2 * MLSTM_HEADS)).reshape(1, GATE_WIDTH)

    qkv, mm, gt, gg, vt_all, ogt_all = _inproj(x2d, norm1_g.reshape(1, D_MODEL), w_in.astype(BF16),
                                               wgt.astype(BF16), conv_w, conv_b.reshape(1, -1), tm=INPROJ_TM,
                                               seq=seq)
    tail_weights = [(w_attn_branch, Q_HEAD_ORDER), (w_mlstm_branch, None), (w_out, None),
                    (w_ffn_gate, None), (w_ffn_up, None), (w_ffn_down, None)]
    ao, hm, wab, wmb, wo, wg, wu, wd = _mixer(attn_sinks, qkv, mm, gt, vt_all, ogt_all, gbias,
                                              mlstm_norm_g.reshape(1, MLSTM_V_WIDTH), tail_weights, batch, seq,
                                              nch=MIXER_CHUNKS)
    return _tail(x2d, ao, hm, gg, wab, wmb, wo, norm2_g.reshape(1, D_MODEL), wg, wu, wd,
                 out_g.reshape(1, D_MODEL), tm=TAIL_TM, ff_split=TAIL_FF_SPLIT)


def kernel(x, norm1_g, w_in, conv_w, conv_b, i_bias, f_bias, mlstm_norm_g, attn_sinks, w_attn_branch,
           w_mlstm_branch, w_out, norm2_g, w_ffn_gate, w_ffn_up, w_ffn_down, final_norm_g):
    batch, seq, d = x.shape
    depth = norm1_g.shape[0]
    assert depth == 1 and d == D_MODEL
    assert seq % (MIXER_CHUNKS * CHUNK) == 0 and seq % INPROJ_TM == 0 and (batch * seq) % TAIL_TM == 0
    assert w_ffn_gate.shape[-1] % MXU_WIDTH == 0
    out = _layer(x.reshape(batch * seq, d), batch, seq, norm1_g[0], w_in[0], conv_w[0], conv_b[0], i_bias[0],
                 f_bias[0], mlstm_norm_g[0], attn_sinks[0], w_attn_branch[0], w_mlstm_branch[0], w_out[0],
                 norm2_g[0], w_ffn_gate[0], w_ffn_up[0], w_ffn_down[0], final_norm_g)
    return out.reshape(batch, seq, d)
```

```python
import functools

import jax
import jax.numpy as jnp
from jax import lax
from jax.experimental import pallas as pl
from jax.experimental.pallas import tpu as pltpu

D_MODEL = 1024
ATTN_HEADS = 8
ATTN_KV_HEADS = 2
ATTN_HEAD_DIM = 64
ATTN_GROUP = ATTN_HEADS // ATTN_KV_HEADS
WINDOW = 128
ATTN_Q_WIDTH = ATTN_HEADS * ATTN_HEAD_DIM
ATTN_KV_WIDTH = ATTN_KV_HEADS * ATTN_HEAD_DIM
MLSTM_HEADS = 4
MLSTM_QK_DIM = 64
MLSTM_V_DIM = 128
MLSTM_QK_WIDTH = MLSTM_HEADS * MLSTM_QK_DIM
MLSTM_V_WIDTH = MLSTM_HEADS * MLSTM_V_DIM
CHUNK = 128
CONV_WIDTH = 4
NORM_EPS = 1e-6

LANES = 128
BF16_SUBLANES = 16
MXU_WIDTH = 256
HALF = LANES // 2
CONV_PAD = 8
GATE_ROWS = 16
NQ_ROWS = BF16_SUBLANES
AUG_ROWS = MLSTM_V_DIM + NQ_ROWS

INPROJ_TM = 1024
INPROJ_ROW_BLOCKS = 2
MIXER_CHUNKS = 16
MIXER_STAGE_LAGS = (0, 1, 2, 3, 4, 4)
TAIL_TM = 512
TAIL_FF_SPLIT = 1
VMEM_LIMIT_BYTES = 56 * 1024 * 1024

QKV_WIDTH = ATTN_Q_WIDTH + 2 * ATTN_KV_WIDTH
MM_WIDTH = 2 * MLSTM_QK_WIDTH
GATE_WIDTH = LANES
MERGE_WIDTH = 2 * D_MODEL
W_MQK_OFF = QKV_WIDTH
W_MVO_OFF = W_MQK_OFF + 2 * MLSTM_QK_WIDTH
W_GATE_OFF = W_MVO_OFF + 2 * MLSTM_V_WIDTH
W_MERGE_OFF = W_GATE_OFF + 2 * MLSTM_HEADS

Q_HEAD_ORDER = tuple(h * ATTN_GROUP + g for g in range(ATTN_GROUP) for h in range(ATTN_KV_HEADS))

BF16 = jnp.bfloat16
F32 = jnp.float32
NEG_INF = float("-inf")


def _dot(a, b):
    return jnp.dot(a, b, preferred_element_type=F32)


def _dot_nt(a, b):
    return lax.dot_general(a, b, (((1,), (1,)), ((), ())), preferred_element_type=F32)


def _rms_scale(x):
    return lax.rsqrt(jnp.mean(x * x, axis=-1, keepdims=True) + NORM_EPS)


def _const_spec(shape):
    nd = len(shape)
    return pl.BlockSpec(shape, lambda *_: (0,) * nd, pipeline_mode=pl.Buffered(1))


def _inproj_kernel(x_ref, g_ref, w_ref, wgt_ref, convw_ref, convb_ref,
                   qkv_ref, mm_ref, gt_ref, gg_ref, vt_ref, ogt_ref, conv_ref, wq_ref, wgg_ref, wvt_ref, *,
                   tiles_per_seq):
    tm = x_ref.shape[0]

    @pl.when(pl.program_id(0) == 0)
    def _():
        src = lax.broadcasted_iota(jnp.int32, (ATTN_Q_WIDTH, ATTN_Q_WIDTH), 0)
        dst = lax.broadcasted_iota(jnp.int32, (ATTN_Q_WIDTH, ATTN_Q_WIDTH), 1)
        head = ((dst % LANES) // ATTN_HEAD_DIM) * ATTN_GROUP + dst // LANES
        perm = (src == head * ATTN_HEAD_DIM + dst % ATTN_HEAD_DIM).astype(BF16)
        wq_ref[...] = _dot(w_ref[:, 0:ATTN_Q_WIDTH], perm).astype(BF16)
        wgg_ref[...] = w_ref[:, W_MERGE_OFF:W_MERGE_OFF + MERGE_WIDTH]
        eye = (lax.broadcasted_iota(jnp.int32, (MLSTM_V_WIDTH, MLSTM_V_WIDTH), 0)
               == lax.broadcasted_iota(jnp.int32, (MLSTM_V_WIDTH, MLSTM_V_WIDTH), 1)).astype(BF16)
        for half in range(2):
            for k0 in range(0, D_MODEL, MLSTM_V_WIDTH):
                c0 = W_MVO_OFF + half * MLSTM_V_WIDTH
                wvt_ref[half * MLSTM_V_WIDTH:(half + 1) * MLSTM_V_WIDTH, k0:k0 + MLSTM_V_WIDTH] = _dot_nt(
                    eye, w_ref[k0:k0 + MLSTM_V_WIDTH, c0:c0 + MLSTM_V_WIDTH]).astype(BF16)

    @pl.when(pl.program_id(0) % tiles_per_seq == 0)
    def _():
        conv_ref[0:CONV_PAD, :] = jnp.zeros((CONV_PAD, 2 * MLSTM_QK_WIDTH), F32)

    q_scale = ATTN_HEAD_DIM ** -0.5
    lane = lax.broadcasted_iota(jnp.int32, (1, 2 * MLSTM_QK_WIDTH), 1)
    k_scale = jnp.where(lane < MLSTM_QK_WIDTH, 1.0, MLSTM_QK_DIM ** -0.5)
    rb = tm // INPROJ_ROW_BLOCKS

    def store_tiles(ref, r, g0, val):
        for k in range(rb // CHUNK):
            for g in range(val.shape[1] // LANES):
                ref[r * (rb // CHUNK) + k, g0 + g] = val[k * CHUNK:(k + 1) * CHUNK, g * LANES:(g + 1) * LANES]

    for r in range(INPROJ_ROW_BLOCKS):
        rows = slice(r * rb, (r + 1) * rb)
        x = x_ref[rows, :]
        u = (x * _rms_scale(x) * g_ref[...]).astype(BF16)
        conv_ref[CONV_PAD + r * rb:CONV_PAD + (r + 1) * rb, :] = _dot(
            u, w_ref[:, W_MQK_OFF:W_MQK_OFF + 2 * MLSTM_QK_WIDTH])
        store_tiles(qkv_ref, r, 0, (_dot(u, wq_ref[...]) * q_scale).astype(BF16))
        store_tiles(qkv_ref, r, ATTN_Q_WIDTH // LANES, _dot(u, w_ref[:, ATTN_Q_WIDTH:QKV_WIDTH]).astype(BF16))
        vo_t = _dot_nt(wvt_ref[...], u).astype(BF16)
        for k in range(rb // CHUNK):
            ck = r * (rb // CHUNK) + k
            vt_ref[ck] = vo_t[0:MLSTM_V_WIDTH, k * CHUNK:(k + 1) * CHUNK]
            ogt_ref[ck] = vo_t[MLSTM_V_WIDTH:2 * MLSTM_V_WIDTH, k * CHUNK:(k + 1) * CHUNK]
        gt_ref[rows, :] = _dot(u, wgt_ref[...])
        gg_ref[rows, :] = _dot(u, wgg_ref[...]).astype(BF16)
        xe = conv_ref[r * rb:r * rb + CONV_PAD + rb, :]
        acc = xe * convw_ref[0:1, :]
        for t in range(1, CONV_WIDTH):
            acc = pltpu.roll(acc, 1, axis=0) + xe * convw_ref[t:t + 1, :]
        acc = acc[CONV_PAD:CONV_PAD + rb, :] + convb_ref[...]
        store_tiles(mm_ref, r, 0, (acc * jax.nn.sigmoid(acc) * k_scale).astype(BF16))
    conv_ref[0:CONV_PAD, :] = conv_ref[tm:tm + CONV_PAD, :]


def _inproj(x2d, g1, w_all, wgt, convw, convb, tm, seq):
    t = x2d.shape[0]
    row = lambda w: pl.BlockSpec((tm, w), lambda i: (i, 0))
    tiles = lambda w: pl.BlockSpec((tm // CHUNK, w // LANES, CHUNK, LANES), lambda i: (i, 0, 0, 0))
    consts = (g1, w_all, wgt, convw, convb)
    return pl.pallas_call(
        functools.partial(_inproj_kernel, tiles_per_seq=seq // tm),
        grid=(t // tm,),
        in_specs=[row(D_MODEL)] + [_const_spec(c.shape) for c in consts],
        out_specs=[tiles(QKV_WIDTH), tiles(MM_WIDTH), row(GATE_WIDTH), row(MERGE_WIDTH),
                   pl.BlockSpec((tm // CHUNK, MLSTM_V_WIDTH, CHUNK), lambda i: (i, 0, 0)),
                   pl.BlockSpec((tm // CHUNK, MLSTM_V_WIDTH, CHUNK), lambda i: (i, 0, 0))],
        out_shape=[jax.ShapeDtypeStruct((t // CHUNK, QKV_WIDTH // LANES, CHUNK, LANES), BF16),
                   jax.ShapeDtypeStruct((t // CHUNK, MM_WIDTH // LANES, CHUNK, LANES), BF16),
                   jax.ShapeDtypeStruct((t, GATE_WIDTH), F32), jax.ShapeDtypeStruct((t, MERGE_WIDTH), BF16),
                   jax.ShapeDtypeStruct((t // CHUNK, MLSTM_V_WIDTH, CHUNK), BF16),
                   jax.ShapeDtypeStruct((t // CHUNK, MLSTM_V_WIDTH, CHUNK), BF16)],
        scratch_shapes=[pltpu.VMEM((CONV_PAD + tm, 2 * MLSTM_QK_WIDTH), F32),
                        pltpu.VMEM((D_MODEL, ATTN_Q_WIDTH), BF16), pltpu.VMEM((D_MODEL, MERGE_WIDTH), BF16),
                        pltpu.VMEM((2 * MLSTM_V_WIDTH, D_MODEL), BF16)],
        compiler_params=pltpu.CompilerParams(dimension_semantics=("arbitrary",),
                                             vmem_limit_bytes=VMEM_LIMIT_BYTES),
        name="inproj",
    )(x2d, *consts)


def _log_sigmoid(x):
    return jnp.minimum(x, 0.0) - jnp.log1p(jnp.exp(-jnp.abs(x)))


def _split3(x):
    hi = x.astype(BF16)
    r1 = x - hi.astype(F32)
    mid = r1.astype(BF16)
    lo = (r1 - mid.astype(F32)).astype(BF16)
    return hi, mid, lo


def _mixer_kernel(sinks_ref, qkv_ref, mm_ref, gt_ref, vt_ref, ogt_ref, gbias_ref, normg_ref, *rest, nch, n_cast):
    cast_in, (ao_ref, hm_ref), rest = rest[:n_cast], rest[n_cast:n_cast + 2], rest[n_cast + 2:]
    cast_out, (btab_ref, kvp_ref, cst_ref, mst_ref, gcol_ref) = rest[:n_cast], rest[n_cast:]
    b_idx = pl.program_id(0)
    j_idx = pl.program_id(1)

    for src, dst in zip(cast_in, cast_out):
        dst[...] = src[...].astype(BF16)

    row_c = lax.broadcasted_iota(jnp.int32, (CHUNK, CHUNK), 0)
    col_c = lax.broadcasted_iota(jnp.int32, (CHUNK, CHUNK), 1)

    @pl.when((b_idx == 0) & (j_idx == 0))
    def _():
        cur = row_c <= col_c
        dist = jnp.where(cur, col_c - row_c, col_c - row_c + WINDOW).astype(F32)
        for j in range(ATTN_HEADS):
            slope = 2.0 ** (-8.0 * (j + 1) / ATTN_HEADS)
            btab_ref[0, j] = -slope * dist
            btab_ref[1, j] = jnp.where(cur, -slope * dist, NEG_INF)
        for h in range(MLSTM_HEADS):
            g_row = normg_ref[:, h * MLSTM_V_DIM:(h + 1) * MLSTM_V_DIM]
            gcol_ref[h] = jnp.broadcast_to(g_row, (MLSTM_V_DIM, LANES)).T

    @pl.when(j_idx == 0)
    def _():
        kvp_ref[...] = jnp.zeros_like(kvp_ref)
        cst_ref[...] = jnp.zeros_like(cst_ref)
        mst_ref[...] = jnp.zeros_like(mst_ref)

    cur_t = row_c <= col_c
    triu_b = cur_t.astype(BF16)
    eye_b = (row_c == col_c).astype(BF16)

    def transpose_bf16(x):
        return _dot_nt(eye_b, x)

    left_lanes = col_c < HALF
    ones_rows = jnp.ones((NQ_ROWS, LANES), BF16)
    left_row = lax.broadcasted_iota(jnp.int32, (1, LANES), 1) < HALF
    zero_b = jnp.zeros((), BF16)
    first = (j_idx == 0).astype(jnp.int32)

    def kv_sides(k_blk, v_blk):
        vt = transpose_bf16(v_blk).astype(BF16)
        k_sides = (jnp.where(left_lanes, k_blk, zero_b), jnp.where(left_lanes, zero_b, k_blk))
        vt_sides = tuple(jnp.concatenate([vt[s * HALF:(s + 1) * HALF], ones_rows], axis=0)
                         for s in range(ATTN_KV_HEADS))
        return k_sides, vt_sides

    k_grp = ATTN_Q_WIDTH // LANES
    v_grp = k_grp + 1
    prev_sides = kv_sides(kvp_ref[0], kvp_ref[1])
    kvp_ref[0] = qkv_ref[nch - 1, k_grp]
    kvp_ref[1] = qkv_ref[nch - 1, v_grp]

    m_rows = [mst_ref[h:h + 1, :] for h in range(MLSTM_HEADS)]
    c_pairs = [cst_ref[pr] for pr in range(MLSTM_HEADS // 2)]

    npair = ATTN_GROUP
    rows_of = lambda c: slice(c * CHUNK, (c + 1) * CHUNK)
    row8 = lax.broadcasted_iota(jnp.int32, (GATE_ROWS, LANES), 0)
    head_rows = row8 < MLSTM_HEADS
    sides = {-1: prev_sides}
    gates, s_t, sc_ts, q_ms, p_t, mx_rows, intra_w, o_t, intra = {}, {}, {}, {}, {}, {}, {}, {}, {}

    def stage_prepare(c):
        rows = rows_of(c)
        sides[c] = kv_sides(qkv_ref[c, k_grp], qkv_ref[c, v_grp])
        g8 = (gt_ref[rows, :] + gbias_ref[...]).T[0:GATE_ROWS, :]
        f8 = pltpu.roll(g8, GATE_ROWS - MLSTM_HEADS, axis=0)
        lf_parts = _split3(jnp.where(head_rows, _log_sigmoid(f8), 0.0))
        b8 = sum(_dot(part, triu_b) for part in lf_parts)
        r8 = jnp.where(head_rows, g8 - b8, 0.0)
        r_t = jnp.concatenate([r8, jnp.zeros((CHUNK - GATE_ROWS, LANES), F32)], axis=0).T
        r_all = jnp.concatenate(
            [jnp.broadcast_to(r_t[:, h:h + 1], (CHUNK, LANES)) for h in range(MLSTM_HEADS)], axis=1)
        gates[c] = (g8, b8, r_all)

    def stage_scores(c):
        q_pairs = [qkv_ref[c, p] for p in range(npair)]
        for side in range(ATTN_KV_HEADS):
            k_side = jnp.concatenate([sides[c - 1][0][side], sides[c][0][side]], axis=0)
            for pp in range(npair // 2):
                q2 = jnp.concatenate([q_pairs[2 * pp], q_pairs[2 * pp + 1]], axis=0)
                s2 = _dot_nt(k_side, q2)
                s_t[(c, side, 2 * pp)] = s2[:, 0:LANES]
                s_t[(c, side, 2 * pp + 1)] = s2[:, LANES:2 * LANES]
        for pr in range(MLSTM_HEADS // 2):
            qpair = mm_ref[c, pr]
            k_pair = mm_ref[c, MLSTM_QK_WIDTH // LANES + pr]
            q_m = jnp.concatenate([jnp.where(left_lanes, qpair, zero_b), jnp.where(left_lanes, zero_b, qpair)],
                                  axis=0)
            q_ms[(c, pr)] = q_m
            sc_ts[(c, pr)] = _dot_nt(k_pair, q_m)

    def stage_weights(c):
        variant = first if c == 0 else 0
        for side in range(ATTN_KV_HEADS):
            for p in range(npair):
                j = side * ATTN_GROUP + p
                s2 = s_t.pop((c, side, p))
                comb = jnp.where(cur_t, s2[CHUNK:2 * CHUNK], s2[0:CHUNK]) + btab_ref[variant, j]
                mx = jnp.maximum(jnp.max(comb, axis=0, keepdims=True), sinks_ref[j])
                e = jnp.exp(comb - mx)
                p_t[(c, side, p)] = jnp.concatenate([jnp.where(cur_t, 0.0, e), jnp.where(cur_t, e, 0.0)],
                                                    axis=0).astype(BF16)
                mx_rows[(c, side, p)] = mx
        g8, b8, r_all = gates[c]
        for h in range(MLSTM_HEADS):
            b_row = b8[h:h + 1, :]
            d_t = jnp.where(cur_t, r_all[:, h * LANES:(h + 1) * LANES] + b_row, NEG_INF)
            mloc = jnp.max(d_t, axis=0, keepdims=True)
            wloc = jnp.exp(d_t - mloc)
            scw = (sc_ts[(c, h // 2)][:, (h % 2) * LANES:(h % 2 + 1) * LANES] * wloc).astype(BF16)
            b_last = b_row[:, CHUNK - 1:CHUNK]
            mloc_last = mloc[:, CHUNK - 1:CHUNK]
            u_row = jnp.exp((g8[h:h + 1, :] - b_row) + b_last - mloc_last)
            intra_w[(c, h)] = (b_row, mloc, scw, b_last, mloc_last, u_row)

    def stage_values(c):
        for side in range(ATTN_KV_HEADS):
            vt_side = jnp.concatenate([sides[c - 1][1][side], sides[c][1][side]], axis=1)
            for pp in range(npair // 2):
                p2 = jnp.concatenate([p_t.pop((c, side, 2 * pp)), p_t.pop((c, side, 2 * pp + 1))], axis=1)
                o2 = _dot(vt_side, p2)
                o_t[(c, side, 2 * pp)] = o2[:, 0:LANES]
                o_t[(c, side, 2 * pp + 1)] = o2[:, LANES:2 * LANES]
        for h in range(MLSTM_HEADS):
            b_row, mloc, scw, b_last, mloc_last, u_row = intra_w.pop((c, h))
            pr, half = h // 2, h % 2
            vt_aug = jnp.concatenate([vt_ref[c, h * MLSTM_V_DIM:(h + 1) * MLSTM_V_DIM, :], ones_rows], axis=0)
            k_pair = mm_ref[c, MLSTM_QK_WIDTH // LANES + pr]
            k_hm = jnp.where(left_lanes, k_pair, zero_b) if half == 0 else jnp.where(left_lanes, zero_b, k_pair)
            num_t = _dot(vt_aug, scw)
            upd_t = _dot((vt_aug.astype(F32) * u_row).astype(BF16), k_hm)
            intra[(c, h)] = (b_row, mloc, num_t, b_last, mloc_last, upd_t)

    def stage_attn_out(c):
        for p in range(npair):
            scaled = []
            for side in range(ATTN_KV_HEADS):
                j = side * ATTN_GROUP + p
                o = o_t.pop((c, side, p))
                den = o[HALF:HALF + 1, :] + jnp.exp(sinks_ref[j] - mx_rows.pop((c, side, p)))
                scaled.append(o[0:HALF] * (1.0 / den))
            ao_ref[c, p * LANES:(p + 1) * LANES, :] = jnp.concatenate(scaled, axis=0).astype(BF16)

    def stage_recurrence(c):
        rows = rows_of(c)
        for pr in range(MLSTM_HEADS // 2):
            c_pair = c_pairs[pr]
            inter_t = _dot_nt(c_pair.astype(BF16), q_ms.pop((c, pr)))
            decs, eus, upds = [], [], []
            for half in range(2):
                h = 2 * pr + half
                b_row, mloc, num_t, b_last, mloc_last, upd_t = intra.pop((c, h))
                m_prev = m_rows[h]
                it = inter_t[:, half * LANES:(half + 1) * LANES]
                a = b_row + m_prev
                m_t = jnp.maximum(a, mloc)
                e_t = jnp.exp(mloc - m_t)
                w_inter = jnp.exp(a - m_t)
                numv = e_t * num_t[0:MLSTM_V_DIM] + w_inter * it[0:MLSTM_V_DIM]
                nq = e_t * num_t[MLSTM_V_DIM:MLSTM_V_DIM + 1] + w_inter * it[MLSTM_V_DIM:MLSTM_V_DIM + 1]
                den = jnp.maximum(jnp.abs(nq), jnp.exp(-m_t))
                ssq = jnp.sum(numv * numv, axis=0, keepdims=True)
                scale = lax.rsqrt(ssq * (1.0 / MLSTM_V_DIM) + NORM_EPS * (den * den))
                og = ogt_ref[c, h * MLSTM_V_DIM:(h + 1) * MLSTM_V_DIM, :].astype(F32)
                gate = 0.5 * jnp.tanh(0.5 * og) + 0.5
                hm_ref[c, h * MLSTM_V_DIM:(h + 1) * MLSTM_V_DIM, :] = (numv * scale * gcol_ref[h] * gate).astype(BF16)
                m_new = jnp.maximum(b_last + m_prev, mloc_last)
                decs.append(jnp.exp(b_last + m_prev - m_new))
                eus.append(jnp.exp(mloc_last - m_new))
                upds.append(upd_t)
                m_rows[h] = m_new
            c_pairs[pr] = (jnp.where(left_row, decs[0], decs[1]) * c_pair
                           + eus[0] * upds[0] + eus[1] * upds[1])

    stages = (stage_prepare, stage_scores, stage_weights, stage_values, stage_attn_out, stage_recurrence)
    for tick in range(nch + max(MIXER_STAGE_LAGS)):
        for stage, lag in zip(stages, MIXER_STAGE_LAGS):
            c = tick - lag
            if 0 <= c < nch:
                stage(c)

    for h in range(MLSTM_HEADS):
        mst_ref[h:h + 1, :] = m_rows[h]
    for pr in range(MLSTM_HEADS // 2):
        cst_ref[pr] = c_pairs[pr]


def _cast_slab_rows(n_rows, n_steps):
    slab = BF16_SUBLANES * pl.cdiv(pl.cdiv(n_rows, n_steps), BF16_SUBLANES)
    while n_rows % slab:
        slab += BF16_SUBLANES
    return slab


def _mixer(sinks, qkv, mm, gt, vt, ogt, gbias, normg, cast_weights, batch, seq, nch):
    tm = nch * CHUNK
    nj = seq // tm
    t = batch * seq
    row = lambda w: pl.BlockSpec((tm, w), lambda b, j: (b * nj + j, 0))
    col = lambda h: pl.BlockSpec((nch, h, CHUNK), lambda b, j: (b * nj + j, 0, 0))
    tiles = lambda w: pl.BlockSpec((nch, w // LANES, CHUNK, LANES), lambda b, j: (b * nj + j, 0, 0, 0))
    cast_in_specs, cast_out_specs, cast_shapes = [], [], []
    for w, block_perm in cast_weights:
        n_rows, n_cols = w.shape
        slab = n_rows // len(block_perm) if block_perm else _cast_slab_rows(n_rows, batch * nj)
        assert slab % BF16_SUBLANES == 0 and n_rows // slab <= batch * nj
        last = n_rows // slab - 1
        dst_map = lambda b, j, last=last: (jnp.minimum(b * nj + j, last), 0)
        if block_perm:
            assert tuple(block_perm) == tuple((s % 2) * (len(block_perm) // 2) + s // 2
                                              for s in range(len(block_perm)))
            stride = len(block_perm) // 2
            src_map = lambda b, j, last=last, stride=stride: (
                (jnp.minimum(b * nj + j, last) % 2) * stride + jnp.minimum(b * nj + j, last) // 2, 0)
        else:
            src_map = dst_map
        cast_in_specs.append(pl.BlockSpec((slab, n_cols), src_map))
        cast_out_specs.append(pl.BlockSpec((slab, n_cols), dst_map))
        cast_shapes.append(jax.ShapeDtypeStruct(w.shape, BF16))
    return pl.pallas_call(
        functools.partial(_mixer_kernel, nch=nch, n_cast=len(cast_weights)),
        grid=(batch, nj),
        in_specs=[pl.BlockSpec(memory_space=pltpu.SMEM),
                  tiles(QKV_WIDTH), tiles(MM_WIDTH), row(GATE_WIDTH),
                  col(MLSTM_V_WIDTH), col(MLSTM_V_WIDTH),
                  _const_spec(gbias.shape), _const_spec(normg.shape)] + cast_in_specs,
        out_specs=[col(ATTN_Q_WIDTH), col(MLSTM_V_WIDTH)] + cast_out_specs,
        out_shape=[jax.ShapeDtypeStruct((t // CHUNK, ATTN_Q_WIDTH, CHUNK), BF16),
                   jax.ShapeDtypeStruct((t // CHUNK, MLSTM_V_WIDTH, CHUNK), BF16)]
                  + cast_shapes,
        scratch_shapes=[
            pltpu.VMEM((2, ATTN_HEADS, CHUNK, CHUNK), F32),
            pltpu.VMEM((2, CHUNK, ATTN_KV_WIDTH), BF16),
            pltpu.VMEM((MLSTM_HEADS // 2, AUG_ROWS, 2 * MLSTM_QK_DIM), F32),
            pltpu.VMEM((8, LANES), F32),
            pltpu.VMEM((MLSTM_HEADS, MLSTM_V_DIM, LANES), F32),
        ],
        compiler_params=pltpu.CompilerParams(dimension_semantics=("arbitrary", "arbitrary"),
                                             vmem_limit_bytes=VMEM_LIMIT_BYTES),
        name="mixer",
    )(sinks, qkv, mm, gt, vt, ogt, gbias, normg, *[w for w, _ in cast_weights])


def _tail_kernel(x_ref, ao_ref, hm_ref, gg_ref, wab_ref, wmb_ref, wo_ref, g2_ref, wg_ref, wu_ref, wd_ref,
                 gf_ref, out_ref, *, ff_split):
    def features_major(ref):
        return jnp.concatenate([ref[k] for k in range(ref.shape[0])], axis=1)

    tn = (((0,), (0,)), ((), ()))
    ya = lax.dot_general(features_major(ao_ref), wab_ref[...], tn, preferred_element_type=F32)
    ym = lax.dot_general(features_major(hm_ref), wmb_ref[...], tn, preferred_element_type=F32)
    ga = jax.nn.sigmoid(gg_ref[:, 0:D_MODEL].astype(F32))
    gm = jax.nn.sigmoid(gg_ref[:, D_MODEL:2 * D_MODEL].astype(F32))
    z = (ga * ya + gm * ym).astype(BF16)
    x1 = x_ref[...] + _dot(z, wo_ref[...])
    f = (x1 * g2_ref[...]).astype(BF16)
    rs = _rms_scale(x1)
    n_tiles = wg_ref.shape[1] // MXU_WIDTH
    bounds = [MXU_WIDTH * ((n_tiles * s + ff_split - 1) // ff_split) for s in range(ff_split + 1)]
    x2 = x1
    for s in range(ff_split):
        cols = slice(bounds[s], bounds[s + 1])
        gte = _dot(f, wg_ref[:, cols]) * rs
        up = _dot(f, wu_ref[:, cols]) * rs
        hh = (gte * jax.nn.sigmoid(gte) * up).astype(BF16)
        x2 = x2 + _dot(hh, wd_ref[cols, :])
    out_ref[...] = x2 * _rms_scale(x2) * gf_ref[...]


def _tail(x2d, ao, hm, gg, wab, wmb, wo, g2, wg, wu, wd, gf, tm, ff_split):
    t = x2d.shape[0]
    row = lambda w: pl.BlockSpec((tm, w), lambda i: (i, 0))
    consts = (wab, wmb, wo, g2, wg, wu, wd, gf)
    return pl.pallas_call(
        functools.partial(_tail_kernel, ff_split=ff_split),
        grid=(t // tm,),
        in_specs=[row(D_MODEL), pl.BlockSpec((tm // CHUNK, ATTN_Q_WIDTH, CHUNK), lambda i: (i, 0, 0)),
                  pl.BlockSpec((tm // CHUNK, MLSTM_V_WIDTH, CHUNK), lambda i: (i, 0, 0)), row(MERGE_WIDTH)]
                 + [_const_spec(c.shape) for c in consts],
        out_specs=row(D_MODEL),
        out_shape=jax.ShapeDtypeStruct((t, D_MODEL), F32),
        compiler_params=pltpu.CompilerParams(dimension_semantics=("arbitrary",),
                                             vmem_limit_bytes=VMEM_LIMIT_BYTES),
        name="tail",
    )(x2d, ao, hm, gg, *consts)


def _layer(x2d, batch, seq, norm1_g, w_in, conv_w, conv_b, i_bias, f_bias, mlstm_norm_g, attn_sinks,
           w_attn_branch, w_mlstm_branch, w_out, norm2_g, w_ffn_gate, w_ffn_up, w_ffn_down, out_g):
    w_if = w_in[:, W_GATE_OFF:W_GATE_OFF + 2 * MLSTM_HEADS]
    wgt = jnp.pad(w_if, ((0, 0), (0, GATE_WIDTH - 2 * MLSTM_HEADS)))
    gbias = jnp.pad(jnp.concatenate([i_bias, f_bias]), (0, GATE_WIDTH - 2 * MLSTM_HEADS)).reshape(1, GATE_WIDTH)

    qkv, mm, gt, gg, vt_all, ogt_all = _inproj(x2d, norm1_g.reshape(1, D_MODEL), w_in.astype(BF16),
                                               wgt.astype(BF16), conv_w, conv_b.reshape(1, -1), tm=INPROJ_TM,
                                               seq=seq)
    tail_weights = [(w_attn_branch, Q_HEAD_ORDER), (w_mlstm_branch, None), (w_out, None),
                    (w_ffn_gate, None), (w_ffn_up, None), (w_ffn_down, None)]
    ao, hm, wab, wmb, wo, wg, wu, wd = _mixer(attn_sinks, qkv, mm, gt, vt_all, ogt_all, gbias,
                                              mlstm_norm_g.reshape(1, MLSTM_V_WIDTH), tail_weights, batch, seq,
                                              nch=MIXER_CHUNKS)
    return _tail(x2d, ao, hm, gg, wab, wmb, wo, norm2_g.reshape(1, D_MODEL), wg, wu, wd,
                 out_g.reshape(1, D_MODEL), tm=TAIL_TM, ff_split=TAIL_FF_SPLIT)


def kernel(x, norm1_g, w_in, conv_w, conv_b, i_bias, f_bias, mlstm_norm_g, attn_sinks, w_attn_branch,
           w_mlstm_branch, w_out, norm2_g, w_ffn_gate, w_ffn_up, w_ffn_down, final_norm_g):
    batch, seq, d = x.shape
    depth = norm1_g.shape[0]
    assert depth == 1 and d == D_MODEL
    assert seq % (MIXER_CHUNKS * CHUNK) == 0 and seq % INPROJ_TM == 0 and (batch * seq) % TAIL_TM == 0
    assert w_ffn_gate.shape[-1] % MXU_WIDTH == 0
    out = _layer(x.reshape(batch * seq, d), batch, seq, norm1_g[0], w_in[0], conv_w[0], conv_b[0], i_bias[0],
                 f_bias[0], mlstm_norm_g[0], attn_sinks[0], w_attn_branch[0], w_mlstm_branch[0], w_out[0],
                 norm2_g[0], w_ffn_gate[0], w_ffn_up[0], w_ffn_down[0], final_norm_g)
    return out.reshape(batch, seq, d)
```

```python
import functools

import jax
import jax.numpy as jnp
from jax import lax
from jax.experimental import pallas as pl
from jax.experimental.pallas import tpu as pltpu

D_MODEL = 1024
ATTN_HEADS = 8
ATTN_KV_HEADS = 2
ATTN_HEAD_DIM = 64
ATTN_GROUP = ATTN_HEADS // ATTN_KV_HEADS
WINDOW = 128
ATTN_Q_WIDTH = ATTN_HEADS * ATTN_HEAD_DIM
ATTN_KV_WIDTH = ATTN_KV_HEADS * ATTN_HEAD_DIM
MLSTM_HEADS = 4
MLSTM_QK_DIM = 64
MLSTM_V_DIM = 128
MLSTM_QK_WIDTH = MLSTM_HEADS * MLSTM_QK_DIM
MLSTM_V_WIDTH = MLSTM_HEADS * MLSTM_V_DIM
CHUNK = 128
CONV_WIDTH = 4
NORM_EPS = 1e-6

LANES = 128
BF16_SUBLANES = 16
MXU_WIDTH = 256
HALF = LANES // 2
CONV_PAD = 8
GATE_ROWS = 16
VO_ROWS = 2 * MLSTM_V_WIDTH
NQ_ROWS = BF16_SUBLANES
AUG_ROWS = MLSTM_V_DIM + NQ_ROWS

INPROJ_TM = 1024
INPROJ_ROW_BLOCKS = 2
MIXER_CHUNKS = 16
MIXER_STAGE_LAGS = (0, 1, 2, 3, 4, 4)
TAIL_TM = 512
TAIL_FF_SPLIT = 1
VMEM_LIMIT_BYTES = 56 * 1024 * 1024

QKV_WIDTH = ATTN_Q_WIDTH + 2 * ATTN_KV_WIDTH
MM_WIDTH = 2 * MLSTM_QK_WIDTH
MERGE_WIDTH = 2 * D_MODEL
W_MQK_OFF = QKV_WIDTH
W_MVO_OFF = W_MQK_OFF + 2 * MLSTM_QK_WIDTH
W_GATE_OFF = W_MVO_OFF + 2 * MLSTM_V_WIDTH
W_MERGE_OFF = W_GATE_OFF + 2 * MLSTM_HEADS

Q_HEAD_ORDER = tuple(h * ATTN_GROUP + g for g in range(ATTN_GROUP) for h in range(ATTN_KV_HEADS))

BF16 = jnp.bfloat16
F32 = jnp.float32
NEG_INF = float("-inf")


def _dot(a, b):
    return jnp.dot(a, b, preferred_element_type=F32)


def _dot_nt(a, b):
    return lax.dot_general(a, b, (((1,), (1,)), ((), ())), preferred_element_type=F32)


def _rms_scale(x):
    return lax.rsqrt(jnp.mean(x * x, axis=-1, keepdims=True) + NORM_EPS)


def _const_spec(shape):
    nd = len(shape)
    return pl.BlockSpec(shape, lambda *_: (0,) * nd, pipeline_mode=pl.Buffered(1))


def _inproj_kernel(x_ref, g_ref, w_ref, convw_ref, convb_ref,
                   qkv_ref, mm_ref, gt_ref, gg_ref, vt_ref, ogt_ref, conv_ref, wq_ref, wgg_ref, wvt_ref, *,
                   tiles_per_seq):
    tm = x_ref.shape[0]

    @pl.when(pl.program_id(0) == 0)
    def _():
        src = lax.broadcasted_iota(jnp.int32, (ATTN_Q_WIDTH, ATTN_Q_WIDTH), 0)
        dst = lax.broadcasted_iota(jnp.int32, (ATTN_Q_WIDTH, ATTN_Q_WIDTH), 1)
        head = ((dst % LANES) // ATTN_HEAD_DIM) * ATTN_GROUP + dst // LANES
        perm = (src == head * ATTN_HEAD_DIM + dst % ATTN_HEAD_DIM).astype(BF16)
        wq_ref[...] = _dot(w_ref[:, 0:ATTN_Q_WIDTH], perm).astype(BF16)
        wgg_ref[...] = w_ref[:, W_MERGE_OFF:W_MERGE_OFF + MERGE_WIDTH]
        eye = (lax.broadcasted_iota(jnp.int32, (MLSTM_V_WIDTH, MLSTM_V_WIDTH), 0)
               == lax.broadcasted_iota(jnp.int32, (MLSTM_V_WIDTH, MLSTM_V_WIDTH), 1)).astype(BF16)
        for half in range(2):
            for k0 in range(0, D_MODEL, MLSTM_V_WIDTH):
                c0 = W_MVO_OFF + half * MLSTM_V_WIDTH
                wvt_ref[half * MLSTM_V_WIDTH:(half + 1) * MLSTM_V_WIDTH, k0:k0 + MLSTM_V_WIDTH] = _dot_nt(
                    eye, w_ref[k0:k0 + MLSTM_V_WIDTH, c0:c0 + MLSTM_V_WIDTH]).astype(BF16)
        gate_t = _dot_nt(eye[0:LANES, 0:LANES], w_ref[:, W_GATE_OFF:W_GATE_OFF + LANES])[0:GATE_ROWS]
        is_gate = lax.broadcasted_iota(jnp.int32, (GATE_ROWS, D_MODEL), 0) < 2 * MLSTM_HEADS
        wvt_ref[VO_ROWS:VO_ROWS + GATE_ROWS, :] = jnp.where(is_gate, gate_t, 0.0).astype(BF16)

    @pl.when(pl.program_id(0) % tiles_per_seq == 0)
    def _():
        conv_ref[0:CONV_PAD, :] = jnp.zeros((CONV_PAD, 2 * MLSTM_QK_WIDTH), F32)

    q_scale = ATTN_HEAD_DIM ** -0.5
    lane = lax.broadcasted_iota(jnp.int32, (1, 2 * MLSTM_QK_WIDTH), 1)
    k_scale = jnp.where(lane < MLSTM_QK_WIDTH, 1.0, MLSTM_QK_DIM ** -0.5)
    rb = tm // INPROJ_ROW_BLOCKS
    for r in range(INPROJ_ROW_BLOCKS):
        rows = slice(r * rb, (r + 1) * rb)
        x = x_ref[rows, :]
        u = (x * _rms_scale(x) * g_ref[...]).astype(BF16)
        conv_ref[CONV_PAD + r * rb:CONV_PAD + (r + 1) * rb, :] = _dot(
            u, w_ref[:, W_MQK_OFF:W_MQK_OFF + 2 * MLSTM_QK_WIDTH])
        qkv_ref[rows, 0:ATTN_Q_WIDTH] = (_dot(u, wq_ref[...]) * q_scale).astype(BF16)
        qkv_ref[rows, ATTN_Q_WIDTH:QKV_WIDTH] = _dot(u, w_ref[:, ATTN_Q_WIDTH:QKV_WIDTH]).astype(BF16)
        vog_t = _dot_nt(wvt_ref[...], u)
        vo_t = vog_t[0:VO_ROWS].astype(BF16)
        for k in range(rb // CHUNK):
            ck = r * (rb // CHUNK) + k
            vt_ref[ck] = vo_t[0:MLSTM_V_WIDTH, k * CHUNK:(k + 1) * CHUNK]
            ogt_ref[ck] = vo_t[MLSTM_V_WIDTH:VO_ROWS, k * CHUNK:(k + 1) * CHUNK]
            gt_ref[ck] = vog_t[VO_ROWS:VO_ROWS + GATE_ROWS, k * CHUNK:(k + 1) * CHUNK]
        gg_ref[rows, :] = _dot(u, wgg_ref[...]).astype(BF16)
        xe = conv_ref[r * rb:r * rb + CONV_PAD + rb, :]
        acc = xe * convw_ref[0:1, :]
        for t in range(1, CONV_WIDTH):
            acc = pltpu.roll(acc, 1, axis=0) + xe * convw_ref[t:t + 1, :]
        acc = acc[CONV_PAD:CONV_PAD + rb, :] + convb_ref[...]
        mm_ref[rows, 0:2 * MLSTM_QK_WIDTH] = (acc * jax.nn.sigmoid(acc) * k_scale).astype(BF16)
    conv_ref[0:CONV_PAD, :] = conv_ref[tm:tm + CONV_PAD, :]


def _inproj(x2d, g1, w_all, convw, convb, tm, seq):
    t = x2d.shape[0]
    row = lambda w: pl.BlockSpec((tm, w), lambda i: (i, 0))
    consts = (g1, w_all, convw, convb)
    return pl.pallas_call(
        functools.partial(_inproj_kernel, tiles_per_seq=seq // tm),
        grid=(t // tm,),
        in_specs=[row(D_MODEL)] + [_const_spec(c.shape) for c in consts],
        out_specs=[row(QKV_WIDTH), row(MM_WIDTH), pl.BlockSpec((tm // CHUNK, GATE_ROWS, CHUNK), lambda i: (i, 0, 0)),
                   row(MERGE_WIDTH),
                   pl.BlockSpec((tm // CHUNK, MLSTM_V_WIDTH, CHUNK), lambda i: (i, 0, 0)),
                   pl.BlockSpec((tm // CHUNK, MLSTM_V_WIDTH, CHUNK), lambda i: (i, 0, 0))],
        out_shape=[jax.ShapeDtypeStruct((t, QKV_WIDTH), BF16), jax.ShapeDtypeStruct((t, MM_WIDTH), BF16),
                   jax.ShapeDtypeStruct((t // CHUNK, GATE_ROWS, CHUNK), F32), jax.ShapeDtypeStruct((t, MERGE_WIDTH), BF16),
                   jax.ShapeDtypeStruct((t // CHUNK, MLSTM_V_WIDTH, CHUNK), BF16),
                   jax.ShapeDtypeStruct((t // CHUNK, MLSTM_V_WIDTH, CHUNK), BF16)],
        scratch_shapes=[pltpu.VMEM((CONV_PAD + tm, 2 * MLSTM_QK_WIDTH), F32),
                        pltpu.VMEM((D_MODEL, ATTN_Q_WIDTH), BF16), pltpu.VMEM((D_MODEL, MERGE_WIDTH), BF16),
                        pltpu.VMEM((VO_ROWS + GATE_ROWS, D_MODEL), BF16)],
        compiler_params=pltpu.CompilerParams(dimension_semantics=("arbitrary",),
                                             vmem_limit_bytes=VMEM_LIMIT_BYTES),
        name="inproj",
    )(x2d, *consts)


def _log_sigmoid(x):
    return jnp.minimum(x, 0.0) - jnp.log1p(jnp.exp(-jnp.abs(x)))


def _split3(x):
    hi = x.astype(BF16)
    r1 = x - hi.astype(F32)
    mid = r1.astype(BF16)
    lo = (r1 - mid.astype(F32)).astype(BF16)
    return hi, mid, lo


def _mixer_kernel(sinks_ref, qkv_ref, mm_ref, gt_ref, vt_ref, ogt_ref, gbias_ref, normg_ref, *rest, nch, n_cast):
    cast_in, (ao_ref, hm_ref), rest = rest[:n_cast], rest[n_cast:n_cast + 2], rest[n_cast + 2:]
    cast_out, (btab_ref, kvp_ref, cst_ref, mst_ref, gcol_ref) = rest[:n_cast], rest[n_cast:]
    b_idx = pl.program_id(0)
    j_idx = pl.program_id(1)

    for src, dst in zip(cast_in, cast_out):
        dst[...] = src[...].astype(BF16)

    row_c = lax.broadcasted_iota(jnp.int32, (CHUNK, CHUNK), 0)
    col_c = lax.broadcasted_iota(jnp.int32, (CHUNK, CHUNK), 1)

    @pl.when((b_idx == 0) & (j_idx == 0))
    def _():
        cur = row_c <= col_c
        dist = jnp.where(cur, col_c - row_c, col_c - row_c + WINDOW).astype(F32)
        for j in range(ATTN_HEADS):
            slope = 2.0 ** (-8.0 * (j + 1) / ATTN_HEADS)
            btab_ref[0, j] = -slope * dist
            btab_ref[1, j] = jnp.where(cur, -slope * dist, NEG_INF)
        for h in range(MLSTM_HEADS):
            g_row = normg_ref[:, h * MLSTM_V_DIM:(h + 1) * MLSTM_V_DIM]
            gcol_ref[h] = jnp.broadcast_to(g_row, (MLSTM_V_DIM, LANES)).T

    @pl.when(j_idx == 0)
    def _():
        kvp_ref[...] = jnp.zeros_like(kvp_ref)
        cst_ref[...] = jnp.zeros_like(cst_ref)
        mst_ref[...] = jnp.zeros_like(mst_ref)

    cur_t = row_c <= col_c
    triu_b = cur_t.astype(BF16)
    eye_b = (row_c == col_c).astype(BF16)

    def transpose_bf16(x):
        return _dot_nt(eye_b, x)

    left_lanes = col_c < HALF
    ones_rows = jnp.ones((NQ_ROWS, LANES), BF16)
    left_row = lax.broadcasted_iota(jnp.int32, (1, LANES), 1) < HALF
    zero_b = jnp.zeros((), BF16)
    first = (j_idx == 0).astype(jnp.int32)

    def kv_sides(k_blk, v_blk):
        vt = transpose_bf16(v_blk).astype(BF16)
        k_sides = (jnp.where(left_lanes, k_blk, zero_b), jnp.where(left_lanes, zero_b, k_blk))
        vt_sides = tuple(jnp.concatenate([vt[s * HALF:(s + 1) * HALF], ones_rows], axis=0)
                         for s in range(ATTN_KV_HEADS))
        return k_sides, vt_sides

    k_off = ATTN_Q_WIDTH
    v_off = ATTN_Q_WIDTH + ATTN_KV_WIDTH
    prev_sides = kv_sides(kvp_ref[0], kvp_ref[1])
    last = slice((nch - 1) * CHUNK, nch * CHUNK)
    kvp_ref[0] = qkv_ref[last, k_off:k_off + ATTN_KV_WIDTH]
    kvp_ref[1] = qkv_ref[last, v_off:v_off + ATTN_KV_WIDTH]

    m_rows = [mst_ref[h:h + 1, :] for h in range(MLSTM_HEADS)]
    c_pairs = [cst_ref[pr] for pr in range(MLSTM_HEADS // 2)]

    npair = ATTN_GROUP
    rows_of = lambda c: slice(c * CHUNK, (c + 1) * CHUNK)
    row8 = lax.broadcasted_iota(jnp.int32, (GATE_ROWS, LANES), 0)
    head_rows = row8 < MLSTM_HEADS
    sides = {-1: prev_sides}
    gates, s_t, sc_ts, q_ms, p_t, mx_rows, intra_w, o_t, intra = {}, {}, {}, {}, {}, {}, {}, {}, {}

    def stage_prepare(c):
        rows = rows_of(c)
        sides[c] = kv_sides(qkv_ref[rows, k_off:k_off + ATTN_KV_WIDTH], qkv_ref[rows, v_off:v_off + ATTN_KV_WIDTH])
        g8 = gt_ref[c] + gbias_ref[...]
        f8 = pltpu.roll(g8, GATE_ROWS - MLSTM_HEADS, axis=0)
        lf_parts = _split3(jnp.where(head_rows, _log_sigmoid(f8), 0.0))
        b8 = sum(_dot(part, triu_b) for part in lf_parts)
        r8 = jnp.where(head_rows, g8 - b8, 0.0)
        r_t = jnp.concatenate([r8, jnp.zeros((CHUNK - GATE_ROWS, LANES), F32)], axis=0).T
        r_all = jnp.concatenate(
            [jnp.broadcast_to(r_t[:, h:h + 1], (CHUNK, LANES)) for h in range(MLSTM_HEADS)], axis=1)
        gates[c] = (g8, b8, r_all)

    def stage_scores(c):
        q = qkv_ref[rows_of(c), 0:ATTN_Q_WIDTH]
        q_pairs = [q[:, p * LANES:(p + 1) * LANES] for p in range(npair)]
        for side in range(ATTN_KV_HEADS):
            k_side = jnp.concatenate([sides[c - 1][0][side], sides[c][0][side]], axis=0)
            for pp in range(npair // 2):
                q2 = jnp.concatenate([q_pairs[2 * pp], q_pairs[2 * pp + 1]], axis=0)
                s2 = _dot_nt(k_side, q2)
                s_t[(c, side, 2 * pp)] = s2[:, 0:LANES]
                s_t[(c, side, 2 * pp + 1)] = s2[:, LANES:2 * LANES]
        for pr in range(MLSTM_HEADS // 2):
            qpair = mm_ref[rows_of(c), pr * LANES:(pr + 1) * LANES]
            k_pair = mm_ref[rows_of(c), MLSTM_QK_WIDTH + pr * LANES:MLSTM_QK_WIDTH + (pr + 1) * LANES]
            q_m = jnp.concatenate([jnp.where(left_lanes, qpair, zero_b), jnp.where(left_lanes, zero_b, qpair)],
                                  axis=0)
            q_ms[(c, pr)] = q_m
            sc_ts[(c, pr)] = _dot_nt(k_pair, q_m)

    def stage_weights(c):
        variant = first if c == 0 else 0
        for side in range(ATTN_KV_HEADS):
            for p in range(npair):
                j = side * ATTN_GROUP + p
                s2 = s_t.pop((c, side, p))
                comb = jnp.where(cur_t, s2[CHUNK:2 * CHUNK], s2[0:CHUNK]) + btab_ref[variant, j]
                mx = jnp.maximum(jnp.max(comb, axis=0, keepdims=True), sinks_ref[j])
                e = jnp.exp(comb - mx)
                p_t[(c, side, p)] = jnp.concatenate([jnp.where(cur_t, 0.0, e), jnp.where(cur_t, e, 0.0)],
                                                    axis=0).astype(BF16)
                mx_rows[(c, side, p)] = mx
        g8, b8, r_all = gates[c]
        for h in range(MLSTM_HEADS):
            b_row = b8[h:h + 1, :]
            d_t = jnp.where(cur_t, r_all[:, h * LANES:(h + 1) * LANES] + b_row, NEG_INF)
            mloc = jnp.max(d_t, axis=0, keepdims=True)
            wloc = jnp.exp(d_t - mloc)
            scw = (sc_ts[(c, h // 2)][:, (h % 2) * LANES:(h % 2 + 1) * LANES] * wloc).astype(BF16)
            b_last = b_row[:, CHUNK - 1:CHUNK]
            mloc_last = mloc[:, CHUNK - 1:CHUNK]
            u_row = jnp.exp((g8[h:h + 1, :] - b_row) + b_last - mloc_last)
            intra_w[(c, h)] = (b_row, mloc, scw, b_last, mloc_last, u_row)

    def stage_values(c):
        for side in range(ATTN_KV_HEADS):
            vt_side = jnp.concatenate([sides[c - 1][1][side], sides[c][1][side]], axis=1)
            for pp in range(npair // 2):
                p2 = jnp.concatenate([p_t.pop((c, side, 2 * pp)), p_t.pop((c, side, 2 * pp + 1))], axis=1)
                o2 = _dot(vt_side, p2)
                o_t[(c, side, 2 * pp)] = o2[:, 0:LANES]
                o_t[(c, side, 2 * pp + 1)] = o2[:, LANES:2 * LANES]
        for h in range(MLSTM_HEADS):
            b_row, mloc, scw, b_last, mloc_last, u_row = intra_w.pop((c, h))
            pr, half = h // 2, h % 2
            vt_aug = jnp.concatenate([vt_ref[c, h * MLSTM_V_DIM:(h + 1) * MLSTM_V_DIM, :], ones_rows], axis=0)
            k_pair = mm_ref[rows_of(c), MLSTM_QK_WIDTH + pr * LANES:MLSTM_QK_WIDTH + (pr + 1) * LANES]
            k_hm = jnp.where(left_lanes, k_pair, zero_b) if half == 0 else jnp.where(left_lanes, zero_b, k_pair)
            num_t = _dot(vt_aug, scw)
            upd_t = _dot((vt_aug.astype(F32) * u_row).astype(BF16), k_hm)
            intra[(c, h)] = (b_row, mloc, num_t, b_last, mloc_last, upd_t)

    def stage_attn_out(c):
        for p in range(npair):
            scaled = []
            for side in range(ATTN_KV_HEADS):
                j = side * ATTN_GROUP + p
                o = o_t.pop((c, side, p))
                den = o[HALF:HALF + 1, :] + jnp.exp(sinks_ref[j] - mx_rows.pop((c, side, p)))
                scaled.append(o[0:HALF] * (1.0 / den))
            ao_ref[c, p * LANES:(p + 1) * LANES, :] = jnp.concatenate(scaled, axis=0).astype(BF16)

    def stage_recurrence(c):
        rows = rows_of(c)
        for pr in range(MLSTM_HEADS // 2):
            c_pair = c_pairs[pr]
            inter_t = _dot_nt(c_pair.astype(BF16), q_ms.pop((c, pr)))
            decs, eus, upds = [], [], []
            for half in range(2):
                h = 2 * pr + half
                b_row, mloc, num_t, b_last, mloc_last, upd_t = intra.pop((c, h))
                m_prev = m_rows[h]
                it = inter_t[:, half * LANES:(half + 1) * LANES]
                a = b_row + m_prev
                m_t = jnp.maximum(a, mloc)
                e_t = jnp.exp(mloc - m_t)
                w_inter = jnp.exp(a - m_t)
                numv = e_t * num_t[0:MLSTM_V_DIM] + w_inter * it[0:MLSTM_V_DIM]
                nq = e_t * num_t[MLSTM_V_DIM:MLSTM_V_DIM + 1] + w_inter * it[MLSTM_V_DIM:MLSTM_V_DIM + 1]
                den = jnp.maximum(jnp.abs(nq), jnp.exp(-m_t))
                ssq = jnp.sum(numv * numv, axis=0, keepdims=True)
                scale = lax.rsqrt(ssq * (1.0 / MLSTM_V_DIM) + NORM_EPS * (den * den))
                og = ogt_ref[c, h * MLSTM_V_DIM:(h + 1) * MLSTM_V_DIM, :].astype(F32)
                gate = 0.5 * jnp.tanh(0.5 * og) + 0.5
                hm_ref[c, h * MLSTM_V_DIM:(h + 1) * MLSTM_V_DIM, :] = (numv * scale * gcol_ref[h] * gate).astype(BF16)
                m_new = jnp.maximum(b_last + m_prev, mloc_last)
                decs.append(jnp.exp(b_last + m_prev - m_new))
                eus.append(jnp.exp(mloc_last - m_new))
                upds.append(upd_t)
                m_rows[h] = m_new
            c_pairs[pr] = (jnp.where(left_row, decs[0], decs[1]) * c_pair
                           + eus[0] * upds[0] + eus[1] * upds[1])

    stages = (stage_prepare, stage_scores, stage_weights, stage_values, stage_attn_out, stage_recurrence)
    for tick in range(nch + max(MIXER_STAGE_LAGS)):
        for stage, lag in zip(stages, MIXER_STAGE_LAGS):
            c = tick - lag
            if 0 <= c < nch:
                stage(c)

    for h in range(MLSTM_HEADS):
        mst_ref[h:h + 1, :] = m_rows[h]
    for pr in range(MLSTM_HEADS // 2):
        cst_ref[pr] = c_pairs[pr]


def _cast_slab_rows(n_rows, n_steps):
    slab = BF16_SUBLANES * pl.cdiv(pl.cdiv(n_rows, n_steps), BF16_SUBLANES)
    while n_rows % slab:
        slab += BF16_SUBLANES
    return slab


def _mixer(sinks, qkv, mm, gt, vt, ogt, gbias, normg, cast_weights, batch, seq, nch):
    tm = nch * CHUNK
    nj = seq // tm
    t = batch * seq
    row = lambda w: pl.BlockSpec((tm, w), lambda b, j: (b * nj + j, 0))
    col = lambda h: pl.BlockSpec((nch, h, CHUNK), lambda b, j: (b * nj + j, 0, 0))
    cast_in_specs, cast_out_specs, cast_shapes = [], [], []
    for w, block_perm in cast_weights:
        n_rows, n_cols = w.shape
        slab = n_rows // len(block_perm) if block_perm else _cast_slab_rows(n_rows, batch * nj)
        assert slab % BF16_SUBLANES == 0 and n_rows // slab <= batch * nj
        last = n_rows // slab - 1
        dst_map = lambda b, j, last=last: (jnp.minimum(b * nj + j, last), 0)
        if block_perm:
            assert tuple(block_perm) == tuple((s % 2) * (len(block_perm) // 2) + s // 2
                                              for s in range(len(block_perm)))
            stride = len(block_perm) // 2
            src_map = lambda b, j, last=last, stride=stride: (
                (jnp.minimum(b * nj + j, last) % 2) * stride + jnp.minimum(b * nj + j, last) // 2, 0)
        else:
            src_map = dst_map
        cast_in_specs.append(pl.BlockSpec((slab, n_cols), src_map))
        cast_out_specs.append(pl.BlockSpec((slab, n_cols), dst_map))
        cast_shapes.append(jax.ShapeDtypeStruct(w.shape, BF16))
    return pl.pallas_call(
        functools.partial(_mixer_kernel, nch=nch, n_cast=len(cast_weights)),
        grid=(batch, nj),
        in_specs=[pl.BlockSpec(memory_space=pltpu.SMEM),
                  row(QKV_WIDTH), row(MM_WIDTH), col(GATE_ROWS),
                  col(MLSTM_V_WIDTH), col(MLSTM_V_WIDTH),
                  _const_spec(gbias.shape), _const_spec(normg.shape)] + cast_in_specs,
        out_specs=[col(ATTN_Q_WIDTH), col(MLSTM_V_WIDTH)] + cast_out_specs,
        out_shape=[jax.ShapeDtypeStruct((t // CHUNK, ATTN_Q_WIDTH, CHUNK), BF16),
                   jax.ShapeDtypeStruct((t // CHUNK, MLSTM_V_WIDTH, CHUNK), BF16)]
                  + cast_shapes,
        scratch_shapes=[
            pltpu.VMEM((2, ATTN_HEADS, CHUNK, CHUNK), F32),
            pltpu.VMEM((2, CHUNK, ATTN_KV_WIDTH), BF16),
            pltpu.VMEM((MLSTM_HEADS // 2, AUG_ROWS, 2 * MLSTM_QK_DIM), F32),
            pltpu.VMEM((8, LANES), F32),
            pltpu.VMEM((MLSTM_HEADS, MLSTM_V_DIM, LANES), F32),
        ],
        compiler_params=pltpu.CompilerParams(dimension_semantics=("arbitrary", "arbitrary"),
                                             vmem_limit_bytes=VMEM_LIMIT_BYTES),
        name="mixer",
    )(sinks, qkv, mm, gt, vt, ogt, gbias, normg, *[w for w, _ in cast_weights])


def _tail_kernel(x_ref, ao_ref, hm_ref, gg_ref, wab_ref, wmb_ref, wo_ref, g2_ref, wg_ref, wu_ref, wd_ref,
                 gf_ref, out_ref, *, ff_split):
    def features_major(ref):
        return jnp.concatenate([ref[k] for k in range(ref.shape[0])], axis=1)

    tn = (((0,), (0,)), ((), ()))
    ya = lax.dot_general(features_major(ao_ref), wab_ref[...], tn, preferred_element_type=F32)
    ym = lax.dot_general(features_major(hm_ref), wmb_ref[...], tn, preferred_element_type=F32)
    ga = jax.nn.sigmoid(gg_ref[:, 0:D_MODEL].astype(F32))
    gm = jax.nn.sigmoid(gg_ref[:, D_MODEL:2 * D_MODEL].astype(F32))
    z = (ga * ya + gm * ym).astype(BF16)
    x1 = x_ref[...] + _dot(z, wo_ref[...])
    f = (x1 * g2_ref[...]).astype(BF16)
    rs = _rms_scale(x1)
    n_tiles = wg_ref.shape[1] // MXU_WIDTH
    bounds = [MXU_WIDTH * ((n_tiles * s + ff_split - 1) // ff_split) for s in range(ff_split + 1)]
    x2 = x1
    for s in range(ff_split):
        cols = slice(bounds[s], bounds[s + 1])
        gte = _dot(f, wg_ref[:, cols]) * rs
        up = _dot(f, wu_ref[:, cols]) * rs
        hh = (gte * jax.nn.sigmoid(gte) * up).astype(BF16)
        x2 = x2 + _dot(hh, wd_ref[cols, :])
    out_ref[...] = x2 * _rms_scale(x2) * gf_ref[...]


def _tail(x2d, ao, hm, gg, wab, wmb, wo, g2, wg, wu, wd, gf, tm, ff_split):
    t = x2d.shape[0]
    row = lambda w: pl.BlockSpec((tm, w), lambda i: (i, 0))
    consts = (wab, wmb, wo, g2, wg, wu, wd, gf)
    return pl.pallas_call(
        functools.partial(_tail_kernel, ff_split=ff_split),
        grid=(t // tm,),
        in_specs=[row(D_MODEL), pl.BlockSpec((tm // CHUNK, ATTN_Q_WIDTH, CHUNK), lambda i: (i, 0, 0)),
                  pl.BlockSpec((tm // CHUNK, MLSTM_V_WIDTH, CHUNK), lambda i: (i, 0, 0)), row(MERGE_WIDTH)]
                 + [_const_spec(c.shape) for c in consts],
        out_specs=row(D_MODEL),
        out_shape=jax.ShapeDtypeStruct((t, D_MODEL), F32),
        compiler_params=pltpu.CompilerParams(dimension_semantics=("arbitrary",),
                                             vmem_limit_bytes=VMEM_LIMIT_BYTES),
        name="tail",
    )(x2d, ao, hm, gg, *consts)


def _layer(x2d, batch, seq, norm1_g, w_in, conv_w, conv_b, i_bias, f_bias, mlstm_norm_g, attn_sinks,
           w_attn_branch, w_mlstm_branch, w_out, norm2_g, w_ffn_gate, w_ffn_up, w_ffn_down, out_g):
    gbias = jnp.broadcast_to(jnp.pad(jnp.concatenate([i_bias, f_bias]), (0, GATE_ROWS - 2 * MLSTM_HEADS))[:, None],
                             (GATE_ROWS, LANES))

    qkv, mm, gt, gg, vt_all, ogt_all = _inproj(x2d, norm1_g.reshape(1, D_MODEL), w_in.astype(BF16), conv_w,
                                               conv_b.reshape(1, -1), tm=INPROJ_TM, seq=seq)
    tail_weights = [(w_attn_branch, Q_HEAD_ORDER), (w_mlstm_branch, None), (w_out, None),
                    (w_ffn_gate, None), (w_ffn_up, None), (w_ffn_down, None)]
    ao, hm, wab, wmb, wo, wg, wu, wd = _mixer(attn_sinks, qkv, mm, gt, vt_all, ogt_all, gbias,
                                              mlstm_norm_g.reshape(1, MLSTM_V_WIDTH), tail_weights, batch, seq,
                                              nch=MIXER_CHUNKS)
    return _tail(x2d, ao, hm, gg, wab, wmb, wo, norm2_g.reshape(1, D_MODEL), wg, wu, wd,
                 out_g.reshape(1, D_MODEL), tm=TAIL_TM, ff_split=TAIL_FF_SPLIT)


def kernel(x, norm1_g, w_in, conv_w, conv_b, i_bias, f_bias, mlstm_norm_g, attn_sinks, w_attn_branch,
           w_mlstm_branch, w_out, norm2_g, w_ffn_gate, w_ffn_up, w_ffn_down, final_norm_g):
    batch, seq, d = x.shape
    depth = norm1_g.shape[0]
    assert depth == 1 and d == D_MODEL
    assert seq % (MIXER_CHUNKS * CHUNK) == 0 and seq % INPROJ_TM == 0 and (batch * seq) % TAIL_TM == 0
    assert w_ffn_gate.shape[-1] % MXU_WIDTH == 0
    out = _layer(x.reshape(batch * seq, d), batch, seq, norm1_g[0], w_in[0], conv_w[0], conv_b[0], i_bias[0],
                 f_bias[0], mlstm_norm_g[0], attn_sinks[0], w_attn_branch[0], w_mlstm_branch[0], w_out[0],
                 norm2_g[0], w_ffn_gate[0], w_ffn_up[0], w_ffn_down[0], final_norm_g)
    return out.reshape(batch, seq, d)
```

```python
import functools

import jax
import jax.numpy as jnp
from jax import lax
from jax.experimental import pallas as pl
from jax.experimental.pallas import tpu as pltpu

D_MODEL = 1024
ATTN_HEADS = 8
ATTN_KV_HEADS = 2
ATTN_HEAD_DIM = 64
ATTN_GROUP = ATTN_HEADS // ATTN_KV_HEADS
WINDOW = 128
ATTN_Q_WIDTH = ATTN_HEADS * ATTN_HEAD_DIM
ATTN_KV_WIDTH = ATTN_KV_HEADS * ATTN_HEAD_DIM
MLSTM_HEADS = 4
MLSTM_QK_DIM = 64
MLSTM_V_DIM = 128
MLSTM_QK_WIDTH = MLSTM_HEADS * MLSTM_QK_DIM
MLSTM_V_WIDTH = MLSTM_HEADS * MLSTM_V_DIM
CHUNK = 128
CONV_WIDTH = 4
NORM_EPS = 1e-6

LANES = 128
BF16_SUBLANES = 16
MXU_WIDTH = 256
HALF = LANES // 2
CONV_PAD = 8
GATE_ROWS = 16
VO_ROWS = 2 * MLSTM_V_WIDTH
NQ_ROWS = BF16_SUBLANES
AUG_ROWS = MLSTM_V_DIM + NQ_ROWS

INPROJ_TM = 1024
W_SLAB_ROWS = 128
INPROJ_ROW_BLOCKS = 2
MIXER_CHUNKS = 16
MIXER_STAGE_LAGS = (0, 1, 2, 3, 4, 4)
TAIL_TM = 512
TAIL_FF_SPLIT = 1
VMEM_LIMIT_BYTES = 56 * 1024 * 1024

QKV_WIDTH = ATTN_Q_WIDTH + 2 * ATTN_KV_WIDTH
MM_WIDTH = 2 * MLSTM_QK_WIDTH
MERGE_WIDTH = 2 * D_MODEL
W_MQK_OFF = QKV_WIDTH
W_MVO_OFF = W_MQK_OFF + 2 * MLSTM_QK_WIDTH
W_GATE_OFF = W_MVO_OFF + 2 * MLSTM_V_WIDTH
W_MERGE_OFF = W_GATE_OFF + 2 * MLSTM_HEADS

Q_HEAD_ORDER = tuple(h * ATTN_GROUP + g for g in range(ATTN_GROUP) for h in range(ATTN_KV_HEADS))

BF16 = jnp.bfloat16
F32 = jnp.float32
NEG_INF = float("-inf")


def _dot(a, b):
    return jnp.dot(a, b, preferred_element_type=F32)


def _dot_nt(a, b):
    return lax.dot_general(a, b, (((1,), (1,)), ((), ())), preferred_element_type=F32)


def _rms_scale(x):
    return lax.rsqrt(jnp.mean(x * x, axis=-1, keepdims=True) + NORM_EPS)


def _const_spec(shape):
    nd = len(shape)
    return pl.BlockSpec(shape, lambda *_: (0,) * nd, pipeline_mode=pl.Buffered(1))


def _inproj_kernel(x_ref, g_ref, w_hbm, convw_ref, convb_ref,
                   qkv_ref, mm_ref, gt_ref, gg_ref, vt_ref, ogt_ref, conv_ref, wq_ref, wgg_ref, wvt_ref,
                   w_ref, stage_ref, stage_sem, *, tiles_per_seq):
    tm = x_ref.shape[0]

    @pl.when(pl.program_id(0) == 0)
    def _():
        n_slabs = D_MODEL // W_SLAB_ROWS

        def slab_copy(i):
            return pltpu.make_async_copy(w_hbm.at[pl.ds(i * W_SLAB_ROWS, W_SLAB_ROWS), :],
                                         stage_ref.at[i % 2], stage_sem.at[i % 2])

        slab_copy(0).start()
        for i in range(n_slabs):
            if i + 1 < n_slabs:
                slab_copy(i + 1).start()
            slab_copy(i).wait()
            w_ref[i * W_SLAB_ROWS:(i + 1) * W_SLAB_ROWS, :] = stage_ref[i % 2].astype(BF16)

        src = lax.broadcasted_iota(jnp.int32, (ATTN_Q_WIDTH, ATTN_Q_WIDTH), 0)
        dst = lax.broadcasted_iota(jnp.int32, (ATTN_Q_WIDTH, ATTN_Q_WIDTH), 1)
        head = ((dst % LANES) // ATTN_HEAD_DIM) * ATTN_GROUP + dst // LANES
        perm = (src == head * ATTN_HEAD_DIM + dst % ATTN_HEAD_DIM).astype(BF16)
        wq_ref[...] = _dot(w_ref[:, 0:ATTN_Q_WIDTH], perm).astype(BF16)
        wgg_ref[...] = w_ref[:, W_MERGE_OFF:W_MERGE_OFF + MERGE_WIDTH]
        eye = (lax.broadcasted_iota(jnp.int32, (MLSTM_V_WIDTH, MLSTM_V_WIDTH), 0)
               == lax.broadcasted_iota(jnp.int32, (MLSTM_V_WIDTH, MLSTM_V_WIDTH), 1)).astype(BF16)
        for half in range(2):
            for k0 in range(0, D_MODEL, MLSTM_V_WIDTH):
                c0 = W_MVO_OFF + half * MLSTM_V_WIDTH
                wvt_ref[half * MLSTM_V_WIDTH:(half + 1) * MLSTM_V_WIDTH, k0:k0 + MLSTM_V_WIDTH] = _dot_nt(
                    eye, w_ref[k0:k0 + MLSTM_V_WIDTH, c0:c0 + MLSTM_V_WIDTH]).astype(BF16)
        gate_t = _dot_nt(eye[0:LANES, 0:LANES], w_ref[:, W_GATE_OFF:W_GATE_OFF + LANES])[0:GATE_ROWS]
        is_gate = lax.broadcasted_iota(jnp.int32, (GATE_ROWS, D_MODEL), 0) < 2 * MLSTM_HEADS
        wvt_ref[VO_ROWS:VO_ROWS + GATE_ROWS, :] = jnp.where(is_gate, gate_t, 0.0).astype(BF16)

    @pl.when(pl.program_id(0) % tiles_per_seq == 0)
    def _():
        conv_ref[0:CONV_PAD, :] = jnp.zeros((CONV_PAD, 2 * MLSTM_QK_WIDTH), F32)

    q_scale = ATTN_HEAD_DIM ** -0.5
    lane = lax.broadcasted_iota(jnp.int32, (1, 2 * MLSTM_QK_WIDTH), 1)
    k_scale = jnp.where(lane < MLSTM_QK_WIDTH, 1.0, MLSTM_QK_DIM ** -0.5)
    rb = tm // INPROJ_ROW_BLOCKS
    for r in range(INPROJ_ROW_BLOCKS):
        rows = slice(r * rb, (r + 1) * rb)
        x = x_ref[rows, :]
        u = (x * _rms_scale(x) * g_ref[...]).astype(BF16)
        conv_ref[CONV_PAD + r * rb:CONV_PAD + (r + 1) * rb, :] = _dot(
            u, w_ref[:, W_MQK_OFF:W_MQK_OFF + 2 * MLSTM_QK_WIDTH])
        qkv_ref[rows, 0:ATTN_Q_WIDTH] = (_dot(u, wq_ref[...]) * q_scale).astype(BF16)
        qkv_ref[rows, ATTN_Q_WIDTH:QKV_WIDTH] = _dot(u, w_ref[:, ATTN_Q_WIDTH:QKV_WIDTH]).astype(BF16)
        vog_t = _dot_nt(wvt_ref[...], u)
        vo_t = vog_t[0:VO_ROWS].astype(BF16)
        for k in range(rb // CHUNK):
            ck = r * (rb // CHUNK) + k
            vt_ref[ck] = vo_t[0:MLSTM_V_WIDTH, k * CHUNK:(k + 1) * CHUNK]
            ogt_ref[ck] = vo_t[MLSTM_V_WIDTH:VO_ROWS, k * CHUNK:(k + 1) * CHUNK]
            gt_ref[ck] = vog_t[VO_ROWS:VO_ROWS + GATE_ROWS, k * CHUNK:(k + 1) * CHUNK]
        gg_ref[rows, :] = _dot(u, wgg_ref[...]).astype(BF16)
        xe = conv_ref[r * rb:r * rb + CONV_PAD + rb, :]
        acc = xe * convw_ref[0:1, :]
        for t in range(1, CONV_WIDTH):
            acc = pltpu.roll(acc, 1, axis=0) + xe * convw_ref[t:t + 1, :]
        acc = acc[CONV_PAD:CONV_PAD + rb, :] + convb_ref[...]
        mm_ref[rows, 0:2 * MLSTM_QK_WIDTH] = (acc * jax.nn.sigmoid(acc) * k_scale).astype(BF16)
    conv_ref[0:CONV_PAD, :] = conv_ref[tm:tm + CONV_PAD, :]


def _inproj(x2d, g1, w_all, convw, convb, tm, seq):
    t = x2d.shape[0]
    row = lambda w: pl.BlockSpec((tm, w), lambda i: (i, 0))
    consts = (g1, w_all, convw, convb)
    return pl.pallas_call(
        functools.partial(_inproj_kernel, tiles_per_seq=seq // tm),
        grid=(t // tm,),
        in_specs=[row(D_MODEL)] + [pl.BlockSpec(memory_space=pl.ANY) if c is w_all else _const_spec(c.shape)
                                   for c in consts],
        out_specs=[row(QKV_WIDTH), row(MM_WIDTH), pl.BlockSpec((tm // CHUNK, GATE_ROWS, CHUNK), lambda i: (i, 0, 0)),
                   row(MERGE_WIDTH),
                   pl.BlockSpec((tm // CHUNK, MLSTM_V_WIDTH, CHUNK), lambda i: (i, 0, 0)),
                   pl.BlockSpec((tm // CHUNK, MLSTM_V_WIDTH, CHUNK), lambda i: (i, 0, 0))],
        out_shape=[jax.ShapeDtypeStruct((t, QKV_WIDTH), BF16), jax.ShapeDtypeStruct((t, MM_WIDTH), BF16),
                   jax.ShapeDtypeStruct((t // CHUNK, GATE_ROWS, CHUNK), F32), jax.ShapeDtypeStruct((t, MERGE_WIDTH), BF16),
                   jax.ShapeDtypeStruct((t // CHUNK, MLSTM_V_WIDTH, CHUNK), BF16),
                   jax.ShapeDtypeStruct((t // CHUNK, MLSTM_V_WIDTH, CHUNK), BF16)],
        scratch_shapes=[pltpu.VMEM((CONV_PAD + tm, 2 * MLSTM_QK_WIDTH), F32),
                        pltpu.VMEM((D_MODEL, ATTN_Q_WIDTH), BF16), pltpu.VMEM((D_MODEL, MERGE_WIDTH), BF16),
                        pltpu.VMEM((VO_ROWS + GATE_ROWS, D_MODEL), BF16),
                        pltpu.VMEM(w_all.shape, BF16),
                        pltpu.VMEM((2, W_SLAB_ROWS, w_all.shape[1]), F32),
                        pltpu.SemaphoreType.DMA((2,))],
        compiler_params=pltpu.CompilerParams(dimension_semantics=("arbitrary",),
                                             vmem_limit_bytes=VMEM_LIMIT_BYTES),
        name="inproj",
    )(x2d, *consts)


def _log_sigmoid(x):
    return jnp.minimum(x, 0.0) - jnp.log1p(jnp.exp(-jnp.abs(x)))


def _split3(x):
    hi = x.astype(BF16)
    r1 = x - hi.astype(F32)
    mid = r1.astype(BF16)
    lo = (r1 - mid.astype(F32)).astype(BF16)
    return hi, mid, lo


def _mixer_kernel(sinks_ref, qkv_ref, mm_ref, gt_ref, vt_ref, ogt_ref, gbias_ref, normg_ref, *rest, nch, n_cast):
    cast_in, (ao_ref, hm_ref), rest = rest[:n_cast], rest[n_cast:n_cast + 2], rest[n_cast + 2:]
    cast_out, (btab_ref, kvp_ref, cst_ref, mst_ref, gcol_ref) = rest[:n_cast], rest[n_cast:]
    b_idx = pl.program_id(0)
    j_idx = pl.program_id(1)

    for src, dst in zip(cast_in, cast_out):
        dst[...] = src[...].astype(BF16)

    row_c = lax.broadcasted_iota(jnp.int32, (CHUNK, CHUNK), 0)
    col_c = lax.broadcasted_iota(jnp.int32, (CHUNK, CHUNK), 1)

    @pl.when((b_idx == 0) & (j_idx == 0))
    def _():
        cur = row_c <= col_c
        dist = jnp.where(cur, col_c - row_c, col_c - row_c + WINDOW).astype(F32)
        for j in range(ATTN_HEADS):
            slope = 2.0 ** (-8.0 * (j + 1) / ATTN_HEADS)
            btab_ref[0, j] = -slope * dist
            btab_ref[1, j] = jnp.where(cur, -slope * dist, NEG_INF)
        for h in range(MLSTM_HEADS):
            g_row = normg_ref[:, h * MLSTM_V_DIM:(h + 1) * MLSTM_V_DIM]
            gcol_ref[h] = jnp.broadcast_to(g_row, (MLSTM_V_DIM, LANES)).T

    @pl.when(j_idx == 0)
    def _():
        kvp_ref[...] = jnp.zeros_like(kvp_ref)
        cst_ref[...] = jnp.zeros_like(cst_ref)
        mst_ref[...] = jnp.zeros_like(mst_ref)

    cur_t = row_c <= col_c
    triu_b = cur_t.astype(BF16)
    eye_b = (row_c == col_c).astype(BF16)

    def transpose_bf16(x):
        return _dot_nt(eye_b, x)

    left_lanes = col_c < HALF
    ones_rows = jnp.ones((NQ_ROWS, LANES), BF16)
    left_row = lax.broadcasted_iota(jnp.int32, (1, LANES), 1) < HALF
    zero_b = jnp.zeros((), BF16)
    first = (j_idx == 0).astype(jnp.int32)

    def kv_sides(k_blk, v_blk):
        vt = transpose_bf16(v_blk).astype(BF16)
        k_sides = (jnp.where(left_lanes, k_blk, zero_b), jnp.where(left_lanes, zero_b, k_blk))
        vt_sides = tuple(jnp.concatenate([vt[s * HALF:(s + 1) * HALF], ones_rows], axis=0)
                         for s in range(ATTN_KV_HEADS))
        return k_sides, vt_sides

    k_off = ATTN_Q_WIDTH
    v_off = ATTN_Q_WIDTH + ATTN_KV_WIDTH
    prev_sides = kv_sides(kvp_ref[0], kvp_ref[1])
    last = slice((nch - 1) * CHUNK, nch * CHUNK)
    kvp_ref[0] = qkv_ref[last, k_off:k_off + ATTN_KV_WIDTH]
    kvp_ref[1] = qkv_ref[last, v_off:v_off + ATTN_KV_WIDTH]

    m_rows = [mst_ref[h:h + 1, :] for h in range(MLSTM_HEADS)]
    c_pairs = [cst_ref[pr] for pr in range(MLSTM_HEADS // 2)]

    npair = ATTN_GROUP
    rows_of = lambda c: slice(c * CHUNK, (c + 1) * CHUNK)
    row8 = lax.broadcasted_iota(jnp.int32, (GATE_ROWS, LANES), 0)
    head_rows = row8 < MLSTM_HEADS
    sides = {-1: prev_sides}
    gates, s_t, sc_ts, q_ms, p_t, mx_rows, intra_w, o_t, intra = {}, {}, {}, {}, {}, {}, {}, {}, {}

    def stage_prepare(c):
        rows = rows_of(c)
        sides[c] = kv_sides(qkv_ref[rows, k_off:k_off + ATTN_KV_WIDTH], qkv_ref[rows, v_off:v_off + ATTN_KV_WIDTH])
        g8 = gt_ref[c] + gbias_ref[...]
        f8 = pltpu.roll(g8, GATE_ROWS - MLSTM_HEADS, axis=0)
        lf_parts = _split3(jnp.where(head_rows, _log_sigmoid(f8), 0.0))
        b8 = sum(_dot(part, triu_b) for part in lf_parts)
        r8 = jnp.where(head_rows, g8 - b8, 0.0)
        r_t = jnp.concatenate([r8, jnp.zeros((CHUNK - GATE_ROWS, LANES), F32)], axis=0).T
        r_all = jnp.concatenate(
            [jnp.broadcast_to(r_t[:, h:h + 1], (CHUNK, LANES)) for h in range(MLSTM_HEADS)], axis=1)
        gates[c] = (g8, b8, r_all)

    def stage_scores(c):
        q = qkv_ref[rows_of(c), 0:ATTN_Q_WIDTH]
        q_pairs = [q[:, p * LANES:(p + 1) * LANES] for p in range(npair)]
        for side in range(ATTN_KV_HEADS):
            k_side = jnp.concatenate([sides[c - 1][0][side], sides[c][0][side]], axis=0)
            for pp in range(npair // 2):
                q2 = jnp.concatenate([q_pairs[2 * pp], q_pairs[2 * pp + 1]], axis=0)
                s2 = _dot_nt(k_side, q2)
                s_t[(c, side, 2 * pp)] = s2[:, 0:LANES]
                s_t[(c, side, 2 * pp + 1)] = s2[:, LANES:2 * LANES]
        for pr in range(MLSTM_HEADS // 2):
            qpair = mm_ref[rows_of(c), pr * LANES:(pr + 1) * LANES]
            k_pair = mm_ref[rows_of(c), MLSTM_QK_WIDTH + pr * LANES:MLSTM_QK_WIDTH + (pr + 1) * LANES]
            q_m = jnp.concatenate([jnp.where(left_lanes, qpair, zero_b), jnp.where(left_lanes, zero_b, qpair)],
                                  axis=0)
            q_ms[(c, pr)] = q_m
            sc_ts[(c, pr)] = _dot_nt(k_pair, q_m)

    def stage_weights(c):
        variant = first if c == 0 else 0
        for side in range(ATTN_KV_HEADS):
            for p in range(npair):
                j = side * ATTN_GROUP + p
                s2 = s_t.pop((c, side, p))
                comb = jnp.where(cur_t, s2[CHUNK:2 * CHUNK], s2[0:CHUNK]) + btab_ref[variant, j]
                mx = jnp.maximum(jnp.max(comb, axis=0, keepdims=True), sinks_ref[j])
                e = jnp.exp(comb - mx)
                p_t[(c, side, p)] = jnp.concatenate([jnp.where(cur_t, 0.0, e), jnp.where(cur_t, e, 0.0)],
                                                    axis=0).astype(BF16)
                mx_rows[(c, side, p)] = mx
        g8, b8, r_all = gates[c]
        for h in range(MLSTM_HEADS):
            b_row = b8[h:h + 1, :]
            d_t = jnp.where(cur_t, r_all[:, h * LANES:(h + 1) * LANES] + b_row, NEG_INF)
            mloc = jnp.max(d_t, axis=0, keepdims=True)
            wloc = jnp.exp(d_t - mloc)
            scw = (sc_ts[(c, h // 2)][:, (h % 2) * LANES:(h % 2 + 1) * LANES] * wloc).astype(BF16)
            b_last = b_row[:, CHUNK - 1:CHUNK]
            mloc_last = mloc[:, CHUNK - 1:CHUNK]
            u_row = jnp.exp((g8[h:h + 1, :] - b_row) + b_last - mloc_last)
            intra_w[(c, h)] = (b_row, mloc, scw, b_last, mloc_last, u_row)

    def stage_values(c):
        for side in range(ATTN_KV_HEADS):
            vt_side = jnp.concatenate([sides[c - 1][1][side], sides[c][1][side]], axis=1)
            for pp in range(npair // 2):
                p2 = jnp.concatenate([p_t.pop((c, side, 2 * pp)), p_t.pop((c, side, 2 * pp + 1))], axis=1)
                o2 = _dot(vt_side, p2)
                o_t[(c, side, 2 * pp)] = o2[:, 0:LANES]
                o_t[(c, side, 2 * pp + 1)] = o2[:, LANES:2 * LANES]
        for h in range(MLSTM_HEADS):
            b_row, mloc, scw, b_last, mloc_last, u_row = intra_w.pop((c, h))
            pr, half = h // 2, h % 2
            vt_aug = jnp.concatenate([vt_ref[c, h * MLSTM_V_DIM:(h + 1) * MLSTM_V_DIM, :], ones_rows], axis=0)
            k_pair = mm_ref[rows_of(c), MLSTM_QK_WIDTH + pr * LANES:MLSTM_QK_WIDTH + (pr + 1) * LANES]
            k_hm = jnp.where(left_lanes, k_pair, zero_b) if half == 0 else jnp.where(left_lanes, zero_b, k_pair)
            num_t = _dot(vt_aug, scw)
            upd_t = _dot((vt_aug.astype(F32) * u_row).astype(BF16), k_hm)
            intra[(c, h)] = (b_row, mloc, num_t, b_last, mloc_last, upd_t)

    def stage_attn_out(c):
        for p in range(npair):
            scaled = []
            for side in range(ATTN_KV_HEADS):
                j = side * ATTN_GROUP + p
                o = o_t.pop((c, side, p))
                den = o[HALF:HALF + 1, :] + jnp.exp(sinks_ref[j] - mx_rows.pop((c, side, p)))
                scaled.append(o[0:HALF] * (1.0 / den))
            ao_ref[c, p * LANES:(p + 1) * LANES, :] = jnp.concatenate(scaled, axis=0).astype(BF16)

    def stage_recurrence(c):
        rows = rows_of(c)
        for pr in range(MLSTM_HEADS // 2):
            c_pair = c_pairs[pr]
            inter_t = _dot_nt(c_pair.astype(BF16), q_ms.pop((c, pr)))
            decs, eus, upds = [], [], []
            for half in range(2):
                h = 2 * pr + half
                b_row, mloc, num_t, b_last, mloc_last, upd_t = intra.pop((c, h))
                m_prev = m_rows[h]
                it = inter_t[:, half * LANES:(half + 1) * LANES]
                a = b_row + m_prev
                m_t = jnp.maximum(a, mloc)
                e_t = jnp.exp(mloc - m_t)
                w_inter = jnp.exp(a - m_t)
                numv = e_t * num_t[0:MLSTM_V_DIM] + w_inter * it[0:MLSTM_V_DIM]
                nq = e_t * num_t[MLSTM_V_DIM:MLSTM_V_DIM + 1] + w_inter * it[MLSTM_V_DIM:MLSTM_V_DIM + 1]
                den = jnp.maximum(jnp.abs(nq), jnp.exp(-m_t))
                ssq = jnp.sum(numv * numv, axis=0, keepdims=True)
                scale = lax.rsqrt(ssq * (1.0 / MLSTM_V_DIM) + NORM_EPS * (den * den))
                og = ogt_ref[c, h * MLSTM_V_DIM:(h + 1) * MLSTM_V_DIM, :].astype(F32)
                gate = 0.5 * jnp.tanh(0.5 * og) + 0.5
                hm_ref[c, h * MLSTM_V_DIM:(h + 1) * MLSTM_V_DIM, :] = (numv * scale * gcol_ref[h] * gate).astype(BF16)
                m_new = jnp.maximum(b_last + m_prev, mloc_last)
                decs.append(jnp.exp(b_last + m_prev - m_new))
                eus.append(jnp.exp(mloc_last - m_new))
                upds.append(upd_t)
                m_rows[h] = m_new
            c_pairs[pr] = (jnp.where(left_row, decs[0], decs[1]) * c_pair
                           + eus[0] * upds[0] + eus[1] * upds[1])

    stages = (stage_prepare, stage_scores, stage_weights, stage_values, stage_attn_out, stage_recurrence)
    for tick in range(nch + max(MIXER_STAGE_LAGS)):
        for stage, lag in zip(stages, MIXER_STAGE_LAGS):
            c = tick - lag
            if 0 <= c < nch:
                stage(c)

    for h in range(MLSTM_HEADS):
        mst_ref[h:h + 1, :] = m_rows[h]
    for pr in range(MLSTM_HEADS // 2):
        cst_ref[pr] = c_pairs[pr]


def _cast_slab_rows(n_rows, n_steps):
    slab = BF16_SUBLANES * pl.cdiv(pl.cdiv(n_rows, n_steps), BF16_SUBLANES)
    while n_rows % slab:
        slab += BF16_SUBLANES
    return slab


def _mixer(sinks, qkv, mm, gt, vt, ogt, gbias, normg, cast_weights, batch, seq, nch):
    tm = nch * CHUNK
    nj = seq // tm
    t = batch * seq
    row = lambda w: pl.BlockSpec((tm, w), lambda b, j: (b * nj + j, 0))
    col = lambda h: pl.BlockSpec((nch, h, CHUNK), lambda b, j: (b * nj + j, 0, 0))
    cast_in_specs, cast_out_specs, cast_shapes = [], [], []
    for w, block_perm in cast_weights:
        n_rows, n_cols = w.shape
        slab = n_rows // len(block_perm) if block_perm else _cast_slab_rows(n_rows, batch * nj)
        assert slab % BF16_SUBLANES == 0 and n_rows // slab <= batch * nj
        last = n_rows // slab - 1
        dst_map = lambda b, j, last=last: (jnp.minimum(b * nj + j, last), 0)
        if block_perm:
            assert tuple(block_perm) == tuple((s % 2) * (len(block_perm) // 2) + s // 2
                                              for s in range(len(block_perm)))
            stride = len(block_perm) // 2
            src_map = lambda b, j, last=last, stride=stride: (
                (jnp.minimum(b * nj + j, last) % 2) * stride + jnp.minimum(b * nj + j, last) // 2, 0)
        else:
            src_map = dst_map
        cast_in_specs.append(pl.BlockSpec((slab, n_cols), src_map))
        cast_out_specs.append(pl.BlockSpec((slab, n_cols), dst_map))
        cast_shapes.append(jax.ShapeDtypeStruct(w.shape, BF16))
    return pl.pallas_call(
        functools.partial(_mixer_kernel, nch=nch, n_cast=len(cast_weights)),
        grid=(batch, nj),
        in_specs=[pl.BlockSpec(memory_space=pltpu.SMEM),
                  row(QKV_WIDTH), row(MM_WIDTH), col(GATE_ROWS),
                  col(MLSTM_V_WIDTH), col(MLSTM_V_WIDTH),
                  _const_spec(gbias.shape), _const_spec(normg.shape)] + cast_in_specs,
        out_specs=[col(ATTN_Q_WIDTH), col(MLSTM_V_WIDTH)] + cast_out_specs,
        out_shape=[jax.ShapeDtypeStruct((t // CHUNK, ATTN_Q_WIDTH, CHUNK), BF16),
                   jax.ShapeDtypeStruct((t // CHUNK, MLSTM_V_WIDTH, CHUNK), BF16)]
                  + cast_shapes,
        scratch_shapes=[
            pltpu.VMEM((2, ATTN_HEADS, CHUNK, CHUNK), F32),
            pltpu.VMEM((2, CHUNK, ATTN_KV_WIDTH), BF16),
            pltpu.VMEM((MLSTM_HEADS // 2, AUG_ROWS, 2 * MLSTM_QK_DIM), F32),
            pltpu.VMEM((8, LANES), F32),
            pltpu.VMEM((MLSTM_HEADS, MLSTM_V_DIM, LANES), F32),
        ],
        compiler_params=pltpu.CompilerParams(dimension_semantics=("arbitrary", "arbitrary"),
                                             vmem_limit_bytes=VMEM_LIMIT_BYTES),
        name="mixer",
    )(sinks, qkv, mm, gt, vt, ogt, gbias, normg, *[w for w, _ in cast_weights])


def _tail_kernel(x_ref, ao_ref, hm_ref, gg_ref, wab_ref, wmb_ref, wo_ref, g2_ref, wg_ref, wu_ref, wd_ref,
                 gf_ref, out_ref, *, ff_split):
    def features_major(ref):
        return jnp.concatenate([ref[k] for k in range(ref.shape[0])], axis=1)

    tn = (((0,), (0,)), ((), ()))
    ya = lax.dot_general(features_major(ao_ref), wab_ref[...], tn, preferred_element_type=F32)
    ym = lax.dot_general(features_major(hm_ref), wmb_ref[...], tn, preferred_element_type=F32)
    ga = jax.nn.sigmoid(gg_ref[:, 0:D_MODEL].astype(F32))
    gm = jax.nn.sigmoid(gg_ref[:, D_MODEL:2 * D_MODEL].astype(F32))
    z = (ga * ya + gm * ym).astype(BF16)
    x1 = x_ref[...] + _dot(z, wo_ref[...])
    f = (x1 * g2_ref[...]).astype(BF16)
    rs = _rms_scale(x1)
    n_tiles = wg_ref.shape[1] // MXU_WIDTH
    bounds = [MXU_WIDTH * ((n_tiles * s + ff_split - 1) // ff_split) for s in range(ff_split + 1)]
    x2 = x1
    for s in range(ff_split):
        cols = slice(bounds[s], bounds[s + 1])
        gte = _dot(f, wg_ref[:, cols]) * rs
        up = _dot(f, wu_ref[:, cols]) * rs
        hh = (gte * jax.nn.sigmoid(gte) * up).astype(BF16)
        x2 = x2 + _dot(hh, wd_ref[cols, :])
    out_ref[...] = x2 * _rms_scale(x2) * gf_ref[...]


def _tail(x2d, ao, hm, gg, wab, wmb, wo, g2, wg, wu, wd, gf, tm, ff_split):
    t = x2d.shape[0]
    row = lambda w: pl.BlockSpec((tm, w), lambda i: (i, 0))
    consts = (wab, wmb, wo, g2, wg, wu, wd, gf)
    return pl.pallas_call(
        functools.partial(_tail_kernel, ff_split=ff_split),
        grid=(t // tm,),
        in_specs=[row(D_MODEL), pl.BlockSpec((tm // CHUNK, ATTN_Q_WIDTH, CHUNK), lambda i: (i, 0, 0)),
                  pl.BlockSpec((tm // CHUNK, MLSTM_V_WIDTH, CHUNK), lambda i: (i, 0, 0)), row(MERGE_WIDTH)]
                 + [_const_spec(c.shape) for c in consts],
        out_specs=row(D_MODEL),
        out_shape=jax.ShapeDtypeStruct((t, D_MODEL), F32),
        compiler_params=pltpu.CompilerParams(dimension_semantics=("arbitrary",),
                                             vmem_limit_bytes=VMEM_LIMIT_BYTES),
        name="tail",
    )(x2d, ao, hm, gg, *consts)


def _layer(x2d, batch, seq, norm1_g, w_in, conv_w, conv_b, i_bias, f_bias, mlstm_norm_g, attn_sinks,
           w_attn_branch, w_mlstm_branch, w_out, norm2_g, w_ffn_gate, w_ffn_up, w_ffn_down, out_g):
    gbias = jnp.broadcast_to(jnp.pad(jnp.concatenate([i_bias, f_bias]), (0, GATE_ROWS - 2 * MLSTM_HEADS))[:, None],
                             (GATE_ROWS, LANES))

    qkv, mm, gt, gg, vt_all, ogt_all = _inproj(x2d, norm1_g.reshape(1, D_MODEL), w_in, conv_w,
                                               conv_b.reshape(1, -1), tm=INPROJ_TM, seq=seq)
    tail_weights = [(w_attn_branch, Q_HEAD_ORDER), (w_mlstm_branch, None), (w_out, None),
                    (w_ffn_gate, None), (w_ffn_up, None), (w_ffn_down, None)]
    ao, hm, wab, wmb, wo, wg, wu, wd = _mixer(attn_sinks, qkv, mm, gt, vt_all, ogt_all, gbias,
                                              mlstm_norm_g.reshape(1, MLSTM_V_WIDTH), tail_weights, batch, seq,
                                              nch=MIXER_CHUNKS)
    return _tail(x2d, ao, hm, gg, wab, wmb, wo, norm2_g.reshape(1, D_MODEL), wg, wu, wd,
                 out_g.reshape(1, D_MODEL), tm=TAIL_TM, ff_split=TAIL_FF_SPLIT)


def kernel(x, norm1_g, w_in, conv_w, conv_b, i_bias, f_bias, mlstm_norm_g, attn_sinks, w_attn_branch,
           w_mlstm_branch, w_out, norm2_g, w_ffn_gate, w_ffn_up, w_ffn_down, final_norm_g):
    batch, seq, d = x.shape
    depth = norm1_g.shape[0]
    assert depth == 1 and d == D_MODEL
    assert seq % (MIXER_CHUNKS * CHUNK) == 0 and seq % INPROJ_TM == 0 and (batch * seq) % TAIL_TM == 0
    assert w_ffn_gate.shape[-1] % MXU_WIDTH == 0
    out = _layer(x.reshape(batch * seq, d), batch, seq, norm1_g[0], w_in[0], conv_w[0], conv_b[0], i_bias[0],
                 f_bias[0], mlstm_norm_g[0], attn_sinks[0], w_attn_branch[0], w_mlstm_branch[0], w_out[0],
                 norm2_g[0], w_ffn_gate[0], w_ffn_up[0], w_ffn_down[0], final_norm_g)
    return out.reshape(batch, seq, d)
```

```python
import functools

import jax
import jax.numpy as jnp
from jax import lax
from jax.experimental import pallas as pl
from jax.experimental.pallas import tpu as pltpu

D_MODEL = 1024
ATTN_HEADS = 8
ATTN_KV_HEADS = 2
ATTN_HEAD_DIM = 64
ATTN_GROUP = ATTN_HEADS // ATTN_KV_HEADS
WINDOW = 128
ATTN_Q_WIDTH = ATTN_HEADS * ATTN_HEAD_DIM
ATTN_KV_WIDTH = ATTN_KV_HEADS * ATTN_HEAD_DIM
MLSTM_HEADS = 4
MLSTM_QK_DIM = 64
MLSTM_V_DIM = 128
MLSTM_QK_WIDTH = MLSTM_HEADS * MLSTM_QK_DIM
MLSTM_V_WIDTH = MLSTM_HEADS * MLSTM_V_DIM
CHUNK = 128
CONV_WIDTH = 4
NORM_EPS = 1e-6

LANES = 128
BF16_SUBLANES = 16
MXU_WIDTH = 256
HALF = LANES // 2
CONV_PAD = 8
GATE_ROWS = 16
VO_ROWS = 2 * MLSTM_V_WIDTH
NQ_ROWS = BF16_SUBLANES
AUG_ROWS = MLSTM_V_DIM + NQ_ROWS

INPROJ_TM = 1024
INPROJ_ROW_BLOCKS = 2
MIXER_CHUNKS = 16
MIXER_STAGE_LAGS = (0, 1, 2, 3, 4, 4)
TAIL_TM = 512
TAIL_FF_SPLIT = 1
VMEM_LIMIT_BYTES = 56 * 1024 * 1024

QKV_WIDTH = ATTN_Q_WIDTH + 2 * ATTN_KV_WIDTH
MM_WIDTH = 2 * MLSTM_QK_WIDTH
MERGE_WIDTH = 2 * D_MODEL
W_MQK_OFF = QKV_WIDTH
W_MVO_OFF = W_MQK_OFF + 2 * MLSTM_QK_WIDTH
W_GATE_OFF = W_MVO_OFF + 2 * MLSTM_V_WIDTH
W_MERGE_OFF = W_GATE_OFF + 2 * MLSTM_HEADS

Q_HEAD_ORDER = tuple(h * ATTN_GROUP + g for g in range(ATTN_GROUP) for h in range(ATTN_KV_HEADS))

BF16 = jnp.bfloat16
F32 = jnp.float32
NEG_INF = float("-inf")


def _dot(a, b):
    return jnp.dot(a, b, preferred_element_type=F32)


def _dot_nt(a, b):
    return lax.dot_general(a, b, (((1,), (1,)), ((), ())), preferred_element_type=F32)


def _rms_scale(x):
    return lax.rsqrt(jnp.mean(x * x, axis=-1, keepdims=True) + NORM_EPS)


def _const_spec(shape):
    nd = len(shape)
    return pl.BlockSpec(shape, lambda *_: (0,) * nd, pipeline_mode=pl.Buffered(1))


def _cast_slab_rows(n_rows, n_steps):
    slab = BF16_SUBLANES * pl.cdiv(pl.cdiv(n_rows, n_steps), BF16_SUBLANES)
    while n_rows % slab:
        slab += BF16_SUBLANES
    return slab


def _cast_specs(cast_weights, n_steps, step):
    in_specs, out_specs, shapes = [], [], []
    for w, block_perm in cast_weights:
        n_rows, n_cols = w.shape
        slab = n_rows // len(block_perm) if block_perm else _cast_slab_rows(n_rows, n_steps)
        assert slab % BF16_SUBLANES == 0 and n_rows // slab <= n_steps
        last = n_rows // slab - 1
        dst_map = lambda *idx, last=last: (jnp.minimum(step(*idx), last), 0)
        if block_perm:
            assert tuple(block_perm) == tuple((s % 2) * (len(block_perm) // 2) + s // 2
                                              for s in range(len(block_perm)))
            stride = len(block_perm) // 2
            src_map = lambda *idx, last=last, stride=stride: (
                (jnp.minimum(step(*idx), last) % 2) * stride + jnp.minimum(step(*idx), last) // 2, 0)
        else:
            src_map = dst_map
        in_specs.append(pl.BlockSpec((slab, n_cols), src_map))
        out_specs.append(pl.BlockSpec((slab, n_cols), dst_map))
        shapes.append(jax.ShapeDtypeStruct(w.shape, BF16))
    return in_specs, out_specs, shapes


def _inproj_kernel(x_ref, g_ref, w_ref, convw_ref, convb_ref, *rest, tiles_per_seq, n_cast):
    cast_in, rest = rest[:n_cast], rest[n_cast:]
    (qkv_ref, mm_ref, gt_ref, gg_ref, vt_ref, ogt_ref), rest = rest[:6], rest[6:]
    cast_out, (conv_ref, wq_ref, wgg_ref, wvt_ref) = rest[:n_cast], rest[n_cast:]
    tm = x_ref.shape[0]

    for src, dst in zip(cast_in, cast_out):
        dst[...] = src[...].astype(BF16)

    @pl.when(pl.program_id(0) == 0)
    def _():
        src = lax.broadcasted_iota(jnp.int32, (ATTN_Q_WIDTH, ATTN_Q_WIDTH), 0)
        dst = lax.broadcasted_iota(jnp.int32, (ATTN_Q_WIDTH, ATTN_Q_WIDTH), 1)
        head = ((dst % LANES) // ATTN_HEAD_DIM) * ATTN_GROUP + dst // LANES
        perm = (src == head * ATTN_HEAD_DIM + dst % ATTN_HEAD_DIM).astype(BF16)
        wq_ref[...] = _dot(w_ref[:, 0:ATTN_Q_WIDTH], perm).astype(BF16)
        wgg_ref[...] = w_ref[:, W_MERGE_OFF:W_MERGE_OFF + MERGE_WIDTH]
        eye = (lax.broadcasted_iota(jnp.int32, (MLSTM_V_WIDTH, MLSTM_V_WIDTH), 0)
               == lax.broadcasted_iota(jnp.int32, (MLSTM_V_WIDTH, MLSTM_V_WIDTH), 1)).astype(BF16)
        for half in range(2):
            for k0 in range(0, D_MODEL, MLSTM_V_WIDTH):
                c0 = W_MVO_OFF + half * MLSTM_V_WIDTH
                wvt_ref[half * MLSTM_V_WIDTH:(half + 1) * MLSTM_V_WIDTH, k0:k0 + MLSTM_V_WIDTH] = _dot_nt(
                    eye, w_ref[k0:k0 + MLSTM_V_WIDTH, c0:c0 + MLSTM_V_WIDTH]).astype(BF16)
        gate_t = _dot_nt(eye[0:LANES, 0:LANES], w_ref[:, W_GATE_OFF:W_GATE_OFF + LANES])[0:GATE_ROWS]
        is_gate = lax.broadcasted_iota(jnp.int32, (GATE_ROWS, D_MODEL), 0) < 2 * MLSTM_HEADS
        wvt_ref[VO_ROWS:VO_ROWS + GATE_ROWS, :] = jnp.where(is_gate, gate_t, 0.0).astype(BF16)

    @pl.when(pl.program_id(0) % tiles_per_seq == 0)
    def _():
        conv_ref[0:CONV_PAD, :] = jnp.zeros((CONV_PAD, 2 * MLSTM_QK_WIDTH), F32)

    q_scale = ATTN_HEAD_DIM ** -0.5
    lane = lax.broadcasted_iota(jnp.int32, (1, 2 * MLSTM_QK_WIDTH), 1)
    k_scale = jnp.where(lane < MLSTM_QK_WIDTH, 1.0, MLSTM_QK_DIM ** -0.5)
    rb = tm // INPROJ_ROW_BLOCKS
    for r in range(INPROJ_ROW_BLOCKS):
        rows = slice(r * rb, (r + 1) * rb)
        x = x_ref[rows, :]
        u = (x * _rms_scale(x) * g_ref[...]).astype(BF16)
        conv_ref[CONV_PAD + r * rb:CONV_PAD + (r + 1) * rb, :] = _dot(
            u, w_ref[:, W_MQK_OFF:W_MQK_OFF + 2 * MLSTM_QK_WIDTH])
        qkv_ref[rows, 0:ATTN_Q_WIDTH] = (_dot(u, wq_ref[...]) * q_scale).astype(BF16)
        qkv_ref[rows, ATTN_Q_WIDTH:QKV_WIDTH] = _dot(u, w_ref[:, ATTN_Q_WIDTH:QKV_WIDTH]).astype(BF16)
        vog_t = _dot_nt(wvt_ref[...], u)
        vo_t = vog_t[0:VO_ROWS].astype(BF16)
        for k in range(rb // CHUNK):
            ck = r * (rb // CHUNK) + k
            vt_ref[ck] = vo_t[0:MLSTM_V_WIDTH, k * CHUNK:(k + 1) * CHUNK]
            ogt_ref[ck] = vo_t[MLSTM_V_WIDTH:VO_ROWS, k * CHUNK:(k + 1) * CHUNK]
            gt_ref[ck] = vog_t[VO_ROWS:VO_ROWS + GATE_ROWS, k * CHUNK:(k + 1) * CHUNK]
        gg_ref[rows, :] = _dot(u, wgg_ref[...]).astype(BF16)
        xe = conv_ref[r * rb:r * rb + CONV_PAD + rb, :]
        acc = xe * convw_ref[0:1, :]
        for t in range(1, CONV_WIDTH):
            acc = pltpu.roll(acc, 1, axis=0) + xe * convw_ref[t:t + 1, :]
        acc = acc[CONV_PAD:CONV_PAD + rb, :] + convb_ref[...]
        mm_ref[rows, 0:2 * MLSTM_QK_WIDTH] = (acc * jax.nn.sigmoid(acc) * k_scale).astype(BF16)
    conv_ref[0:CONV_PAD, :] = conv_ref[tm:tm + CONV_PAD, :]


def _inproj(x2d, g1, w_all, convw, convb, cast_weights, tm, seq):
    t = x2d.shape[0]
    row = lambda w: pl.BlockSpec((tm, w), lambda i: (i, 0))
    consts = (g1, w_all, convw, convb)
    cast_in_specs, cast_out_specs, cast_shapes = _cast_specs(cast_weights, t // tm, lambda i: i)
    return pl.pallas_call(
        functools.partial(_inproj_kernel, tiles_per_seq=seq // tm, n_cast=len(cast_weights)),
        grid=(t // tm,),
        in_specs=[row(D_MODEL)] + [_const_spec(c.shape) for c in consts] + cast_in_specs,
        out_specs=[row(QKV_WIDTH), row(MM_WIDTH), pl.BlockSpec((tm // CHUNK, GATE_ROWS, CHUNK), lambda i: (i, 0, 0)),
                   row(MERGE_WIDTH),
                   pl.BlockSpec((tm // CHUNK, MLSTM_V_WIDTH, CHUNK), lambda i: (i, 0, 0)),
                   pl.BlockSpec((tm // CHUNK, MLSTM_V_WIDTH, CHUNK), lambda i: (i, 0, 0))] + cast_out_specs,
        out_shape=[jax.ShapeDtypeStruct((t, QKV_WIDTH), BF16), jax.ShapeDtypeStruct((t, MM_WIDTH), BF16),
                   jax.ShapeDtypeStruct((t // CHUNK, GATE_ROWS, CHUNK), F32), jax.ShapeDtypeStruct((t, MERGE_WIDTH), BF16),
                   jax.ShapeDtypeStruct((t // CHUNK, MLSTM_V_WIDTH, CHUNK), BF16),
                   jax.ShapeDtypeStruct((t // CHUNK, MLSTM_V_WIDTH, CHUNK), BF16)] + cast_shapes,
        scratch_shapes=[pltpu.VMEM((CONV_PAD + tm, 2 * MLSTM_QK_WIDTH), F32),
                        pltpu.VMEM((D_MODEL, ATTN_Q_WIDTH), BF16), pltpu.VMEM((D_MODEL, MERGE_WIDTH), BF16),
                        pltpu.VMEM((VO_ROWS + GATE_ROWS, D_MODEL), BF16)],
        compiler_params=pltpu.CompilerParams(dimension_semantics=("arbitrary",),
                                             vmem_limit_bytes=VMEM_LIMIT_BYTES),
        name="inproj",
    )(x2d, *consts, *[w for w, _ in cast_weights])


def _log_sigmoid(x):
    return jnp.minimum(x, 0.0) - jnp.log1p(jnp.exp(-jnp.abs(x)))


def _split3(x):
    hi = x.astype(BF16)
    r1 = x - hi.astype(F32)
    mid = r1.astype(BF16)
    lo = (r1 - mid.astype(F32)).astype(BF16)
    return hi, mid, lo


def _mixer_kernel(sinks_ref, qkv_ref, mm_ref, gt_ref, vt_ref, ogt_ref, gbias_ref, normg_ref, *rest, nch, n_cast):
    cast_in, (ao_ref, hm_ref), rest = rest[:n_cast], rest[n_cast:n_cast + 2], rest[n_cast + 2:]
    cast_out, (btab_ref, kvp_ref, cst_ref, mst_ref, gcol_ref) = rest[:n_cast], rest[n_cast:]
    b_idx = pl.program_id(0)
    j_idx = pl.program_id(1)

    for src, dst in zip(cast_in, cast_out):
        dst[...] = src[...].astype(BF16)

    row_c = lax.broadcasted_iota(jnp.int32, (CHUNK, CHUNK), 0)
    col_c = lax.broadcasted_iota(jnp.int32, (CHUNK, CHUNK), 1)

    @pl.when((b_idx == 0) & (j_idx == 0))
    def _():
        cur = row_c <= col_c
        dist = jnp.where(cur, col_c - row_c, col_c - row_c + WINDOW).astype(F32)
        for j in range(ATTN_HEADS):
            slope = 2.0 ** (-8.0 * (j + 1) / ATTN_HEADS)
            btab_ref[0, j] = -slope * dist
            btab_ref[1, j] = jnp.where(cur, -slope * dist, NEG_INF)
        for h in range(MLSTM_HEADS):
            g_row = normg_ref[:, h * MLSTM_V_DIM:(h + 1) * MLSTM_V_DIM]
            gcol_ref[h] = jnp.broadcast_to(g_row, (MLSTM_V_DIM, LANES)).T

    @pl.when(j_idx == 0)
    def _():
        kvp_ref[...] = jnp.zeros_like(kvp_ref)
        cst_ref[...] = jnp.zeros_like(cst_ref)
        mst_ref[...] = jnp.zeros_like(mst_ref)

    cur_t = row_c <= col_c
    triu_b = cur_t.astype(BF16)
    eye_b = (row_c == col_c).astype(BF16)

    def transpose_bf16(x):
        return _dot_nt(eye_b, x)

    left_lanes = col_c < HALF
    ones_rows = jnp.ones((NQ_ROWS, LANES), BF16)
    left_row = lax.broadcasted_iota(jnp.int32, (1, LANES), 1) < HALF
    zero_b = jnp.zeros((), BF16)
    first = (j_idx == 0).astype(jnp.int32)

    def kv_sides(k_blk, v_blk):
        vt = transpose_bf16(v_blk).astype(BF16)
        k_sides = (jnp.where(left_lanes, k_blk, zero_b), jnp.where(left_lanes, zero_b, k_blk))
        vt_sides = tuple(jnp.concatenate([vt[s * HALF:(s + 1) * HALF], ones_rows], axis=0)
                         for s in range(ATTN_KV_HEADS))
        return k_sides, vt_sides

    k_off = ATTN_Q_WIDTH
    v_off = ATTN_Q_WIDTH + ATTN_KV_WIDTH
    prev_sides = kv_sides(kvp_ref[0], kvp_ref[1])
    last = slice((nch - 1) * CHUNK, nch * CHUNK)
    kvp_ref[0] = qkv_ref[last, k_off:k_off + ATTN_KV_WIDTH]
    kvp_ref[1] = qkv_ref[last, v_off:v_off + ATTN_KV_WIDTH]

    m_rows = [mst_ref[h:h + 1, :] for h in range(MLSTM_HEADS)]
    c_pairs = [cst_ref[pr] for pr in range(MLSTM_HEADS // 2)]

    npair = ATTN_GROUP
    rows_of = lambda c: slice(c * CHUNK, (c + 1) * CHUNK)
    row8 = lax.broadcasted_iota(jnp.int32, (GATE_ROWS, LANES), 0)
    head_rows = row8 < MLSTM_HEADS
    sides = {-1: prev_sides}
    gates, s_t, sc_ts, q_ms, p_t, mx_rows, intra_w, o_t, intra = {}, {}, {}, {}, {}, {}, {}, {}, {}

    def stage_prepare(c):
        rows = rows_of(c)
        sides[c] = kv_sides(qkv_ref[rows, k_off:k_off + ATTN_KV_WIDTH], qkv_ref[rows, v_off:v_off + ATTN_KV_WIDTH])
        g8 = gt_ref[c] + gbias_ref[...]
        f8 = pltpu.roll(g8, GATE_ROWS - MLSTM_HEADS, axis=0)
        lf_parts = _split3(jnp.where(head_rows, _log_sigmoid(f8), 0.0))
        b8 = sum(_dot(part, triu_b) for part in lf_parts)
        r8 = jnp.where(head_rows, g8 - b8, 0.0)
        r_t = jnp.concatenate([r8, jnp.zeros((CHUNK - GATE_ROWS, LANES), F32)], axis=0).T
        r_all = jnp.concatenate(
            [jnp.broadcast_to(r_t[:, h:h + 1], (CHUNK, LANES)) for h in range(MLSTM_HEADS)], axis=1)
        gates[c] = (g8, b8, r_all)

    def stage_scores(c):
        q = qkv_ref[rows_of(c), 0:ATTN_Q_WIDTH]
        q_pairs = [q[:, p * LANES:(p + 1) * LANES] for p in range(npair)]
        for side in range(ATTN_KV_HEADS):
            k_side = jnp.concatenate([sides[c - 1][0][side], sides[c][0][side]], axis=0)
            for pp in range(npair // 2):
                q2 = jnp.concatenate([q_pairs[2 * pp], q_pairs[2 * pp + 1]], axis=0)
                s2 = _dot_nt(k_side, q2)
                s_t[(c, side, 2 * pp)] = s2[:, 0:LANES]
                s_t[(c, side, 2 * pp + 1)] = s2[:, LANES:2 * LANES]
        for pr in range(MLSTM_HEADS // 2):
            qpair = mm_ref[rows_of(c), pr * LANES:(pr + 1) * LANES]
            k_pair = mm_ref[rows_of(c), MLSTM_QK_WIDTH + pr * LANES:MLSTM_QK_WIDTH + (pr + 1) * LANES]
            q_m = jnp.concatenate([jnp.where(left_lanes, qpair, zero_b), jnp.where(left_lanes, zero_b, qpair)],
                                  axis=0)
            q_ms[(c, pr)] = q_m
            sc_ts[(c, pr)] = _dot_nt(k_pair, q_m)

    def stage_weights(c):
        variant = first if c == 0 else 0
        for side in range(ATTN_KV_HEADS):
            for p in range(npair):
                j = side * ATTN_GROUP + p
                s2 = s_t.pop((c, side, p))
                comb = jnp.where(cur_t, s2[CHUNK:2 * CHUNK], s2[0:CHUNK]) + btab_ref[variant, j]
                mx = jnp.maximum(jnp.max(comb, axis=0, keepdims=True), sinks_ref[j])
                e = jnp.exp(comb - mx)
                p_t[(c, side, p)] = jnp.concatenate([jnp.where(cur_t, 0.0, e), jnp.where(cur_t, e, 0.0)],
                                                    axis=0).astype(BF16)
                mx_rows[(c, side, p)] = mx
        g8, b8, r_all = gates[c]
        for h in range(MLSTM_HEADS):
            b_row = b8[h:h + 1, :]
            d_t = jnp.where(cur_t, r_all[:, h * LANES:(h + 1) * LANES] + b_row, NEG_INF)
            mloc = jnp.max(d_t, axis=0, keepdims=True)
            wloc = jnp.exp(d_t - mloc)
            scw = (sc_ts[(c, h // 2)][:, (h % 2) * LANES:(h % 2 + 1) * LANES] * wloc).astype(BF16)
            b_last = b_row[:, CHUNK - 1:CHUNK]
            mloc_last = mloc[:, CHUNK - 1:CHUNK]
            u_row = jnp.exp((g8[h:h + 1, :] - b_row) + b_last - mloc_last)
            intra_w[(c, h)] = (b_row, mloc, scw, b_last, mloc_last, u_row)

    def stage_values(c):
        for side in range(ATTN_KV_HEADS):
            vt_side = jnp.concatenate([sides[c - 1][1][side], sides[c][1][side]], axis=1)
            for pp in range(npair // 2):
                p2 = jnp.concatenate([p_t.pop((c, side, 2 * pp)), p_t.pop((c, side, 2 * pp + 1))], axis=1)
                o2 = _dot(vt_side, p2)
                o_t[(c, side, 2 * pp)] = o2[:, 0:LANES]
                o_t[(c, side, 2 * pp + 1)] = o2[:, LANES:2 * LANES]
        for h in range(MLSTM_HEADS):
            b_row, mloc, scw, b_last, mloc_last, u_row = intra_w.pop((c, h))
            pr, half = h // 2, h % 2
            vt_aug = jnp.concatenate([vt_ref[c, h * MLSTM_V_DIM:(h + 1) * MLSTM_V_DIM, :], ones_rows], axis=0)
            k_pair = mm_ref[rows_of(c), MLSTM_QK_WIDTH + pr * LANES:MLSTM_QK_WIDTH + (pr + 1) * LANES]
            k_hm = jnp.where(left_lanes, k_pair, zero_b) if half == 0 else jnp.where(left_lanes, zero_b, k_pair)
            num_t = _dot(vt_aug, scw)
            upd_t = _dot((vt_aug.astype(F32) * u_row).astype(BF16), k_hm)
            intra[(c, h)] = (b_row, mloc, num_t, b_last, mloc_last, upd_t)

    def stage_attn_out(c):
        for p in range(npair):
            scaled = []
            for side in range(ATTN_KV_HEADS):
                j = side * ATTN_GROUP + p
                o = o_t.pop((c, side, p))
                den = o[HALF:HALF + 1, :] + jnp.exp(sinks_ref[j] - mx_rows.pop((c, side, p)))
                scaled.append(o[0:HALF] * (1.0 / den))
            ao_ref[c, p * LANES:(p + 1) * LANES, :] = jnp.concatenate(scaled, axis=0).astype(BF16)

    def stage_recurrence(c):
        rows = rows_of(c)
        for pr in range(MLSTM_HEADS // 2):
            c_pair = c_pairs[pr]
            inter_t = _dot_nt(c_pair.astype(BF16), q_ms.pop((c, pr)))
            decs, eus, upds = [], [], []
            for half in range(2):
                h = 2 * pr + half
                b_row, mloc, num_t, b_last, mloc_last, upd_t = intra.pop((c, h))
                m_prev = m_rows[h]
                it = inter_t[:, half * LANES:(half + 1) * LANES]
                a = b_row + m_prev
                m_t = jnp.maximum(a, mloc)
                e_t = jnp.exp(mloc - m_t)
                w_inter = jnp.exp(a - m_t)
                numv = e_t * num_t[0:MLSTM_V_DIM] + w_inter * it[0:MLSTM_V_DIM]
                nq = e_t * num_t[MLSTM_V_DIM:MLSTM_V_DIM + 1] + w_inter * it[MLSTM_V_DIM:MLSTM_V_DIM + 1]
                den = jnp.maximum(jnp.abs(nq), jnp.exp(-m_t))
                ssq = jnp.sum(numv * numv, axis=0, keepdims=True)
                scale = lax.rsqrt(ssq * (1.0 / MLSTM_V_DIM) + NORM_EPS * (den * den))
                og = ogt_ref[c, h * MLSTM_V_DIM:(h + 1) * MLSTM_V_DIM, :].astype(F32)
                gate = 0.5 * jnp.tanh(0.5 * og) + 0.5
                hm_ref[c, h * MLSTM_V_DIM:(h + 1) * MLSTM_V_DIM, :] = (numv * scale * gcol_ref[h] * gate).astype(BF16)
                m_new = jnp.maximum(b_last + m_prev, mloc_last)
                decs.append(jnp.exp(b_last + m_prev - m_new))
                eus.append(jnp.exp(mloc_last - m_new))
                upds.append(upd_t)
                m_rows[h] = m_new
            c_pairs[pr] = (jnp.where(left_row, decs[0], decs[1]) * c_pair
                           + eus[0] * upds[0] + eus[1] * upds[1])

    stages = (stage_prepare, stage_scores, stage_weights, stage_values, stage_attn_out, stage_recurrence)
    for tick in range(nch + max(MIXER_STAGE_LAGS)):
        for stage, lag in zip(stages, MIXER_STAGE_LAGS):
            c = tick - lag
            if 0 <= c < nch:
                stage(c)

    for h in range(MLSTM_HEADS):
        mst_ref[h:h + 1, :] = m_rows[h]
    for pr in range(MLSTM_HEADS // 2):
        cst_ref[pr] = c_pairs[pr]


def _mixer(sinks, qkv, mm, gt, vt, ogt, gbias, normg, cast_weights, batch, seq, nch):
    tm = nch * CHUNK
    nj = seq // tm
    t = batch * seq
    row = lambda w: pl.BlockSpec((tm, w), lambda b, j: (b * nj + j, 0))
    col = lambda h: pl.BlockSpec((nch, h, CHUNK), lambda b, j: (b * nj + j, 0, 0))
    cast_in_specs, cast_out_specs, cast_shapes = _cast_specs(cast_weights, batch * nj, lambda b, j: b * nj + j)
    return pl.pallas_call(
        functools.partial(_mixer_kernel, nch=nch, n_cast=len(cast_weights)),
        grid=(batch, nj),
        in_specs=[pl.BlockSpec(memory_space=pltpu.SMEM),
                  row(QKV_WIDTH), row(MM_WIDTH), col(GATE_ROWS),
                  col(MLSTM_V_WIDTH), col(MLSTM_V_WIDTH),
                  _const_spec(gbias.shape), _const_spec(normg.shape)] + cast_in_specs,
        out_specs=[col(ATTN_Q_WIDTH), col(MLSTM_V_WIDTH)] + cast_out_specs,
        out_shape=[jax.ShapeDtypeStruct((t // CHUNK, ATTN_Q_WIDTH, CHUNK), BF16),
                   jax.ShapeDtypeStruct((t // CHUNK, MLSTM_V_WIDTH, CHUNK), BF16)]
                  + cast_shapes,
        scratch_shapes=[
            pltpu.VMEM((2, ATTN_HEADS, CHUNK, CHUNK), F32),
            pltpu.VMEM((2, CHUNK, ATTN_KV_WIDTH), BF16),
            pltpu.VMEM((MLSTM_HEADS // 2, AUG_ROWS, 2 * MLSTM_QK_DIM), F32),
            pltpu.VMEM((8, LANES), F32),
            pltpu.VMEM((MLSTM_HEADS, MLSTM_V_DIM, LANES), F32),
        ],
        compiler_params=pltpu.CompilerParams(dimension_semantics=("arbitrary", "arbitrary"),
                                             vmem_limit_bytes=VMEM_LIMIT_BYTES),
        name="mixer",
    )(sinks, qkv, mm, gt, vt, ogt, gbias, normg, *[w for w, _ in cast_weights])


def _tail_kernel(x_ref, ao_ref, hm_ref, gg_ref, wab_ref, wmb_ref, wo_ref, g2_ref, wg_ref, wu_ref, wd_ref,
                 gf_ref, out_ref, *, ff_split):
    def features_major(ref):
        return jnp.concatenate([ref[k] for k in range(ref.shape[0])], axis=1)

    tn = (((0,), (0,)), ((), ()))
    ya = lax.dot_general(features_major(ao_ref), wab_ref[...], tn, preferred_element_type=F32)
    ym = lax.dot_general(features_major(hm_ref), wmb_ref[...], tn, preferred_element_type=F32)
    ga = jax.nn.sigmoid(gg_ref[:, 0:D_MODEL].astype(F32))
    gm = jax.nn.sigmoid(gg_ref[:, D_MODEL:2 * D_MODEL].astype(F32))
    z = (ga * ya + gm * ym).astype(BF16)
    x1 = x_ref[...] + _dot(z, wo_ref[...])
    f = (x1 * g2_ref[...]).astype(BF16)
    rs = _rms_scale(x1)
    n_tiles = wg_ref.shape[1] // MXU_WIDTH
    bounds = [MXU_WIDTH * ((n_tiles * s + ff_split - 1) // ff_split) for s in range(ff_split + 1)]
    x2 = x1
    for s in range(ff_split):
        cols = slice(bounds[s], bounds[s + 1])
        gte = _dot(f, wg_ref[:, cols]) * rs
        up = _dot(f, wu_ref[:, cols]) * rs
        hh = (gte * jax.nn.sigmoid(gte) * up).astype(BF16)
        x2 = x2 + _dot(hh, wd_ref[cols, :])
    out_ref[...] = x2 * _rms_scale(x2) * gf_ref[...]


def _tail(x2d, ao, hm, gg, wab, wmb, wo, g2, wg, wu, wd, gf, tm, ff_split):
    t = x2d.shape[0]
    row = lambda w: pl.BlockSpec((tm, w), lambda i: (i, 0))
    consts = (wab, wmb, wo, g2, wg, wu, wd, gf)
    return pl.pallas_call(
        functools.partial(_tail_kernel, ff_split=ff_split),
        grid=(t // tm,),
        in_specs=[row(D_MODEL), pl.BlockSpec((tm // CHUNK, ATTN_Q_WIDTH, CHUNK), lambda i: (i, 0, 0)),
                  pl.BlockSpec((tm // CHUNK, MLSTM_V_WIDTH, CHUNK), lambda i: (i, 0, 0)), row(MERGE_WIDTH)]
                 + [_const_spec(c.shape) for c in consts],
        out_specs=row(D_MODEL),
        out_shape=jax.ShapeDtypeStruct((t, D_MODEL), F32),
        compiler_params=pltpu.CompilerParams(dimension_semantics=("arbitrary",),
                                             vmem_limit_bytes=VMEM_LIMIT_BYTES),
        name="tail",
    )(x2d, ao, hm, gg, *consts)


def _layer(x2d, batch, seq, norm1_g, w_in, conv_w, conv_b, i_bias, f_bias, mlstm_norm_g, attn_sinks,
           w_attn_branch, w_mlstm_branch, w_out, norm2_g, w_ffn_gate, w_ffn_up, w_ffn_down, out_g):
    gbias = jnp.broadcast_to(jnp.pad(jnp.concatenate([i_bias, f_bias]), (0, GATE_ROWS - 2 * MLSTM_HEADS))[:, None],
                             (GATE_ROWS, LANES))

    tail_weights = [(w_attn_branch, Q_HEAD_ORDER), (w_mlstm_branch, None), (w_out, None),
                    (w_ffn_gate, None), (w_ffn_up, None), (w_ffn_down, None)]
    qkv, mm, gt, gg, vt_all, ogt_all, wab, wmb, wo, wg, wu, wd = _inproj(
        x2d, norm1_g.reshape(1, D_MODEL), w_in.astype(BF16), conv_w, conv_b.reshape(1, -1), tail_weights,
        tm=INPROJ_TM, seq=seq)
    ao, hm = _mixer(attn_sinks, qkv, mm, gt, vt_all, ogt_all, gbias, mlstm_norm_g.reshape(1, MLSTM_V_WIDTH), [],
                    batch, seq, nch=MIXER_CHUNKS)
    return _tail(x2d, ao, hm, gg, wab, wmb, wo, norm2_g.reshape(1, D_MODEL), wg, wu, wd,
                 out_g.reshape(1, D_MODEL), tm=TAIL_TM, ff_split=TAIL_FF_SPLIT)


def kernel(x, norm1_g, w_in, conv_w, conv_b, i_bias, f_bias, mlstm_norm_g, attn_sinks, w_attn_branch,
           w_mlstm_branch, w_out, norm2_g, w_ffn_gate, w_ffn_up, w_ffn_down, final_norm_g):
    batch, seq, d = x.shape
    depth = norm1_g.shape[0]
    assert depth == 1 and d == D_MODEL
    assert seq % (MIXER_CHUNKS * CHUNK) == 0 and seq % INPROJ_TM == 0 and (batch * seq) % TAIL_TM == 0
    assert w_ffn_gate.shape[-1] % MXU_WIDTH == 0
    out = _layer(x.reshape(batch * seq, d), batch, seq, norm1_g[0], w_in[0], conv_w[0], conv_b[0], i_bias[0],
                 f_bias[0], mlstm_norm_g[0], attn_sinks[0], w_attn_branch[0], w_mlstm_branch[0], w_out[0],
                 norm2_g[0], w_ffn_gate[0], w_ffn_up[0], w_ffn_down[0], final_norm_g)
    return out.reshape(batch, seq, d)
```
